```python
import math
import jax, jax.numpy as jnp
from jax import lax
import numpy as np

D_MODEL = 1024
BATCH = 8
SEQ = 2048
DEPTH = 1
DEC_BATCH = 32
DEC_SEQ = 1
PAST_LEN = 16384
PAGE_SIZE = 128

N_META = 16
A_HEADS = 8
A_NOPE = 64
A_ROPE = 32
A_QK = A_NOPE + A_ROPE
A_V = 64
A_WIDTH = A_HEADS * A_V
Q_LORA = 384
KV_LORA = 256
ROPE_THETA = 10000.0
Q_BLOCK = 128
B_HEADS = 4
B_DK = 128
B_DV = 128
B_FDIM = B_HEADS * B_DK
B_WIDTH = B_HEADS * B_DV
CHUNK = 64
EPS = 1e-6
SPLIT_SIZES = (Q_LORA, KV_LORA, A_ROPE, B_FDIM, B_FDIM, B_WIDTH, A_WIDTH, B_WIDTH, D_MODEL, D_MODEL)
SPLIT_IDX = tuple(int(c) for c in np.cumsum(SPLIT_SIZES)[:-1])
IN_COLS = sum(SPLIT_SIZES)

kernel_name = 'mla_hgrn2_gated_hybrid_step'


def rmsnorm(x, g):
    xf = x.astype(jnp.float32)
    r = lax.rsqrt(jnp.mean(xf * xf, axis=-1, keepdims=True) + EPS)
    return (xf * r).astype(x.dtype) * g


def rope(x, pos):
    half = A_ROPE // 2
    inv = ROPE_THETA ** (-jnp.arange(half, dtype=jnp.float32) / half)
    ang = pos.astype(jnp.float32)[:, None] * inv
    ang = ang.reshape(ang.shape[0], *([1] * (x.ndim - 3)), half)
    cos, sin = jnp.cos(ang).astype(x.dtype), jnp.sin(ang).astype(x.dtype)
    x1, x2 = x[..., :half], x[..., half:]
    return jnp.concatenate([x1 * cos - x2 * sin, x1 * sin + x2 * cos], axis=-1)


def mla_queries(cq, pos, g_cq, w_uq, g_qn):
    q = (rmsnorm(cq, g_cq) @ w_uq).reshape(*cq.shape[:-1], A_HEADS, A_QK)
    q = jnp.concatenate([q[..., :A_NOPE], rope(q[..., A_NOPE:], pos)], axis=-1)
    return rmsnorm(q, g_qn)


def mla_keys(ckv, kr, w_uk, g_kn):
    kn = jnp.einsum('bkc,chd->bkhd', ckv, w_uk)
    kr = jnp.broadcast_to(kr[:, :, None, :], kn.shape[:-1] + (A_ROPE,))
    return rmsnorm(jnp.concatenate([kn, kr], axis=-1), g_kn)


def mla_prompt(q, k, v, with_meta):
    b = q.shape[0]
    scale = A_QK ** -0.5
    kpos = jnp.arange(k.shape[1])

    def attend(qb, qpos, kk, vv, kp):
        s = jnp.einsum('bqhd,bkhd->bhqk', qb, kk).astype(jnp.float32) * scale
        s = jnp.where(kp[None, :] <= qpos[:, None], s, -jnp.inf)
        p = jax.nn.softmax(s, axis=-1).astype(vv.dtype)
        return jnp.einsum('bhqk,bkhd->bqhd', p, vv)

    nb = SEQ // Q_BLOCK
    qr = q[:, -SEQ:].reshape(b, nb, Q_BLOCK, A_HEADS, A_QK).swapaxes(0, 1)

    def block(args):
        qb, j = args
        return attend(qb, N_META + j * Q_BLOCK + jnp.arange(Q_BLOCK), k, v, kpos)

    o = lax.map(block, (qr, jnp.arange(nb)))
    o = o.swapaxes(0, 1).reshape(b, SEQ, A_HEADS, A_V)
    if with_meta:
        mpos = jnp.arange(N_META)
        om = attend(q[:, :N_META], mpos, k[:, :N_META], v[:, :N_META], mpos)
        o = jnp.concatenate([om, o], axis=1)
    return o


def online_update(carry, s, c):
    m, l_, acc = carry
    m_new = jnp.maximum(m, s.max(-1))
    a = jnp.exp(m - m_new)
    p = jnp.exp(s - m_new[..., None])
    acc = acc * a[..., None] + jnp.einsum('bhqk,bkc->bhqc', p, c.astype(jnp.float32))
    return (m_new, l_ * a + p.sum(-1), acc)


def mla_sample(q, ckv, kr, cache_lat, cache_kr, page_table, w_uk, w_uv, g_kn):
    db, t = q.shape[:2]
    scale = A_QK ** -0.5

    def scores(c, r):
        return jnp.einsum('bqhd,bkhd->bhqk', q, mla_keys(c, r, w_uk, g_kn)).astype(jnp.float32) * scale

    def page_step(carry, pt):
        c = cache_lat[pt]
        r = cache_kr[pt]
        return online_update(carry, scores(c, r), c), None

    init = (jnp.full((db, A_HEADS, t), -jnp.inf, jnp.float32),
            jnp.zeros((db, A_HEADS, t), jnp.float32),
            jnp.zeros((db, A_HEADS, t, KV_LORA), jnp.float32))
    carry, _ = lax.scan(page_step, init, page_table.T)
    causal = jnp.tril(jnp.ones((t, t), bool))
    s_new = jnp.where(causal, scores(ckv, kr), -jnp.inf)
    _, l_, acc = online_update(carry, s_new, ckv)
    o_lat = (acc / l_[..., None]).astype(w_uv.dtype)
    return jnp.einsum('bhqc,chd->bqhd', o_lat, w_uv)


def hgrn2_inputs(bq, bf, bi, lb):
    z = bf.astype(jnp.float32)
    logf = jnp.log(lb + (1.0 - lb) * jax.nn.sigmoid(z))
    k = (1.0 - lb) * jax.nn.sigmoid(-z)
    q = jax.nn.silu(bq.astype(jnp.float32))
    heads = lambda a: a.reshape(*a.shape[:-1], B_HEADS, -1)
    return heads(q), heads(k), heads(bi.astype(jnp.float32)), heads(logf)


def hgrn2_chunked(q, k, v, logf, s0, chunk):
    b, L = q.shape[:2]
    n = L // chunk
    chunks = lambda a: a.reshape(b, n, chunk, *a.shape[2:]).swapaxes(0, 1)
    causal = jnp.tril(jnp.ones((chunk, chunk), bool))[None, :, :, None, None]

    def step(S, xs):
        qc, kc, vc, gc = xs
        cum = jnp.cumsum(gc, axis=1)
        decay = jnp.exp(jnp.where(causal, cum[:, :, None] - cum[:, None, :], -jnp.inf))
        attn = jnp.einsum('bthk,bshk,btshk->bhts', qc, kc, decay)
        o = jnp.einsum('bhts,bshv->bthv', attn, vc) + jnp.einsum('bthk,bhkv->bthv', qc * jnp.exp(cum), S)
        last = cum[:, -1]
        S = jnp.exp(last)[..., None] * S + jnp.einsum('bshk,bshv->bhkv', kc * jnp.exp(last[:, None] - cum), vc)
        return S, o

    S, o = lax.scan(step, s0, (chunks(q), chunks(k), chunks(v), chunks(logf)))
    return o.swapaxes(0, 1).reshape(b, L, *o.shape[3:]), S


def merge(h, oa, ob, ga, gb, ma, mb, g_bn, w_oa, w_ob, w_o):
    ya = (oa.reshape(*oa.shape[:2], A_WIDTH) * jax.nn.silu(ga)) @ w_oa
    obn = rmsnorm(ob.astype(h.dtype), g_bn).reshape(*ob.shape[:2], B_WIDTH)
    yb = (obn * jax.nn.silu(gb)) @ w_ob
    return h + (jax.nn.sigmoid(ma) * ya + jax.nn.sigmoid(mb) * yb) @ w_o


def setup_inputs(seed: int = 0) -> dict:
    key = jax.random.key(seed)
    ks = jax.random.split(key, 24)
    f32 = jnp.float32
    n_pages = PAST_LEN // PAGE_SIZE
    n_used = DEC_BATCH * n_pages
    n_pool = n_used + max(1, n_used // 4)
    nrm = lambda k, shape, s=1.0: s * jax.random.normal(k, shape, f32)
    gain = lambda k, n: 1.0 + 0.01 * jax.random.normal(k, (DEPTH, n), f32)
    page_table = jax.random.permutation(ks[5], n_pool)[:n_used].reshape(DEC_BATCH, n_pages).astype(jnp.int32)
    return {
        'x_prompt': nrm(ks[0], (BATCH, SEQ, D_MODEL)),
        'x_sample': nrm(ks[1], (DEC_BATCH, DEC_SEQ, D_MODEL)),
        'cache_latent': nrm(ks[2], (DEPTH, n_pool, PAGE_SIZE, KV_LORA)),
        'cache_krope': nrm(ks[3], (DEPTH, n_pool, PAGE_SIZE, A_ROPE)),
        'state_hgrn': nrm(ks[4], (DEPTH, DEC_BATCH, B_HEADS, B_DK, B_DV), 0.3),
        'page_table': page_table,
        'meta_tokens': nrm(ks[6], (N_META, D_MODEL)),
        'norm_g': gain(ks[7], D_MODEL),
        'w_in': nrm(ks[8], (DEPTH, D_MODEL, IN_COLS), D_MODEL ** -0.5),
        'g_cq': gain(ks[9], Q_LORA),
        'w_uq': nrm(ks[10], (DEPTH, Q_LORA, A_HEADS * A_QK), Q_LORA ** -0.5),
        'g_ckv': gain(ks[11], KV_LORA),
        'w_uk': nrm(ks[12], (DEPTH, KV_LORA, A_HEADS, A_NOPE), KV_LORA ** -0.5),
        'w_uv': nrm(ks[13], (DEPTH, KV_LORA, A_HEADS, A_V), KV_LORA ** -0.5),
        'g_qn': gain(ks[14], A_QK),
        'g_kn': gain(ks[15], A_QK),
        'lb_logits': nrm(ks[16], (DEPTH + 1, B_FDIM), 0.5),
        'g_bn': gain(ks[17], B_DV),
        'w_oa': nrm(ks[18], (DEPTH, A_WIDTH, D_MODEL), A_WIDTH ** -0.5),
        'w_ob': nrm(ks[19], (DEPTH, B_WIDTH, D_MODEL), B_WIDTH ** -0.5),
        'w_o': nrm(ks[20], (DEPTH, D_MODEL, D_MODEL), D_MODEL ** -0.5),
    }


def reference(x_prompt, x_sample, cache_latent, cache_krope, state_hgrn, page_table, meta_tokens,
              norm_g, w_in, g_cq, w_uq, g_ckv, w_uk, w_uv, g_qn, g_kn, lb_logits, g_bn, w_oa, w_ob, w_o):
    f32 = jnp.float32
    lb_all = jnp.cumsum(jax.nn.softmax(lb_logits.astype(f32), axis=0), axis=0)
    b = x_prompt.shape[0]
    hp = jnp.concatenate([jnp.broadcast_to(meta_tokens[None].astype(x_prompt.dtype), (b, N_META, D_MODEL)), x_prompt], axis=1)
    hs = x_sample
    pos_p = jnp.arange(N_META + SEQ)
    pos_s = PAST_LEN + jnp.arange(DEC_SEQ)
    lat_p, kr_p, st_p, lat_s, kr_s, st_s = [], [], [], [], [], []
    for l in range(DEPTH):
        last = l == DEPTH - 1
        r0 = N_META if last else 0
        xn = rmsnorm(hp, norm_g[l])
        cq, ckv, kr, bq, bf, bi, ga, gb, ma, mb = jnp.split(xn @ w_in[l], SPLIT_IDX, axis=-1)
        ckv = rmsnorm(ckv, g_ckv[l])
        kr = rope(kr, pos_p)
        k = mla_keys(ckv, kr, w_uk[l], g_kn[l])
        v = jnp.einsum('bkc,chd->bkhd', ckv, w_uv[l])
        q = mla_queries(cq[:, r0:], pos_p[r0:], g_cq[l], w_uq[l], g_qn[l])
        oa = mla_prompt(q, k, v, not last)
        hq, hk, hv, hf = hgrn2_inputs(bq, bf, bi, lb_all[l])
        s0 = jnp.zeros((b, B_HEADS, B_DK, B_DV), f32)
        om, s_meta = hgrn2_chunked(hq[:, :N_META], hk[:, :N_META], hv[:, :N_META], hf[:, :N_META], s0, N_META)
        orl, s_fin = hgrn2_chunked(hq[:, N_META:], hk[:, N_META:], hv[:, N_META:], hf[:, N_META:], s_meta, CHUNK)
        ob = orl if last else jnp.concatenate([om, orl], axis=1)
        hp = merge(hp[:, r0:], oa, ob, ga[:, r0:], gb[:, r0:], ma[:, r0:], mb[:, r0:], g_bn[l], w_oa[l], w_ob[l], w_o[l])
        lat_p.append(ckv)
        kr_p.append(kr)
        st_p.append(s_fin.astype(state_hgrn.dtype))
        xn = rmsnorm(hs, norm_g[l])
        cq, ckv, kr, bq, bf, bi, ga, gb, ma, mb = jnp.split(xn @ w_in[l], SPLIT_IDX, axis=-1)
        ckv = rmsnorm(ckv, g_ckv[l])
        kr = rope(kr, pos_s)
        q = mla_queries(cq, pos_s, g_cq[l], w_uq[l], g_qn[l])
        oa = mla_sample(q, ckv, kr, cache_latent[l], cache_krope[l], page_table, w_uk[l], w_uv[l], g_kn[l])
        hq, hk, hv, hf = hgrn2_inputs(bq, bf, bi, lb_all[l])
        ob, s_new = hgrn2_chunked(hq, hk, hv, hf, state_hgrn[l].astype(f32), math.gcd(DEC_SEQ, CHUNK))
        hs = merge(hs, oa, ob, ga, gb, ma, mb, g_bn[l], w_oa[l], w_ob[l], w_o[l])
        lat_s.append(ckv)
        kr_s.append(kr)
        st_s.append(s_new.astype(state_hgrn.dtype))
    return (hp, hs, jnp.stack(lat_p), jnp.stack(kr_p), jnp.stack(st_p), jnp.stack(lat_s), jnp.stack(kr_s), jnp.stack(st_s))
```

```python
import functools

import jax
import jax.numpy as jnp
from jax import lax
from jax.experimental import pallas as pl
from jax.experimental.pallas import tpu as pltpu

F32 = jnp.float32
BF16 = jnp.bfloat16

D_MODEL = 1024
N_META = 16
A_HEADS = 8
A_NOPE = 64
A_ROPE = 32
A_QK = A_NOPE + A_ROPE
A_V = 64
A_WIDTH = A_HEADS * A_V
Q_LORA = 384
KV_LORA = 256
ROPE_THETA = 10000.0
B_HEADS = 4
B_DK = 128
B_DV = 128
B_FDIM = B_HEADS * B_DK
B_WIDTH = B_HEADS * B_DV
EPS = 1e-6
PAGE = 128

LANES = 128
HEAD_PAD = LANES
VMEM_LIMIT = 56 * 1024 * 1024

ROW_TILE = 256
ATTN_TQ = 256
ATTN_TK = 256
HGRN_CHUNK = 64
HGRN_TILE = 256
PAGES_PER_GROUP = 8
POS_BLOCK = 2 * PAGE


def _full(shape):
    return pl.BlockSpec(shape, lambda *_: (0,) * len(shape))


def _params(n_axes):
    return pltpu.CompilerParams(dimension_semantics=("arbitrary",) * n_axes,
                                vmem_limit_bytes=VMEM_LIMIT)


def _dot(a, b):
    return jnp.dot(a, b, preferred_element_type=F32)


def _dot_nt(a, b, precision=None):
    return lax.dot_general(a, b, (((1,), (1,)), ((), ())), precision=precision,
                           preferred_element_type=F32)


def _rms(x, g):
    r = lax.rsqrt(jnp.mean(x * x, axis=-1, keepdims=True) + EPS)
    return (x * r) * g


def _rope_lanes(t, c, s1, s2):
    return t * c + pltpu.roll(t, LANES - A_ROPE // 2, axis=1) * s1 + pltpu.roll(t, A_ROPE // 2, axis=1) * s2


def _pre_kernel(x_ref, ng_ref, wq_ref, wkv_ref, wkr_ref, wb_ref, gcq_ref, wuq_ref, gckv_ref,
                wuk_ref, wuv_ref, gqn_ref, gkn_ref, c_ref, s1_ref, s2_ref, lb_ref,
                lat_ref, krp_ref, q_ref, k_ref, v_ref, hq_ref, hk_ref, hv_ref, hg_ref, *, layer):
    x = x_ref[...]
    xn = _rms(x, ng_ref[...]).astype(BF16)
    c, s1, s2 = c_ref[...], s1_ref[...], s2_ref[...]

    cq = _dot(xn, wq_ref[...])
    cqn = _rms(cq, gcq_ref[...]).astype(BF16)
    qraw = _dot(cqn, wuq_ref[...])
    gqn = gqn_ref[...]
    for h in range(A_HEADS):
        t = _rope_lanes(qraw[:, h * HEAD_PAD:(h + 1) * HEAD_PAD], c, s1, s2)
        r = lax.rsqrt(jnp.sum(t * t, axis=-1, keepdims=True) * (1.0 / A_QK) + EPS)
        q_ref[:, h * HEAD_PAD:(h + 1) * HEAD_PAD] = ((t * r) * gqn).astype(BF16)

    ckv = _rms(_dot(xn, wkv_ref[...]), gckv_ref[...])
    lat_ref[...] = ckv
    ckvb = ckv.astype(BF16)
    krp = _rope_lanes(_dot(xn, wkr_ref[...]), c, s1, s2)
    krp_ref[...] = krp
    kraw = _dot(ckvb, wuk_ref[...])
    gkn = gkn_ref[...]
    for h in range(A_HEADS):
        t = kraw[:, h * HEAD_PAD:(h + 1) * HEAD_PAD] + krp
        r = lax.rsqrt(jnp.sum(t * t, axis=-1, keepdims=True) * (1.0 / A_QK) + EPS)
        k_ref[:, h * HEAD_PAD:(h + 1) * HEAD_PAD] = ((t * r) * gkn).astype(BF16)
    v_ref[...] = _dot(ckvb, wuv_ref[...]).astype(BF16)

    lbl = lb_ref[...]
    e = jnp.exp(lbl - jnp.max(lbl, axis=0, keepdims=True))
    lb = jnp.sum(e[:layer + 1], axis=0, keepdims=True) / jnp.sum(e, axis=0, keepdims=True)
    b = _dot(xn, wb_ref[...])
    bq, z, bi = b[:, :B_FDIM], b[:, B_FDIM:2 * B_FDIM], b[:, 2 * B_FDIM:]
    hq_ref[...] = bq * jax.nn.sigmoid(bq)
    hg_ref[...] = jnp.log(lb + (1.0 - lb) * jax.nn.sigmoid(z))
    hk_ref[...] = (1.0 - lb) * jax.nn.sigmoid(-z)
    hv_ref[...] = bi


def _pre(x, tabs, w, tm, tab_blocks, layer):
    rows = x.shape[0]
    grid = (rows // tm,)
    row = lambda n: pl.BlockSpec((tm, n), lambda i: (i, 0))
    tab = pl.BlockSpec((tm, LANES), lambda i: (i % tab_blocks, 0))
    ins = [x, w["norm_g"], w["wq"], w["wkv"], w["wkr"], w["wb"], w["g_cq"], w["w_uq"], w["g_ckv"],
           w["w_uk"], w["w_uv"], w["g_qn"], w["g_kn"], tabs[0], tabs[1], tabs[2], w["lb_logits"]]
    in_specs = [row(D_MODEL)] + [_full(a.shape) for a in ins[1:13]] + [tab, tab, tab] + [_full(ins[16].shape)]
    outs = [(KV_LORA, F32), (LANES, F32), (A_HEADS * HEAD_PAD, BF16), (A_HEADS * HEAD_PAD, BF16),
            (A_WIDTH, BF16), (B_FDIM, F32), (B_FDIM, F32), (B_WIDTH, F32), (B_FDIM, F32)]
    return pl.pallas_call(
        functools.partial(_pre_kernel, layer=layer),
        grid=grid, in_specs=in_specs,
        out_specs=[row(n) for n, _ in outs],
        out_shape=[jax.ShapeDtypeStruct((rows, n), d) for n, d in outs],
        compiler_params=_params(1), name="pre_proj")(*ins)


def _attn_kernel(q_ref, k_ref, v_ref, km_ref, vm_ref, o_ref):
    i = pl.program_id(1)
    tq, tk = ATTN_TQ, ATTN_TK
    scale = A_QK ** -0.5
    lane = lax.broadcasted_iota(jnp.int32, (tq, LANES), 1)
    left = lane < A_V
    rows = lax.broadcasted_iota(jnp.int32, (tq, tk), 0)
    cols = lax.broadcasted_iota(jnp.int32, (tq, tk), 1)
    causal = cols <= rows

    for p in range(A_HEADS // 2):
        hs = (2 * p, 2 * p + 1)
        qs = [q_ref[:, h * HEAD_PAD:(h + 1) * HEAD_PAD] for h in hs]
        vsl = slice(p * LANES, (p + 1) * LANES)

        def update(carry, kblk, vblk, mask):
            ms, ls, acc = carry
            new_m, new_l, alphas, pvs = [], [], [], []
            for j in range(2):
                s = _dot_nt(qs[j], kblk[j]) * scale
                if mask is not None:
                    s = jnp.where(mask, s, -jnp.inf)
                m_new = jnp.maximum(ms[j], jnp.max(s, axis=-1, keepdims=True))
                a = jnp.exp(ms[j] - m_new)
                pr = jnp.exp(s - m_new)
                new_m.append(m_new)
                new_l.append(ls[j] * a + jnp.sum(pr, axis=-1, keepdims=True))
                alphas.append(a)
                pvs.append(_dot(pr.astype(BF16), vblk))
            acc = jnp.where(left, alphas[0], alphas[1]) * acc + jnp.where(left, pvs[0], pvs[1])
            return (tuple(new_m), tuple(new_l), acc)

        neg = jnp.full((tq, 1), -jnp.inf, F32)
        zero = jnp.zeros((tq, 1), F32)
        carry = ((neg, neg), (zero, zero), jnp.zeros((tq, LANES), F32))
        carry = update(carry, [km_ref[:, h * HEAD_PAD:(h + 1) * HEAD_PAD] for h in hs], vm_ref[:, vsl], None)

        def body(kb, carry):
            r0 = pl.multiple_of(kb * tk, tk)
            kblk = [k_ref[pl.ds(r0, tk), h * HEAD_PAD:(h + 1) * HEAD_PAD] for h in hs]
            return update(carry, kblk, v_ref[pl.ds(r0, tk), vsl], None)

        carry = lax.fori_loop(0, i, body, carry)
        r0 = pl.multiple_of(i * tk, tk)
        kblk = [k_ref[pl.ds(r0, tk), h * HEAD_PAD:(h + 1) * HEAD_PAD] for h in hs]
        ms, ls, acc = update(carry, kblk, v_ref[pl.ds(r0, tk), vsl], causal)
        o_ref[:, vsl] = acc / jnp.where(left, ls[0], ls[1])


def _attn(q, k, v, km, vm, batch, seq):
    nq = seq // ATTN_TQ
    return pl.pallas_call(
        _attn_kernel, grid=(batch, nq),
        in_specs=[pl.BlockSpec((ATTN_TQ, A_HEADS * HEAD_PAD), lambda b, i: (b * nq + i, 0)),
                  pl.BlockSpec((seq, A_HEADS * HEAD_PAD), lambda b, i: (b, 0)),
                  pl.BlockSpec((seq, A_WIDTH), lambda b, i: (b, 0)),
                  _full(km.shape), _full(vm.shape)],
        out_specs=pl.BlockSpec((ATTN_TQ, A_WIDTH), lambda b, i: (b * nq + i, 0)),
        out_shape=jax.ShapeDtypeStruct((batch * seq, A_WIDTH), F32),
        compiler_params=_params(2), name="prompt_attn")(q, k, v, km, vm)


def _split3(x):
    a = x.astype(BF16)
    r = x - a.astype(F32)
    b = r.astype(BF16)
    c = (r - b.astype(F32)).astype(BF16)
    return a, b, c


def _hgrn_chunk(q, kk, v, g, st, tril, chunk):
    g1, g2, g3 = _split3(g)
    cum = _dot(tril, g1) + _dot(tril, g2) + _dot(tril, g3)
    nblk = chunk // 8
    cb = [cum[8 * j:8 * j + 8] for j in range(nblk)]
    qb = [q[8 * j:8 * j + 8] for j in range(nblk)]
    lane = lax.broadcasted_iota(jnp.int32, (8, chunk), 1)
    sub = lax.broadcasted_iota(jnp.int32, (8, chunk), 0)
    ab = [jnp.zeros((8, chunk), F32) for _ in range(nblk)]
    for s in range(chunk):
        cs = cum[s:s + 1]
        ks = kk[s:s + 1]
        sel = lane == s
        for j in range(s // 8, nblk):
            col = jnp.sum(jnp.exp(cb[j] - cs) * qb[j] * ks, axis=-1, keepdims=True)
            ab[j] = jnp.where(sel, col, ab[j])
    ab = [jnp.where(lane <= sub + 8 * j, ab[j], 0.0) for j in range(nblk)]
    attn = jnp.concatenate(ab, axis=0)
    last = cum[chunk - 1:chunk]
    o = _dot(attn.astype(BF16), v.astype(BF16)) + _dot_nt((q * jnp.exp(cum)).astype(BF16), st.astype(BF16))
    kdec = (kk * jnp.exp(last - cum)).astype(BF16)
    upd = lax.dot_general(v.astype(BF16), kdec, (((0,), (0,)), ((), ())), preferred_element_type=F32)
    return o, jnp.exp(last) * st + upd


def _hgrn_kernel(q_ref, k_ref, v_ref, g_ref, st0_ref, o_ref, st_ref, st_sc, *, chunk, tile, transpose_out):
    t = pl.program_id(1)

    @pl.when(t == 0)
    def _():
        st_sc[...] = st0_ref[...]

    r = lax.broadcasted_iota(jnp.int32, (chunk, chunk), 0)
    c = lax.broadcasted_iota(jnp.int32, (chunk, chunk), 1)
    tril = (c <= r).astype(BF16)

    def body(ci, _):
        r0 = pl.multiple_of(ci * chunk, chunk)
        for h in range(B_HEADS):
            hs = slice(h * B_DK, (h + 1) * B_DK)
            o, st = _hgrn_chunk(q_ref[pl.ds(r0, chunk), hs], k_ref[pl.ds(r0, chunk), hs],
                                v_ref[pl.ds(r0, chunk), hs], g_ref[pl.ds(r0, chunk), hs],
                                st_sc[h], tril, chunk)
            o_ref[pl.ds(r0, chunk), hs] = o
            st_sc[h] = st
        return 0

    lax.fori_loop(0, tile // chunk, body, 0)

    @pl.when(t == pl.num_programs(1) - 1)
    def _():
        for h in range(B_HEADS):
            st_ref[0, h] = st_sc[h].T if transpose_out else st_sc[h]


def _hgrn(hq, hk, hv, hg, st0, batch, seq, chunk, tile, transpose_out):
    nt = seq // tile
    row = pl.BlockSpec((tile, B_FDIM), lambda b, t: (b * nt + t, 0))
    return pl.pallas_call(
        functools.partial(_hgrn_kernel, chunk=chunk, tile=tile, transpose_out=transpose_out),
        grid=(batch, nt),
        in_specs=[row, row, row, row, _full(st0.shape)],
        out_specs=[row, pl.BlockSpec((1, B_HEADS, B_DV, B_DK), lambda b, t: (b, 0, 0, 0))],
        out_shape=[jax.ShapeDtypeStruct((batch * seq, B_WIDTH), F32),
                   jax.ShapeDtypeStruct((batch, B_HEADS, B_DV, B_DK), F32)],
        scratch_shapes=[pltpu.VMEM((B_HEADS, B_DV, B_DK), F32)],
        compiler_params=_params(2), name="hgrn_chunks")(hq, hk, hv, hg, st0)


def _paged_kernel(pt_ref, lat_hbm, kr_hbm, qcol_ref, qr_ref, qh_ref, kh_ref, latn_ref, wukt_ref,
                  gkc_ref, gkr_ref, wuv_ref, o_ref, latbuf, krbuf, sems, *, n_pages):
    b = pl.program_id(0)
    gp = PAGES_PER_GROUP
    n_groups = n_pages // gp
    scale = A_QK ** -0.5

    def page_copies(g, slot):
        out = []
        for j in range(gp):
            page = pt_ref[b, g * gp + j]
            out.append(pltpu.make_async_copy(lat_hbm.at[page], latbuf.at[slot, pl.ds(j * PAGE, PAGE)], sems.at[0, slot]))
            out.append(pltpu.make_async_copy(kr_hbm.at[page], krbuf.at[slot, pl.ds(j * PAGE, PAGE)], sems.at[1, slot]))
        return out

    def start(g, slot):
        for cp in page_copies(g, slot):
            cp.start()

    def wait(g, slot):
        for cp in page_copies(g, slot):
            cp.wait()

    qg = qcol_ref[0] * gkc_ref[...]
    qrg = (qr_ref[0] * gkr_ref[...]).astype(F32)
    ones = jnp.ones((A_HEADS, A_ROPE), F32)
    wukt = wukt_ref[...]

    def block(slot, r0, carry):
        m, l, acc = carry
        cb = latbuf[slot, pl.ds(r0, POS_BLOCK), :].astype(BF16)
        kr = krbuf[slot, pl.ds(r0, POS_BLOCK), :]
        knt = _dot_nt(wukt, cb)
        ss = jnp.sum((knt * knt).reshape(A_HEADS, A_NOPE, POS_BLOCK), axis=1)
        ss = ss + _dot_nt(ones, kr * kr, precision=lax.Precision.HIGHEST)
        raw = jnp.sum((knt * qg).reshape(A_HEADS, A_NOPE, POS_BLOCK), axis=1)
        raw = raw + _dot_nt(qrg, kr, precision=lax.Precision.HIGHEST)
        s = raw * lax.rsqrt(ss * (1.0 / A_QK) + EPS) * scale
        m_new = jnp.maximum(m, jnp.max(s, axis=-1, keepdims=True))
        a = jnp.exp(m - m_new)
        p = jnp.exp(s - m_new)
        l = l * a + jnp.sum(p, axis=-1, keepdims=True)
        acc = acc * a + _dot(p.astype(BF16), cb)
        return m_new, l, acc

    def group(g, carry):
        slot = lax.rem(g, 2)
        wait(g, slot)

        @pl.when(g + 1 < n_groups)
        def _():
            start(g + 1, 1 - slot)

        for j in range(gp * PAGE // POS_BLOCK):
            carry = block(slot, j * POS_BLOCK, carry)
        return carry

    start(0, 0)
    init = (jnp.full((A_HEADS, 1), -jnp.inf, F32), jnp.zeros((A_HEADS, 1), F32),
            jnp.zeros((A_HEADS, KV_LORA), F32))
    m, l, acc = lax.fori_loop(0, n_groups, group, init)

    s_new = jnp.sum(qh_ref[0].astype(F32) * kh_ref[0].astype(F32), axis=-1, keepdims=True) * scale
    m_new = jnp.maximum(m, s_new)
    a = jnp.exp(m - m_new)
    p = jnp.exp(s_new - m_new)
    acc = acc * a + p * latn_ref[0]
    l = l * a + p
    o_lat = (acc / l).astype(BF16)
    full = _dot(o_lat, wuv_ref[...])
    hrow = lax.broadcasted_iota(jnp.int32, (A_HEADS, A_WIDTH), 0)
    hcol = lax.broadcasted_iota(jnp.int32, (A_HEADS, A_WIDTH), 1) // A_V
    o_ref[0] = jnp.sum(jnp.where(hrow == hcol, full, 0.0), axis=0, keepdims=True)


def _paged(page_table, cache_lat, cache_kr, qcol, qr, qh, kh, latn, wukt, gkc, gkr, wuv):
    nb, n_pages = page_table.shape
    gp = PAGES_PER_GROUP
    per_b = lambda shape: pl.BlockSpec((1,) + shape, lambda b, pt: (b,) + (0,) * len(shape))
    full = lambda a: pl.BlockSpec(a.shape, lambda b, pt: (0,) * a.ndim)
    grid_spec = pltpu.PrefetchScalarGridSpec(
        num_scalar_prefetch=1, grid=(nb,),
        in_specs=[pl.BlockSpec(memory_space=pl.ANY), pl.BlockSpec(memory_space=pl.ANY),
                  per_b((A_HEADS * A_NOPE, 1)), per_b((A_HEADS, A_ROPE)), per_b((A_HEADS, HEAD_PAD)),
                  per_b((A_HEADS, HEAD_PAD)), per_b((1, KV_LORA)), full(wukt), full(gkc), full(gkr), full(wuv)],
        out_specs=per_b((1, A_WIDTH)),
        scratch_shapes=[pltpu.VMEM((2, gp * PAGE, KV_LORA), F32), pltpu.VMEM((2, gp * PAGE, A_ROPE), F32),
                        pltpu.SemaphoreType.DMA((2, 2))])
    return pl.pallas_call(
        functools.partial(_paged_kernel, n_pages=n_pages), grid_spec=grid_spec,
        out_shape=jax.ShapeDtypeStruct((nb, 1, A_WIDTH), F32),
        compiler_params=_params(1), name="paged_attn")(
            page_table, cache_lat, cache_kr, qcol, qr, qh, kh, latn, wukt, gkc, gkr, wuv)


def _hstep_kernel(s_ref, q_ref, k_ref, g_ref, v_ref, o_ref, sn_ref):
    for h in range(B_HEADS):
        sn = jnp.exp(g_ref[0, h]) * s_ref[0, h] + k_ref[0, h] * v_ref[0, h]
        sn_ref[0, h] = sn
        o_ref[0, h] = jnp.sum(q_ref[0, h] * sn, axis=0, keepdims=True)


def _hstep(state, qc, kc, gc, vr):
    nb = state.shape[0]
    st = pl.BlockSpec((1, B_HEADS, B_DK, B_DV), lambda b: (b, 0, 0, 0))
    col = pl.BlockSpec((1, B_HEADS, B_DK, 1), lambda b: (b, 0, 0, 0))
    rowv = pl.BlockSpec((1, B_HEADS, 1, B_DV), lambda b: (b, 0, 0, 0))
    return pl.pallas_call(
        _hstep_kernel, grid=(nb,), in_specs=[st, col, col, col, rowv], out_specs=[rowv, st],
        out_shape=[jax.ShapeDtypeStruct((nb, B_HEADS, 1, B_DV), F32), jax.ShapeDtypeStruct(state.shape, F32)],
        compiler_params=_params(1), name="hgrn_step")(state, qc, kc, gc, vr)


def _merge_kernel(x_ref, ng_ref, wg_ref, oa_ref, ob_ref, gbn_ref, woa_ref, wob_ref, wo_ref, y_ref):
    x = x_ref[...]
    xn = _rms(x, ng_ref[...]).astype(BF16)
    gates = _dot(xn, wg_ref[...])
    ga, gb = gates[:, :A_WIDTH], gates[:, A_WIDTH:A_WIDTH + B_WIDTH]
    ma = gates[:, A_WIDTH + B_WIDTH:A_WIDTH + B_WIDTH + D_MODEL]
    mb = gates[:, A_WIDTH + B_WIDTH + D_MODEL:]
    ya = _dot((oa_ref[...] * (ga * jax.nn.sigmoid(ga))).astype(BF16), woa_ref[...])
    gbn = gbn_ref[...]
    obn = jnp.concatenate([_rms(ob_ref[:, h * B_DV:(h + 1) * B_DV], gbn) for h in range(B_HEADS)], axis=-1)
    yb = _dot((obn * (gb * jax.nn.sigmoid(gb))).astype(BF16), wob_ref[...])
    mix = jax.nn.sigmoid(ma) * ya + jax.nn.sigmoid(mb) * yb
    y_ref[...] = x + _dot(mix.astype(BF16), wo_ref[...])


def _merge(x, oa, ob, w, tm):
    rows = x.shape[0]
    row = lambda n: pl.BlockSpec((tm, n), lambda i: (i, 0))
    ins = [x, w["norm_g"], w["wg"], oa, ob, w["g_bn"], w["w_oa"], w["w_ob"], w["w_o"]]
    in_specs = [row(D_MODEL), _full(ins[1].shape), _full(ins[2].shape), row(A_WIDTH), row(B_WIDTH)] + \
               [_full(a.shape) for a in ins[5:]]
    return pl.pallas_call(
        _merge_kernel, grid=(rows // tm,), in_specs=in_specs, out_specs=row(D_MODEL),
        out_shape=jax.ShapeDtypeStruct((rows, D_MODEL), F32),
        compiler_params=_params(1), name="merge_out")(*ins)


def _head_pad_cols(w3, width):
    pad = jnp.zeros(w3.shape[:2] + (HEAD_PAD - width,), w3.dtype)
    return jnp.concatenate([w3, pad], axis=-1).reshape(w3.shape[0], -1)


def _rope_tables(pos):
    half = A_ROPE // 2
    inv = ROPE_THETA ** (-jnp.arange(half, dtype=F32) / half)
    ang = pos.astype(F32)[:, None] * inv
    cos, sin = jnp.cos(ang), jnp.sin(ang)
    n = pos.shape[0]
    one, zero = jnp.ones((n, A_NOPE), F32), jnp.zeros((n, A_NOPE), F32)
    tail = jnp.zeros((n, HEAD_PAD - A_QK), F32)
    z16 = jnp.zeros((n, half), F32)
    c = jnp.concatenate([one, cos, cos, tail], axis=-1)
    s1 = jnp.concatenate([zero, -sin, z16, tail], axis=-1)
    s2 = jnp.concatenate([zero, z16, sin, tail], axis=-1)
    return c, s1, s2


def _lane_gain(g):
    return jnp.concatenate([g, jnp.zeros((HEAD_PAD - A_QK,), g.dtype)])[None, :]


def kernel(x_prompt, x_sample, cache_latent, cache_krope, state_hgrn, page_table, meta_tokens,
           norm_g, w_in, g_cq, w_uq, g_ckv, w_uk, w_uv, g_qn, g_kn, lb_logits, g_bn, w_oa, w_ob, w_o):
    batch, seq, _ = x_prompt.shape
    dec_batch, dec_seq, _ = x_sample.shape
    depth = w_in.shape[0]
    assert depth == 1 and dec_seq == 1
    past_len = page_table.shape[1] * PAGE
    l = 0

    o0 = Q_LORA
    o1 = o0 + KV_LORA
    o2 = o1 + A_ROPE
    o3 = o2 + 3 * B_FDIM
    wi = w_in[l]
    wkr = jnp.zeros((D_MODEL, HEAD_PAD), F32).at[:, A_NOPE:A_QK].set(wi[:, o1:o2])
    w = {
        "norm_g": norm_g[l][None, :],
        "wq": wi[:, :o0].astype(BF16), "wkv": wi[:, o0:o1].astype(BF16), "wkr": wkr.astype(BF16),
        "wb": wi[:, o2:o3].astype(BF16), "wg": wi[:, o3:].astype(BF16),
        "g_cq": g_cq[l][None, :], "g_ckv": g_ckv[l][None, :],
        "w_uq": _head_pad_cols(w_uq[l].reshape(Q_LORA, A_HEADS, A_QK), A_QK).astype(BF16),
        "w_uk": _head_pad_cols(w_uk[l], A_NOPE).astype(BF16),
        "w_uv": w_uv[l].reshape(KV_LORA, A_WIDTH).astype(BF16),
        "g_qn": _lane_gain(g_qn[l]), "g_kn": _lane_gain(g_kn[l]),
        "lb_logits": lb_logits, "g_bn": g_bn[l][None, :],
        "w_oa": w_oa[l].astype(BF16), "w_ob": w_ob[l].astype(BF16), "w_o": w_o[l].astype(BF16),
    }

    xp = x_prompt.reshape(batch * seq, D_MODEL)
    tabs_p = _rope_tables(N_META + jnp.arange(seq))
    lat_p, krp_p, q_p, k_p, v_p, hq, hk, hv, hg = _pre(xp, tabs_p, w, ROW_TILE, seq // ROW_TILE, l)
    tabs_m = _rope_tables(jnp.arange(N_META))
    lat_m, krp_m, _, k_m, v_m, mq, mk, mv, mg = _pre(meta_tokens, tabs_m, w, N_META, 1, l)

    oa = _attn(q_p, k_p, v_p, k_m, v_m, batch, seq)
    zero_state = jnp.zeros((B_HEADS, B_DV, B_DK), F32)
    _, st_meta = _hgrn(mq, mk, mv, mg, zero_state, 1, N_META, N_META, N_META, False)
    ob, st_fin = _hgrn(hq, hk, hv, hg, st_meta[0], batch, seq, HGRN_CHUNK, HGRN_TILE, True)
    y_prompt = _merge(xp, oa, ob, w, ROW_TILE).reshape(batch, seq, D_MODEL)

    lat_all = jnp.concatenate([jnp.broadcast_to(lat_m[None], (batch, N_META, KV_LORA)),
                               lat_p.reshape(batch, seq, KV_LORA)], axis=1)
    kr_all = jnp.concatenate([jnp.broadcast_to(krp_m[None, :, A_NOPE:A_QK], (batch, N_META, A_ROPE)),
                              krp_p[:, A_NOPE:A_QK].reshape(batch, seq, A_ROPE)], axis=1)

    xs = x_sample.reshape(dec_batch, D_MODEL)
    tabs_s = _rope_tables(jnp.full((dec_batch,), past_len, jnp.int32))
    lat_s, krp_s, q_s, k_s, _, sq, sk, sv, sg = _pre(xs, tabs_s, w, dec_batch, 1, l)
    q_s3 = q_s.reshape(dec_batch, A_HEADS, HEAD_PAD)
    k_s3 = k_s.reshape(dec_batch, A_HEADS, HEAD_PAD)
    qcol = q_s3[:, :, :A_NOPE].astype(F32).reshape(dec_batch, A_HEADS * A_NOPE, 1)
    qrope = q_s3[:, :, A_NOPE:A_QK].astype(F32)
    gkc = jnp.tile(g_kn[l][:A_NOPE], A_HEADS)[:, None]
    gkr = g_kn[l][None, A_NOPE:]
    wukt = w_uk[l].reshape(KV_LORA, A_HEADS * A_NOPE).T.astype(BF16)
    oa_s = _paged(page_table, cache_latent[l], cache_krope[l], qcol, qrope, q_s3, k_s3,
                  lat_s[:, None, :], wukt, gkc, gkr, w["w_uv"])
    col = lambda a: a.reshape(dec_batch, B_HEADS, B_DK, 1)
    ob_s, st_s = _hstep(state_hgrn[l], col(sq), col(sk), col(sg), sv.reshape(dec_batch, B_HEADS, 1, B_DV))
    y_sample = _merge(xs, oa_s.reshape(dec_batch, A_WIDTH), ob_s.reshape(dec_batch, B_WIDTH), w, dec_batch)

    return (y_prompt, y_sample.reshape(dec_batch, dec_seq, D_MODEL),
            lat_all[None], kr_all[None], st_fin[None],
            lat_s.reshape(1, dec_batch, dec_seq, KV_LORA),
            krp_s[:, A_NOPE:A_QK].reshape(1, dec_batch, dec_seq, A_ROPE),
            st_s[None])
```

```python
import functools

import jax
import jax.numpy as jnp
from jax import lax
from jax.experimental import pallas as pl
from jax.experimental.pallas import tpu as pltpu

F32 = jnp.float32
BF16 = jnp.bfloat16

D_MODEL = 1024
N_META = 16
A_HEADS = 8
A_NOPE = 64
A_ROPE = 32
A_QK = A_NOPE + A_ROPE
A_V = 64
A_WIDTH = A_HEADS * A_V
Q_LORA = 384
KV_LORA = 256
ROPE_THETA = 10000.0
B_HEADS = 4
B_DK = 128
B_DV = 128
B_FDIM = B_HEADS * B_DK
B_WIDTH = B_HEADS * B_DV
EPS = 1e-6
PAGE = 128

LANES = 128
HEAD_PAD = LANES
VMEM_LIMIT = 56 * 1024 * 1024

ROW_TILE = 256
ATTN_TQ = 256
ATTN_TK = 256
HGRN_CHUNK = 64
HGRN_TILE = 256
PAGES_PER_GROUP = 8
POS_BLOCK = 2 * PAGE


def _full(shape):
    return pl.BlockSpec(shape, lambda *_: (0,) * len(shape))


def _params(n_axes):
    return pltpu.CompilerParams(dimension_semantics=("arbitrary",) * n_axes,
                                vmem_limit_bytes=VMEM_LIMIT)


def _dot(a, b):
    return jnp.dot(a, b, preferred_element_type=F32)


def _dot_nt(a, b, precision=None):
    return lax.dot_general(a, b, (((1,), (1,)), ((), ())), precision=precision,
                           preferred_element_type=F32)


def _rms(x, g):
    r = lax.rsqrt(jnp.mean(x * x, axis=-1, keepdims=True) + EPS)
    return (x * r) * g


def _rope_lanes(t, c, s1, s2):
    return t * c + pltpu.roll(t, LANES - A_ROPE // 2, axis=1) * s1 + pltpu.roll(t, A_ROPE // 2, axis=1) * s2


def _pre_kernel(x_ref, ng_ref, wq_ref, wkv_ref, wkr_ref, wb_ref, gcq_ref, wuq_ref, gckv_ref,
                wuk_ref, wuv_ref, gqn_ref, gkn_ref, c_ref, s1_ref, s2_ref, lb_ref,
                lat_ref, krp_ref, q_ref, k_ref, v_ref, hq_ref, hk_ref, hv_ref, hg_ref, *, layer):
    x = x_ref[...]
    xn = _rms(x, ng_ref[...]).astype(BF16)
    c, s1, s2 = c_ref[...], s1_ref[...], s2_ref[...]

    cq = _dot(xn, wq_ref[...])
    cqn = _rms(cq, gcq_ref[...]).astype(BF16)
    qraw = _dot(cqn, wuq_ref[...])
    gqn = gqn_ref[...]
    for h in range(A_HEADS):
        t = _rope_lanes(qraw[:, h * HEAD_PAD:(h + 1) * HEAD_PAD], c, s1, s2)
        r = lax.rsqrt(jnp.sum(t * t, axis=-1, keepdims=True) * (1.0 / A_QK) + EPS)
        q_ref[:, h * HEAD_PAD:(h + 1) * HEAD_PAD] = ((t * r) * gqn).astype(BF16)

    ckv = _rms(_dot(xn, wkv_ref[...]), gckv_ref[...])
    lat_ref[...] = ckv
    ckvb = ckv.astype(BF16)
    krp = _rope_lanes(_dot(xn, wkr_ref[...]), c, s1, s2)
    krp_ref[...] = krp
    kraw = _dot(ckvb, wuk_ref[...])
    gkn = gkn_ref[...]
    for h in range(A_HEADS):
        t = kraw[:, h * HEAD_PAD:(h + 1) * HEAD_PAD] + krp
        r = lax.rsqrt(jnp.sum(t * t, axis=-1, keepdims=True) * (1.0 / A_QK) + EPS)
        k_ref[:, h * HEAD_PAD:(h + 1) * HEAD_PAD] = ((t * r) * gkn).astype(BF16)
    v_ref[...] = _dot(ckvb, wuv_ref[...]).astype(BF16)

    lbl = lb_ref[...]
    e = jnp.exp(lbl - jnp.max(lbl, axis=0, keepdims=True))
    lb = jnp.sum(e[:layer + 1], axis=0, keepdims=True) / jnp.sum(e, axis=0, keepdims=True)
    b = _dot(xn, wb_ref[...])
    bq, z, bi = b[:, :B_FDIM], b[:, B_FDIM:2 * B_FDIM], b[:, 2 * B_FDIM:]
    hq_ref[...] = bq * jax.nn.sigmoid(bq)
    hg_ref[...] = jnp.log(lb + (1.0 - lb) * jax.nn.sigmoid(z))
    hk_ref[...] = (1.0 - lb) * jax.nn.sigmoid(-z)
    hv_ref[...] = bi


def _pre(x, tabs, w, tm, tab_blocks, layer):
    rows = x.shape[0]
    grid = (rows // tm,)
    row = lambda n: pl.BlockSpec((tm, n), lambda i: (i, 0))
    tab = pl.BlockSpec((tm, LANES), lambda i: (i % tab_blocks, 0))
    ins = [x, w["norm_g"], w["wq"], w["wkv"], w["wkr"], w["wb"], w["g_cq"], w["w_uq"], w["g_ckv"],
           w["w_uk"], w["w_uv"], w["g_qn"], w["g_kn"], tabs[0], tabs[1], tabs[2], w["lb_logits"]]
    in_specs = [row(D_MODEL)] + [_full(a.shape) for a in ins[1:13]] + [tab, tab, tab] + [_full(ins[16].shape)]
    outs = [(KV_LORA, F32), (LANES, F32), (A_HEADS * HEAD_PAD, BF16), (A_HEADS * HEAD_PAD, BF16),
            (A_WIDTH, BF16), (B_FDIM, F32), (B_FDIM, F32), (B_WIDTH, F32), (B_FDIM, F32)]
    return pl.pallas_call(
        functools.partial(_pre_kernel, layer=layer),
        grid=grid, in_specs=in_specs,
        out_specs=[row(n) for n, _ in outs],
        out_shape=[jax.ShapeDtypeStruct((rows, n), d) for n, d in outs],
        compiler_params=_params(1), name="pre_proj")(*ins)


def _attn_kernel(q_ref, k_ref, v_ref, km_ref, vm_ref, o_ref):
    i = pl.program_id(1)
    tq, tk = ATTN_TQ, ATTN_TK
    scale = A_QK ** -0.5
    lane = lax.broadcasted_iota(jnp.int32, (tq, LANES), 1)
    left = lane < A_V
    rows = lax.broadcasted_iota(jnp.int32, (tq, tk), 0)
    cols = lax.broadcasted_iota(jnp.int32, (tq, tk), 1)
    causal = cols <= rows

    for p in range(A_HEADS // 2):
        hs = (2 * p, 2 * p + 1)
        qs = [q_ref[:, h * HEAD_PAD:(h + 1) * HEAD_PAD] for h in hs]
        vsl = slice(p * LANES, (p + 1) * LANES)

        def update(carry, kblk, vblk, mask):
            ms, ls, acc = carry
            new_m, new_l, alphas, pvs = [], [], [], []
            for j in range(2):
                s = _dot_nt(qs[j], kblk[j]) * scale
                if mask is not None:
                    s = jnp.where(mask, s, -jnp.inf)
                m_new = jnp.maximum(ms[j], jnp.max(s, axis=-1, keepdims=True))
                a = jnp.exp(ms[j] - m_new)
                pr = jnp.exp(s - m_new)
                new_m.append(m_new)
                new_l.append(ls[j] * a + jnp.sum(pr, axis=-1, keepdims=True))
                alphas.append(a)
                pvs.append(_dot(pr.astype(BF16), vblk))
            acc = jnp.where(left, alphas[0], alphas[1]) * acc + jnp.where(left, pvs[0], pvs[1])
            return (tuple(new_m), tuple(new_l), acc)

        neg = jnp.full((tq, 1), -jnp.inf, F32)
        zero = jnp.zeros((tq, 1), F32)
        carry = ((neg, neg), (zero, zero), jnp.zeros((tq, LANES), F32))
        carry = update(carry, [km_ref[:, h * HEAD_PAD:(h + 1) * HEAD_PAD] for h in hs], vm_ref[:, vsl], None)

        def body(kb, carry):
            r0 = pl.multiple_of(kb * tk, tk)
            kblk = [k_ref[pl.ds(r0, tk), h * HEAD_PAD:(h + 1) * HEAD_PAD] for h in hs]
            return update(carry, kblk, v_ref[pl.ds(r0, tk), vsl], None)

        carry = lax.fori_loop(0, i, body, carry)
        r0 = pl.multiple_of(i * tk, tk)
        kblk = [k_ref[pl.ds(r0, tk), h * HEAD_PAD:(h + 1) * HEAD_PAD] for h in hs]
        ms, ls, acc = update(carry, kblk, v_ref[pl.ds(r0, tk), vsl], causal)
        o_ref[:, vsl] = acc / jnp.where(left, ls[0], ls[1])


def _attn(q, k, v, km, vm, batch, seq):
    nq = seq // ATTN_TQ
    return pl.pallas_call(
        _attn_kernel, grid=(batch, nq),
        in_specs=[pl.BlockSpec((ATTN_TQ, A_HEADS * HEAD_PAD), lambda b, i: (b * nq + i, 0)),
                  pl.BlockSpec((seq, A_HEADS * HEAD_PAD), lambda b, i: (b, 0)),
                  pl.BlockSpec((seq, A_WIDTH), lambda b, i: (b, 0)),
                  _full(km.shape), _full(vm.shape)],
        out_specs=pl.BlockSpec((ATTN_TQ, A_WIDTH), lambda b, i: (b * nq + i, 0)),
        out_shape=jax.ShapeDtypeStruct((batch * seq, A_WIDTH), F32),
        compiler_params=_params(2), name="prompt_attn")(q, k, v, km, vm)


def _split3(x):
    a = x.astype(BF16)
    r = x - a.astype(F32)
    b = r.astype(BF16)
    c = (r - b.astype(F32)).astype(BF16)
    return a, b, c


def _hgrn_chunk(q, kk, v, g, st, tril, chunk):
    g1, g2, g3 = _split3(g)
    cum = _dot(tril, g1) + _dot(tril, g2) + _dot(tril, g3)
    nblk = chunk // 8
    cb = [cum[8 * j:8 * j + 8] for j in range(nblk)]
    qb = [q[8 * j:8 * j + 8] for j in range(nblk)]
    lane = lax.broadcasted_iota(jnp.int32, (8, chunk), 1)
    sub = lax.broadcasted_iota(jnp.int32, (8, chunk), 0)
    ab = [jnp.zeros((8, chunk), F32) for _ in range(nblk)]
    for s in range(chunk):
        cs = cum[s:s + 1]
        ks = kk[s:s + 1]
        sel = lane == s
        for j in range(s // 8, nblk):
            col = jnp.sum(jnp.exp(cb[j] - cs) * qb[j] * ks, axis=-1, keepdims=True)
            ab[j] = jnp.where(sel, col, ab[j])
    ab = [jnp.where(lane <= sub + 8 * j, ab[j], 0.0) for j in range(nblk)]
    attn = jnp.concatenate(ab, axis=0)
    last = cum[chunk - 1:chunk]
    o = _dot(attn.astype(BF16), v.astype(BF16)) + _dot_nt((q * jnp.exp(cum)).astype(BF16), st.astype(BF16))
    kdec = (kk * jnp.exp(last - cum)).astype(BF16)
    upd = lax.dot_general(v.astype(BF16), kdec, (((0,), (0,)), ((), ())), preferred_element_type=F32)
    return o, jnp.exp(last) * st + upd


def _hgrn_kernel(q_ref, k_ref, v_ref, g_ref, st0_ref, o_ref, st_ref, st_sc, *, chunk, tile, transpose_out):
    t = pl.program_id(1)

    @pl.when(t == 0)
    def _():
        st_sc[...] = st0_ref[...]

    r = lax.broadcasted_iota(jnp.int32, (chunk, chunk), 0)
    c = lax.broadcasted_iota(jnp.int32, (chunk, chunk), 1)
    tril = (c <= r).astype(BF16)

    def body(ci, _):
        r0 = pl.multiple_of(ci * chunk, chunk)
        for h in range(B_HEADS):
            hs = slice(h * B_DK, (h + 1) * B_DK)
            o, st = _hgrn_chunk(q_ref[pl.ds(r0, chunk), hs], k_ref[pl.ds(r0, chunk), hs],
                                v_ref[pl.ds(r0, chunk), hs], g_ref[pl.ds(r0, chunk), hs],
                                st_sc[h], tril, chunk)
            o_ref[pl.ds(r0, chunk), hs] = o
            st_sc[h] = st
        return 0

    lax.fori_loop(0, tile // chunk, body, 0)

    @pl.when(t == pl.num_programs(1) - 1)
    def _():
        for h in range(B_HEADS):
            st_ref[0, h] = st_sc[h].T if transpose_out else st_sc[h]


def _hgrn(hq, hk, hv, hg, st0, batch, seq, chunk, tile, transpose_out):
    nt = seq // tile
    row = pl.BlockSpec((tile, B_FDIM), lambda b, t: (b * nt + t, 0))
    return pl.pallas_call(
        functools.partial(_hgrn_kernel, chunk=chunk, tile=tile, transpose_out=transpose_out),
        grid=(batch, nt),
        in_specs=[row, row, row, row, _full(st0.shape)],
        out_specs=[row, pl.BlockSpec((1, B_HEADS, B_DV, B_DK), lambda b, t: (b, 0, 0, 0))],
        out_shape=[jax.ShapeDtypeStruct((batch * seq, B_WIDTH), F32),
                   jax.ShapeDtypeStruct((batch, B_HEADS, B_DV, B_DK), F32)],
        scratch_shapes=[pltpu.VMEM((B_HEADS, B_DV, B_DK), F32)],
        compiler_params=_params(2), name="hgrn_chunks")(hq, hk, hv, hg, st0)


def _split2(x):
    hi = x.astype(BF16)
    return hi, (x - hi.astype(F32)).astype(BF16)


def _paged_kernel(pt_ref, lat_hbm, kr_hbm, qrow_ref, qr_ref, qh_ref, kh_ref, latn_ref, wukt_ref,
                  gkrow_ref, gkr_ref, wuv_ref, o_ref, lhs, latbuf, krbuf, cbuf, sems, *, layer, n_pages):
    b = pl.program_id(0)
    gp = PAGES_PER_GROUP
    n_groups = n_pages // gp
    gpos = gp * PAGE
    nk = A_HEADS * A_NOPE
    log2e = 1.4426950408889634
    sc2 = (A_QK ** -0.5) * log2e

    def page_copies(g, slot):
        out = []
        for j in range(gp):
            page = pt_ref[b, g * gp + j]
            dst = pl.ds(j * PAGE, PAGE)
            out.append(pltpu.make_async_copy(lat_hbm.at[layer, page], latbuf.at[slot, dst], sems.at[0, slot]))
            out.append(pltpu.make_async_copy(kr_hbm.at[layer, page], krbuf.at[slot, dst], sems.at[1, slot]))
        return out

    def start(g, slot):
        for cp in page_copies(g, slot):
            cp.start()

    def wait(g, slot):
        for cp in page_copies(g, slot):
            cp.wait()

    @pl.when(b == 0)
    def _():
        lhs[0:nk, :] = wukt_ref[...]

    hrow = lax.broadcasted_iota(jnp.int32, (A_HEADS, nk), 0)
    hcol = lax.broadcasted_iota(jnp.int32, (A_HEADS, nk), 1) // A_NOPE
    qsel = jnp.where(hrow == hcol, qrow_ref[0] * gkrow_ref[...], 0.0)
    q_hi, q_lo = _split2(qsel)
    qabs = _dot(q_hi, wukt_ref[...]) + _dot(q_lo, wukt_ref[...])
    lhs[nk:nk + 2 * A_HEADS, :] = jnp.concatenate(_split2(qabs), axis=0)
    qrg = jnp.concatenate(_split2(qr_ref[0] * gkr_ref[...]), axis=0)
    ones = jnp.ones((A_HEADS, A_ROPE), BF16)

    def scores(slot):
        parts = []
        for j in range(gpos // POS_BLOCK):
            rows = pl.ds(j * POS_BLOCK, POS_BLOCK)
            cb = latbuf[slot, rows, :].astype(BF16)
            cbuf[slot, rows, :] = cb
            kx = _dot_nt(lhs[...], cb)
            kn = kx[:nk]
            ss = jnp.sum((kn * kn).reshape(A_HEADS, A_NOPE, POS_BLOCK), axis=1)
            raw = kx[nk:nk + A_HEADS] + kx[nk + A_HEADS:]
            kr = krbuf[slot, rows, :]
            rr = _dot_nt(qrg, kr.astype(BF16))
            raw = raw + rr[:A_HEADS] + rr[A_HEADS:]
            sq_hi, sq_lo = _split2(kr * kr)
            ss = ss + _dot_nt(ones, sq_hi) + _dot_nt(ones, sq_lo)
            parts.append(raw * lax.rsqrt(ss * (1.0 / A_QK) + EPS) * sc2)
        return jnp.concatenate(parts, axis=-1)

    def absorb(s, slot, carry):
        m, l, acc = carry
        m_new = jnp.maximum(m, jnp.max(s, axis=-1, keepdims=True))
        a = jnp.exp2(m - m_new)
        p = jnp.exp2(s - m_new)
        l = l * a + jnp.sum(p, axis=-1, keepdims=True)
        acc = acc * a + _dot(p.astype(BF16), cbuf[slot])
        return m_new, l, acc

    start(0, 0)
    wait(0, 0)
    if n_groups > 1:
        start(1, 1)
    s0 = scores(0)
    init = (jnp.full((A_HEADS, 1), -jnp.inf, F32), jnp.zeros((A_HEADS, 1), F32),
            jnp.zeros((A_HEADS, KV_LORA), F32))

    def group(g, carry):
        s_prev, state = carry
        slot = lax.rem(g, 2)
        wait(g, slot)

        @pl.when(g + 1 < n_groups)
        def _():
            start(g + 1, 1 - slot)

        s_cur = scores(slot)
        return s_cur, absorb(s_prev, 1 - slot, state)

    s_last, state = lax.fori_loop(1, n_groups, group, (s0, init))
    m, l, acc = absorb(s_last, (n_groups - 1) % 2, state)

    s_new = jnp.sum(qh_ref[0].astype(F32) * kh_ref[0].astype(F32), axis=-1, keepdims=True) * sc2
    m_new = jnp.maximum(m, s_new)
    a = jnp.exp2(m - m_new)
    p = jnp.exp2(s_new - m_new)
    acc = acc * a + p * latn_ref[0]
    l = l * a + p
    o_lat = (acc / l).astype(BF16)
    full = _dot(o_lat, wuv_ref[...])
    vrow = lax.broadcasted_iota(jnp.int32, (A_HEADS, A_WIDTH), 0)
    vcol = lax.broadcasted_iota(jnp.int32, (A_HEADS, A_WIDTH), 1) // A_V
    o_ref[0] = jnp.sum(jnp.where(vrow == vcol, full, 0.0), axis=0, keepdims=True)


def _paged(page_table, cache_lat, cache_kr, qrow, qr, qh, kh, latn, wukt, gkrow, gkr, wuv, layer):
    nb, n_pages = page_table.shape
    gp = PAGES_PER_GROUP
    per_b = lambda shape: pl.BlockSpec((1,) + shape, lambda b, pt: (b,) + (0,) * len(shape))
    full = lambda a: pl.BlockSpec(a.shape, lambda b, pt: (0,) * a.ndim)
    grid_spec = pltpu.PrefetchScalarGridSpec(
        num_scalar_prefetch=1, grid=(nb,),
        in_specs=[pl.BlockSpec(memory_space=pl.ANY), pl.BlockSpec(memory_space=pl.ANY),
                  per_b((1, A_HEADS * A_NOPE)), per_b((A_HEADS, A_ROPE)), per_b((A_HEADS, HEAD_PAD)),
                  per_b((A_HEADS, HEAD_PAD)), per_b((1, KV_LORA)), full(wukt), full(gkrow), full(gkr), full(wuv)],
        out_specs=per_b((1, A_WIDTH)),
        scratch_shapes=[pltpu.VMEM((A_HEADS * A_NOPE + 2 * A_HEADS, KV_LORA), BF16),
                        pltpu.VMEM((2, gp * PAGE, KV_LORA), F32), pltpu.VMEM((2, gp * PAGE, A_ROPE), F32),
                        pltpu.VMEM((2, gp * PAGE, KV_LORA), BF16), pltpu.SemaphoreType.DMA((2, 2))])
    return pl.pallas_call(
        functools.partial(_paged_kernel, layer=layer, n_pages=n_pages), grid_spec=grid_spec,
        out_shape=jax.ShapeDtypeStruct((nb, 1, A_WIDTH), F32),
        compiler_params=_params(1), name="paged_attn")(
            page_table, cache_lat, cache_kr, qrow, qr, qh, kh, latn, wukt, gkrow, gkr, wuv)


def _hstep_kernel(s_ref, q_ref, k_ref, g_ref, v_ref, o_ref, sn_ref):
    for h in range(B_HEADS):
        sn = jnp.exp(g_ref[0, h]) * s_ref[0, h] + k_ref[0, h] * v_ref[0, h]
        sn_ref[0, h] = sn
        o_ref[0, h] = jnp.sum(q_ref[0, h] * sn, axis=0, keepdims=True)


def _hstep(state, qc, kc, gc, vr):
    nb = state.shape[0]
    st = pl.BlockSpec((1, B_HEADS, B_DK, B_DV), lambda b: (b, 0, 0, 0))
    col = pl.BlockSpec((1, B_HEADS, B_DK, 1), lambda b: (b, 0, 0, 0))
    rowv = pl.BlockSpec((1, B_HEADS, 1, B_DV), lambda b: (b, 0, 0, 0))
    return pl.pallas_call(
        _hstep_kernel, grid=(nb,), in_specs=[st, col, col, col, rowv], out_specs=[rowv, st],
        out_shape=[jax.ShapeDtypeStruct((nb, B_HEADS, 1, B_DV), F32), jax.ShapeDtypeStruct(state.shape, F32)],
        compiler_params=_params(1), name="hgrn_step")(state, qc, kc, gc, vr)


def _merge_kernel(x_ref, ng_ref, wg_ref, oa_ref, ob_ref, gbn_ref, woa_ref, wob_ref, wo_ref, y_ref):
    x = x_ref[...]
    xn = _rms(x, ng_ref[...]).astype(BF16)
    gates = _dot(xn, wg_ref[...])
    ga, gb = gates[:, :A_WIDTH], gates[:, A_WIDTH:A_WIDTH + B_WIDTH]
    ma = gates[:, A_WIDTH + B_WIDTH:A_WIDTH + B_WIDTH + D_MODEL]
    mb = gates[:, A_WIDTH + B_WIDTH + D_MODEL:]
    ya = _dot((oa_ref[...] * (ga * jax.nn.sigmoid(ga))).astype(BF16), woa_ref[...])
    gbn = gbn_ref[...]
    obn = jnp.concatenate([_rms(ob_ref[:, h * B_DV:(h + 1) * B_DV], gbn) for h in range(B_HEADS)], axis=-1)
    yb = _dot((obn * (gb * jax.nn.sigmoid(gb))).astype(BF16), wob_ref[...])
    mix = jax.nn.sigmoid(ma) * ya + jax.nn.sigmoid(mb) * yb
    y_ref[...] = x + _dot(mix.astype(BF16), wo_ref[...])


def _merge(x, oa, ob, w, tm):
    rows = x.shape[0]
    row = lambda n: pl.BlockSpec((tm, n), lambda i: (i, 0))
    ins = [x, w["norm_g"], w["wg"], oa, ob, w["g_bn"], w["w_oa"], w["w_ob"], w["w_o"]]
    in_specs = [row(D_MODEL), _full(ins[1].shape), _full(ins[2].shape), row(A_WIDTH), row(B_WIDTH)] + \
               [_full(a.shape) for a in ins[5:]]
    return pl.pallas_call(
        _merge_kernel, grid=(rows // tm,), in_specs=in_specs, out_specs=row(D_MODEL),
        out_shape=jax.ShapeDtypeStruct((rows, D_MODEL), F32),
        compiler_params=_params(1), name="merge_out")(*ins)


def _head_pad_cols(w3, width):
    pad = jnp.zeros(w3.shape[:2] + (HEAD_PAD - width,), w3.dtype)
    return jnp.concatenate([w3, pad], axis=-1).reshape(w3.shape[0], -1)


def _rope_tables(pos):
    half = A_ROPE // 2
    inv = ROPE_THETA ** (-jnp.arange(half, dtype=F32) / half)
    ang = pos.astype(F32)[:, None] * inv
    cos, sin = jnp.cos(ang), jnp.sin(ang)
    n = pos.shape[0]
    one, zero = jnp.ones((n, A_NOPE), F32), jnp.zeros((n, A_NOPE), F32)
    tail = jnp.zeros((n, HEAD_PAD - A_QK), F32)
    z16 = jnp.zeros((n, half), F32)
    c = jnp.concatenate([one, cos, cos, tail], axis=-1)
    s1 = jnp.concatenate([zero, -sin, z16, tail], axis=-1)
    s2 = jnp.concatenate([zero, z16, sin, tail], axis=-1)
    return c, s1, s2


def _lane_gain(g):
    return jnp.concatenate([g, jnp.zeros((HEAD_PAD - A_QK,), g.dtype)])[None, :]


def kernel(x_prompt, x_sample, cache_latent, cache_krope, state_hgrn, page_table, meta_tokens,
           norm_g, w_in, g_cq, w_uq, g_ckv, w_uk, w_uv, g_qn, g_kn, lb_logits, g_bn, w_oa, w_ob, w_o):
    batch, seq, _ = x_prompt.shape
    dec_batch, dec_seq, _ = x_sample.shape
    depth = w_in.shape[0]
    assert depth == 1 and dec_seq == 1
    past_len = page_table.shape[1] * PAGE
    l = 0

    o0 = Q_LORA
    o1 = o0 + KV_LORA
    o2 = o1 + A_ROPE
    o3 = o2 + 3 * B_FDIM
    wi = w_in[l]
    wkr = jnp.zeros((D_MODEL, HEAD_PAD), F32).at[:, A_NOPE:A_QK].set(wi[:, o1:o2])
    w = {
        "norm_g": norm_g[l][None, :],
        "wq": wi[:, :o0].astype(BF16), "wkv": wi[:, o0:o1].astype(BF16), "wkr": wkr.astype(BF16),
        "wb": wi[:, o2:o3].astype(BF16), "wg": wi[:, o3:].astype(BF16),
        "g_cq": g_cq[l][None, :], "g_ckv": g_ckv[l][None, :],
        "w_uq": _head_pad_cols(w_uq[l].reshape(Q_LORA, A_HEADS, A_QK), A_QK).astype(BF16),
        "w_uk": _head_pad_cols(w_uk[l], A_NOPE).astype(BF16),
        "w_uv": w_uv[l].reshape(KV_LORA, A_WIDTH).astype(BF16),
        "g_qn": _lane_gain(g_qn[l]), "g_kn": _lane_gain(g_kn[l]),
        "lb_logits": lb_logits, "g_bn": g_bn[l][None, :],
        "w_oa": w_oa[l].astype(BF16), "w_ob": w_ob[l].astype(BF16), "w_o": w_o[l].astype(BF16),
    }

    xp = x_prompt.reshape(batch * seq, D_MODEL)
    tabs_p = _rope_tables(N_META + jnp.arange(seq))
    lat_p, krp_p, q_p, k_p, v_p, hq, hk, hv, hg = _pre(xp, tabs_p, w, ROW_TILE, seq // ROW_TILE, l)
    tabs_m = _rope_tables(jnp.arange(N_META))
    lat_m, krp_m, _, k_m, v_m, mq, mk, mv, mg = _pre(meta_tokens, tabs_m, w, N_META, 1, l)

    oa = _attn(q_p, k_p, v_p, k_m, v_m, batch, seq)
    zero_state = jnp.zeros((B_HEADS, B_DV, B_DK), F32)
    _, st_meta = _hgrn(mq, mk, mv, mg, zero_state, 1, N_META, N_META, N_META, False)
    ob, st_fin = _hgrn(hq, hk, hv, hg, st_meta[0], batch, seq, HGRN_CHUNK, HGRN_TILE, True)
    y_prompt = _merge(xp, oa, ob, w, ROW_TILE).reshape(batch, seq, D_MODEL)

    lat_all = jnp.concatenate([jnp.broadcast_to(lat_m[None], (batch, N_META, KV_LORA)),
                               lat_p.reshape(batch, seq, KV_LORA)], axis=1)
    kr_all = jnp.concatenate([jnp.broadcast_to(krp_m[None, :, A_NOPE:A_QK], (batch, N_META, A_ROPE)),
                              krp_p[:, A_NOPE:A_QK].reshape(batch, seq, A_ROPE)], axis=1)

    xs = x_sample.reshape(dec_batch, D_MODEL)
    tabs_s = _rope_tables(jnp.full((dec_batch,), past_len, jnp.int32))
    lat_s, krp_s, q_s, k_s, _, sq, sk, sv, sg = _pre(xs, tabs_s, w, dec_batch, 1, l)
    q_s3 = q_s.reshape(dec_batch, A_HEADS, HEAD_PAD)
    k_s3 = k_s.reshape(dec_batch, A_HEADS, HEAD_PAD)
    qrow = q_s3[:, :, :A_NOPE].astype(F32).reshape(dec_batch, 1, A_HEADS * A_NOPE)
    qrope = q_s3[:, :, A_NOPE:A_QK].astype(F32)
    gkrow = jnp.tile(g_kn[l][:A_NOPE], A_HEADS)[None, :]
    gkr = g_kn[l][None, A_NOPE:]
    wukt = w_uk[l].reshape(KV_LORA, A_HEADS * A_NOPE).T.astype(BF16)
    oa_s = _paged(page_table, cache_latent, cache_krope, qrow, qrope, q_s3, k_s3,
                  lat_s[:, None, :], wukt, gkrow, gkr, w["w_uv"], l)
    col = lambda a: a.reshape(dec_batch, B_HEADS, B_DK, 1)
    ob_s, st_s = _hstep(state_hgrn[l], col(sq), col(sk), col(sg), sv.reshape(dec_batch, B_HEADS, 1, B_DV))
    y_sample = _merge(xs, oa_s.reshape(dec_batch, A_WIDTH), ob_s.reshape(dec_batch, B_WIDTH), w, dec_batch)

    return (y_prompt, y_sample.reshape(dec_batch, dec_seq, D_MODEL),
            lat_all[None], kr_all[None], st_fin[None],
            lat_s.reshape(1, dec_batch, dec_seq, KV_LORA),
            krp_s[:, A_NOPE:A_QK].reshape(1, dec_batch, dec_seq, A_ROPE),
            st_s[None])
```

```python
import functools

import jax
import jax.numpy as jnp
from jax import lax
from jax.experimental import pallas as pl
from jax.experimental.pallas import tpu as pltpu

F32 = jnp.float32
BF16 = jnp.bfloat16

D_MODEL = 1024
N_META = 16
A_HEADS = 8
A_NOPE = 64
A_ROPE = 32
A_QK = A_NOPE + A_ROPE
A_V = 64
A_WIDTH = A_HEADS * A_V
Q_LORA = 384
KV_LORA = 256
ROPE_THETA = 10000.0
B_HEADS = 4
B_DK = 128
B_DV = 128
B_FDIM = B_HEADS * B_DK
B_WIDTH = B_HEADS * B_DV
EPS = 1e-6
PAGE = 128

LANES = 128
HEAD_PAD = LANES
VMEM_LIMIT = 56 * 1024 * 1024

ROW_TILE = 256
ATTN_TQ = 256
ATTN_TK = 256
HGRN_CHUNK = 64
HGRN_TILE = 256
PAGES_PER_GROUP = 8
POS_BLOCK = 2 * PAGE


def _full(shape):
    return pl.BlockSpec(shape, lambda *_: (0,) * len(shape))


def _params(n_axes):
    return pltpu.CompilerParams(dimension_semantics=("arbitrary",) * n_axes,
                                vmem_limit_bytes=VMEM_LIMIT)


def _dot(a, b):
    return jnp.dot(a, b, preferred_element_type=F32)


def _dot_nt(a, b, precision=None):
    return lax.dot_general(a, b, (((1,), (1,)), ((), ())), precision=precision,
                           preferred_element_type=F32)


def _rms(x, g):
    r = lax.rsqrt(jnp.mean(x * x, axis=-1, keepdims=True) + EPS)
    return (x * r) * g


def _rope_lanes(t, c, s1, s2):
    return t * c + pltpu.roll(t, LANES - A_ROPE // 2, axis=1) * s1 + pltpu.roll(t, A_ROPE // 2, axis=1) * s2


def _pre_kernel(x_ref, ng_ref, wq_ref, wkv_ref, wkr_ref, wb_ref, gcq_ref, wuq_ref, gckv_ref,
                wuk_ref, wuv_ref, gqn_ref, gkn_ref, c_ref, s1_ref, s2_ref, lb_ref,
                lat_ref, krp_ref, q_ref, k_ref, vt_ref, hq_ref, hk_ref, hv_ref, hg_ref, *, layer, qscale):
    x = x_ref[...]
    xn = _rms(x, ng_ref[...]).astype(BF16)
    c, s1, s2 = c_ref[...], s1_ref[...], s2_ref[...]

    cq = _dot(xn, wq_ref[...])
    cqn = _rms(cq, gcq_ref[...]).astype(BF16)
    qraw = _dot(cqn, wuq_ref[...])
    gqn = gqn_ref[...]
    for h in range(A_HEADS):
        t = _rope_lanes(qraw[:, h * HEAD_PAD:(h + 1) * HEAD_PAD], c, s1, s2)
        r = lax.rsqrt(jnp.sum(t * t, axis=-1, keepdims=True) * (1.0 / A_QK) + EPS)
        q_ref[:, h * HEAD_PAD:(h + 1) * HEAD_PAD] = (((t * r) * gqn) * qscale).astype(BF16)

    ckv = _rms(_dot(xn, wkv_ref[...]), gckv_ref[...])
    lat_ref[...] = ckv
    ckvb = ckv.astype(BF16)
    krp = _rope_lanes(_dot(xn, wkr_ref[...]), c, s1, s2)
    krp_ref[...] = krp
    kraw = _dot(ckvb, wuk_ref[...])
    gkn = gkn_ref[...]
    for h in range(A_HEADS):
        t = kraw[:, h * HEAD_PAD:(h + 1) * HEAD_PAD] + krp
        r = lax.rsqrt(jnp.sum(t * t, axis=-1, keepdims=True) * (1.0 / A_QK) + EPS)
        k_ref[:, h * HEAD_PAD:(h + 1) * HEAD_PAD] = ((t * r) * gkn).astype(BF16)
    vt_ref[...] = _dot_nt(wuv_ref[...], ckvb).astype(BF16)

    lbl = lb_ref[...]
    e = jnp.exp(lbl - jnp.max(lbl, axis=0, keepdims=True))
    lb = jnp.sum(e[:layer + 1], axis=0, keepdims=True) / jnp.sum(e, axis=0, keepdims=True)
    b = _dot(xn, wb_ref[...])
    bq, z, bi = b[:, :B_FDIM], b[:, B_FDIM:2 * B_FDIM], b[:, 2 * B_FDIM:]
    hq_ref[...] = bq * jax.nn.sigmoid(bq)
    hg_ref[...] = jnp.log(lb + (1.0 - lb) * jax.nn.sigmoid(z))
    hk_ref[...] = (1.0 - lb) * jax.nn.sigmoid(-z)
    hv_ref[...] = bi


def _pre(x, tabs, w, tm, tab_blocks, layer, qscale=1.0):
    rows = x.shape[0]
    grid = (rows // tm,)
    row = lambda n: pl.BlockSpec((tm, n), lambda i: (i, 0))
    tab = pl.BlockSpec((tm, LANES), lambda i: (i % tab_blocks, 0))
    ins = [x, w["norm_g"], w["wq"], w["wkv"], w["wkr"], w["wb"], w["g_cq"], w["w_uq"], w["g_ckv"],
           w["w_uk"], w["w_uvt"], w["g_qn"], w["g_kn"], tabs[0], tabs[1], tabs[2], w["lb_logits"]]
    in_specs = [row(D_MODEL)] + [_full(a.shape) for a in ins[1:13]] + [tab, tab, tab] + [_full(ins[16].shape)]
    outs = [(KV_LORA, F32), (LANES, F32), (A_HEADS * HEAD_PAD, BF16), (A_HEADS * HEAD_PAD, BF16),
            None, (B_FDIM, F32), (B_FDIM, F32), (B_WIDTH, F32), (B_FDIM, F32)]
    out_specs = [pl.BlockSpec((A_WIDTH, tm), lambda i: (0, i)) if o is None else row(o[0]) for o in outs]
    out_shape = [jax.ShapeDtypeStruct((A_WIDTH, rows), BF16) if o is None else
                 jax.ShapeDtypeStruct((rows, o[0]), o[1]) for o in outs]
    return pl.pallas_call(
        functools.partial(_pre_kernel, layer=layer, qscale=qscale),
        grid=grid, in_specs=in_specs, out_specs=out_specs, out_shape=out_shape,
        compiler_params=_params(1), name="pre_proj")(*ins)


def _attn_kernel(q_ref, k_ref, vt_ref, km_ref, vtm_ref, o_ref, ot_sc, s_sc, m_sc, l_sc):
    i = pl.program_id(1)
    tq, tk = ATTN_TQ, ATTN_TK
    nsub = tq // tk
    key = lax.broadcasted_iota(jnp.int32, (tk, tq), 0)
    qry = lax.broadcasted_iota(jnp.int32, (tk, tq), 1)

    hsl = [slice(h * HEAD_PAD, (h + 1) * HEAD_PAD) for h in range(A_HEADS)]
    vsl = [slice(h * A_V, (h + 1) * A_V) for h in range(A_HEADS)]

    def softmax_pv(h, s, vtblk, first):
        hr = slice(h, h + 1)
        m_blk = jnp.max(s, axis=0, keepdims=True)
        if first:
            m_new = m_blk
        else:
            m_new = jnp.maximum(m_sc[hr, :], m_blk)
            a = jnp.exp2(m_sc[hr, :] - m_new)
        p = jnp.exp2(s - m_new)
        psum = jnp.sum(p, axis=0, keepdims=True)
        pv = _dot(vtblk, p.astype(BF16))
        m_sc[hr, :] = m_new
        if first:
            l_sc[hr, :] = psum
            ot_sc[vsl[h], :] = pv
        else:
            l_sc[hr, :] = l_sc[hr, :] * a + psum
            ot_sc[vsl[h], :] = ot_sc[vsl[h], :] * a + pv

    for h in range(A_HEADS):
        softmax_pv(h, _dot_nt(km_ref[:, hsl[h]], q_ref[:, hsl[h]]), vtm_ref[vsl[h], :], True)

    def block(r0, mask):
        for h in range(A_HEADS):
            s_sc[h] = _dot_nt(k_ref[pl.ds(r0, tk), hsl[h]], q_ref[:, hsl[h]])
        for h in range(A_HEADS):
            s = s_sc[h]
            if mask is not None:
                s = jnp.where(mask, s, -jnp.inf)
            softmax_pv(h, s, vt_ref[vsl[h], pl.ds(r0, tk)], False)

    def body(kb, _):
        block(pl.multiple_of(kb * tk, tk), None)
        return 0

    lax.fori_loop(0, i * nsub, body, 0)
    for j in range(nsub):
        block(pl.multiple_of(i * tq + j * tk, tk), key + j * tk <= qry)
    for h in range(A_HEADS):
        ot_sc[vsl[h], :] = ot_sc[vsl[h], :] / l_sc[h:h + 1, :]
    o_ref[...] = ot_sc[...].T


def _attn(q, k, vt, km, vtm, batch, seq):
    nq = seq // ATTN_TQ
    return pl.pallas_call(
        _attn_kernel, grid=(batch, nq),
        in_specs=[pl.BlockSpec((ATTN_TQ, A_HEADS * HEAD_PAD), lambda b, i: (b * nq + i, 0)),
                  pl.BlockSpec((seq, A_HEADS * HEAD_PAD), lambda b, i: (b, 0)),
                  pl.BlockSpec((A_WIDTH, seq), lambda b, i: (0, b)),
                  _full(km.shape), _full(vtm.shape)],
        out_specs=pl.BlockSpec((ATTN_TQ, A_WIDTH), lambda b, i: (b * nq + i, 0)),
        out_shape=jax.ShapeDtypeStruct((batch * seq, A_WIDTH), F32),
        scratch_shapes=[pltpu.VMEM((A_WIDTH, ATTN_TQ), F32), pltpu.VMEM((A_HEADS, ATTN_TK, ATTN_TQ), F32),
                        pltpu.VMEM((A_HEADS, ATTN_TQ), F32), pltpu.VMEM((A_HEADS, ATTN_TQ), F32)],
        compiler_params=_params(2), name="prompt_attn")(q, k, vt, km, vtm)


def _split3(x):
    a = x.astype(BF16)
    r = x - a.astype(F32)
    b = r.astype(BF16)
    c = (r - b.astype(F32)).astype(BF16)
    return a, b, c


def _hgrn_chunk(q, kk, v, g, st, tril, chunk):
    g1, g2, g3 = _split3(g)
    cum = _dot(tril, g1) + _dot(tril, g2) + _dot(tril, g3)
    nblk = chunk // 8
    cb = [cum[8 * j:8 * j + 8] for j in range(nblk)]
    qb = [q[8 * j:8 * j + 8] for j in range(nblk)]
    lane = lax.broadcasted_iota(jnp.int32, (8, chunk), 1)
    sub = lax.broadcasted_iota(jnp.int32, (8, chunk), 0)
    ab = [jnp.zeros((8, chunk), F32) for _ in range(nblk)]
    for s in range(chunk):
        cs = cum[s:s + 1]
        ks = kk[s:s + 1]
        sel = lane == s
        for j in range(s // 8, nblk):
            col = jnp.sum(jnp.exp(cb[j] - cs) * qb[j] * ks, axis=-1, keepdims=True)
            ab[j] = jnp.where(sel, col, ab[j])
    ab = [jnp.where(lane <= sub + 8 * j, ab[j], 0.0) for j in range(nblk)]
    attn = jnp.concatenate(ab, axis=0)
    last = cum[chunk - 1:chunk]
    o = _dot(attn.astype(BF16), v.astype(BF16)) + _dot_nt((q * jnp.exp(cum)).astype(BF16), st.astype(BF16))
    kdec = (kk * jnp.exp(last - cum)).astype(BF16)
    upd = lax.dot_general(v.astype(BF16), kdec, (((0,), (0,)), ((), ())), preferred_element_type=F32)
    return o, jnp.exp(last) * st + upd


def _hgrn_kernel(q_ref, k_ref, v_ref, g_ref, st0_ref, o_ref, st_ref, st_sc, *, chunk, tile, transpose_out):
    t = pl.program_id(1)

    @pl.when(t == 0)
    def _():
        st_sc[...] = st0_ref[...]

    r = lax.broadcasted_iota(jnp.int32, (chunk, chunk), 0)
    c = lax.broadcasted_iota(jnp.int32, (chunk, chunk), 1)
    tril = (c <= r).astype(BF16)

    def body(ci, _):
        r0 = pl.multiple_of(ci * chunk, chunk)
        for h in range(B_HEADS):
            hs = slice(h * B_DK, (h + 1) * B_DK)
            o, st = _hgrn_chunk(q_ref[pl.ds(r0, chunk), hs], k_ref[pl.ds(r0, chunk), hs],
                                v_ref[pl.ds(r0, chunk), hs], g_ref[pl.ds(r0, chunk), hs],
                                st_sc[h], tril, chunk)
            o_ref[pl.ds(r0, chunk), hs] = o
            st_sc[h] = st
        return 0

    lax.fori_loop(0, tile // chunk, body, 0)

    @pl.when(t == pl.num_programs(1) - 1)
    def _():
        for h in range(B_HEADS):
            st_ref[0, h] = st_sc[h].T if transpose_out else st_sc[h]


def _hgrn(hq, hk, hv, hg, st0, batch, seq, chunk, tile, transpose_out):
    nt = seq // tile
    row = pl.BlockSpec((tile, B_FDIM), lambda b, t: (b * nt + t, 0))
    return pl.pallas_call(
        functools.partial(_hgrn_kernel, chunk=chunk, tile=tile, transpose_out=transpose_out),
        grid=(batch, nt),
        in_specs=[row, row, row, row, _full(st0.shape)],
        out_specs=[row, pl.BlockSpec((1, B_HEADS, B_DV, B_DK), lambda b, t: (b, 0, 0, 0))],
        out_shape=[jax.ShapeDtypeStruct((batch * seq, B_WIDTH), F32),
                   jax.ShapeDtypeStruct((batch, B_HEADS, B_DV, B_DK), F32)],
        scratch_shapes=[pltpu.VMEM((B_HEADS, B_DV, B_DK), F32)],
        compiler_params=_params(2), name="hgrn_chunks")(hq, hk, hv, hg, st0)


def _split2(x):
    hi = x.astype(BF16)
    return hi, (x - hi.astype(F32)).astype(BF16)


def _paged_kernel(pt_ref, lat_hbm, kr_hbm, qrow_ref, qr_ref, qh_ref, kh_ref, latn_ref, wukt_ref,
                  gkrow_ref, gkr_ref, wuv_ref, o_ref, lhs, latbuf, krbuf, cbuf, sems, *, layer, n_pages):
    b = pl.program_id(0)
    gp = PAGES_PER_GROUP
    n_groups = n_pages // gp
    gpos = gp * PAGE
    nk = A_HEADS * A_NOPE
    log2e = 1.4426950408889634
    sc2 = (A_QK ** -0.5) * log2e

    def page_copies(g, slot):
        out = []
        for j in range(gp):
            page = pt_ref[b, g * gp + j]
            dst = pl.ds(j * PAGE, PAGE)
            out.append(pltpu.make_async_copy(lat_hbm.at[layer, page], latbuf.at[slot, dst], sems.at[0, slot]))
            out.append(pltpu.make_async_copy(kr_hbm.at[layer, page], krbuf.at[slot, j], sems.at[1, slot]))
        return out

    def start(g, slot):
        for cp in page_copies(g, slot):
            cp.start()

    def wait(g, slot):
        for cp in page_copies(g, slot):
            cp.wait()

    @pl.when(b == 0)
    def _():
        lhs[0:nk, :] = wukt_ref[...]

    hrow = lax.broadcasted_iota(jnp.int32, (A_HEADS, nk), 0)
    hcol = lax.broadcasted_iota(jnp.int32, (A_HEADS, nk), 1) // A_NOPE
    qsel = jnp.where(hrow == hcol, qrow_ref[0] * gkrow_ref[...], 0.0)
    q_hi, q_lo = _split2(qsel)
    qabs = _dot(q_hi, wukt_ref[...]) + _dot(q_lo, wukt_ref[...])
    lhs[nk:nk + 2 * A_HEADS, :] = jnp.concatenate(_split2(qabs), axis=0)
    qrg = jnp.concatenate(_split2(qr_ref[0] * gkr_ref[...]), axis=0)
    ppb = POS_BLOCK // PAGE

    def scores(slot):
        parts = []
        for j in range(gpos // POS_BLOCK):
            rows = pl.ds(j * POS_BLOCK, POS_BLOCK)
            cb = latbuf[slot, rows, :].astype(BF16)
            cbuf[slot, rows, :] = cb
            kx = _dot_nt(lhs[...], cb)
            kn = kx[:nk]
            ss = jnp.sum((kn * kn).reshape(A_HEADS, A_NOPE, POS_BLOCK), axis=1)
            raw = kx[nk:nk + A_HEADS] + kx[nk + A_HEADS:]
            krt = jnp.concatenate([krbuf[slot, j * ppb + t] for t in range(ppb)], axis=-1)
            rr = _dot(qrg, krt.astype(BF16))
            raw = raw + rr[:A_HEADS] + rr[A_HEADS:]
            ss = ss + jnp.sum(krt * krt, axis=0, keepdims=True)
            parts.append(raw * lax.rsqrt(ss * (1.0 / A_QK) + EPS) * sc2)
        return jnp.concatenate(parts, axis=-1)

    def absorb(s, slot, carry):
        m, l, acc = carry
        m_new = jnp.maximum(m, jnp.max(s, axis=-1, keepdims=True))
        a = jnp.exp2(m - m_new)
        p = jnp.exp2(s - m_new)
        l = l * a + jnp.sum(p, axis=-1, keepdims=True)
        acc = acc * a + _dot(p.astype(BF16), cbuf[slot])
        return m_new, l, acc

    start(0, 0)
    wait(0, 0)
    if n_groups > 1:
        start(1, 1)
    s0 = scores(0)
    init = (jnp.full((A_HEADS, 1), -jnp.inf, F32), jnp.zeros((A_HEADS, 1), F32),
            jnp.zeros((A_HEADS, KV_LORA), F32))

    def group(g, carry):
        s_prev, state = carry
        slot = lax.rem(g, 2)
        wait(g, slot)

        @pl.when(g + 1 < n_groups)
        def _():
            start(g + 1, 1 - slot)

        s_cur = scores(slot)
        return s_cur, absorb(s_prev, 1 - slot, state)

    s_last, state = lax.fori_loop(1, n_groups, group, (s0, init))
    m, l, acc = absorb(s_last, (n_groups - 1) % 2, state)

    s_new = jnp.sum(qh_ref[0].astype(F32) * kh_ref[0].astype(F32), axis=-1, keepdims=True) * sc2
    m_new = jnp.maximum(m, s_new)
    a = jnp.exp2(m - m_new)
    p = jnp.exp2(s_new - m_new)
    acc = acc * a + p * latn_ref[0]
    l = l * a + p
    o_lat = (acc / l).astype(BF16)
    full = _dot(o_lat, wuv_ref[...])
    vrow = lax.broadcasted_iota(jnp.int32, (A_HEADS, A_WIDTH), 0)
    vcol = lax.broadcasted_iota(jnp.int32, (A_HEADS, A_WIDTH), 1) // A_V
    o_ref[0] = jnp.sum(jnp.where(vrow == vcol, full, 0.0), axis=0, keepdims=True)


def _paged(page_table, cache_lat, cache_kr, qrow, qr, qh, kh, latn, wukt, gkrow, gkr, wuv, layer):
    nb, n_pages = page_table.shape
    gp = PAGES_PER_GROUP
    per_b = lambda shape: pl.BlockSpec((1,) + shape, lambda b, pt: (b,) + (0,) * len(shape))
    full = lambda a: pl.BlockSpec(a.shape, lambda b, pt: (0,) * a.ndim)
    grid_spec = pltpu.PrefetchScalarGridSpec(
        num_scalar_prefetch=1, grid=(nb,),
        in_specs=[pl.BlockSpec(memory_space=pl.ANY), pl.BlockSpec(memory_space=pl.ANY),
                  per_b((1, A_HEADS * A_NOPE)), per_b((A_HEADS, A_ROPE)), per_b((A_HEADS, HEAD_PAD)),
                  per_b((A_HEADS, HEAD_PAD)), per_b((1, KV_LORA)), full(wukt), full(gkrow), full(gkr), full(wuv)],
        out_specs=per_b((1, A_WIDTH)),
        scratch_shapes=[pltpu.VMEM((A_HEADS * A_NOPE + 2 * A_HEADS, KV_LORA), BF16),
                        pltpu.VMEM((2, gp * PAGE, KV_LORA), F32), pltpu.VMEM((2, gp, A_ROPE, PAGE), F32),
                        pltpu.VMEM((2, gp * PAGE, KV_LORA), BF16), pltpu.SemaphoreType.DMA((2, 2))])
    return pl.pallas_call(
        functools.partial(_paged_kernel, layer=layer, n_pages=n_pages), grid_spec=grid_spec,
        out_shape=jax.ShapeDtypeStruct((nb, 1, A_WIDTH), F32),
        compiler_params=_params(1), name="paged_attn")(
            page_table, cache_lat, cache_kr, qrow, qr, qh, kh, latn, wukt, gkrow, gkr, wuv)


def _hstep_kernel(s_ref, q_ref, k_ref, g_ref, v_ref, o_ref, sn_ref):
    for h in range(B_HEADS):
        sn = jnp.exp(g_ref[0, h]) * s_ref[0, h] + k_ref[0, h] * v_ref[0, h]
        sn_ref[0, h] = sn
        o_ref[0, h] = jnp.sum(q_ref[0, h] * sn, axis=0, keepdims=True)


def _hstep(state, qc, kc, gc, vr):
    nb = state.shape[0]
    st = pl.BlockSpec((1, B_HEADS, B_DK, B_DV), lambda b: (b, 0, 0, 0))
    col = pl.BlockSpec((1, B_HEADS, B_DK, 1), lambda b: (b, 0, 0, 0))
    rowv = pl.BlockSpec((1, B_HEADS, 1, B_DV), lambda b: (b, 0, 0, 0))
    return pl.pallas_call(
        _hstep_kernel, grid=(nb,), in_specs=[st, col, col, col, rowv], out_specs=[rowv, st],
        out_shape=[jax.ShapeDtypeStruct((nb, B_HEADS, 1, B_DV), F32), jax.ShapeDtypeStruct(state.shape, F32)],
        compiler_params=_params(1), name="hgrn_step")(state, qc, kc, gc, vr)


def _merge_kernel(x_ref, ng_ref, wg_ref, oa_ref, ob_ref, gbn_ref, woa_ref, wob_ref, wo_ref, y_ref):
    x = x_ref[...]
    xn = _rms(x, ng_ref[...]).astype(BF16)
    gates = _dot(xn, wg_ref[...])
    ga, gb = gates[:, :A_WIDTH], gates[:, A_WIDTH:A_WIDTH + B_WIDTH]
    ma = gates[:, A_WIDTH + B_WIDTH:A_WIDTH + B_WIDTH + D_MODEL]
    mb = gates[:, A_WIDTH + B_WIDTH + D_MODEL:]
    ya = _dot((oa_ref[...] * (ga * jax.nn.sigmoid(ga))).astype(BF16), woa_ref[...])
    gbn = gbn_ref[...]
    obn = jnp.concatenate([_rms(ob_ref[:, h * B_DV:(h + 1) * B_DV], gbn) for h in range(B_HEADS)], axis=-1)
    yb = _dot((obn * (gb * jax.nn.sigmoid(gb))).astype(BF16), wob_ref[...])
    mix = jax.nn.sigmoid(ma) * ya + jax.nn.sigmoid(mb) * yb
    y_ref[...] = x + _dot(mix.astype(BF16), wo_ref[...])


def _merge(x, oa, ob, w, tm):
    rows = x.shape[0]
    row = lambda n: pl.BlockSpec((tm, n), lambda i: (i, 0))
    ins = [x, w["norm_g"], w["wg"], oa, ob, w["g_bn"], w["w_oa"], w["w_ob"], w["w_o"]]
    in_specs = [row(D_MODEL), _full(ins[1].shape), _full(ins[2].shape), row(A_WIDTH), row(B_WIDTH)] + \
               [_full(a.shape) for a in ins[5:]]
    return pl.pallas_call(
        _merge_kernel, grid=(rows // tm,), in_specs=in_specs, out_specs=row(D_MODEL),
        out_shape=jax.ShapeDtypeStruct((rows, D_MODEL), F32),
        compiler_params=_params(1), name="merge_out")(*ins)


def _head_pad_cols(w3, width):
    pad = jnp.zeros(w3.shape[:2] + (HEAD_PAD - width,), w3.dtype)
    return jnp.concatenate([w3, pad], axis=-1).reshape(w3.shape[0], -1)


def _rope_tables(pos):
    half = A_ROPE // 2
    inv = ROPE_THETA ** (-jnp.arange(half, dtype=F32) / half)
    ang = pos.astype(F32)[:, None] * inv
    cos, sin = jnp.cos(ang), jnp.sin(ang)
    n = pos.shape[0]
    one, zero = jnp.ones((n, A_NOPE), F32), jnp.zeros((n, A_NOPE), F32)
    tail = jnp.zeros((n, HEAD_PAD - A_QK), F32)
    z16 = jnp.zeros((n, half), F32)
    c = jnp.concatenate([one, cos, cos, tail], axis=-1)
    s1 = jnp.concatenate([zero, -sin, z16, tail], axis=-1)
    s2 = jnp.concatenate([zero, z16, sin, tail], axis=-1)
    return c, s1, s2


def _lane_gain(g):
    return jnp.concatenate([g, jnp.zeros((HEAD_PAD - A_QK,), g.dtype)])[None, :]


def kernel(x_prompt, x_sample, cache_latent, cache_krope, state_hgrn, page_table, meta_tokens,
           norm_g, w_in, g_cq, w_uq, g_ckv, w_uk, w_uv, g_qn, g_kn, lb_logits, g_bn, w_oa, w_ob, w_o):
    batch, seq, _ = x_prompt.shape
    dec_batch, dec_seq, _ = x_sample.shape
    depth = w_in.shape[0]
    assert depth == 1 and dec_seq == 1
    past_len = page_table.shape[1] * PAGE
    l = 0

    o0 = Q_LORA
    o1 = o0 + KV_LORA
    o2 = o1 + A_ROPE
    o3 = o2 + 3 * B_FDIM
    wi = w_in[l]
    wkr = jnp.zeros((D_MODEL, HEAD_PAD), F32).at[:, A_NOPE:A_QK].set(wi[:, o1:o2])
    w = {
        "norm_g": norm_g[l][None, :],
        "wq": wi[:, :o0].astype(BF16), "wkv": wi[:, o0:o1].astype(BF16), "wkr": wkr.astype(BF16),
        "wb": wi[:, o2:o3].astype(BF16), "wg": wi[:, o3:].astype(BF16),
        "g_cq": g_cq[l][None, :], "g_ckv": g_ckv[l][None, :],
        "w_uq": _head_pad_cols(w_uq[l].reshape(Q_LORA, A_HEADS, A_QK), A_QK).astype(BF16),
        "w_uk": _head_pad_cols(w_uk[l], A_NOPE).astype(BF16),
        "w_uv": w_uv[l].reshape(KV_LORA, A_WIDTH).astype(BF16),
        "w_uvt": w_uv[l].reshape(KV_LORA, A_WIDTH).T.astype(BF16),
        "g_qn": _lane_gain(g_qn[l]), "g_kn": _lane_gain(g_kn[l]),
        "lb_logits": lb_logits, "g_bn": g_bn[l][None, :],
        "w_oa": w_oa[l].astype(BF16), "w_ob": w_ob[l].astype(BF16), "w_o": w_o[l].astype(BF16),
    }

    xp = x_prompt.reshape(batch * seq, D_MODEL)
    tabs_p = _rope_tables(N_META + jnp.arange(seq))
    qscale = (A_QK ** -0.5) * 1.4426950408889634
    lat_p, krp_p, q_p, k_p, vt_p, hq, hk, hv, hg = _pre(xp, tabs_p, w, ROW_TILE, seq // ROW_TILE, l, qscale)
    tabs_m = _rope_tables(jnp.arange(N_META))
    lat_m, krp_m, _, k_m, vt_m, mq, mk, mv, mg = _pre(meta_tokens, tabs_m, w, N_META, 1, l)

    oa = _attn(q_p, k_p, vt_p, k_m, vt_m, batch, seq)
    zero_state = jnp.zeros((B_HEADS, B_DV, B_DK), F32)
    _, st_meta = _hgrn(mq, mk, mv, mg, zero_state, 1, N_META, N_META, N_META, False)
    ob, st_fin = _hgrn(hq, hk, hv, hg, st_meta[0], batch, seq, HGRN_CHUNK, HGRN_TILE, True)
    y_prompt = _merge(xp, oa, ob, w, ROW_TILE).reshape(batch, seq, D_MODEL)

    lat_all = jnp.concatenate([jnp.broadcast_to(lat_m[None], (batch, N_META, KV_LORA)),
                               lat_p.reshape(batch, seq, KV_LORA)], axis=1)
    kr_all = jnp.concatenate([jnp.broadcast_to(krp_m[None, :, A_NOPE:A_QK], (batch, N_META, A_ROPE)),
                              krp_p[:, A_NOPE:A_QK].reshape(batch, seq, A_ROPE)], axis=1)

    xs = x_sample.reshape(dec_batch, D_MODEL)
    tabs_s = _rope_tables(jnp.full((dec_batch,), past_len, jnp.int32))
    lat_s, krp_s, q_s, k_s, _, sq, sk, sv, sg = _pre(xs, tabs_s, w, dec_batch, 1, l)
    q_s3 = q_s.reshape(dec_batch, A_HEADS, HEAD_PAD)
    k_s3 = k_s.reshape(dec_batch, A_HEADS, HEAD_PAD)
    qrow = q_s3[:, :, :A_NOPE].astype(F32).reshape(dec_batch, 1, A_HEADS * A_NOPE)
    qrope = q_s3[:, :, A_NOPE:A_QK].astype(F32)
    gkrow = jnp.tile(g_kn[l][:A_NOPE], A_HEADS)[None, :]
    gkr = g_kn[l][None, A_NOPE:]
    wukt = w_uk[l].reshape(KV_LORA, A_HEADS * A_NOPE).T.astype(BF16)
    oa_s = _paged(page_table, cache_latent, jnp.swapaxes(cache_krope, 2, 3), qrow, qrope, q_s3, k_s3,
                  lat_s[:, None, :], wukt, gkrow, gkr, w["w_uv"], l)
    col = lambda a: a.reshape(dec_batch, B_HEADS, B_DK, 1)
    ob_s, st_s = _hstep(state_hgrn[l], col(sq), col(sk), col(sg), sv.reshape(dec_batch, B_HEADS, 1, B_DV))
    y_sample = _merge(xs, oa_s.reshape(dec_batch, A_WIDTH), ob_s.reshape(dec_batch, B_WIDTH), w, dec_batch)

    return (y_prompt, y_sample.reshape(dec_batch, dec_seq, D_MODEL),
            lat_all[None], kr_all[None], st_fin[None],
            lat_s.reshape(1, dec_batch, dec_seq, KV_LORA),
            krp_s[:, A_NOPE:A_QK].reshape(1, dec_batch, dec_seq, A_ROPE),
            st_s[None])
```

```python
import functools

import jax
import jax.numpy as jnp
from jax import lax
from jax.experimental import pallas as pl
from jax.experimental.pallas import tpu as pltpu

F32 = jnp.float32
BF16 = jnp.bfloat16

D_MODEL = 1024
N_META = 16
A_HEADS = 8
A_NOPE = 64
A_ROPE = 32
A_QK = A_NOPE + A_ROPE
A_V = 64
A_WIDTH = A_HEADS * A_V
Q_LORA = 384
KV_LORA = 256
ROPE_THETA = 10000.0
B_HEADS = 4
B_DK = 128
B_DV = 128
B_FDIM = B_HEADS * B_DK
B_WIDTH = B_HEADS * B_DV
EPS = 1e-6
PAGE = 128

LANES = 128
HEAD_PAD = LANES
VMEM_LIMIT = 56 * 1024 * 1024

ROW_TILE = 256
ATTN_TQ = 256
ATTN_TK = 256
HGRN_CHUNK = 64
HGRN_TILE = 256
PAGES_PER_GROUP = 8
POS_BLOCK = 2 * PAGE
DMA_SLOTS = 4


def _full(shape):
    return pl.BlockSpec(shape, lambda *_: (0,) * len(shape))


def _params(n_axes):
    return pltpu.CompilerParams(dimension_semantics=("arbitrary",) * n_axes,
                                vmem_limit_bytes=VMEM_LIMIT)


def _dot(a, b):
    return jnp.dot(a, b, preferred_element_type=F32)


def _dot_nt(a, b, precision=None):
    return lax.dot_general(a, b, (((1,), (1,)), ((), ())), precision=precision,
                           preferred_element_type=F32)


def _rms(x, g):
    r = lax.rsqrt(jnp.mean(x * x, axis=-1, keepdims=True) + EPS)
    return (x * r) * g


def _rope_lanes(t, c, s1, s2):
    return t * c + pltpu.roll(t, LANES - A_ROPE // 2, axis=1) * s1 + pltpu.roll(t, A_ROPE // 2, axis=1) * s2


def _pre_kernel(x_ref, ng_ref, wq_ref, wkv_ref, wkr_ref, wb_ref, gcq_ref, wuq_ref, gckv_ref,
                wuk_ref, wuv_ref, gqn_ref, gkn_ref, c_ref, s1_ref, s2_ref, lb_ref,
                lat_ref, krp_ref, q_ref, k_ref, vt_ref, hq_ref, hk_ref, hv_ref, hg_ref, *, layer, qscale):
    x = x_ref[...]
    xn = _rms(x, ng_ref[...]).astype(BF16)
    c, s1, s2 = c_ref[...], s1_ref[...], s2_ref[...]

    cq = _dot(xn, wq_ref[...])
    cqn = _rms(cq, gcq_ref[...]).astype(BF16)
    qraw = _dot(cqn, wuq_ref[...])
    gqn = gqn_ref[...]
    for h in range(A_HEADS):
        t = _rope_lanes(qraw[:, h * HEAD_PAD:(h + 1) * HEAD_PAD], c, s1, s2)
        r = lax.rsqrt(jnp.sum(t * t, axis=-1, keepdims=True) * (1.0 / A_QK) + EPS)
        q_ref[:, h * HEAD_PAD:(h + 1) * HEAD_PAD] = (((t * r) * gqn) * qscale).astype(BF16)

    ckv = _rms(_dot(xn, wkv_ref[...]), gckv_ref[...])
    lat_ref[...] = ckv
    ckvb = ckv.astype(BF16)
    krp = _rope_lanes(_dot(xn, wkr_ref[...]), c, s1, s2)
    krp_ref[...] = krp
    kraw = _dot(ckvb, wuk_ref[...])
    gkn = gkn_ref[...]
    for h in range(A_HEADS):
        t = kraw[:, h * HEAD_PAD:(h + 1) * HEAD_PAD] + krp
        r = lax.rsqrt(jnp.sum(t * t, axis=-1, keepdims=True) * (1.0 / A_QK) + EPS)
        k_ref[:, h * HEAD_PAD:(h + 1) * HEAD_PAD] = ((t * r) * gkn).astype(BF16)
    vt_ref[...] = _dot_nt(wuv_ref[...], ckvb).astype(BF16)

    lbl = lb_ref[...]
    e = jnp.exp(lbl - jnp.max(lbl, axis=0, keepdims=True))
    lb = jnp.sum(e[:layer + 1], axis=0, keepdims=True) / jnp.sum(e, axis=0, keepdims=True)
    b = _dot(xn, wb_ref[...])
    bq, z, bi = b[:, :B_FDIM], b[:, B_FDIM:2 * B_FDIM], b[:, 2 * B_FDIM:]
    hq_ref[...] = bq * jax.nn.sigmoid(bq)
    hg_ref[...] = jnp.log(lb + (1.0 - lb) * jax.nn.sigmoid(z))
    hk_ref[...] = (1.0 - lb) * jax.nn.sigmoid(-z)
    hv_ref[...] = bi


def _pre(x, tabs, w, tm, tab_blocks, layer, qscale=1.0):
    rows = x.shape[0]
    grid = (rows // tm,)
    row = lambda n: pl.BlockSpec((tm, n), lambda i: (i, 0))
    tab = pl.BlockSpec((tm, LANES), lambda i: (i % tab_blocks, 0))
    ins = [x, w["norm_g"], w["wq"], w["wkv"], w["wkr"], w["wb"], w["g_cq"], w["w_uq"], w["g_ckv"],
           w["w_uk"], w["w_uvt"], w["g_qn"], w["g_kn"], tabs[0], tabs[1], tabs[2], w["lb_logits"]]
    in_specs = [row(D_MODEL)] + [_full(a.shape) for a in ins[1:13]] + [tab, tab, tab] + [_full(ins[16].shape)]
    outs = [(KV_LORA, F32), (LANES, F32), (A_HEADS * HEAD_PAD, BF16), (A_HEADS * HEAD_PAD, BF16),
            None, (B_FDIM, F32), (B_FDIM, F32), (B_WIDTH, F32), (B_FDIM, F32)]
    out_specs = [pl.BlockSpec((A_WIDTH, tm), lambda i: (0, i)) if o is None else row(o[0]) for o in outs]
    out_shape = [jax.ShapeDtypeStruct((A_WIDTH, rows), BF16) if o is None else
                 jax.ShapeDtypeStruct((rows, o[0]), o[1]) for o in outs]
    return pl.pallas_call(
        functools.partial(_pre_kernel, layer=layer, qscale=qscale),
        grid=grid, in_specs=in_specs, out_specs=out_specs, out_shape=out_shape,
        compiler_params=_params(1), name="pre_proj")(*ins)


def _attn_kernel(q_ref, k_ref, vt_ref, km_ref, vtm_ref, o_ref, ot_sc, s_sc, m_sc, l_sc):
    i = pl.program_id(1)
    tq, tk = ATTN_TQ, ATTN_TK
    nsub = tq // tk
    key = lax.broadcasted_iota(jnp.int32, (tk, tq), 0)
    qry = lax.broadcasted_iota(jnp.int32, (tk, tq), 1)

    hsl = [slice(h * HEAD_PAD, (h + 1) * HEAD_PAD) for h in range(A_HEADS)]
    vsl = [slice(h * A_V, (h + 1) * A_V) for h in range(A_HEADS)]

    def softmax_pv(h, s, vtblk, first):
        hr = slice(h, h + 1)
        m_blk = jnp.max(s, axis=0, keepdims=True)
        if first:
            m_new = m_blk
        else:
            m_new = jnp.maximum(m_sc[hr, :], m_blk)
            a = jnp.exp2(m_sc[hr, :] - m_new)
        p = jnp.exp2(s - m_new)
        psum = jnp.sum(p, axis=0, keepdims=True)
        pv = _dot(vtblk, p.astype(BF16))
        m_sc[hr, :] = m_new
        if first:
            l_sc[hr, :] = psum
            ot_sc[vsl[h], :] = pv
        else:
            l_sc[hr, :] = l_sc[hr, :] * a + psum
            ot_sc[vsl[h], :] = ot_sc[vsl[h], :] * a + pv

    for h in range(A_HEADS):
        softmax_pv(h, _dot_nt(km_ref[:, hsl[h]], q_ref[:, hsl[h]]), vtm_ref[vsl[h], :], True)

    def block(r0, mask):
        for h in range(A_HEADS):
            s_sc[h] = _dot_nt(k_ref[pl.ds(r0, tk), hsl[h]], q_ref[:, hsl[h]])
        for h in range(A_HEADS):
            s = s_sc[h]
            if mask is not None:
                s = jnp.where(mask, s, -jnp.inf)
            softmax_pv(h, s, vt_ref[vsl[h], pl.ds(r0, tk)], False)

    def body(kb, _):
        block(pl.multiple_of(kb * tk, tk), None)
        return 0

    lax.fori_loop(0, i * nsub, body, 0)
    for j in range(nsub):
        block(pl.multiple_of(i * tq + j * tk, tk), key + j * tk <= qry)
    for h in range(A_HEADS):
        ot_sc[vsl[h], :] = ot_sc[vsl[h], :] / l_sc[h:h + 1, :]
    o_ref[...] = ot_sc[...].T


def _attn(q, k, vt, km, vtm, batch, seq):
    nq = seq // ATTN_TQ
    return pl.pallas_call(
        _attn_kernel, grid=(batch, nq),
        in_specs=[pl.BlockSpec((ATTN_TQ, A_HEADS * HEAD_PAD), lambda b, i: (b * nq + i, 0)),
                  pl.BlockSpec((seq, A_HEADS * HEAD_PAD), lambda b, i: (b, 0)),
                  pl.BlockSpec((A_WIDTH, seq), lambda b, i: (0, b)),
                  _full(km.shape), _full(vtm.shape)],
        out_specs=pl.BlockSpec((ATTN_TQ, A_WIDTH), lambda b, i: (b * nq + i, 0)),
        out_shape=jax.ShapeDtypeStruct((batch * seq, A_WIDTH), F32),
        scratch_shapes=[pltpu.VMEM((A_WIDTH, ATTN_TQ), F32), pltpu.VMEM((A_HEADS, ATTN_TK, ATTN_TQ), F32),
                        pltpu.VMEM((A_HEADS, ATTN_TQ), F32), pltpu.VMEM((A_HEADS, ATTN_TQ), F32)],
        compiler_params=_params(2), name="prompt_attn")(q, k, vt, km, vtm)


def _split3(x):
    a = x.astype(BF16)
    r = x - a.astype(F32)
    b = r.astype(BF16)
    c = (r - b.astype(F32)).astype(BF16)
    return a, b, c


def _hgrn_chunk(q, kk, v, g, st, tril, chunk):
    g1, g2, g3 = _split3(g)
    cum = _dot(tril, g1) + _dot(tril, g2) + _dot(tril, g3)
    nblk = chunk // 8
    cb = [cum[8 * j:8 * j + 8] for j in range(nblk)]
    qb = [q[8 * j:8 * j + 8] for j in range(nblk)]
    lane = lax.broadcasted_iota(jnp.int32, (8, chunk), 1)
    sub = lax.broadcasted_iota(jnp.int32, (8, chunk), 0)
    ab = [jnp.zeros((8, chunk), F32) for _ in range(nblk)]
    for s in range(chunk):
        cs = cum[s:s + 1]
        ks = kk[s:s + 1]
        sel = lane == s
        for j in range(s // 8, nblk):
            col = jnp.sum(jnp.exp(cb[j] - cs) * qb[j] * ks, axis=-1, keepdims=True)
            ab[j] = jnp.where(sel, col, ab[j])
    ab = [jnp.where(lane <= sub + 8 * j, ab[j], 0.0) for j in range(nblk)]
    attn = jnp.concatenate(ab, axis=0)
    last = cum[chunk - 1:chunk]
    o = _dot(attn.astype(BF16), v.astype(BF16)) + _dot_nt((q * jnp.exp(cum)).astype(BF16), st.astype(BF16))
    kdec = (kk * jnp.exp(last - cum)).astype(BF16)
    upd = lax.dot_general(v.astype(BF16), kdec, (((0,), (0,)), ((), ())), preferred_element_type=F32)
    return o, jnp.exp(last) * st + upd


def _hgrn_kernel(q_ref, k_ref, v_ref, g_ref, st0_ref, o_ref, st_ref, st_sc, *, chunk, tile, transpose_out):
    t = pl.program_id(1)

    @pl.when(t == 0)
    def _():
        st_sc[...] = st0_ref[...]

    r = lax.broadcasted_iota(jnp.int32, (chunk, chunk), 0)
    c = lax.broadcasted_iota(jnp.int32, (chunk, chunk), 1)
    tril = (c <= r).astype(BF16)

    def body(ci, _):
        r0 = pl.multiple_of(ci * chunk, chunk)
        for h in range(B_HEADS):
            hs = slice(h * B_DK, (h + 1) * B_DK)
            o, st = _hgrn_chunk(q_ref[pl.ds(r0, chunk), hs], k_ref[pl.ds(r0, chunk), hs],
                                v_ref[pl.ds(r0, chunk), hs], g_ref[pl.ds(r0, chunk), hs],
                                st_sc[h], tril, chunk)
            o_ref[pl.ds(r0, chunk), hs] = o
            st_sc[h] = st
        return 0

    lax.fori_loop(0, tile // chunk, body, 0)

    @pl.when(t == pl.num_programs(1) - 1)
    def _():
        for h in range(B_HEADS):
            st_ref[0, h] = st_sc[h].T if transpose_out else st_sc[h]


def _hgrn(hq, hk, hv, hg, st0, batch, seq, chunk, tile, transpose_out):
    nt = seq // tile
    row = pl.BlockSpec((tile, B_FDIM), lambda b, t: (b * nt + t, 0))
    return pl.pallas_call(
        functools.partial(_hgrn_kernel, chunk=chunk, tile=tile, transpose_out=transpose_out),
        grid=(batch, nt),
        in_specs=[row, row, row, row, _full(st0.shape)],
        out_specs=[row, pl.BlockSpec((1, B_HEADS, B_DV, B_DK), lambda b, t: (b, 0, 0, 0))],
        out_shape=[jax.ShapeDtypeStruct((batch * seq, B_WIDTH), F32),
                   jax.ShapeDtypeStruct((batch, B_HEADS, B_DV, B_DK), F32)],
        scratch_shapes=[pltpu.VMEM((B_HEADS, B_DV, B_DK), F32)],
        compiler_params=_params(2), name="hgrn_chunks")(hq, hk, hv, hg, st0)


def _split2(x):
    hi = x.astype(BF16)
    return hi, (x - hi.astype(F32)).astype(BF16)


def _paged_kernel(pt_ref, lat_hbm, kr_hbm, qrow_ref, qr_ref, qh_ref, kh_ref, latn_ref, wukt_ref,
                  gkrow_ref, gkr_ref, wuv_ref, o_ref, lhs, latbuf, krbuf, cbuf, sems, *, layer, n_pages):
    b = pl.program_id(0)
    gp = PAGES_PER_GROUP
    n_groups = n_pages // gp
    gpos = gp * PAGE
    nk = A_HEADS * A_NOPE
    log2e = 1.4426950408889634
    sc2 = (A_QK ** -0.5) * log2e

    def page_copies(bb, g, slot):
        out = []
        for j in range(gp):
            page = pt_ref[bb, g * gp + j]
            dst = pl.ds(j * PAGE, PAGE)
            out.append(pltpu.make_async_copy(lat_hbm.at[layer, page], latbuf.at[slot, dst], sems.at[0, slot]))
            out.append(pltpu.make_async_copy(kr_hbm.at[layer, page], krbuf.at[slot, j], sems.at[1, slot]))
        return out

    def start(bb, g, slot):
        for cp in page_copies(bb, g, slot):
            cp.start()

    def wait(g, slot):
        for cp in page_copies(b, g, slot):
            cp.wait()

    ring = DMA_SLOTS
    ahead = ring - 1
    assert n_groups % ring == 0 and ahead <= n_groups

    def fetch_ahead(g):
        gg = g + ahead
        wrap = gg >= n_groups
        bb = jnp.where(wrap, b + 1, b)
        g2 = jnp.where(wrap, gg - n_groups, gg)

        @pl.when(bb < pl.num_programs(0))
        def _():
            start(bb, g2, lax.rem(gg, ring))

    @pl.when(b == 0)
    def _():
        for d in range(ahead):
            start(0, d, d)

    @pl.when(b == 0)
    def _():
        lhs[0:nk, :] = wukt_ref[...]

    hrow = lax.broadcasted_iota(jnp.int32, (A_HEADS, nk), 0)
    hcol = lax.broadcasted_iota(jnp.int32, (A_HEADS, nk), 1) // A_NOPE
    qsel = jnp.where(hrow == hcol, qrow_ref[0] * gkrow_ref[...], 0.0)
    q_hi, q_lo = _split2(qsel)
    qabs = _dot(q_hi, wukt_ref[...]) + _dot(q_lo, wukt_ref[...])
    lhs[nk:nk + 2 * A_HEADS, :] = jnp.concatenate(_split2(qabs), axis=0)
    qrg = jnp.concatenate(_split2(qr_ref[0] * gkr_ref[...]), axis=0)
    ppb = POS_BLOCK // PAGE

    def scores(slot, cslot):
        parts = []
        for j in range(gpos // POS_BLOCK):
            rows = pl.ds(j * POS_BLOCK, POS_BLOCK)
            cb = latbuf[slot, rows, :].astype(BF16)
            cbuf[cslot, rows, :] = cb
            kx = _dot_nt(lhs[...], cb)
            kn = kx[:nk]
            ss = jnp.sum((kn * kn).reshape(A_HEADS, A_NOPE, POS_BLOCK), axis=1)
            raw = kx[nk:nk + A_HEADS] + kx[nk + A_HEADS:]
            krt = jnp.concatenate([krbuf[slot, j * ppb + t] for t in range(ppb)], axis=-1)
            rr = _dot(qrg, krt.astype(BF16))
            raw = raw + rr[:A_HEADS] + rr[A_HEADS:]
            ss = ss + jnp.sum(krt * krt, axis=0, keepdims=True)
            parts.append(raw * lax.rsqrt(ss * (1.0 / A_QK) + EPS) * sc2)
        return jnp.concatenate(parts, axis=-1)

    def absorb(s, slot, carry):
        m, l, acc = carry
        m_new = jnp.maximum(m, jnp.max(s, axis=-1, keepdims=True))
        a = jnp.exp2(m - m_new)
        p = jnp.exp2(s - m_new)
        l = l * a + jnp.sum(p, axis=-1, keepdims=True)
        acc = acc * a + _dot(p.astype(BF16), cbuf[slot])
        return m_new, l, acc

    wait(0, 0)
    fetch_ahead(0)
    s0 = scores(0, 0)
    init = (jnp.full((A_HEADS, 1), -jnp.inf, F32), jnp.zeros((A_HEADS, 1), F32),
            jnp.zeros((A_HEADS, KV_LORA), F32))

    def group(g, carry):
        s_prev, state = carry
        cslot = lax.rem(g, 2)
        wait(g, lax.rem(g, ring))
        fetch_ahead(g)
        s_cur = scores(lax.rem(g, ring), cslot)
        return s_cur, absorb(s_prev, 1 - cslot, state)

    s_last, state = lax.fori_loop(1, n_groups, group, (s0, init))
    m, l, acc = absorb(s_last, (n_groups - 1) % 2, state)

    s_new = jnp.sum(qh_ref[0].astype(F32) * kh_ref[0].astype(F32), axis=-1, keepdims=True) * sc2
    m_new = jnp.maximum(m, s_new)
    a = jnp.exp2(m - m_new)
    p = jnp.exp2(s_new - m_new)
    acc = acc * a + p * latn_ref[0]
    l = l * a + p
    o_lat = (acc / l).astype(BF16)
    full = _dot(o_lat, wuv_ref[...])
    vrow = lax.broadcasted_iota(jnp.int32, (A_HEADS, A_WIDTH), 0)
    vcol = lax.broadcasted_iota(jnp.int32, (A_HEADS, A_WIDTH), 1) // A_V
    o_ref[0] = jnp.sum(jnp.where(vrow == vcol, full, 0.0), axis=0, keepdims=True)


def _paged(page_table, cache_lat, cache_kr, qrow, qr, qh, kh, latn, wukt, gkrow, gkr, wuv, layer):
    nb, n_pages = page_table.shape
    gp = PAGES_PER_GROUP
    per_b = lambda shape: pl.BlockSpec((1,) + shape, lambda b, pt: (b,) + (0,) * len(shape))
    full = lambda a: pl.BlockSpec(a.shape, lambda b, pt: (0,) * a.ndim)
    grid_spec = pltpu.PrefetchScalarGridSpec(
        num_scalar_prefetch=1, grid=(nb,),
        in_specs=[pl.BlockSpec(memory_space=pl.ANY), pl.BlockSpec(memory_space=pl.ANY),
                  per_b((1, A_HEADS * A_NOPE)), per_b((A_HEADS, A_ROPE)), per_b((A_HEADS, HEAD_PAD)),
                  per_b((A_HEADS, HEAD_PAD)), per_b((1, KV_LORA)), full(wukt), full(gkrow), full(gkr), full(wuv)],
        out_specs=per_b((1, A_WIDTH)),
        scratch_shapes=[pltpu.VMEM((A_HEADS * A_NOPE + 2 * A_HEADS, KV_LORA), BF16),
                        pltpu.VMEM((DMA_SLOTS, gp * PAGE, KV_LORA), F32), pltpu.VMEM((DMA_SLOTS, gp, A_ROPE, PAGE), F32),
                        pltpu.VMEM((2, gp * PAGE, KV_LORA), BF16), pltpu.SemaphoreType.DMA((2, DMA_SLOTS))])
    return pl.pallas_call(
        functools.partial(_paged_kernel, layer=layer, n_pages=n_pages), grid_spec=grid_spec,
        out_shape=jax.ShapeDtypeStruct((nb, 1, A_WIDTH), F32),
        compiler_params=_params(1), name="paged_attn")(
            page_table, cache_lat, cache_kr, qrow, qr, qh, kh, latn, wukt, gkrow, gkr, wuv)


def _hstep_kernel(s_ref, q_ref, k_ref, g_ref, v_ref, o_ref, sn_ref):
    for h in range(B_HEADS):
        sn = jnp.exp(g_ref[0, h]) * s_ref[0, h] + k_ref[0, h] * v_ref[0, h]
        sn_ref[0, h] = sn
        o_ref[0, h] = jnp.sum(q_ref[0, h] * sn, axis=0, keepdims=True)


def _hstep(state, qc, kc, gc, vr):
    nb = state.shape[0]
    st = pl.BlockSpec((1, B_HEADS, B_DK, B_DV), lambda b: (b, 0, 0, 0))
    col = pl.BlockSpec((1, B_HEADS, B_DK, 1), lambda b: (b, 0, 0, 0))
    rowv = pl.BlockSpec((1, B_HEADS, 1, B_DV), lambda b: (b, 0, 0, 0))
    return pl.pallas_call(
        _hstep_kernel, grid=(nb,), in_specs=[st, col, col, col, rowv], out_specs=[rowv, st],
        out_shape=[jax.ShapeDtypeStruct((nb, B_HEADS, 1, B_DV), F32), jax.ShapeDtypeStruct(state.shape, F32)],
        compiler_params=_params(1), name="hgrn_step")(state, qc, kc, gc, vr)


def _merge_kernel(x_ref, ng_ref, wg_ref, oa_ref, ob_ref, gbn_ref, woa_ref, wob_ref, wo_ref, y_ref):
    x = x_ref[...]
    xn = _rms(x, ng_ref[...]).astype(BF16)
    gates = _dot(xn, wg_ref[...])
    ga, gb = gates[:, :A_WIDTH], gates[:, A_WIDTH:A_WIDTH + B_WIDTH]
    ma = gates[:, A_WIDTH + B_WIDTH:A_WIDTH + B_WIDTH + D_MODEL]
    mb = gates[:, A_WIDTH + B_WIDTH + D_MODEL:]
    ya = _dot((oa_ref[...] * (ga * jax.nn.sigmoid(ga))).astype(BF16), woa_ref[...])
    gbn = gbn_ref[...]
    obn = jnp.concatenate([_rms(ob_ref[:, h * B_DV:(h + 1) * B_DV], gbn) for h in range(B_HEADS)], axis=-1)
    yb = _dot((obn * (gb * jax.nn.sigmoid(gb))).astype(BF16), wob_ref[...])
    mix = jax.nn.sigmoid(ma) * ya + jax.nn.sigmoid(mb) * yb
    y_ref[...] = x + _dot(mix.astype(BF16), wo_ref[...])


def _merge(x, oa, ob, w, tm):
    rows = x.shape[0]
    row = lambda n: pl.BlockSpec((tm, n), lambda i: (i, 0))
    ins = [x, w["norm_g"], w["wg"], oa, ob, w["g_bn"], w["w_oa"], w["w_ob"], w["w_o"]]
    in_specs = [row(D_MODEL), _full(ins[1].shape), _full(ins[2].shape), row(A_WIDTH), row(B_WIDTH)] + \
               [_full(a.shape) for a in ins[5:]]
    return pl.pallas_call(
        _merge_kernel, grid=(rows // tm,), in_specs=in_specs, out_specs=row(D_MODEL),
        out_shape=jax.ShapeDtypeStruct((rows, D_MODEL), F32),
        compiler_params=_params(1), name="merge_out")(*ins)


def _head_pad_cols(w3, width):
    pad = jnp.zeros(w3.shape[:2] + (HEAD_PAD - width,), w3.dtype)
    return jnp.concatenate([w3, pad], axis=-1).reshape(w3.shape[0], -1)


def _rope_tables(pos):
    half = A_ROPE // 2
    inv = ROPE_THETA ** (-jnp.arange(half, dtype=F32) / half)
    ang = pos.astype(F32)[:, None] * inv
    cos, sin = jnp.cos(ang), jnp.sin(ang)
    n = pos.shape[0]
    one, zero = jnp.ones((n, A_NOPE), F32), jnp.zeros((n, A_NOPE), F32)
    tail = jnp.zeros((n, HEAD_PAD - A_QK), F32)
    z16 = jnp.zeros((n, half), F32)
    c = jnp.concatenate([one, cos, cos, tail], axis=-1)
    s1 = jnp.concatenate([zero, -sin, z16, tail], axis=-1)
    s2 = jnp.concatenate([zero, z16, sin, tail], axis=-1)
    return c, s1, s2


def _lane_gain(g):
    return jnp.concatenate([g, jnp.zeros((HEAD_PAD - A_QK,), g.dtype)])[None, :]


def kernel(x_prompt, x_sample, cache_latent, cache_krope, state_hgrn, page_table, meta_tokens,
           norm_g, w_in, g_cq, w_uq, g_ckv, w_uk, w_uv, g_qn, g_kn, lb_logits, g_bn, w_oa, w_ob, w_o):
    batch, seq, _ = x_prompt.shape
    dec_batch, dec_seq, _ = x_sample.shape
    depth = w_in.shape[0]
    assert depth == 1 and dec_seq == 1
    past_len = page_table.shape[1] * PAGE
    l = 0

    o0 = Q_LORA
    o1 = o0 + KV_LORA
    o2 = o1 + A_ROPE
    o3 = o2 + 3 * B_FDIM
    wi = w_in[l]
    wkr = jnp.zeros((D_MODEL, HEAD_PAD), F32).at[:, A_NOPE:A_QK].set(wi[:, o1:o2])
    w = {
        "norm_g": norm_g[l][None, :],
        "wq": wi[:, :o0].astype(BF16), "wkv": wi[:, o0:o1].astype(BF16), "wkr": wkr.astype(BF16),
        "wb": wi[:, o2:o3].astype(BF16), "wg": wi[:, o3:].astype(BF16),
        "g_cq": g_cq[l][None, :], "g_ckv": g_ckv[l][None, :],
        "w_uq": _head_pad_cols(w_uq[l].reshape(Q_LORA, A_HEADS, A_QK), A_QK).astype(BF16),
        "w_uk": _head_pad_cols(w_uk[l], A_NOPE).astype(BF16),
        "w_uv": w_uv[l].reshape(KV_LORA, A_WIDTH).astype(BF16),
        "w_uvt": w_uv[l].reshape(KV_LORA, A_WIDTH).T.astype(BF16),
        "g_qn": _lane_gain(g_qn[l]), "g_kn": _lane_gain(g_kn[l]),
        "lb_logits": lb_logits, "g_bn": g_bn[l][None, :],
        "w_oa": w_oa[l].astype(BF16), "w_ob": w_ob[l].astype(BF16), "w_o": w_o[l].astype(BF16),
    }

    xp = x_prompt.reshape(batch * seq, D_MODEL)
    tabs_p = _rope_tables(N_META + jnp.arange(seq))
    qscale = (A_QK ** -0.5) * 1.4426950408889634
    lat_p, krp_p, q_p, k_p, vt_p, hq, hk, hv, hg = _pre(xp, tabs_p, w, ROW_TILE, seq // ROW_TILE, l, qscale)
    tabs_m = _rope_tables(jnp.arange(N_META))
    lat_m, krp_m, _, k_m, vt_m, mq, mk, mv, mg = _pre(meta_tokens, tabs_m, w, N_META, 1, l)

    oa = _attn(q_p, k_p, vt_p, k_m, vt_m, batch, seq)
    zero_state = jnp.zeros((B_HEADS, B_DV, B_DK), F32)
    _, st_meta = _hgrn(mq, mk, mv, mg, zero_state, 1, N_META, N_META, N_META, False)
    ob, st_fin = _hgrn(hq, hk, hv, hg, st_meta[0], batch, seq, HGRN_CHUNK, HGRN_TILE, True)
    y_prompt = _merge(xp, oa, ob, w, ROW_TILE).reshape(batch, seq, D_MODEL)

    lat_all = jnp.concatenate([jnp.broadcast_to(lat_m[None], (batch, N_META, KV_LORA)),
                               lat_p.reshape(batch, seq, KV_LORA)], axis=1)
    kr_all = jnp.concatenate([jnp.broadcast_to(krp_m[None, :, A_NOPE:A_QK], (batch, N_META, A_ROPE)),
                              krp_p[:, A_NOPE:A_QK].reshape(batch, seq, A_ROPE)], axis=1)

    xs = x_sample.reshape(dec_batch, D_MODEL)
    tabs_s = _rope_tables(jnp.full((dec_batch,), past_len, jnp.int32))
    lat_s, krp_s, q_s, k_s, _, sq, sk, sv, sg = _pre(xs, tabs_s, w, dec_batch, 1, l)
    q_s3 = q_s.reshape(dec_batch, A_HEADS, HEAD_PAD)
    k_s3 = k_s.reshape(dec_batch, A_HEADS, HEAD_PAD)
    qrow = q_s3[:, :, :A_NOPE].astype(F32).reshape(dec_batch, 1, A_HEADS * A_NOPE)
    qrope = q_s3[:, :, A_NOPE:A_QK].astype(F32)
    gkrow = jnp.tile(g_kn[l][:A_NOPE], A_HEADS)[None, :]
    gkr = g_kn[l][None, A_NOPE:]
    wukt = w_uk[l].reshape(KV_LORA, A_HEADS * A_NOPE).T.astype(BF16)
    oa_s = _paged(page_table, cache_latent, jnp.swapaxes(cache_krope, 2, 3), qrow, qrope, q_s3, k_s3,
                  lat_s[:, None, :], wukt, gkrow, gkr, w["w_uv"], l)
    col = lambda a: a.reshape(dec_batch, B_HEADS, B_DK, 1)
    ob_s, st_s = _hstep(state_hgrn[l], col(sq), col(sk), col(sg), sv.reshape(dec_batch, B_HEADS, 1, B_DV))
    y_sample = _merge(xs, oa_s.reshape(dec_batch, A_WIDTH), ob_s.reshape(dec_batch, B_WIDTH), w, dec_batch)

    return (y_prompt, y_sample.reshape(dec_batch, dec_seq, D_MODEL),
            lat_all[None], kr_all[None], st_fin[None],
            lat_s.reshape(1, dec_batch, dec_seq, KV_LORA),
            krp_s[:, A_NOPE:A_QK].reshape(1, dec_batch, dec_seq, A_ROPE),
            st_s[None])
```

```python
import functools

import jax
import jax.numpy as jnp
from jax import lax
from jax.experimental import pallas as pl
from jax.experimental.pallas import tpu as pltpu

F32 = jnp.float32
BF16 = jnp.bfloat16

D_MODEL = 1024
N_META = 16
A_HEADS = 8
A_NOPE = 64
A_ROPE = 32
A_QK = A_NOPE + A_ROPE
A_V = 64
A_WIDTH = A_HEADS * A_V
Q_LORA = 384
KV_LORA = 256
ROPE_THETA = 10000.0
B_HEADS = 4
B_DK = 128
B_DV = 128
B_FDIM = B_HEADS * B_DK
B_WIDTH = B_HEADS * B_DV
EPS = 1e-6
PAGE = 128

LANES = 128
LOG2E = 1.4426950408889634
HEAD_PAD = LANES
VMEM_LIMIT = 56 * 1024 * 1024

ROW_TILE = 256
ATTN_TQ = 256
ATTN_TK = 256
HGRN_CHUNK = 64
HGRN_TILE = 256
PAGES_PER_GROUP = 8
POS_BLOCK = 2 * PAGE
DMA_SLOTS = 4


def _full(shape):
    return pl.BlockSpec(shape, lambda *_: (0,) * len(shape))


def _params(n_axes):
    return pltpu.CompilerParams(dimension_semantics=("arbitrary",) * n_axes,
                                vmem_limit_bytes=VMEM_LIMIT)


def _dot(a, b):
    return jnp.dot(a, b, preferred_element_type=F32)


def _dot_nt(a, b, precision=None):
    return lax.dot_general(a, b, (((1,), (1,)), ((), ())), precision=precision,
                           preferred_element_type=F32)


def _rms(x, g):
    r = lax.rsqrt(jnp.mean(x * x, axis=-1, keepdims=True) + EPS)
    return (x * r) * g


def _rope_lanes(t, c, s1, s2):
    return t * c + pltpu.roll(t, LANES - A_ROPE // 2, axis=1) * s1 + pltpu.roll(t, A_ROPE // 2, axis=1) * s2


def _pre_kernel(x_ref, ng_ref, wq_ref, wkv_ref, wkr_ref, wb_ref, gcq_ref, wuq_ref, gckv_ref,
                wuk_ref, wuv_ref, gqn_ref, gkn_ref, c_ref, s1_ref, s2_ref, lb_ref,
                lat_ref, krp_ref, q_ref, k_ref, vt_ref, hq_ref, hk_ref, hv_ref, hg_ref, *, layer, qscale):
    x = x_ref[...]
    xn = _rms(x, ng_ref[...]).astype(BF16)
    c, s1, s2 = c_ref[...], s1_ref[...], s2_ref[...]

    cq = _dot(xn, wq_ref[...])
    cqn = _rms(cq, gcq_ref[...]).astype(BF16)
    qraw = _dot(cqn, wuq_ref[...])
    gqn = gqn_ref[...]
    for h in range(A_HEADS):
        t = _rope_lanes(qraw[:, h * HEAD_PAD:(h + 1) * HEAD_PAD], c, s1, s2)
        r = lax.rsqrt(jnp.sum(t * t, axis=-1, keepdims=True) * (1.0 / A_QK) + EPS)
        q_ref[:, h * HEAD_PAD:(h + 1) * HEAD_PAD] = (((t * r) * gqn) * qscale).astype(BF16)

    ckv = _rms(_dot(xn, wkv_ref[...]), gckv_ref[...])
    lat_ref[...] = ckv
    ckvb = ckv.astype(BF16)
    krp = _rope_lanes(_dot(xn, wkr_ref[...]), c, s1, s2)
    krp_ref[...] = krp
    kraw = _dot(ckvb, wuk_ref[...])
    gkn = gkn_ref[...]
    for h in range(A_HEADS):
        t = kraw[:, h * HEAD_PAD:(h + 1) * HEAD_PAD] + krp
        r = lax.rsqrt(jnp.sum(t * t, axis=-1, keepdims=True) * (1.0 / A_QK) + EPS)
        k_ref[:, h * HEAD_PAD:(h + 1) * HEAD_PAD] = ((t * r) * gkn).astype(BF16)
    vt_ref[...] = _dot_nt(wuv_ref[...], ckvb).astype(BF16)

    lbl = lb_ref[...]
    e = jnp.exp(lbl - jnp.max(lbl, axis=0, keepdims=True))
    lb = jnp.sum(e[:layer + 1], axis=0, keepdims=True) / jnp.sum(e, axis=0, keepdims=True)
    b = _dot(xn, wb_ref[...])
    bq, z, bi = b[:, :B_FDIM], b[:, B_FDIM:2 * B_FDIM], b[:, 2 * B_FDIM:]
    hq_ref[...] = bq * jax.nn.sigmoid(bq)
    hg_ref[...] = jnp.log(lb + (1.0 - lb) * jax.nn.sigmoid(z))
    hk_ref[...] = (1.0 - lb) * jax.nn.sigmoid(-z)
    hv_ref[...] = bi


def _pre(x, tabs, w, tm, tab_blocks, layer, qscale=1.0):
    rows = x.shape[0]
    grid = (rows // tm,)
    row = lambda n: pl.BlockSpec((tm, n), lambda i: (i, 0))
    tab = pl.BlockSpec((tm, LANES), lambda i: (i % tab_blocks, 0))
    ins = [x, w["norm_g"], w["wq"], w["wkv"], w["wkr"], w["wb"], w["g_cq"], w["w_uq"], w["g_ckv"],
           w["w_uk"], w["w_uvt"], w["g_qn"], w["g_kn"], tabs[0], tabs[1], tabs[2], w["lb_logits"]]
    in_specs = [row(D_MODEL)] + [_full(a.shape) for a in ins[1:13]] + [tab, tab, tab] + [_full(ins[16].shape)]
    outs = [(KV_LORA, F32), (LANES, F32), (A_HEADS * HEAD_PAD, BF16), (A_HEADS * HEAD_PAD, BF16),
            None, (B_FDIM, F32), (B_FDIM, F32), (B_WIDTH, F32), (B_FDIM, F32)]
    out_specs = [pl.BlockSpec((A_WIDTH, tm), lambda i: (0, i)) if o is None else row(o[0]) for o in outs]
    out_shape = [jax.ShapeDtypeStruct((A_WIDTH, rows), BF16) if o is None else
                 jax.ShapeDtypeStruct((rows, o[0]), o[1]) for o in outs]
    return pl.pallas_call(
        functools.partial(_pre_kernel, layer=layer, qscale=qscale),
        grid=grid, in_specs=in_specs, out_specs=out_specs, out_shape=out_shape,
        compiler_params=_params(1), name="pre_proj")(*ins)


def _attn_kernel(q_ref, k_ref, vt_ref, km_ref, vtm_ref, o_ref, ot_sc, s_sc, m_sc, l_sc):
    i = pl.program_id(1)
    tq, tk = ATTN_TQ, ATTN_TK
    nsub = tq // tk
    key = lax.broadcasted_iota(jnp.int32, (tk, tq), 0)
    qry = lax.broadcasted_iota(jnp.int32, (tk, tq), 1)

    hsl = [slice(h * HEAD_PAD, (h + 1) * HEAD_PAD) for h in range(A_HEADS)]
    vsl = [slice(h * A_V, (h + 1) * A_V) for h in range(A_HEADS)]

    def softmax_pv(h, s, vtblk, first):
        hr = slice(h, h + 1)
        m_blk = jnp.max(s, axis=0, keepdims=True)
        if first:
            m_new = m_blk
        else:
            m_new = jnp.maximum(m_sc[hr, :], m_blk)
            a = jnp.exp2(m_sc[hr, :] - m_new)
        p = jnp.exp2(s - m_new)
        psum = jnp.sum(p, axis=0, keepdims=True)
        pv = _dot(vtblk, p.astype(BF16))
        m_sc[hr, :] = m_new
        if first:
            l_sc[hr, :] = psum
            ot_sc[vsl[h], :] = pv
        else:
            l_sc[hr, :] = l_sc[hr, :] * a + psum
            ot_sc[vsl[h], :] = ot_sc[vsl[h], :] * a + pv

    nm = km_ref.shape[0]
    for h in range(A_HEADS):
        s_sc[h, 0:nm, :] = _dot_nt(km_ref[:, hsl[h]], q_ref[:, hsl[h]])
    for h in range(A_HEADS):
        softmax_pv(h, s_sc[h, 0:nm, :], vtm_ref[vsl[h], :], True)

    def block(r0, mask):
        for h in range(A_HEADS):
            s_sc[h] = _dot_nt(k_ref[pl.ds(r0, tk), hsl[h]], q_ref[:, hsl[h]])
        for h in range(A_HEADS):
            s = s_sc[h]
            if mask is not None:
                s = jnp.where(mask, s, -jnp.inf)
            softmax_pv(h, s, vt_ref[vsl[h], pl.ds(r0, tk)], False)

    def body(kb, _):
        block(pl.multiple_of(kb * tk, tk), None)
        return 0

    lax.fori_loop(0, i * nsub, body, 0)
    for j in range(nsub):
        block(pl.multiple_of(i * tq + j * tk, tk), key + j * tk <= qry)
    for h in range(A_HEADS):
        ot_sc[vsl[h], :] = ot_sc[vsl[h], :] / l_sc[h:h + 1, :]
    o_ref[...] = ot_sc[...].T


def _attn(q, k, vt, km, vtm, batch, seq):
    nq = seq // ATTN_TQ
    return pl.pallas_call(
        _attn_kernel, grid=(batch, nq),
        in_specs=[pl.BlockSpec((ATTN_TQ, A_HEADS * HEAD_PAD), lambda b, i: (b * nq + i, 0)),
                  pl.BlockSpec((seq, A_HEADS * HEAD_PAD), lambda b, i: (b, 0)),
                  pl.BlockSpec((A_WIDTH, seq), lambda b, i: (0, b)),
                  _full(km.shape), _full(vtm.shape)],
        out_specs=pl.BlockSpec((ATTN_TQ, A_WIDTH), lambda b, i: (b * nq + i, 0)),
        out_shape=jax.ShapeDtypeStruct((batch * seq, A_WIDTH), F32),
        scratch_shapes=[pltpu.VMEM((A_WIDTH, ATTN_TQ), F32), pltpu.VMEM((A_HEADS, ATTN_TK, ATTN_TQ), F32),
                        pltpu.VMEM((A_HEADS, ATTN_TQ), F32), pltpu.VMEM((A_HEADS, ATTN_TQ), F32)],
        compiler_params=_params(2), name="prompt_attn")(q, k, vt, km, vtm)


def _split3(x):
    a = x.astype(BF16)
    r = x - a.astype(F32)
    b = r.astype(BF16)
    c = (r - b.astype(F32)).astype(BF16)
    return a, b, c


def _hgrn_chunk(q, kk, v, g, st, tril, chunk, cum_ref, krow):
    g1, g2, g3 = _split3(g)
    cum = (_dot(tril, g1) + _dot(tril, g2) + _dot(tril, g3)) * LOG2E
    cum_ref[...] = cum
    nblk = chunk // 8
    cb = [cum[8 * j:8 * j + 8] for j in range(nblk)]
    qb = [q[8 * j:8 * j + 8] for j in range(nblk)]
    lane = lax.broadcasted_iota(jnp.int32, (8, chunk), 1)
    sub = lax.broadcasted_iota(jnp.int32, (8, chunk), 0)
    ab = []
    for j in range(nblk):
        a = jnp.zeros((8, chunk), F32)
        for s in range(8 * j, 8 * j + 8):
            col = jnp.sum(jnp.exp2(cb[j] - cum_ref[s:s + 1, :]) * qb[j] * krow(s), axis=-1, keepdims=True)
            a = jnp.where(lane == s, col, a)
        ab.append(jnp.where(lane <= sub + 8 * j, a, 0.0))
    attn = jnp.concatenate(ab, axis=0)
    row = lax.broadcasted_iota(jnp.int32, (chunk, chunk), 0)
    col = lax.broadcasted_iota(jnp.int32, (chunk, chunk), 1)
    dead = jnp.full((8, B_DK), -jnp.inf, F32)
    w = 8
    while w < chunk:
        eq, ek = [], []
        for j in range(nblk):
            blk = (8 * j) // w
            if blk % 2:
                eq.append(cb[j] - cum_ref[blk * w - 1:blk * w, :])
                ek.append(dead)
            else:
                eq.append(dead)
                ek.append(cum_ref[blk * w + w - 1:blk * w + w, :] - cb[j])
        qa = (q * jnp.exp2(jnp.concatenate(eq, axis=0))).astype(BF16)
        ka = (kk * jnp.exp2(jnp.concatenate(ek, axis=0))).astype(BF16)
        sh = w.bit_length() - 1
        rb, cbk = row >> sh, col >> sh
        attn = attn + jnp.where((rb == cbk + 1) & ((cbk & 1) == 0), _dot_nt(qa, ka), 0.0)
        w *= 2
    last = cum_ref[chunk - 1:chunk, :]
    o = _dot(attn.astype(BF16), v.astype(BF16)) + _dot_nt((q * jnp.exp2(cum)).astype(BF16), st.astype(BF16))
    kdec = (kk * jnp.exp2(last - cum)).astype(BF16)
    upd = lax.dot_general(v.astype(BF16), kdec, (((0,), (0,)), ((), ())), preferred_element_type=F32)
    return o, jnp.exp2(last) * st + upd


def _hgrn_kernel(q_ref, k_ref, v_ref, g_ref, st0_ref, o_ref, st_ref, st_sc, cum_sc, kk_sc, *, chunk, tile,
                 transpose_out):
    t = pl.program_id(1)

    @pl.when(t == 0)
    def _():
        st_sc[...] = st0_ref[...]

    r = lax.broadcasted_iota(jnp.int32, (chunk, chunk), 0)
    c = lax.broadcasted_iota(jnp.int32, (chunk, chunk), 1)
    tril = (c <= r).astype(BF16)

    def body(ci, _):
        r0 = pl.multiple_of(ci * chunk, chunk)
        for h in range(B_HEADS):
            hs = slice(h * B_DK, (h + 1) * B_DK)
            kk = k_ref[pl.ds(r0, chunk), hs]
            kk_sc[h] = kk
            krow = lambda s, h=h: kk_sc[h, s:s + 1, :]
            o, st = _hgrn_chunk(q_ref[pl.ds(r0, chunk), hs], kk,
                                v_ref[pl.ds(r0, chunk), hs], g_ref[pl.ds(r0, chunk), hs],
                                st_sc[h], tril, chunk, cum_sc.at[h], krow)
            o_ref[pl.ds(r0, chunk), hs] = o
            st_sc[h] = st
        return 0

    lax.fori_loop(0, tile // chunk, body, 0)

    @pl.when(t == pl.num_programs(1) - 1)
    def _():
        for h in range(B_HEADS):
            st_ref[0, h] = st_sc[h].T if transpose_out else st_sc[h]


def _hgrn(hq, hk, hv, hg, st0, batch, seq, chunk, tile, transpose_out):
    nt = seq // tile
    row = pl.BlockSpec((tile, B_FDIM), lambda b, t: (b * nt + t, 0))
    return pl.pallas_call(
        functools.partial(_hgrn_kernel, chunk=chunk, tile=tile, transpose_out=transpose_out),
        grid=(batch, nt),
        in_specs=[row, row, row, row, _full(st0.shape)],
        out_specs=[row, pl.BlockSpec((1, B_HEADS, B_DV, B_DK), lambda b, t: (b, 0, 0, 0))],
        out_shape=[jax.ShapeDtypeStruct((batch * seq, B_WIDTH), F32),
                   jax.ShapeDtypeStruct((batch, B_HEADS, B_DV, B_DK), F32)],
        scratch_shapes=[pltpu.VMEM((B_HEADS, B_DV, B_DK), F32), pltpu.VMEM((B_HEADS, chunk, B_DK), F32),
                        pltpu.VMEM((B_HEADS, chunk, B_DK), F32)],
        compiler_params=_params(2), name="hgrn_chunks")(hq, hk, hv, hg, st0)


def _split2(x):
    hi = x.astype(BF16)
    return hi, (x - hi.astype(F32)).astype(BF16)


def _paged_kernel(pt_ref, lat_hbm, kr_hbm, qrow_ref, qr_ref, qh_ref, kh_ref, latn_ref, wukt_ref,
                  gkrow_ref, gkr_ref, wuv_ref, o_ref, lhs, latbuf, krbuf, cbuf, sems, *, layer, n_pages):
    b = pl.program_id(0)
    gp = PAGES_PER_GROUP
    n_groups = n_pages // gp
    gpos = gp * PAGE
    nk = A_HEADS * A_NOPE
    log2e = 1.4426950408889634
    sc2 = (A_QK ** -0.5) * log2e

    def page_copies(bb, g, slot):
        out = []
        for j in range(gp):
            page = pt_ref[bb, g * gp + j]
            dst = pl.ds(j * PAGE, PAGE)
            out.append(pltpu.make_async_copy(lat_hbm.at[layer, page], latbuf.at[slot, dst], sems.at[0, slot]))
            out.append(pltpu.make_async_copy(kr_hbm.at[layer, page], krbuf.at[slot, j], sems.at[1, slot]))
        return out

    def start(bb, g, slot):
        for cp in page_copies(bb, g, slot):
            cp.start()

    def wait(g, slot):
        for cp in page_copies(b, g, slot):
            cp.wait()

    ring = DMA_SLOTS
    ahead = ring - 1
    assert n_groups % ring == 0 and ahead <= n_groups

    def fetch_ahead(g):
        gg = g + ahead
        wrap = gg >= n_groups
        bb = jnp.where(wrap, b + 1, b)
        g2 = jnp.where(wrap, gg - n_groups, gg)

        @pl.when(bb < pl.num_programs(0))
        def _():
            start(bb, g2, lax.rem(gg, ring))

    @pl.when(b == 0)
    def _():
        for d in range(ahead):
            start(0, d, d)

    @pl.when(b == 0)
    def _():
        lhs[0:nk, :] = wukt_ref[...]

    hrow = lax.broadcasted_iota(jnp.int32, (A_HEADS, nk), 0)
    hcol = lax.broadcasted_iota(jnp.int32, (A_HEADS, nk), 1) // A_NOPE
    qsel = jnp.where(hrow == hcol, qrow_ref[0] * gkrow_ref[...], 0.0)
    q_hi, q_lo = _split2(qsel)
    qabs = _dot(q_hi, wukt_ref[...]) + _dot(q_lo, wukt_ref[...])
    lhs[nk:nk + 2 * A_HEADS, :] = jnp.concatenate(_split2(qabs), axis=0)
    qrg = jnp.concatenate(_split2(qr_ref[0] * gkr_ref[...]), axis=0)
    ppb = POS_BLOCK // PAGE

    def scores(slot, cslot):
        parts = []
        for j in range(gpos // POS_BLOCK):
            rows = pl.ds(j * POS_BLOCK, POS_BLOCK)
            cb = latbuf[slot, rows, :].astype(BF16)
            cbuf[cslot, rows, :] = cb
            kx = _dot_nt(lhs[...], cb)
            kn = kx[:nk]
            ss = jnp.sum((kn * kn).reshape(A_HEADS, A_NOPE, POS_BLOCK), axis=1)
            raw = kx[nk:nk + A_HEADS] + kx[nk + A_HEADS:]
            krt = jnp.concatenate([krbuf[slot, j * ppb + t] for t in range(ppb)], axis=-1)
            rr = _dot(qrg, krt.astype(BF16))
            raw = raw + rr[:A_HEADS] + rr[A_HEADS:]
            ss = ss + jnp.sum(krt * krt, axis=0, keepdims=True)
            parts.append(raw * lax.rsqrt(ss * (1.0 / A_QK) + EPS) * sc2)
        return jnp.concatenate(parts, axis=-1)

    def absorb(s, slot, carry):
        m, l, acc = carry
        m_new = jnp.maximum(m, jnp.max(s, axis=-1, keepdims=True))
        a = jnp.exp2(m - m_new)
        p = jnp.exp2(s - m_new)
        l = l * a + jnp.sum(p, axis=-1, keepdims=True)
        acc = acc * a + _dot(p.astype(BF16), cbuf[slot])
        return m_new, l, acc

    wait(0, 0)
    fetch_ahead(0)
    s0 = scores(0, 0)
    init = (jnp.full((A_HEADS, 1), -jnp.inf, F32), jnp.zeros((A_HEADS, 1), F32),
            jnp.zeros((A_HEADS, KV_LORA), F32))

    def group(g, carry):
        s_prev, state = carry
        cslot = lax.rem(g, 2)
        wait(g, lax.rem(g, ring))
        fetch_ahead(g)
        s_cur = scores(lax.rem(g, ring), cslot)
        return s_cur, absorb(s_prev, 1 - cslot, state)

    s_last, state = lax.fori_loop(1, n_groups, group, (s0, init))
    m, l, acc = absorb(s_last, (n_groups - 1) % 2, state)

    s_new = jnp.sum(qh_ref[0].astype(F32) * kh_ref[0].astype(F32), axis=-1, keepdims=True) * sc2
    m_new = jnp.maximum(m, s_new)
    a = jnp.exp2(m - m_new)
    p = jnp.exp2(s_new - m_new)
    acc = acc * a + p * latn_ref[0]
    l = l * a + p
    o_lat = (acc / l).astype(BF16)
    full = _dot(o_lat, wuv_ref[...])
    vrow = lax.broadcasted_iota(jnp.int32, (A_HEADS, A_WIDTH), 0)
    vcol = lax.broadcasted_iota(jnp.int32, (A_HEADS, A_WIDTH), 1) // A_V
    o_ref[0] = jnp.sum(jnp.where(vrow == vcol, full, 0.0), axis=0, keepdims=True)


def _paged(page_table, cache_lat, cache_kr, qrow, qr, qh, kh, latn, wukt, gkrow, gkr, wuv, layer):
    nb, n_pages = page_table.shape
    gp = PAGES_PER_GROUP
    per_b = lambda shape: pl.BlockSpec((1,) + shape, lambda b, pt: (b,) + (0,) * len(shape))
    full = lambda a: pl.BlockSpec(a.shape, lambda b, pt: (0,) * a.ndim)
    grid_spec = pltpu.PrefetchScalarGridSpec(
        num_scalar_prefetch=1, grid=(nb,),
        in_specs=[pl.BlockSpec(memory_space=pl.ANY), pl.BlockSpec(memory_space=pl.ANY),
                  per_b((1, A_HEADS * A_NOPE)), per_b((A_HEADS, A_ROPE)), per_b((A_HEADS, HEAD_PAD)),
                  per_b((A_HEADS, HEAD_PAD)), per_b((1, KV_LORA)), full(wukt), full(gkrow), full(gkr), full(wuv)],
        out_specs=per_b((1, A_WIDTH)),
        scratch_shapes=[pltpu.VMEM((A_HEADS * A_NOPE + 2 * A_HEADS, KV_LORA), BF16),
                        pltpu.VMEM((DMA_SLOTS, gp * PAGE, KV_LORA), F32), pltpu.VMEM((DMA_SLOTS, gp, A_ROPE, PAGE), F32),
                        pltpu.VMEM((2, gp * PAGE, KV_LORA), BF16), pltpu.SemaphoreType.DMA((2, DMA_SLOTS))])
    return pl.pallas_call(
        functools.partial(_paged_kernel, layer=layer, n_pages=n_pages), grid_spec=grid_spec,
        out_shape=jax.ShapeDtypeStruct((nb, 1, A_WIDTH), F32),
        compiler_params=_params(1), name="paged_attn")(
            page_table, cache_lat, cache_kr, qrow, qr, qh, kh, latn, wukt, gkrow, gkr, wuv)


def _hstep_kernel(s_ref, q_ref, k_ref, g_ref, v_ref, o_ref, sn_ref):
    for h in range(B_HEADS):
        sn = jnp.exp(g_ref[0, h]) * s_ref[0, h] + k_ref[0, h] * v_ref[0, h]
        sn_ref[0, h] = sn
        o_ref[0, h] = jnp.sum(q_ref[0, h] * sn, axis=0, keepdims=True)


def _hstep(state, qc, kc, gc, vr):
    nb = state.shape[0]
    st = pl.BlockSpec((1, B_HEADS, B_DK, B_DV), lambda b: (b, 0, 0, 0))
    col = pl.BlockSpec((1, B_HEADS, B_DK, 1), lambda b: (b, 0, 0, 0))
    rowv = pl.BlockSpec((1, B_HEADS, 1, B_DV), lambda b: (b, 0, 0, 0))
    return pl.pallas_call(
        _hstep_kernel, grid=(nb,), in_specs=[st, col, col, col, rowv], out_specs=[rowv, st],
        out_shape=[jax.ShapeDtypeStruct((nb, B_HEADS, 1, B_DV), F32), jax.ShapeDtypeStruct(state.shape, F32)],
        compiler_params=_params(1), name="hgrn_step")(state, qc, kc, gc, vr)


def _merge_kernel(x_ref, ng_ref, wg_ref, oa_ref, ob_ref, gbn_ref, woa_ref, wob_ref, wo_ref, y_ref):
    x = x_ref[...]
    xn = _rms(x, ng_ref[...]).astype(BF16)
    gates = _dot(xn, wg_ref[...])
    ga, gb = gates[:, :A_WIDTH], gates[:, A_WIDTH:A_WIDTH + B_WIDTH]
    ma = gates[:, A_WIDTH + B_WIDTH:A_WIDTH + B_WIDTH + D_MODEL]
    mb = gates[:, A_WIDTH + B_WIDTH + D_MODEL:]
    ya = _dot((oa_ref[...] * (ga * jax.nn.sigmoid(ga))).astype(BF16), woa_ref[...])
    gbn = gbn_ref[...]
    obn = jnp.concatenate([_rms(ob_ref[:, h * B_DV:(h + 1) * B_DV], gbn) for h in range(B_HEADS)], axis=-1)
    yb = _dot((obn * (gb * jax.nn.sigmoid(gb))).astype(BF16), wob_ref[...])
    mix = jax.nn.sigmoid(ma) * ya + jax.nn.sigmoid(mb) * yb
    y_ref[...] = x + _dot(mix.astype(BF16), wo_ref[...])


def _merge(x, oa, ob, w, tm):
    rows = x.shape[0]
    row = lambda n: pl.BlockSpec((tm, n), lambda i: (i, 0))
    ins = [x, w["norm_g"], w["wg"], oa, ob, w["g_bn"], w["w_oa"], w["w_ob"], w["w_o"]]
    in_specs = [row(D_MODEL), _full(ins[1].shape), _full(ins[2].shape), row(A_WIDTH), row(B_WIDTH)] + \
               [_full(a.shape) for a in ins[5:]]
    return pl.pallas_call(
        _merge_kernel, grid=(rows // tm,), in_specs=in_specs, out_specs=row(D_MODEL),
        out_shape=jax.ShapeDtypeStruct((rows, D_MODEL), F32),
        compiler_params=_params(1), name="merge_out")(*ins)


def _head_pad_cols(w3, width):
    pad = jnp.zeros(w3.shape[:2] + (HEAD_PAD - width,), w3.dtype)
    return jnp.concatenate([w3, pad], axis=-1).reshape(w3.shape[0], -1)


def _rope_tables(pos):
    half = A_ROPE // 2
    inv = ROPE_THETA ** (-jnp.arange(half, dtype=F32) / half)
    ang = pos.astype(F32)[:, None] * inv
    cos, sin = jnp.cos(ang), jnp.sin(ang)
    n = pos.shape[0]
    one, zero = jnp.ones((n, A_NOPE), F32), jnp.zeros((n, A_NOPE), F32)
    tail = jnp.zeros((n, HEAD_PAD - A_QK), F32)
    z16 = jnp.zeros((n, half), F32)
    c = jnp.concatenate([one, cos, cos, tail], axis=-1)
    s1 = jnp.concatenate([zero, -sin, z16, tail], axis=-1)
    s2 = jnp.concatenate([zero, z16, sin, tail], axis=-1)
    return c, s1, s2


def _lane_gain(g):
    return jnp.concatenate([g, jnp.zeros((HEAD_PAD - A_QK,), g.dtype)])[None, :]


def kernel(x_prompt, x_sample, cache_latent, cache_krope, state_hgrn, page_table, meta_tokens,
           norm_g, w_in, g_cq, w_uq, g_ckv, w_uk, w_uv, g_qn, g_kn, lb_logits, g_bn, w_oa, w_ob, w_o):
    batch, seq, _ = x_prompt.shape
    dec_batch, dec_seq, _ = x_sample.shape
    depth = w_in.shape[0]
    assert depth == 1 and dec_seq == 1
    past_len = page_table.shape[1] * PAGE
    l = 0

    o0 = Q_LORA
    o1 = o0 + KV_LORA
    o2 = o1 + A_ROPE
    o3 = o2 + 3 * B_FDIM
    wi = w_in[l]
    wkr = jnp.zeros((D_MODEL, HEAD_PAD), F32).at[:, A_NOPE:A_QK].set(wi[:, o1:o2])
    w = {
        "norm_g": norm_g[l][None, :],
        "wq": wi[:, :o0].astype(BF16), "wkv": wi[:, o0:o1].astype(BF16), "wkr": wkr.astype(BF16),
        "wb": wi[:, o2:o3].astype(BF16), "wg": wi[:, o3:].astype(BF16),
        "g_cq": g_cq[l][None, :], "g_ckv": g_ckv[l][None, :],
        "w_uq": _head_pad_cols(w_uq[l].reshape(Q_LORA, A_HEADS, A_QK), A_QK).astype(BF16),
        "w_uk": _head_pad_cols(w_uk[l], A_NOPE).astype(BF16),
        "w_uv": w_uv[l].reshape(KV_LORA, A_WIDTH).astype(BF16),
        "w_uvt": w_uv[l].reshape(KV_LORA, A_WIDTH).T.astype(BF16),
        "g_qn": _lane_gain(g_qn[l]), "g_kn": _lane_gain(g_kn[l]),
        "lb_logits": lb_logits, "g_bn": g_bn[l][None, :],
        "w_oa": w_oa[l].astype(BF16), "w_ob": w_ob[l].astype(BF16), "w_o": w_o[l].astype(BF16),
    }

    xp = x_prompt.reshape(batch * seq, D_MODEL)
    tabs_p = _rope_tables(N_META + jnp.arange(seq))
    qscale = (A_QK ** -0.5) * 1.4426950408889634
    lat_p, krp_p, q_p, k_p, vt_p, hq, hk, hv, hg = _pre(xp, tabs_p, w, ROW_TILE, seq // ROW_TILE, l, qscale)
    tabs_m = _rope_tables(jnp.arange(N_META))
    lat_m, krp_m, _, k_m, vt_m, mq, mk, mv, mg = _pre(meta_tokens, tabs_m, w, N_META, 1, l)

    oa = _attn(q_p, k_p, vt_p, k_m, vt_m, batch, seq)
    zero_state = jnp.zeros((B_HEADS, B_DV, B_DK), F32)
    _, st_meta = _hgrn(mq, mk, mv, mg, zero_state, 1, N_META, N_META, N_META, False)
    ob, st_fin = _hgrn(hq, hk, hv, hg, st_meta[0], batch, seq, HGRN_CHUNK, HGRN_TILE, True)
    y_prompt = _merge(xp, oa, ob, w, ROW_TILE).reshape(batch, seq, D_MODEL)

    lat_all = jnp.concatenate([jnp.broadcast_to(lat_m[None], (batch, N_META, KV_LORA)),
                               lat_p.reshape(batch, seq, KV_LORA)], axis=1)
    kr_all = jnp.concatenate([jnp.broadcast_to(krp_m[None, :, A_NOPE:A_QK], (batch, N_META, A_ROPE)),
                              krp_p[:, A_NOPE:A_QK].reshape(batch, seq, A_ROPE)], axis=1)

    xs = x_sample.reshape(dec_batch, D_MODEL)
    tabs_s = _rope_tables(jnp.full((dec_batch,), past_len, jnp.int32))
    lat_s, krp_s, q_s, k_s, _, sq, sk, sv, sg = _pre(xs, tabs_s, w, dec_batch, 1, l)
    q_s3 = q_s.reshape(dec_batch, A_HEADS, HEAD_PAD)
    k_s3 = k_s.reshape(dec_batch, A_HEADS, HEAD_PAD)
    qrow = q_s3[:, :, :A_NOPE].astype(F32).reshape(dec_batch, 1, A_HEADS * A_NOPE)
    qrope = q_s3[:, :, A_NOPE:A_QK].astype(F32)
    gkrow = jnp.tile(g_kn[l][:A_NOPE], A_HEADS)[None, :]
    gkr = g_kn[l][None, A_NOPE:]
    wukt = w_uk[l].reshape(KV_LORA, A_HEADS * A_NOPE).T.astype(BF16)
    oa_s = _paged(page_table, cache_latent, jnp.swapaxes(cache_krope, 2, 3), qrow, qrope, q_s3, k_s3,
                  lat_s[:, None, :], wukt, gkrow, gkr, w["w_uv"], l)
    col = lambda a: a.reshape(dec_batch, B_HEADS, B_DK, 1)
    ob_s, st_s = _hstep(state_hgrn[l], col(sq), col(sk), col(sg), sv.reshape(dec_batch, B_HEADS, 1, B_DV))
    y_sample = _merge(xs, oa_s.reshape(dec_batch, A_WIDTH), ob_s.reshape(dec_batch, B_WIDTH), w, dec_batch)

    return (y_prompt, y_sample.reshape(dec_batch, dec_seq, D_MODEL),
            lat_all[None], kr_all[None], st_fin[None],
            lat_s.reshape(1, dec_batch, dec_seq, KV_LORA),
            krp_s[:, A_NOPE:A_QK].reshape(1, dec_batch, dec_seq, A_ROPE),
            st_s[None])
```

```python
import functools

import jax
import jax.numpy as jnp
from jax import lax
from jax.experimental import pallas as pl
from jax.experimental.pallas import tpu as pltpu

F32 = jnp.float32
BF16 = jnp.bfloat16

D_MODEL = 1024
N_META = 16
A_HEADS = 8
A_NOPE = 64
A_ROPE = 32
A_QK = A_NOPE + A_ROPE
A_V = 64
A_WIDTH = A_HEADS * A_V
Q_LORA = 384
KV_LORA = 256
ROPE_THETA = 10000.0
B_HEADS = 4
B_DK = 128
B_DV = 128
B_FDIM = B_HEADS * B_DK
B_WIDTH = B_HEADS * B_DV
EPS = 1e-6
PAGE = 128

LANES = 128
LOG2E = 1.4426950408889634
HEAD_PAD = LANES
VMEM_LIMIT = 56 * 1024 * 1024

PRE_TILE = 256
ROW_TILE = 256
ATTN_TQ = 256
ATTN_TK = 256
HGRN_CHUNK = 64
HGRN_TILE = 256
PAGES_PER_GROUP = 16
HSTEP_ROWS = 4
POS_BLOCK = 2 * PAGE
DMA_SLOTS = 4


def _full(shape):
    return pl.BlockSpec(shape, lambda *_: (0,) * len(shape))


def _params(n_axes):
    return pltpu.CompilerParams(dimension_semantics=("arbitrary",) * n_axes,
                                vmem_limit_bytes=VMEM_LIMIT)


def _dot(a, b):
    return jnp.dot(a, b, preferred_element_type=F32)


def _dot_nt(a, b, precision=None):
    return lax.dot_general(a, b, (((1,), (1,)), ((), ())), precision=precision,
                           preferred_element_type=F32)


def _rms(x, g):
    r = lax.rsqrt(jnp.mean(x * x, axis=-1, keepdims=True) + EPS)
    return (x * r) * g


def _rope_lanes(t, c, s1, s2):
    return t * c + pltpu.roll(t, LANES - A_ROPE // 2, axis=1) * s1 + pltpu.roll(t, A_ROPE // 2, axis=1) * s2


def _pre_kernel(x_ref, ng_ref, wa_ref, gcq_ref, wuq_ref, gckv_ref,
                wuk_ref, wuv_ref, gqn_ref, gkn_ref, c_ref, s1_ref, s2_ref, lb_ref,
                lat_ref, krp_ref, q_ref, k_ref, vt_ref, hq_ref, hk_ref, hv_ref, hg_ref, *, layer, qscale):
    x = x_ref[...]
    xn = _rms(x, ng_ref[...]).astype(BF16)
    c, s1, s2 = c_ref[...], s1_ref[...], s2_ref[...]

    o1, o2, o3 = Q_LORA, Q_LORA + KV_LORA, Q_LORA + KV_LORA + LANES
    y = _dot(xn, wa_ref[:, :o3])
    cq = y[:, :o1]
    cqn = _rms(cq, gcq_ref[...]).astype(BF16)
    qraw = _dot(cqn, wuq_ref[...])
    gqn = gqn_ref[...]
    for h in range(A_HEADS):
        t = _rope_lanes(qraw[:, h * HEAD_PAD:(h + 1) * HEAD_PAD], c, s1, s2)
        r = lax.rsqrt(jnp.sum(t * t, axis=-1, keepdims=True) * (1.0 / A_QK) + EPS)
        q_ref[:, h * HEAD_PAD:(h + 1) * HEAD_PAD] = (((t * r) * gqn) * qscale).astype(BF16)

    ckv = _rms(y[:, o1:o2], gckv_ref[...])
    lat_ref[...] = ckv
    ckvb = ckv.astype(BF16)
    krp = _rope_lanes(y[:, o2:o3], c, s1, s2)
    krp_ref[...] = krp
    kraw = _dot(ckvb, wuk_ref[...])
    gkn = gkn_ref[...]
    for h in range(A_HEADS):
        t = kraw[:, h * HEAD_PAD:(h + 1) * HEAD_PAD] + krp
        r = lax.rsqrt(jnp.sum(t * t, axis=-1, keepdims=True) * (1.0 / A_QK) + EPS)
        k_ref[:, h * HEAD_PAD:(h + 1) * HEAD_PAD] = ((t * r) * gkn).astype(BF16)
    vt_ref[...] = _dot_nt(wuv_ref[...], ckvb).astype(BF16)

    lbl = lb_ref[...]
    e = jnp.exp(lbl - jnp.max(lbl, axis=0, keepdims=True))
    lb = jnp.sum(e[:layer + 1], axis=0, keepdims=True) / jnp.sum(e, axis=0, keepdims=True)
    b = _dot(xn, wa_ref[:, o3:])
    bq, z, bi = b[:, :B_FDIM], b[:, B_FDIM:2 * B_FDIM], b[:, 2 * B_FDIM:]
    hq_ref[...] = bq * jax.nn.sigmoid(bq)
    hg_ref[...] = jnp.log(lb + (1.0 - lb) * jax.nn.sigmoid(z))
    hk_ref[...] = (1.0 - lb) * jax.nn.sigmoid(-z)
    hv_ref[...] = bi


def _pre(x, tabs, w, tm, tab_blocks, layer, qscale=1.0):
    rows = x.shape[0]
    grid = (rows // tm,)
    row = lambda n: pl.BlockSpec((tm, n), lambda i: (i, 0))
    tab = pl.BlockSpec((tm, LANES), lambda i: (i % tab_blocks, 0))
    ins = [x, w["norm_g"], w["wa"], w["g_cq"], w["w_uq"], w["g_ckv"],
           w["w_uk"], w["w_uvt"], w["g_qn"], w["g_kn"], tabs[0], tabs[1], tabs[2], w["lb_logits"]]
    in_specs = [row(D_MODEL)] + [_full(a.shape) for a in ins[1:10]] + [tab, tab, tab] + [_full(ins[13].shape)]
    outs = [(KV_LORA, F32), (LANES, F32), (A_HEADS * HEAD_PAD, BF16), (A_HEADS * HEAD_PAD, BF16),
            None, (B_FDIM, F32), (B_FDIM, F32), (B_WIDTH, F32), (B_FDIM, F32)]
    out_specs = [pl.BlockSpec((A_WIDTH, tm), lambda i: (0, i)) if o is None else row(o[0]) for o in outs]
    out_shape = [jax.ShapeDtypeStruct((A_WIDTH, rows), BF16) if o is None else
                 jax.ShapeDtypeStruct((rows, o[0]), o[1]) for o in outs]
    return pl.pallas_call(
        functools.partial(_pre_kernel, layer=layer, qscale=qscale),
        grid=grid, in_specs=in_specs, out_specs=out_specs, out_shape=out_shape,
        compiler_params=_params(1), name="pre_proj")(*ins)


def _attn_kernel(q_ref, k_ref, vt_ref, km_ref, vtm_ref, o_ref, ot_sc, s_sc, m_sc, l_sc):
    i = pl.program_id(1)
    tq, tk = ATTN_TQ, ATTN_TK
    nsub = tq // tk
    key = lax.broadcasted_iota(jnp.int32, (tk, tq), 0)
    qry = lax.broadcasted_iota(jnp.int32, (tk, tq), 1)

    hsl = [slice(h * HEAD_PAD, (h + 1) * HEAD_PAD) for h in range(A_HEADS)]
    vsl = [slice(h * A_V, (h + 1) * A_V) for h in range(A_HEADS)]

    def softmax_pv(h, s, vtblk, first):
        hr = slice(h, h + 1)
        m_blk = jnp.max(s, axis=0, keepdims=True)
        if first:
            m_new = m_blk
        else:
            m_new = jnp.maximum(m_sc[hr, :], m_blk)
            a = jnp.exp2(m_sc[hr, :] - m_new)
        p = jnp.exp2(s - m_new)
        psum = jnp.sum(p, axis=0, keepdims=True)
        pv = _dot(vtblk, p.astype(BF16))
        m_sc[hr, :] = m_new
        if first:
            l_sc[hr, :] = psum
            ot_sc[vsl[h], :] = pv
        else:
            l_sc[hr, :] = l_sc[hr, :] * a + psum
            ot_sc[vsl[h], :] = ot_sc[vsl[h], :] * a + pv

    nm = km_ref.shape[0]
    for h in range(A_HEADS):
        s_sc[h, 0:nm, :] = _dot_nt(km_ref[:, hsl[h]], q_ref[:, hsl[h]])
    for h in range(A_HEADS):
        softmax_pv(h, s_sc[h, 0:nm, :], vtm_ref[vsl[h], :], True)

    def block(r0, mask):
        for h in range(A_HEADS):
            s_sc[h] = _dot_nt(k_ref[pl.ds(r0, tk), hsl[h]], q_ref[:, hsl[h]])
        for h in range(A_HEADS):
            s = s_sc[h]
            if mask is not None:
                s = jnp.where(mask, s, -jnp.inf)
            softmax_pv(h, s, vt_ref[vsl[h], pl.ds(r0, tk)], False)

    def body(kb, _):
        block(pl.multiple_of(kb * tk, tk), None)
        return 0

    lax.fori_loop(0, i * nsub, body, 0)
    for j in range(nsub):
        block(pl.multiple_of(i * tq + j * tk, tk), key + j * tk <= qry)
    for h in range(A_HEADS):
        ot_sc[vsl[h], :] = ot_sc[vsl[h], :] / l_sc[h:h + 1, :]
    o_ref[...] = ot_sc[...].T


def _attn(q, k, vt, km, vtm, batch, seq):
    nq = seq // ATTN_TQ
    return pl.pallas_call(
        _attn_kernel, grid=(batch, nq),
        in_specs=[pl.BlockSpec((ATTN_TQ, A_HEADS * HEAD_PAD), lambda b, i: (b * nq + i, 0)),
                  pl.BlockSpec((seq, A_HEADS * HEAD_PAD), lambda b, i: (b, 0)),
                  pl.BlockSpec((A_WIDTH, seq), lambda b, i: (0, b)),
                  _full(km.shape), _full(vtm.shape)],
        out_specs=pl.BlockSpec((ATTN_TQ, A_WIDTH), lambda b, i: (b * nq + i, 0)),
        out_shape=jax.ShapeDtypeStruct((batch * seq, A_WIDTH), F32),
        scratch_shapes=[pltpu.VMEM((A_WIDTH, ATTN_TQ), F32), pltpu.VMEM((A_HEADS, ATTN_TK, ATTN_TQ), F32),
                        pltpu.VMEM((A_HEADS, ATTN_TQ), F32), pltpu.VMEM((A_HEADS, ATTN_TQ), F32)],
        compiler_params=_params(2), name="prompt_attn")(q, k, vt, km, vtm)


def _split3(x):
    a = x.astype(BF16)
    r = x - a.astype(F32)
    b = r.astype(BF16)
    c = (r - b.astype(F32)).astype(BF16)
    return a, b, c


def _hgrn_chunk(q, kk, v, g, st, tril, chunk, cum_ref, krow):
    g1, g2, g3 = _split3(g)
    cum = (_dot(tril, g1) + _dot(tril, g2) + _dot(tril, g3)) * LOG2E
    cum_ref[...] = cum
    nblk = chunk // 8
    cb = [cum[8 * j:8 * j + 8] for j in range(nblk)]
    qb = [q[8 * j:8 * j + 8] for j in range(nblk)]
    lane = lax.broadcasted_iota(jnp.int32, (8, chunk), 1)
    sub = lax.broadcasted_iota(jnp.int32, (8, chunk), 0)
    ab = []
    for j in range(nblk):
        a = jnp.zeros((8, chunk), F32)
        for s in range(8 * j, 8 * j + 8):
            col = jnp.sum(jnp.exp2(cb[j] - cum_ref[s:s + 1, :]) * qb[j] * krow(s), axis=-1, keepdims=True)
            a = jnp.where(lane == s, col, a)
        ab.append(jnp.where(lane <= sub + 8 * j, a, 0.0))
    attn = jnp.concatenate(ab, axis=0)
    row = lax.broadcasted_iota(jnp.int32, (chunk, chunk), 0)
    col = lax.broadcasted_iota(jnp.int32, (chunk, chunk), 1)
    dead = jnp.full((8, B_DK), -jnp.inf, F32)
    w = 8
    while w < chunk:
        eq, ek = [], []
        for j in range(nblk):
            blk = (8 * j) // w
            if blk % 2:
                eq.append(cb[j] - cum_ref[blk * w - 1:blk * w, :])
                ek.append(dead)
            else:
                eq.append(dead)
                ek.append(cum_ref[blk * w + w - 1:blk * w + w, :] - cb[j])
        qa = (q * jnp.exp2(jnp.concatenate(eq, axis=0))).astype(BF16)
        ka = (kk * jnp.exp2(jnp.concatenate(ek, axis=0))).astype(BF16)
        sh = w.bit_length() - 1
        rb, cbk = row >> sh, col >> sh
        attn = attn + jnp.where((rb == cbk + 1) & ((cbk & 1) == 0), _dot_nt(qa, ka), 0.0)
        w *= 2
    last = cum_ref[chunk - 1:chunk, :]
    o = _dot(attn.astype(BF16), v.astype(BF16)) + _dot_nt((q * jnp.exp2(cum)).astype(BF16), st.astype(BF16))
    kdec = (kk * jnp.exp2(last - cum)).astype(BF16)
    upd = lax.dot_general(v.astype(BF16), kdec, (((0,), (0,)), ((), ())), preferred_element_type=F32)
    return o, jnp.exp2(last) * st + upd


def _hgrn_kernel(q_ref, k_ref, v_ref, g_ref, st0_ref, o_ref, st_ref, st_sc, cum_sc, kk_sc, *, chunk, tile,
                 transpose_out):
    t = pl.program_id(1)

    @pl.when(t == 0)
    def _():
        st_sc[...] = st0_ref[...]

    r = lax.broadcasted_iota(jnp.int32, (chunk, chunk), 0)
    c = lax.broadcasted_iota(jnp.int32, (chunk, chunk), 1)
    tril = (c <= r).astype(BF16)

    def body(ci, _):
        r0 = pl.multiple_of(ci * chunk, chunk)
        for h in range(B_HEADS):
            hs = slice(h * B_DK, (h + 1) * B_DK)
            kk = k_ref[pl.ds(r0, chunk), hs]
            kk_sc[h] = kk
            krow = lambda s, h=h: kk_sc[h, s:s + 1, :]
            o, st = _hgrn_chunk(q_ref[pl.ds(r0, chunk), hs], kk,
                                v_ref[pl.ds(r0, chunk), hs], g_ref[pl.ds(r0, chunk), hs],
                                st_sc[h], tril, chunk, cum_sc.at[h], krow)
            o_ref[pl.ds(r0, chunk), hs] = o
            st_sc[h] = st
        return 0

    lax.fori_loop(0, tile // chunk, body, 0)

    @pl.when(t == pl.num_programs(1) - 1)
    def _():
        for h in range(B_HEADS):
            st_ref[0, h] = st_sc[h].T if transpose_out else st_sc[h]


def _hgrn(hq, hk, hv, hg, st0, batch, seq, chunk, tile, transpose_out):
    nt = seq // tile
    row = pl.BlockSpec((tile, B_FDIM), lambda b, t: (b * nt + t, 0))
    return pl.pallas_call(
        functools.partial(_hgrn_kernel, chunk=chunk, tile=tile, transpose_out=transpose_out),
        grid=(batch, nt),
        in_specs=[row, row, row, row, _full(st0.shape)],
        out_specs=[row, pl.BlockSpec((1, B_HEADS, B_DV, B_DK), lambda b, t: (b, 0, 0, 0))],
        out_shape=[jax.ShapeDtypeStruct((batch * seq, B_WIDTH), F32),
                   jax.ShapeDtypeStruct((batch, B_HEADS, B_DV, B_DK), F32)],
        scratch_shapes=[pltpu.VMEM((B_HEADS, B_DV, B_DK), F32), pltpu.VMEM((B_HEADS, chunk, B_DK), F32),
                        pltpu.VMEM((B_HEADS, chunk, B_DK), F32)],
        compiler_params=_params(2), name="hgrn_chunks")(hq, hk, hv, hg, st0)


def _split2(x):
    hi = x.astype(BF16)
    return hi, (x - hi.astype(F32)).astype(BF16)


def _paged_kernel(pt_ref, lat_hbm, kr_hbm, qrow_ref, qr_ref, qh_ref, kh_ref, latn_ref, wukt_ref,
                  gkrow_ref, gkr_ref, wuv_ref, o_ref, lhs, latbuf, krbuf, cbuf, sems, *, layer, n_pages):
    b = pl.program_id(0)
    gp = PAGES_PER_GROUP
    n_groups = n_pages // gp
    gpos = gp * PAGE
    nk = A_HEADS * A_NOPE
    log2e = 1.4426950408889634
    sc2 = (A_QK ** -0.5) * log2e

    def page_copies(bb, g, slot):
        out = []
        for j in range(gp):
            page = pt_ref[bb, g * gp + j]
            dst = pl.ds(j * PAGE, PAGE)
            out.append(pltpu.make_async_copy(lat_hbm.at[layer, page], latbuf.at[slot, dst], sems.at[0, slot]))
            out.append(pltpu.make_async_copy(kr_hbm.at[layer, page], krbuf.at[slot, j], sems.at[1, slot]))
        return out

    def start(bb, g, slot):
        for cp in page_copies(bb, g, slot):
            cp.start()

    def wait(g, slot):
        for cp in page_copies(b, g, slot):
            cp.wait()

    ring = DMA_SLOTS
    ahead = ring - 1
    assert n_groups % ring == 0 and ahead <= n_groups

    def fetch_ahead(g):
        gg = g + ahead
        wrap = gg >= n_groups
        bb = jnp.where(wrap, b + 1, b)
        g2 = jnp.where(wrap, gg - n_groups, gg)

        @pl.when(bb < pl.num_programs(0))
        def _():
            start(bb, g2, lax.rem(gg, ring))

    @pl.when(b == 0)
    def _():
        for d in range(ahead):
            start(0, d, d)

    @pl.when(b == 0)
    def _():
        lhs[0:nk, :] = wukt_ref[...]

    hrow = lax.broadcasted_iota(jnp.int32, (A_HEADS, nk), 0)
    hcol = lax.broadcasted_iota(jnp.int32, (A_HEADS, nk), 1) // A_NOPE
    qsel = jnp.where(hrow == hcol, qrow_ref[0] * gkrow_ref[...], 0.0)
    q_hi, q_lo = _split2(qsel)
    qabs = _dot(q_hi, wukt_ref[...]) + _dot(q_lo, wukt_ref[...])
    lhs[nk:nk + 2 * A_HEADS, :] = jnp.concatenate(_split2(qabs), axis=0)
    qrg = jnp.concatenate(_split2(qr_ref[0] * gkr_ref[...]), axis=0)
    ppb = POS_BLOCK // PAGE

    def scores(slot, cslot):
        parts = []
        for j in range(gpos // POS_BLOCK):
            rows = pl.ds(j * POS_BLOCK, POS_BLOCK)
            cb = latbuf[slot, rows, :].astype(BF16)
            cbuf[cslot, rows, :] = cb
            kx = _dot_nt(lhs[...], cb)
            kn = kx[:nk]
            ss = jnp.sum((kn * kn).reshape(A_HEADS, A_NOPE, POS_BLOCK), axis=1)
            raw = kx[nk:nk + A_HEADS] + kx[nk + A_HEADS:]
            krt = jnp.concatenate([krbuf[slot, j * ppb + t] for t in range(ppb)], axis=-1)
            rr = _dot(qrg, krt.astype(BF16))
            raw = raw + rr[:A_HEADS] + rr[A_HEADS:]
            ss = ss + jnp.sum(krt * krt, axis=0, keepdims=True)
            parts.append(raw * lax.rsqrt(ss * (1.0 / A_QK) + EPS) * sc2)
        return jnp.concatenate(parts, axis=-1)

    def absorb(s, slot, carry):
        m, l, acc = carry
        m_new = jnp.maximum(m, jnp.max(s, axis=-1, keepdims=True))
        a = jnp.exp2(m - m_new)
        p = jnp.exp2(s - m_new)
        l = l * a + jnp.sum(p, axis=-1, keepdims=True)
        acc = acc * a + _dot(p.astype(BF16), cbuf[slot])
        return m_new, l, acc

    wait(0, 0)
    fetch_ahead(0)
    s0 = scores(0, 0)
    init = (jnp.full((A_HEADS, 1), -jnp.inf, F32), jnp.zeros((A_HEADS, 1), F32),
            jnp.zeros((A_HEADS, KV_LORA), F32))

    def group(g, carry):
        s_prev, state = carry
        cslot = lax.rem(g, 2)
        wait(g, lax.rem(g, ring))
        fetch_ahead(g)
        s_cur = scores(lax.rem(g, ring), cslot)
        return s_cur, absorb(s_prev, 1 - cslot, state)

    s_last, state = lax.fori_loop(1, n_groups, group, (s0, init))
    m, l, acc = absorb(s_last, (n_groups - 1) % 2, state)

    s_new = jnp.sum(qh_ref[0].astype(F32) * kh_ref[0].astype(F32), axis=-1, keepdims=True) * sc2
    m_new = jnp.maximum(m, s_new)
    a = jnp.exp2(m - m_new)
    p = jnp.exp2(s_new - m_new)
    acc = acc * a + p * latn_ref[0]
    l = l * a + p
    o_lat = (acc / l).astype(BF16)
    full = _dot(o_lat, wuv_ref[...])
    vrow = lax.broadcasted_iota(jnp.int32, (A_HEADS, A_WIDTH), 0)
    vcol = lax.broadcasted_iota(jnp.int32, (A_HEADS, A_WIDTH), 1) // A_V
    o_ref[0] = jnp.sum(jnp.where(vrow == vcol, full, 0.0), axis=0, keepdims=True)


def _paged(page_table, cache_lat, cache_kr, qrow, qr, qh, kh, latn, wukt, gkrow, gkr, wuv, layer):
    nb, n_pages = page_table.shape
    gp = PAGES_PER_GROUP
    per_b = lambda shape: pl.BlockSpec((1,) + shape, lambda b, pt: (b,) + (0,) * len(shape))
    full = lambda a: pl.BlockSpec(a.shape, lambda b, pt: (0,) * a.ndim)
    grid_spec = pltpu.PrefetchScalarGridSpec(
        num_scalar_prefetch=1, grid=(nb,),
        in_specs=[pl.BlockSpec(memory_space=pl.ANY), pl.BlockSpec(memory_space=pl.ANY),
                  per_b((1, A_HEADS * A_NOPE)), per_b((A_HEADS, A_ROPE)), per_b((A_HEADS, HEAD_PAD)),
                  per_b((A_HEADS, HEAD_PAD)), per_b((1, KV_LORA)), full(wukt), full(gkrow), full(gkr), full(wuv)],
        out_specs=per_b((1, A_WIDTH)),
        scratch_shapes=[pltpu.VMEM((A_HEADS * A_NOPE + 2 * A_HEADS, KV_LORA), BF16),
                        pltpu.VMEM((DMA_SLOTS, gp * PAGE, KV_LORA), F32), pltpu.VMEM((DMA_SLOTS, gp, A_ROPE, PAGE), F32),
                        pltpu.VMEM((2, gp * PAGE, KV_LORA), BF16), pltpu.SemaphoreType.DMA((2, DMA_SLOTS))])
    return pl.pallas_call(
        functools.partial(_paged_kernel, layer=layer, n_pages=n_pages), grid_spec=grid_spec,
        out_shape=jax.ShapeDtypeStruct((nb, 1, A_WIDTH), F32),
        compiler_params=_params(1), name="paged_attn")(
            page_table, cache_lat, cache_kr, qrow, qr, qh, kh, latn, wukt, gkrow, gkr, wuv)


def _hstep_kernel(s_ref, q_ref, k_ref, g_ref, v_ref, o_ref, sn_ref, *, layer, rows):
    r = lax.broadcasted_iota(jnp.int32, (B_DK, B_DK), 0)
    c = lax.broadcasted_iota(jnp.int32, (B_DK, B_DK), 1)
    eye = r == c

    def column(row):
        return jnp.sum(jnp.where(eye, row, 0.0), axis=-1, keepdims=True)

    for i in range(rows):
        for h in range(B_HEADS):
            hs = slice(h * B_DK, (h + 1) * B_DK)
            sn = column(jnp.exp(g_ref[i, :, hs])) * s_ref[layer, i, h] + column(k_ref[i, :, hs]) * v_ref[i, :, hs]
            sn_ref[i, h] = sn
            o_ref[i, :, hs] = jnp.sum(column(q_ref[i, :, hs]) * sn, axis=0, keepdims=True)


def _hstep(state, q, k, g, v, layer):
    nb = state.shape[1]
    rows = HSTEP_ROWS
    st_in = pl.BlockSpec((state.shape[0], rows, B_HEADS, B_DK, B_DV), lambda b: (0, b, 0, 0, 0))
    st_out = pl.BlockSpec((rows, B_HEADS, B_DK, B_DV), lambda b: (b, 0, 0, 0))
    row = pl.BlockSpec((rows, 1, B_FDIM), lambda b: (b, 0, 0))
    return pl.pallas_call(
        functools.partial(_hstep_kernel, layer=layer, rows=rows), grid=(nb // rows,),
        in_specs=[st_in, row, row, row, row], out_specs=[row, st_out],
        out_shape=[jax.ShapeDtypeStruct((nb, 1, B_WIDTH), F32), jax.ShapeDtypeStruct(state.shape[1:], F32)],
        compiler_params=_params(1), name="hgrn_step")(state, q, k, g, v)


def _merge_kernel(x_ref, ng_ref, wg_ref, oa_ref, ob_ref, gbn_ref, woa_ref, wob_ref, wo_ref, y_ref):
    x = x_ref[...]
    xn = _rms(x, ng_ref[...]).astype(BF16)
    gates = _dot(xn, wg_ref[...])
    ga, gb = gates[:, :A_WIDTH], gates[:, A_WIDTH:A_WIDTH + B_WIDTH]
    ma = gates[:, A_WIDTH + B_WIDTH:A_WIDTH + B_WIDTH + D_MODEL]
    mb = gates[:, A_WIDTH + B_WIDTH + D_MODEL:]
    ya = _dot((oa_ref[...] * (ga * jax.nn.sigmoid(ga))).astype(BF16), woa_ref[...])
    gbn = gbn_ref[...]
    obn = jnp.concatenate([_rms(ob_ref[:, h * B_DV:(h + 1) * B_DV], gbn) for h in range(B_HEADS)], axis=-1)
    yb = _dot((obn * (gb * jax.nn.sigmoid(gb))).astype(BF16), wob_ref[...])
    mix = jax.nn.sigmoid(ma) * ya + jax.nn.sigmoid(mb) * yb
    y_ref[...] = x + _dot(mix.astype(BF16), wo_ref[...])


def _merge(x, oa, ob, w, tm):
    rows = x.shape[0]
    row = lambda n: pl.BlockSpec((tm, n), lambda i: (i, 0))
    ins = [x, w["norm_g"], w["wg"], oa, ob, w["g_bn"], w["w_oa"], w["w_ob"], w["w_o"]]
    in_specs = [row(D_MODEL), _full(ins[1].shape), _full(ins[2].shape), row(A_WIDTH), row(B_WIDTH)] + \
               [_full(a.shape) for a in ins[5:]]
    return pl.pallas_call(
        _merge_kernel, grid=(rows // tm,), in_specs=in_specs, out_specs=row(D_MODEL),
        out_shape=jax.ShapeDtypeStruct((rows, D_MODEL), F32),
        compiler_params=_params(1), name="merge_out")(*ins)


def _head_pad_cols(w3, width):
    pad = jnp.zeros(w3.shape[:2] + (HEAD_PAD - width,), w3.dtype)
    return jnp.concatenate([w3, pad], axis=-1).reshape(w3.shape[0], -1)


def _rope_tables(pos):
    half = A_ROPE // 2
    inv = ROPE_THETA ** (-jnp.arange(half, dtype=F32) / half)
    ang = pos.astype(F32)[:, None] * inv
    cos, sin = jnp.cos(ang), jnp.sin(ang)
    n = pos.shape[0]
    one, zero = jnp.ones((n, A_NOPE), F32), jnp.zeros((n, A_NOPE), F32)
    tail = jnp.zeros((n, HEAD_PAD - A_QK), F32)
    z16 = jnp.zeros((n, half), F32)
    c = jnp.concatenate([one, cos, cos, tail], axis=-1)
    s1 = jnp.concatenate([zero, -sin, z16, tail], axis=-1)
    s2 = jnp.concatenate([zero, z16, sin, tail], axis=-1)
    return c, s1, s2


def _lane_gain(g):
    return jnp.concatenate([g, jnp.zeros((HEAD_PAD - A_QK,), g.dtype)])[None, :]


def kernel(x_prompt, x_sample, cache_latent, cache_krope, state_hgrn, page_table, meta_tokens,
           norm_g, w_in, g_cq, w_uq, g_ckv, w_uk, w_uv, g_qn, g_kn, lb_logits, g_bn, w_oa, w_ob, w_o):
    batch, seq, _ = x_prompt.shape
    dec_batch, dec_seq, _ = x_sample.shape
    depth = w_in.shape[0]
    assert depth == 1 and dec_seq == 1
    past_len = page_table.shape[1] * PAGE
    l = 0

    o0 = Q_LORA
    o1 = o0 + KV_LORA
    o2 = o1 + A_ROPE
    o3 = o2 + 3 * B_FDIM
    wi = w_in[l]
    wkr = jnp.zeros((D_MODEL, HEAD_PAD), F32).at[:, A_NOPE:A_QK].set(wi[:, o1:o2])
    w = {
        "norm_g": norm_g[l][None, :],
        "wa": jnp.concatenate([wi[:, :o1], wkr, wi[:, o2:o3]], axis=1).astype(BF16), "wg": wi[:, o3:].astype(BF16),
        "g_cq": g_cq[l][None, :], "g_ckv": g_ckv[l][None, :],
        "w_uq": _head_pad_cols(w_uq[l].reshape(Q_LORA, A_HEADS, A_QK), A_QK).astype(BF16),
        "w_uk": _head_pad_cols(w_uk[l], A_NOPE).astype(BF16),
        "w_uv": w_uv[l].reshape(KV_LORA, A_WIDTH).astype(BF16),
        "w_uvt": w_uv[l].reshape(KV_LORA, A_WIDTH).T.astype(BF16),
        "g_qn": _lane_gain(g_qn[l]), "g_kn": _lane_gain(g_kn[l]),
        "lb_logits": lb_logits, "g_bn": g_bn[l][None, :],
        "w_oa": w_oa[l].astype(BF16), "w_ob": w_ob[l].astype(BF16), "w_o": w_o[l].astype(BF16),
    }

    xp = x_prompt.reshape(batch * seq, D_MODEL)
    tabs_p = _rope_tables(N_META + jnp.arange(seq))
    qscale = (A_QK ** -0.5) * 1.4426950408889634
    lat_p, krp_p, q_p, k_p, vt_p, hq, hk, hv, hg = _pre(xp, tabs_p, w, PRE_TILE, seq // PRE_TILE, l, qscale)
    tabs_m = _rope_tables(jnp.arange(N_META))
    lat_m, krp_m, _, k_m, vt_m, mq, mk, mv, mg = _pre(meta_tokens, tabs_m, w, N_META, 1, l)

    oa = _attn(q_p, k_p, vt_p, k_m, vt_m, batch, seq)
    zero_state = jnp.zeros((B_HEADS, B_DV, B_DK), F32)
    _, st_meta = _hgrn(mq, mk, mv, mg, zero_state, 1, N_META, N_META, N_META, False)
    ob, st_fin = _hgrn(hq, hk, hv, hg, st_meta[0], batch, seq, HGRN_CHUNK, HGRN_TILE, True)
    y_prompt = _merge(xp, oa, ob, w, ROW_TILE).reshape(batch, seq, D_MODEL)

    lat_all = jnp.concatenate([jnp.broadcast_to(lat_m[None], (batch, N_META, KV_LORA)),
                               lat_p.reshape(batch, seq, KV_LORA)], axis=1)
    kr_all = jnp.concatenate([jnp.broadcast_to(krp_m[None, :, A_NOPE:A_QK], (batch, N_META, A_ROPE)),
                              krp_p[:, A_NOPE:A_QK].reshape(batch, seq, A_ROPE)], axis=1)

    xs = x_sample.reshape(dec_batch, D_MODEL)
    tabs_s = _rope_tables(jnp.full((dec_batch,), past_len, jnp.int32))
    lat_s, krp_s, q_s, k_s, _, sq, sk, sv, sg = _pre(xs, tabs_s, w, dec_batch, 1, l)
    q_s3 = q_s.reshape(dec_batch, A_HEADS, HEAD_PAD)
    k_s3 = k_s.reshape(dec_batch, A_HEADS, HEAD_PAD)
    qrow = q_s3[:, :, :A_NOPE].astype(F32).reshape(dec_batch, 1, A_HEADS * A_NOPE)
    qrope = q_s3[:, :, A_NOPE:A_QK].astype(F32)
    gkrow = jnp.tile(g_kn[l][:A_NOPE], A_HEADS)[None, :]
    gkr = g_kn[l][None, A_NOPE:]
    wukt = w_uk[l].reshape(KV_LORA, A_HEADS * A_NOPE).T.astype(BF16)
    oa_s = _paged(page_table, cache_latent, jnp.swapaxes(cache_krope, 2, 3), qrow, qrope, q_s3, k_s3,
                  lat_s[:, None, :], wukt, gkrow, gkr, w["w_uv"], l)
    row3 = lambda a: a.reshape(dec_batch, 1, B_FDIM)
    ob_s, st_s = _hstep(state_hgrn, row3(sq), row3(sk), row3(sg), row3(sv), l)
    y_sample = _merge(xs, oa_s.reshape(dec_batch, A_WIDTH), ob_s.reshape(dec_batch, B_WIDTH), w, dec_batch)

    return (y_prompt, y_sample.reshape(dec_batch, dec_seq, D_MODEL),
            lat_all[None], kr_all[None], st_fin[None],
            lat_s.reshape(1, dec_batch, dec_seq, KV_LORA),
            krp_s[:, A_NOPE:A_QK].reshape(1, dec_batch, dec_seq, A_ROPE),
            st_s[None])
```

```python
import functools

import jax
import jax.numpy as jnp
from jax import lax
from jax.experimental import pallas as pl
from jax.experimental.pallas import tpu as pltpu

F32 = jnp.float32
BF16 = jnp.bfloat16

D_MODEL = 1024
N_META = 16
A_HEADS = 8
A_NOPE = 64
A_ROPE = 32
A_QK = A_NOPE + A_ROPE
A_V = 64
A_WIDTH = A_HEADS * A_V
Q_LORA = 384
KV_LORA = 256
ROPE_THETA = 10000.0
B_HEADS = 4
B_DK = 128
B_DV = 128
B_FDIM = B_HEADS * B_DK
B_WIDTH = B_HEADS * B_DV
EPS = 1e-6
PAGE = 128

LANES = 128
LOG2E = 1.4426950408889634
HEAD_PAD = LANES
VMEM_LIMIT = 56 * 1024 * 1024

PRE_TILE = 256
ROW_TILE = 256
ATTN_TQ = 256
ATTN_TK = 256
HGRN_CHUNK = 64
HGRN_TILE = 256
PAGES_PER_GROUP = 16
HSTEP_ROWS = 4
POS_BLOCK = 2 * PAGE
DMA_SLOTS = 4


def _full(shape):
    return pl.BlockSpec(shape, lambda *_: (0,) * len(shape))


def _params(n_axes):
    return pltpu.CompilerParams(dimension_semantics=("arbitrary",) * n_axes,
                                vmem_limit_bytes=VMEM_LIMIT)


def _dot(a, b):
    return jnp.dot(a, b, preferred_element_type=F32)


def _dot_nt(a, b, precision=None):
    return lax.dot_general(a, b, (((1,), (1,)), ((), ())), precision=precision,
                           preferred_element_type=F32)


def _rms(x, g):
    r = lax.rsqrt(jnp.mean(x * x, axis=-1, keepdims=True) + EPS)
    return (x * r) * g


def _rope_lanes(t, c, s1, s2):
    return t * c + pltpu.roll(t, LANES - A_ROPE // 2, axis=1) * s1 + pltpu.roll(t, A_ROPE // 2, axis=1) * s2


def _pre_kernel(x_ref, ng_ref, wa_ref, gcq_ref, wuq_ref, gckv_ref,
                wuk_ref, wuv_ref, gqn_ref, gkn_ref, c_ref, s1_ref, s2_ref, lb_ref,
                lat_ref, krp_ref, q_ref, k_ref, vt_ref, hq_ref, hk_ref, hv_ref, hg_ref, *, layer, qscale):
    x = x_ref[...]
    xn = _rms(x, ng_ref[...]).astype(BF16)
    c, s1, s2 = c_ref[...], s1_ref[...], s2_ref[...]

    o1, o2, o3 = Q_LORA, Q_LORA + KV_LORA, Q_LORA + KV_LORA + LANES
    y = _dot(xn, wa_ref[:, :o3])
    cq = y[:, :o1]
    cqn = _rms(cq, gcq_ref[...]).astype(BF16)
    qraw = _dot(cqn, wuq_ref[...])
    gqn = gqn_ref[...]
    for h in range(A_HEADS):
        t = _rope_lanes(qraw[:, h * HEAD_PAD:(h + 1) * HEAD_PAD], c, s1, s2)
        r = lax.rsqrt(jnp.sum(t * t, axis=-1, keepdims=True) * (1.0 / A_QK) + EPS)
        q_ref[:, h * HEAD_PAD:(h + 1) * HEAD_PAD] = (((t * r) * gqn) * qscale).astype(BF16)

    ckv = _rms(y[:, o1:o2], gckv_ref[...])
    lat_ref[...] = ckv
    ckvb = ckv.astype(BF16)
    krp = _rope_lanes(y[:, o2:o3], c, s1, s2)
    krp_ref[...] = krp
    kraw = _dot(ckvb, wuk_ref[...])
    gkn = gkn_ref[...]
    for h in range(A_HEADS):
        t = kraw[:, h * HEAD_PAD:(h + 1) * HEAD_PAD] + krp
        r = lax.rsqrt(jnp.sum(t * t, axis=-1, keepdims=True) * (1.0 / A_QK) + EPS)
        k_ref[:, h * HEAD_PAD:(h + 1) * HEAD_PAD] = ((t * r) * gkn).astype(BF16)
    vt_ref[...] = _dot_nt(wuv_ref[...], ckvb).astype(BF16)

    lbl = lb_ref[...]
    e = jnp.exp(lbl - jnp.max(lbl, axis=0, keepdims=True))
    lb = jnp.sum(e[:layer + 1], axis=0, keepdims=True) / jnp.sum(e, axis=0, keepdims=True)
    b = _dot(xn, wa_ref[:, o3:])
    bq, z, bi = b[:, :B_FDIM], b[:, B_FDIM:2 * B_FDIM], b[:, 2 * B_FDIM:]
    hq_ref[...] = bq * jax.nn.sigmoid(bq)
    hg_ref[...] = jnp.log(lb + (1.0 - lb) * jax.nn.sigmoid(z))
    hk_ref[...] = (1.0 - lb) * jax.nn.sigmoid(-z)
    hv_ref[...] = bi


def _pre(x, tabs, w, tm, tab_blocks, layer, qscale=1.0):
    rows = x.shape[0]
    grid = (rows // tm,)
    row = lambda n: pl.BlockSpec((tm, n), lambda i: (i, 0))
    tab = pl.BlockSpec((tm, LANES), lambda i: (i % tab_blocks, 0))
    ins = [x, w["norm_g"], w["wa"], w["g_cq"], w["w_uq"], w["g_ckv"],
           w["w_uk"], w["w_uvt"], w["g_qn"], w["g_kn"], tabs[0], tabs[1], tabs[2], w["lb_logits"]]
    in_specs = [row(D_MODEL)] + [_full(a.shape) for a in ins[1:10]] + [tab, tab, tab] + [_full(ins[13].shape)]
    outs = [(KV_LORA, F32), (LANES, F32), (A_HEADS * HEAD_PAD, BF16), (A_HEADS * HEAD_PAD, BF16),
            None, (B_FDIM, F32), (B_FDIM, F32), (B_WIDTH, F32), (B_FDIM, F32)]
    out_specs = [pl.BlockSpec((A_WIDTH, tm), lambda i: (0, i)) if o is None else row(o[0]) for o in outs]
    out_shape = [jax.ShapeDtypeStruct((A_WIDTH, rows), BF16) if o is None else
                 jax.ShapeDtypeStruct((rows, o[0]), o[1]) for o in outs]
    return pl.pallas_call(
        functools.partial(_pre_kernel, layer=layer, qscale=qscale),
        grid=grid, in_specs=in_specs, out_specs=out_specs, out_shape=out_shape,
        compiler_params=_params(1), name="pre_proj")(*ins)


def _attn_kernel(q_ref, k_ref, vt_ref, km_ref, vtm_ref, o_ref, ot_sc, s_sc, m_sc, l_sc):
    i = pl.program_id(1)
    tq, tk = ATTN_TQ, ATTN_TK
    nsub = tq // tk
    key = lax.broadcasted_iota(jnp.int32, (tk, tq), 0)
    qry = lax.broadcasted_iota(jnp.int32, (tk, tq), 1)

    hsl = [slice(h * HEAD_PAD, (h + 1) * HEAD_PAD) for h in range(A_HEADS)]
    vsl = [slice(h * A_V, (h + 1) * A_V) for h in range(A_HEADS)]

    def softmax_pv(h, s, vtblk, first):
        hr = slice(h, h + 1)
        m_blk = jnp.max(s, axis=0, keepdims=True)
        if first:
            m_new = m_blk
        else:
            m_new = jnp.maximum(m_sc[hr, :], m_blk)
            a = jnp.exp2(m_sc[hr, :] - m_new)
        p = jnp.exp2(s - m_new)
        psum = jnp.sum(p, axis=0, keepdims=True)
        pv = _dot(vtblk, p.astype(BF16))
        m_sc[hr, :] = m_new
        if first:
            l_sc[hr, :] = psum
            ot_sc[vsl[h], :] = pv
        else:
            l_sc[hr, :] = l_sc[hr, :] * a + psum
            ot_sc[vsl[h], :] = ot_sc[vsl[h], :] * a + pv

    nm = km_ref.shape[0]
    for h in range(A_HEADS):
        s_sc[h, 0:nm, :] = _dot_nt(km_ref[:, hsl[h]], q_ref[:, hsl[h]])
    for h in range(A_HEADS):
        softmax_pv(h, s_sc[h, 0:nm, :], vtm_ref[vsl[h], :], True)

    def block(r0, mask):
        for h in range(A_HEADS):
            s_sc[h] = _dot_nt(k_ref[pl.ds(r0, tk), hsl[h]], q_ref[:, hsl[h]])
        for h in range(A_HEADS):
            s = s_sc[h]
            if mask is not None:
                s = jnp.where(mask, s, -jnp.inf)
            softmax_pv(h, s, vt_ref[vsl[h], pl.ds(r0, tk)], False)

    def body(kb, _):
        block(pl.multiple_of(kb * tk, tk), None)
        return 0

    lax.fori_loop(0, i * nsub, body, 0)
    for j in range(nsub):
        block(pl.multiple_of(i * tq + j * tk, tk), key + j * tk <= qry)
    for h in range(A_HEADS):
        ot_sc[vsl[h], :] = ot_sc[vsl[h], :] / l_sc[h:h + 1, :]
    o_ref[...] = ot_sc[...].T


def _attn(q, k, vt, km, vtm, batch, seq):
    nq = seq // ATTN_TQ
    return pl.pallas_call(
        _attn_kernel, grid=(batch, nq),
        in_specs=[pl.BlockSpec((ATTN_TQ, A_HEADS * HEAD_PAD), lambda b, i: (b * nq + i, 0)),
                  pl.BlockSpec((seq, A_HEADS * HEAD_PAD), lambda b, i: (b, 0)),
                  pl.BlockSpec((A_WIDTH, seq), lambda b, i: (0, b)),
                  _full(km.shape), _full(vtm.shape)],
        out_specs=pl.BlockSpec((ATTN_TQ, A_WIDTH), lambda b, i: (b * nq + i, 0)),
        out_shape=jax.ShapeDtypeStruct((batch * seq, A_WIDTH), F32),
        scratch_shapes=[pltpu.VMEM((A_WIDTH, ATTN_TQ), F32), pltpu.VMEM((A_HEADS, ATTN_TK, ATTN_TQ), F32),
                        pltpu.VMEM((A_HEADS, ATTN_TQ), F32), pltpu.VMEM((A_HEADS, ATTN_TQ), F32)],
        compiler_params=_params(2), name="prompt_attn")(q, k, vt, km, vtm)


def _split3(x):
    a = x.astype(BF16)
    r = x - a.astype(F32)
    b = r.astype(BF16)
    c = (r - b.astype(F32)).astype(BF16)
    return a, b, c


def _hgrn_chunk(q, kk, v, g, st, tril, chunk, cum_ref, krow):
    g1, g2, g3 = _split3(g)
    cum = (_dot(tril, g1) + _dot(tril, g2) + _dot(tril, g3)) * LOG2E
    cum_ref[...] = cum
    yield None
    nblk = chunk // 8
    cb = [cum[8 * j:8 * j + 8] for j in range(nblk)]
    qb = [q[8 * j:8 * j + 8] for j in range(nblk)]
    lane = lax.broadcasted_iota(jnp.int32, (8, chunk), 1)
    sub = lax.broadcasted_iota(jnp.int32, (8, chunk), 0)
    ab = []
    for j in range(nblk):
        a = jnp.zeros((8, chunk), F32)
        for s in range(8 * j, 8 * j + 8):
            col = jnp.sum(jnp.exp2(cb[j] - cum_ref[s:s + 1, :]) * qb[j] * krow(s), axis=-1, keepdims=True)
            a = jnp.where(lane == s, col, a)
        ab.append(jnp.where(lane <= sub + 8 * j, a, 0.0))
        yield None
    attn = jnp.concatenate(ab, axis=0)
    row = lax.broadcasted_iota(jnp.int32, (chunk, chunk), 0)
    col = lax.broadcasted_iota(jnp.int32, (chunk, chunk), 1)
    dead = jnp.full((8, B_DK), -jnp.inf, F32)
    w = 8
    while w < chunk:
        eq, ek = [], []
        for j in range(nblk):
            blk = (8 * j) // w
            if blk % 2:
                eq.append(cb[j] - cum_ref[blk * w - 1:blk * w, :])
                ek.append(dead)
            else:
                eq.append(dead)
                ek.append(cum_ref[blk * w + w - 1:blk * w + w, :] - cb[j])
        qa = (q * jnp.exp2(jnp.concatenate(eq, axis=0))).astype(BF16)
        ka = (kk * jnp.exp2(jnp.concatenate(ek, axis=0))).astype(BF16)
        sh = w.bit_length() - 1
        rb, cbk = row >> sh, col >> sh
        attn = attn + jnp.where((rb == cbk + 1) & ((cbk & 1) == 0), _dot_nt(qa, ka), 0.0)
        w *= 2
        yield None
    last = cum_ref[chunk - 1:chunk, :]
    o = _dot(attn.astype(BF16), v.astype(BF16)) + _dot_nt((q * jnp.exp2(cum)).astype(BF16), st.astype(BF16))
    kdec = (kk * jnp.exp2(last - cum)).astype(BF16)
    upd = lax.dot_general(v.astype(BF16), kdec, (((0,), (0,)), ((), ())), preferred_element_type=F32)
    yield o, jnp.exp2(last) * st + upd


def _hgrn_kernel(q_ref, k_ref, v_ref, g_ref, st0_ref, o_ref, st_ref, st_sc, cum_sc, kk_sc, *, chunk, tile,
                 transpose_out):
    t = pl.program_id(1)

    @pl.when(t == 0)
    def _():
        st_sc[...] = st0_ref[...]

    r = lax.broadcasted_iota(jnp.int32, (chunk, chunk), 0)
    c = lax.broadcasted_iota(jnp.int32, (chunk, chunk), 1)
    tril = (c <= r).astype(BF16)

    def body(ci, _):
        r0 = pl.multiple_of(ci * chunk, chunk)
        hsl = [slice(h * B_DK, (h + 1) * B_DK) for h in range(B_HEADS)]
        stages = {}
        for h in range(B_HEADS):
            kk = k_ref[pl.ds(r0, chunk), hsl[h]]
            kk_sc[h] = kk
            krow = lambda s, h=h: kk_sc[h, s:s + 1, :]
            stages[h] = _hgrn_chunk(q_ref[pl.ds(r0, chunk), hsl[h]], kk,
                                    v_ref[pl.ds(r0, chunk), hsl[h]], g_ref[pl.ds(r0, chunk), hsl[h]],
                                    st_sc[h], tril, chunk, cum_sc.at[h], krow)
        while stages:
            for h in list(stages):
                out = next(stages[h])
                if out is not None:
                    o_ref[pl.ds(r0, chunk), hsl[h]] = out[0]
                    st_sc[h] = out[1]
                    del stages[h]
        return 0

    lax.fori_loop(0, tile // chunk, body, 0)

    @pl.when(t == pl.num_programs(1) - 1)
    def _():
        for h in range(B_HEADS):
            st_ref[0, h] = st_sc[h].T if transpose_out else st_sc[h]


def _hgrn(hq, hk, hv, hg, st0, batch, seq, chunk, tile, transpose_out):
    nt = seq // tile
    row = pl.BlockSpec((tile, B_FDIM), lambda b, t: (b * nt + t, 0))
    return pl.pallas_call(
        functools.partial(_hgrn_kernel, chunk=chunk, tile=tile, transpose_out=transpose_out),
        grid=(batch, nt),
        in_specs=[row, row, row, row, _full(st0.shape)],
        out_specs=[row, pl.BlockSpec((1, B_HEADS, B_DV, B_DK), lambda b, t: (b, 0, 0, 0))],
        out_shape=[jax.ShapeDtypeStruct((batch * seq, B_WIDTH), F32),
                   jax.ShapeDtypeStruct((batch, B_HEADS, B_DV, B_DK), F32)],
        scratch_shapes=[pltpu.VMEM((B_HEADS, B_DV, B_DK), F32), pltpu.VMEM((B_HEADS, chunk, B_DK), F32),
                        pltpu.VMEM((B_HEADS, chunk, B_DK), F32)],
        compiler_params=_params(2), name="hgrn_chunks")(hq, hk, hv, hg, st0)


def _split2(x):
    hi = x.astype(BF16)
    return hi, (x - hi.astype(F32)).astype(BF16)


def _paged_kernel(pt_ref, lat_hbm, kr_hbm, qrow_ref, qr_ref, qh_ref, kh_ref, latn_ref, wukt_ref,
                  gkrow_ref, gkr_ref, wuv_ref, o_ref, lhs, latbuf, krbuf, cbuf, kx_sc, sems, *, layer, n_pages):
    b = pl.program_id(0)
    gp = PAGES_PER_GROUP
    n_groups = n_pages // gp
    gpos = gp * PAGE
    nk = A_HEADS * A_NOPE
    log2e = 1.4426950408889634
    sc2 = (A_QK ** -0.5) * log2e

    def page_copies(bb, g, slot):
        out = []
        for j in range(gp):
            page = pt_ref[bb, g * gp + j]
            dst = pl.ds(j * PAGE, PAGE)
            out.append(pltpu.make_async_copy(lat_hbm.at[layer, page], latbuf.at[slot, dst], sems.at[0, slot]))
            out.append(pltpu.make_async_copy(kr_hbm.at[layer, page], krbuf.at[slot, j], sems.at[1, slot]))
        return out

    def start(bb, g, slot):
        for cp in page_copies(bb, g, slot):
            cp.start()

    def wait(g, slot):
        for cp in page_copies(b, g, slot):
            cp.wait()

    ring = DMA_SLOTS
    ahead = ring - 1
    assert n_groups % ring == 0 and ahead <= n_groups

    def fetch_ahead(g):
        gg = g + ahead
        wrap = gg >= n_groups
        bb = jnp.where(wrap, b + 1, b)
        g2 = jnp.where(wrap, gg - n_groups, gg)

        @pl.when(bb < pl.num_programs(0))
        def _():
            start(bb, g2, lax.rem(gg, ring))

    @pl.when(b == 0)
    def _():
        for d in range(ahead):
            start(0, d, d)

    @pl.when(b == 0)
    def _():
        lhs[0:nk, :] = wukt_ref[...]

    hrow = lax.broadcasted_iota(jnp.int32, (A_HEADS, nk), 0)
    hcol = lax.broadcasted_iota(jnp.int32, (A_HEADS, nk), 1) // A_NOPE
    qsel = jnp.where(hrow == hcol, qrow_ref[0] * gkrow_ref[...], 0.0)
    q_hi, q_lo = _split2(qsel)
    qabs = _dot(q_hi, wukt_ref[...]) + _dot(q_lo, wukt_ref[...])
    lhs[nk:nk + 2 * A_HEADS, :] = jnp.concatenate(_split2(qabs), axis=0)
    qrg = jnp.concatenate(_split2(qr_ref[0] * gkr_ref[...]), axis=0)
    ppb = POS_BLOCK // PAGE

    def scores(slot, cslot):
        parts = []
        for j in range(gpos // POS_BLOCK):
            rows = pl.ds(j * POS_BLOCK, POS_BLOCK)
            cb = latbuf[slot, rows, :].astype(BF16)
            cbuf[cslot, rows, :] = cb
            kx_sc[j] = _dot_nt(lhs[...], cb)
        for j in range(gpos // POS_BLOCK):
            kn = kx_sc[j, 0:nk, :]
            ss = jnp.sum((kn * kn).reshape(A_HEADS, A_NOPE, POS_BLOCK), axis=1)
            raw = kx_sc[j, nk:nk + A_HEADS, :] + kx_sc[j, nk + A_HEADS:nk + 2 * A_HEADS, :]
            krt = jnp.concatenate([krbuf[slot, j * ppb + t] for t in range(ppb)], axis=-1)
            rr = _dot(qrg, krt.astype(BF16))
            raw = raw + rr[:A_HEADS] + rr[A_HEADS:]
            ss = ss + jnp.sum(krt * krt, axis=0, keepdims=True)
            parts.append(raw * lax.rsqrt(ss * (1.0 / A_QK) + EPS) * sc2)
        return jnp.concatenate(parts, axis=-1)

    def absorb(s, slot, carry):
        m, l, acc = carry
        m_new = jnp.maximum(m, jnp.max(s, axis=-1, keepdims=True))
        a = jnp.exp2(m - m_new)
        p = jnp.exp2(s - m_new)
        l = l * a + jnp.sum(p, axis=-1, keepdims=True)
        acc = acc * a + _dot(p.astype(BF16), cbuf[slot])
        return m_new, l, acc

    wait(0, 0)
    fetch_ahead(0)
    s0 = scores(0, 0)
    init = (jnp.full((A_HEADS, 1), -jnp.inf, F32), jnp.zeros((A_HEADS, 1), F32),
            jnp.zeros((A_HEADS, KV_LORA), F32))

    def group(g, carry):
        s_prev, state = carry
        cslot = lax.rem(g, 2)
        wait(g, lax.rem(g, ring))
        fetch_ahead(g)
        s_cur = scores(lax.rem(g, ring), cslot)
        return s_cur, absorb(s_prev, 1 - cslot, state)

    s_last, state = lax.fori_loop(1, n_groups, group, (s0, init))
    m, l, acc = absorb(s_last, (n_groups - 1) % 2, state)

    s_new = jnp.sum(qh_ref[0].astype(F32) * kh_ref[0].astype(F32), axis=-1, keepdims=True) * sc2
    m_new = jnp.maximum(m, s_new)
    a = jnp.exp2(m - m_new)
    p = jnp.exp2(s_new - m_new)
    acc = acc * a + p * latn_ref[0]
    l = l * a + p
    o_lat = (acc / l).astype(BF16)
    full = _dot(o_lat, wuv_ref[...])
    vrow = lax.broadcasted_iota(jnp.int32, (A_HEADS, A_WIDTH), 0)
    vcol = lax.broadcasted_iota(jnp.int32, (A_HEADS, A_WIDTH), 1) // A_V
    o_ref[0] = jnp.sum(jnp.where(vrow == vcol, full, 0.0), axis=0, keepdims=True)


def _paged(page_table, cache_lat, cache_kr, qrow, qr, qh, kh, latn, wukt, gkrow, gkr, wuv, layer):
    nb, n_pages = page_table.shape
    gp = PAGES_PER_GROUP
    per_b = lambda shape: pl.BlockSpec((1,) + shape, lambda b, pt: (b,) + (0,) * len(shape))
    full = lambda a: pl.BlockSpec(a.shape, lambda b, pt: (0,) * a.ndim)
    grid_spec = pltpu.PrefetchScalarGridSpec(
        num_scalar_prefetch=1, grid=(nb,),
        in_specs=[pl.BlockSpec(memory_space=pl.ANY), pl.BlockSpec(memory_space=pl.ANY),
                  per_b((1, A_HEADS * A_NOPE)), per_b((A_HEADS, A_ROPE)), per_b((A_HEADS, HEAD_PAD)),
                  per_b((A_HEADS, HEAD_PAD)), per_b((1, KV_LORA)), full(wukt), full(gkrow), full(gkr), full(wuv)],
        out_specs=per_b((1, A_WIDTH)),
        scratch_shapes=[pltpu.VMEM((A_HEADS * A_NOPE + 2 * A_HEADS, KV_LORA), BF16),
                        pltpu.VMEM((DMA_SLOTS, gp * PAGE, KV_LORA), F32), pltpu.VMEM((DMA_SLOTS, gp, A_ROPE, PAGE), F32),
                        pltpu.VMEM((2, gp * PAGE, KV_LORA), BF16),
                        pltpu.VMEM((gp * PAGE // POS_BLOCK, A_HEADS * A_NOPE + 2 * A_HEADS, POS_BLOCK), F32),
                        pltpu.SemaphoreType.DMA((2, DMA_SLOTS))])
    return pl.pallas_call(
        functools.partial(_paged_kernel, layer=layer, n_pages=n_pages), grid_spec=grid_spec,
        out_shape=jax.ShapeDtypeStruct((nb, 1, A_WIDTH), F32),
        compiler_params=_params(1), name="paged_attn")(
            page_table, cache_lat, cache_kr, qrow, qr, qh, kh, latn, wukt, gkrow, gkr, wuv)


def _hstep_kernel(s_ref, q_ref, k_ref, g_ref, v_ref, o_ref, sn_ref, *, layer, rows):
    r = lax.broadcasted_iota(jnp.int32, (B_DK, B_DK), 0)
    c = lax.broadcasted_iota(jnp.int32, (B_DK, B_DK), 1)
    eye = r == c

    def column(row):
        return jnp.sum(jnp.where(eye, row, 0.0), axis=-1, keepdims=True)

    for i in range(rows):
        for h in range(B_HEADS):
            hs = slice(h * B_DK, (h + 1) * B_DK)
            sn = column(jnp.exp(g_ref[i, :, hs])) * s_ref[layer, i, h] + column(k_ref[i, :, hs]) * v_ref[i, :, hs]
            sn_ref[i, h] = sn
            o_ref[i, :, hs] = jnp.sum(column(q_ref[i, :, hs]) * sn, axis=0, keepdims=True)


def _hstep(state, q, k, g, v, layer):
    nb = state.shape[1]
    rows = HSTEP_ROWS
    st_in = pl.BlockSpec((state.shape[0], rows, B_HEADS, B_DK, B_DV), lambda b: (0, b, 0, 0, 0))
    st_out = pl.BlockSpec((rows, B_HEADS, B_DK, B_DV), lambda b: (b, 0, 0, 0))
    row = pl.BlockSpec((rows, 1, B_FDIM), lambda b: (b, 0, 0))
    return pl.pallas_call(
        functools.partial(_hstep_kernel, layer=layer, rows=rows), grid=(nb // rows,),
        in_specs=[st_in, row, row, row, row], out_specs=[row, st_out],
        out_shape=[jax.ShapeDtypeStruct((nb, 1, B_WIDTH), F32), jax.ShapeDtypeStruct(state.shape[1:], F32)],
        compiler_params=_params(1), name="hgrn_step")(state, q, k, g, v)


def _merge_kernel(x_ref, ng_ref, wg_ref, oa_ref, ob_ref, gbn_ref, woa_ref, wob_ref, wo_ref, y_ref):
    x = x_ref[...]
    xn = _rms(x, ng_ref[...]).astype(BF16)
    gates = _dot(xn, wg_ref[...])
    ga, gb = gates[:, :A_WIDTH], gates[:, A_WIDTH:A_WIDTH + B_WIDTH]
    ma = gates[:, A_WIDTH + B_WIDTH:A_WIDTH + B_WIDTH + D_MODEL]
    mb = gates[:, A_WIDTH + B_WIDTH + D_MODEL:]
    ya = _dot((oa_ref[...] * (ga * jax.nn.sigmoid(ga))).astype(BF16), woa_ref[...])
    gbn = gbn_ref[...]
    obn = jnp.concatenate([_rms(ob_ref[:, h * B_DV:(h + 1) * B_DV], gbn) for h in range(B_HEADS)], axis=-1)
    yb = _dot((obn * (gb * jax.nn.sigmoid(gb))).astype(BF16), wob_ref[...])
    mix = jax.nn.sigmoid(ma) * ya + jax.nn.sigmoid(mb) * yb
    y_ref[...] = x + _dot(mix.astype(BF16), wo_ref[...])


def _merge(x, oa, ob, w, tm):
    rows = x.shape[0]
    row = lambda n: pl.BlockSpec((tm, n), lambda i: (i, 0))
    ins = [x, w["norm_g"], w["wg"], oa, ob, w["g_bn"], w["w_oa"], w["w_ob"], w["w_o"]]
    in_specs = [row(D_MODEL), _full(ins[1].shape), _full(ins[2].shape), row(A_WIDTH), row(B_WIDTH)] + \
               [_full(a.shape) for a in ins[5:]]
    return pl.pallas_call(
        _merge_kernel, grid=(rows // tm,), in_specs=in_specs, out_specs=row(D_MODEL),
        out_shape=jax.ShapeDtypeStruct((rows, D_MODEL), F32),
        compiler_params=_params(1), name="merge_out")(*ins)


def _head_pad_cols(w3, width):
    pad = jnp.zeros(w3.shape[:2] + (HEAD_PAD - width,), w3.dtype)
    return jnp.concatenate([w3, pad], axis=-1).reshape(w3.shape[0], -1)


def _rope_tables(pos):
    half = A_ROPE // 2
    inv = ROPE_THETA ** (-jnp.arange(half, dtype=F32) / half)
    ang = pos.astype(F32)[:, None] * inv
    cos, sin = jnp.cos(ang), jnp.sin(ang)
    n = pos.shape[0]
    one, zero = jnp.ones((n, A_NOPE), F32), jnp.zeros((n, A_NOPE), F32)
    tail = jnp.zeros((n, HEAD_PAD - A_QK), F32)
    z16 = jnp.zeros((n, half), F32)
    c = jnp.concatenate([one, cos, cos, tail], axis=-1)
    s1 = jnp.concatenate([zero, -sin, z16, tail], axis=-1)
    s2 = jnp.concatenate([zero, z16, sin, tail], axis=-1)
    return c, s1, s2


def _lane_gain(g):
    return jnp.concatenate([g, jnp.zeros((HEAD_PAD - A_QK,), g.dtype)])[None, :]


def kernel(x_prompt, x_sample, cache_latent, cache_krope, state_hgrn, page_table, meta_tokens,
           norm_g, w_in, g_cq, w_uq, g_ckv, w_uk, w_uv, g_qn, g_kn, lb_logits, g_bn, w_oa, w_ob, w_o):
    batch, seq, _ = x_prompt.shape
    dec_batch, dec_seq, _ = x_sample.shape
    depth = w_in.shape[0]
    assert depth == 1 and dec_seq == 1
    past_len = page_table.shape[1] * PAGE
    l = 0

    o0 = Q_LORA
    o1 = o0 + KV_LORA
    o2 = o1 + A_ROPE
    o3 = o2 + 3 * B_FDIM
    wi = w_in[l]
    wkr = jnp.zeros((D_MODEL, HEAD_PAD), F32).at[:, A_NOPE:A_QK].set(wi[:, o1:o2])
    w = {
        "norm_g": norm_g[l][None, :],
        "wa": jnp.concatenate([wi[:, :o1], wkr, wi[:, o2:o3]], axis=1).astype(BF16), "wg": wi[:, o3:].astype(BF16),
        "g_cq": g_cq[l][None, :], "g_ckv": g_ckv[l][None, :],
        "w_uq": _head_pad_cols(w_uq[l].reshape(Q_LORA, A_HEADS, A_QK), A_QK).astype(BF16),
        "w_uk": _head_pad_cols(w_uk[l], A_NOPE).astype(BF16),
        "w_uv": w_uv[l].reshape(KV_LORA, A_WIDTH).astype(BF16),
        "w_uvt": w_uv[l].reshape(KV_LORA, A_WIDTH).T.astype(BF16),
        "g_qn": _lane_gain(g_qn[l]), "g_kn": _lane_gain(g_kn[l]),
        "lb_logits": lb_logits, "g_bn": g_bn[l][None, :],
        "w_oa": w_oa[l].astype(BF16), "w_ob": w_ob[l].astype(BF16), "w_o": w_o[l].astype(BF16),
    }

    xp = x_prompt.reshape(batch * seq, D_MODEL)
    tabs_p = _rope_tables(N_META + jnp.arange(seq))
    qscale = (A_QK ** -0.5) * 1.4426950408889634
    lat_p, krp_p, q_p, k_p, vt_p, hq, hk, hv, hg = _pre(xp, tabs_p, w, PRE_TILE, seq // PRE_TILE, l, qscale)
    tabs_m = _rope_tables(jnp.arange(N_META))
    lat_m, krp_m, _, k_m, vt_m, mq, mk, mv, mg = _pre(meta_tokens, tabs_m, w, N_META, 1, l)

    oa = _attn(q_p, k_p, vt_p, k_m, vt_m, batch, seq)
    zero_state = jnp.zeros((B_HEADS, B_DV, B_DK), F32)
    _, st_meta = _hgrn(mq, mk, mv, mg, zero_state, 1, N_META, N_META, N_META, False)
    ob, st_fin = _hgrn(hq, hk, hv, hg, st_meta[0], batch, seq, HGRN_CHUNK, HGRN_TILE, True)
    y_prompt = _merge(xp, oa, ob, w, ROW_TILE).reshape(batch, seq, D_MODEL)

    lat_all = jnp.concatenate([jnp.broadcast_to(lat_m[None], (batch, N_META, KV_LORA)),
                               lat_p.reshape(batch, seq, KV_LORA)], axis=1)
    kr_all = jnp.concatenate([jnp.broadcast_to(krp_m[None, :, A_NOPE:A_QK], (batch, N_META, A_ROPE)),
                              krp_p[:, A_NOPE:A_QK].reshape(batch, seq, A_ROPE)], axis=1)

    xs = x_sample.reshape(dec_batch, D_MODEL)
    tabs_s = _rope_tables(jnp.full((dec_batch,), past_len, jnp.int32))
    lat_s, krp_s, q_s, k_s, _, sq, sk, sv, sg = _pre(xs, tabs_s, w, dec_batch, 1, l)
    q_s3 = q_s.reshape(dec_batch, A_HEADS, HEAD_PAD)
    k_s3 = k_s.reshape(dec_batch, A_HEADS, HEAD_PAD)
    qrow = q_s3[:, :, :A_NOPE].astype(F32).reshape(dec_batch, 1, A_HEADS * A_NOPE)
    qrope = q_s3[:, :, A_NOPE:A_QK].astype(F32)
    gkrow = jnp.tile(g_kn[l][:A_NOPE], A_HEADS)[None, :]
    gkr = g_kn[l][None, A_NOPE:]
    wukt = w_uk[l].reshape(KV_LORA, A_HEADS * A_NOPE).T.astype(BF16)
    oa_s = _paged(page_table, cache_latent, jnp.swapaxes(cache_krope, 2, 3), qrow, qrope, q_s3, k_s3,
                  lat_s[:, None, :], wukt, gkrow, gkr, w["w_uv"], l)
    row3 = lambda a: a.reshape(dec_batch, 1, B_FDIM)
    ob_s, st_s = _hstep(state_hgrn, row3(sq), row3(sk), row3(sg), row3(sv), l)
    y_sample = _merge(xs, oa_s.reshape(dec_batch, A_WIDTH), ob_s.reshape(dec_batch, B_WIDTH), w, dec_batch)

    return (y_prompt, y_sample.reshape(dec_batch, dec_seq, D_MODEL),
            lat_all[None], kr_all[None], st_fin[None],
            lat_s.reshape(1, dec_batch, dec_seq, KV_LORA),
            krp_s[:, A_NOPE:A_QK].reshape(1, dec_batch, dec_seq, A_ROPE),
            st_s[None])
```

```python
import functools

import jax
import jax.numpy as jnp
from jax import lax
from jax.experimental import pallas as pl
from jax.experimental.pallas import tpu as pltpu

F32 = jnp.float32
BF16 = jnp.bfloat16

D_MODEL = 1024
N_META = 16
A_HEADS = 8
A_NOPE = 64
A_ROPE = 32
A_QK = A_NOPE + A_ROPE
A_V = 64
A_WIDTH = A_HEADS * A_V
Q_LORA = 384
KV_LORA = 256
ROPE_THETA = 10000.0
B_HEADS = 4
B_DK = 128
B_DV = 128
B_FDIM = B_HEADS * B_DK
B_WIDTH = B_HEADS * B_DV
EPS = 1e-6
PAGE = 128

LANES = 128
LOG2E = 1.4426950408889634
HEAD_PAD = LANES
VMEM_LIMIT = 56 * 1024 * 1024

PRE_TILE = 256
ROW_TILE = 256
ATTN_TQ = 256
ATTN_TK = 256
HGRN_CHUNK = 64
HGRN_TILE = 256
PAGES_PER_GROUP = 16
HSTEP_ROWS = 4
POS_BLOCK = 2 * PAGE
DMA_SLOTS = 4


def _full(shape):
    return pl.BlockSpec(shape, lambda *_: (0,) * len(shape))


def _params(n_axes):
    return pltpu.CompilerParams(dimension_semantics=("arbitrary",) * n_axes,
                                vmem_limit_bytes=VMEM_LIMIT)


def _dot(a, b):
    return jnp.dot(a, b, preferred_element_type=F32)


def _dot_nt(a, b, precision=None):
    return lax.dot_general(a, b, (((1,), (1,)), ((), ())), precision=precision,
                           preferred_element_type=F32)


def _rms(x, g):
    r = lax.rsqrt(jnp.mean(x * x, axis=-1, keepdims=True) + EPS)
    return (x * r) * g


def _rope_lanes(t, c, s1, s2):
    return t * c + pltpu.roll(t, LANES - A_ROPE // 2, axis=1) * s1 + pltpu.roll(t, A_ROPE // 2, axis=1) * s2


def _pre_kernel(x_ref, ng_ref, wa_ref, gcq_ref, wuq_ref, gckv_ref,
                wuk_ref, wuv_ref, gqn_ref, gkn_ref, c_ref, s1_ref, s2_ref, lb_ref,
                lat_ref, krp_ref, q_ref, k_ref, vt_ref, hq_ref, hk_ref, hv_ref, hg_ref, *, layer, qscale):
    x = x_ref[...]
    xn = _rms(x, ng_ref[...]).astype(BF16)
    c, s1, s2 = c_ref[...], s1_ref[...], s2_ref[...]

    o1, o2, o3 = Q_LORA, Q_LORA + KV_LORA, Q_LORA + KV_LORA + LANES
    y = _dot(xn, wa_ref[:, :o3])
    cq = y[:, :o1]
    cqn = _rms(cq, gcq_ref[...]).astype(BF16)
    qraw = _dot(cqn, wuq_ref[...])
    gqn = gqn_ref[...]
    for h in range(A_HEADS):
        t = _rope_lanes(qraw[:, h * HEAD_PAD:(h + 1) * HEAD_PAD], c, s1, s2)
        r = lax.rsqrt(jnp.sum(t * t, axis=-1, keepdims=True) * (1.0 / A_QK) + EPS)
        q_ref[:, h * HEAD_PAD:(h + 1) * HEAD_PAD] = (((t * r) * gqn) * qscale).astype(BF16)

    ckv = _rms(y[:, o1:o2], gckv_ref[...])
    lat_ref[...] = ckv
    ckvb = ckv.astype(BF16)
    krp = _rope_lanes(y[:, o2:o3], c, s1, s2)
    krp_ref[...] = krp
    kraw = _dot(ckvb, wuk_ref[...])
    gkn = gkn_ref[...]
    for h in range(A_HEADS):
        t = kraw[:, h * HEAD_PAD:(h + 1) * HEAD_PAD] + krp
        r = lax.rsqrt(jnp.sum(t * t, axis=-1, keepdims=True) * (1.0 / A_QK) + EPS)
        k_ref[:, h * HEAD_PAD:(h + 1) * HEAD_PAD] = ((t * r) * gkn).astype(BF16)
    vt_ref[...] = _dot_nt(wuv_ref[...], ckvb).astype(BF16)

    lbl = lb_ref[...]
    e = jnp.exp(lbl - jnp.max(lbl, axis=0, keepdims=True))
    lb = jnp.sum(e[:layer + 1], axis=0, keepdims=True) / jnp.sum(e, axis=0, keepdims=True)
    b = _dot(xn, wa_ref[:, o3:])
    bq, z, bi = b[:, :B_FDIM], b[:, B_FDIM:2 * B_FDIM], b[:, 2 * B_FDIM:]
    hq_ref[...] = bq * jax.nn.sigmoid(bq)
    hg_ref[...] = jnp.log(lb + (1.0 - lb) * jax.nn.sigmoid(z))
    hk_ref[...] = (1.0 - lb) * jax.nn.sigmoid(-z)
    hv_ref[...] = bi


def _pre(x, tabs, w, tm, tab_blocks, layer, qscale=1.0):
    rows = x.shape[0]
    grid = (rows // tm,)
    row = lambda n: pl.BlockSpec((tm, n), lambda i: (i, 0))
    tab = pl.BlockSpec((tm, LANES), lambda i: (i % tab_blocks, 0))
    ins = [x, w["norm_g"], w["wa"], w["g_cq"], w["w_uq"], w["g_ckv"],
           w["w_uk"], w["w_uvt"], w["g_qn"], w["g_kn"], tabs[0], tabs[1], tabs[2], w["lb_logits"]]
    in_specs = [row(D_MODEL)] + [_full(a.shape) for a in ins[1:10]] + [tab, tab, tab] + [_full(ins[13].shape)]
    outs = [(KV_LORA, F32), (LANES, F32), (A_HEADS * HEAD_PAD, BF16), (A_HEADS * HEAD_PAD, BF16),
            None, (B_FDIM, F32), (B_FDIM, F32), (B_WIDTH, F32), (B_FDIM, F32)]
    out_specs = [pl.BlockSpec((A_WIDTH, tm), lambda i: (0, i)) if o is None else row(o[0]) for o in outs]
    out_shape = [jax.ShapeDtypeStruct((A_WIDTH, rows), BF16) if o is None else
                 jax.ShapeDtypeStruct((rows, o[0]), o[1]) for o in outs]
    return pl.pallas_call(
        functools.partial(_pre_kernel, layer=layer, qscale=qscale),
        grid=grid, in_specs=in_specs, out_specs=out_specs, out_shape=out_shape,
        compiler_params=_params(1), name="pre_proj")(*ins)


def _attn_kernel(q_ref, k_ref, vt_ref, km_ref, vtm_ref, o_ref, ot_sc, s_sc, sm_sc, m_sc, l_sc):
    i = pl.program_id(1)
    tq, tk = ATTN_TQ, ATTN_TK
    assert tq == tk
    key = lax.broadcasted_iota(jnp.int32, (tk, tq), 0)
    qry = lax.broadcasted_iota(jnp.int32, (tk, tq), 1)
    hsl = [slice(h * HEAD_PAD, (h + 1) * HEAD_PAD) for h in range(A_HEADS)]
    vsl = [slice(h * A_V, (h + 1) * A_V) for h in range(A_HEADS)]

    d0 = pl.multiple_of(i * tq, tq)
    for h in range(A_HEADS):
        s_sc[h, 0:tk, :] = _dot_nt(k_ref[pl.ds(d0, tk), hsl[h]], q_ref[:, hsl[h]])
        sm_sc[h] = _dot_nt(km_ref[:, hsl[h]], q_ref[:, hsl[h]])
    for h in range(A_HEADS):
        hr = slice(h, h + 1)
        s = jnp.where(key <= qry, s_sc[h, 0:tk, :], -jnp.inf)
        sm = sm_sc[h]
        m_new = jnp.maximum(jnp.max(s, axis=0, keepdims=True), jnp.max(sm, axis=0, keepdims=True))
        p = jnp.exp2(s - m_new)
        pm = jnp.exp2(sm - m_new)
        m_sc[hr, :] = m_new
        l_sc[hr, :] = jnp.sum(p, axis=0, keepdims=True) + jnp.sum(pm, axis=0, keepdims=True)
        ot_sc[vsl[h], :] = (_dot(vt_ref[vsl[h], pl.ds(d0, tk)], p.astype(BF16)) +
                            _dot(vtm_ref[vsl[h], :], pm.astype(BF16)))

    def visible(r0, keys):
        for h in range(A_HEADS):
            s_sc[h, 0:keys, :] = _dot_nt(k_ref[pl.ds(r0, keys), hsl[h]], q_ref[:, hsl[h]])
        for h in range(A_HEADS):
            hr = slice(h, h + 1)
            s = s_sc[h, 0:keys, :]
            m_new = jnp.maximum(m_sc[hr, :], jnp.max(s, axis=0, keepdims=True))
            a = jnp.exp2(m_sc[hr, :] - m_new)
            p = jnp.exp2(s - m_new)
            m_sc[hr, :] = m_new
            l_sc[hr, :] = l_sc[hr, :] * a + jnp.sum(p, axis=0, keepdims=True)
            ot_sc[vsl[h], :] = ot_sc[vsl[h], :] * a + _dot(vt_ref[vsl[h], pl.ds(r0, keys)], p.astype(BF16))

    def body(kb, _):
        visible(pl.multiple_of(kb * 2 * tk, 2 * tk), 2 * tk)
        return 0

    lax.fori_loop(0, i // 2, body, 0)

    @pl.when(i % 2 == 1)
    def _():
        visible(pl.multiple_of((i - 1) * tk, tk), tk)

    for h in range(A_HEADS):
        ot_sc[vsl[h], :] = ot_sc[vsl[h], :] / l_sc[h:h + 1, :]
    o_ref[...] = ot_sc[...].T


def _attn(q, k, vt, km, vtm, batch, seq):
    nq = seq // ATTN_TQ
    return pl.pallas_call(
        _attn_kernel, grid=(batch, nq),
        in_specs=[pl.BlockSpec((ATTN_TQ, A_HEADS * HEAD_PAD), lambda b, i: (b * nq + i, 0)),
                  pl.BlockSpec((seq, A_HEADS * HEAD_PAD), lambda b, i: (b, 0)),
                  pl.BlockSpec((A_WIDTH, seq), lambda b, i: (0, b)),
                  _full(km.shape), _full(vtm.shape)],
        out_specs=pl.BlockSpec((ATTN_TQ, A_WIDTH), lambda b, i: (b * nq + i, 0)),
        out_shape=jax.ShapeDtypeStruct((batch * seq, A_WIDTH), F32),
        scratch_shapes=[pltpu.VMEM((A_WIDTH, ATTN_TQ), F32), pltpu.VMEM((A_HEADS, 2 * ATTN_TK, ATTN_TQ), F32),
                        pltpu.VMEM((A_HEADS, km.shape[0], ATTN_TQ), F32),
                        pltpu.VMEM((A_HEADS, ATTN_TQ), F32), pltpu.VMEM((A_HEADS, ATTN_TQ), F32)],
        compiler_params=_params(2), name="prompt_attn")(q, k, vt, km, vtm)


def _split3(x):
    a = x.astype(BF16)
    r = x - a.astype(F32)
    b = r.astype(BF16)
    c = (r - b.astype(F32)).astype(BF16)
    return a, b, c


def _hgrn_chunk(q, kk, v, g, st, tril, chunk, cum_ref, krow):
    g1, g2, g3 = _split3(g)
    cum = (_dot(tril, g1) + _dot(tril, g2) + _dot(tril, g3)) * LOG2E
    cum_ref[...] = cum
    yield None
    nblk = chunk // 8
    cb = [cum[8 * j:8 * j + 8] for j in range(nblk)]
    qb = [q[8 * j:8 * j + 8] for j in range(nblk)]
    lane = lax.broadcasted_iota(jnp.int32, (8, chunk), 1)
    sub = lax.broadcasted_iota(jnp.int32, (8, chunk), 0)
    ab = []
    for j in range(nblk):
        a = jnp.zeros((8, chunk), F32)
        for s in range(8 * j, 8 * j + 8):
            col = jnp.sum(jnp.exp2(cb[j] - cum_ref[s:s + 1, :]) * qb[j] * krow(s), axis=-1, keepdims=True)
            a = jnp.where(lane == s, col, a)
        ab.append(jnp.where(lane <= sub + 8 * j, a, 0.0))
        yield None
    attn = jnp.concatenate(ab, axis=0)
    row = lax.broadcasted_iota(jnp.int32, (chunk, chunk), 0)
    col = lax.broadcasted_iota(jnp.int32, (chunk, chunk), 1)
    dead = jnp.full((8, B_DK), -jnp.inf, F32)
    w = 8
    while w < chunk:
        eq, ek = [], []
        for j in range(nblk):
            blk = (8 * j) // w
            if blk % 2:
                eq.append(cb[j] - cum_ref[blk * w - 1:blk * w, :])
                ek.append(dead)
            else:
                eq.append(dead)
                ek.append(cum_ref[blk * w + w - 1:blk * w + w, :] - cb[j])
        qa = (q * jnp.exp2(jnp.concatenate(eq, axis=0))).astype(BF16)
        ka = (kk * jnp.exp2(jnp.concatenate(ek, axis=0))).astype(BF16)
        sh = w.bit_length() - 1
        rb, cbk = row >> sh, col >> sh
        attn = attn + jnp.where((rb == cbk + 1) & ((cbk & 1) == 0), _dot_nt(qa, ka), 0.0)
        w *= 2
        yield None
    last = cum_ref[chunk - 1:chunk, :]
    o = _dot(attn.astype(BF16), v.astype(BF16)) + _dot_nt((q * jnp.exp2(cum)).astype(BF16), st.astype(BF16))
    kdec = (kk * jnp.exp2(last - cum)).astype(BF16)
    upd = lax.dot_general(v.astype(BF16), kdec, (((0,), (0,)), ((), ())), preferred_element_type=F32)
    yield o, jnp.exp2(last) * st + upd


def _hgrn_kernel(q_ref, k_ref, v_ref, g_ref, st0_ref, o_ref, st_ref, st_sc, cum_sc, kk_sc, *, chunk, tile,
                 transpose_out):
    t = pl.program_id(1)

    @pl.when(t == 0)
    def _():
        st_sc[...] = st0_ref[...]

    r = lax.broadcasted_iota(jnp.int32, (chunk, chunk), 0)
    c = lax.broadcasted_iota(jnp.int32, (chunk, chunk), 1)
    tril = (c <= r).astype(BF16)

    def body(ci, _):
        r0 = pl.multiple_of(ci * chunk, chunk)
        hsl = [slice(h * B_DK, (h + 1) * B_DK) for h in range(B_HEADS)]
        stages = {}
        for h in range(B_HEADS):
            kk = k_ref[pl.ds(r0, chunk), hsl[h]]
            kk_sc[h] = kk
            krow = lambda s, h=h: kk_sc[h, s:s + 1, :]
            stages[h] = _hgrn_chunk(q_ref[pl.ds(r0, chunk), hsl[h]], kk,
                                    v_ref[pl.ds(r0, chunk), hsl[h]], g_ref[pl.ds(r0, chunk), hsl[h]],
                                    st_sc[h], tril, chunk, cum_sc.at[h], krow)
        while stages:
            for h in list(stages):
                out = next(stages[h])
                if out is not None:
                    o_ref[pl.ds(r0, chunk), hsl[h]] = out[0]
                    st_sc[h] = out[1]
                    del stages[h]
        return 0

    lax.fori_loop(0, tile // chunk, body, 0)

    @pl.when(t == pl.num_programs(1) - 1)
    def _():
        for h in range(B_HEADS):
            st_ref[0, h] = st_sc[h].T if transpose_out else st_sc[h]


def _hgrn(hq, hk, hv, hg, st0, batch, seq, chunk, tile, transpose_out):
    nt = seq // tile
    row = pl.BlockSpec((tile, B_FDIM), lambda b, t: (b * nt + t, 0))
    return pl.pallas_call(
        functools.partial(_hgrn_kernel, chunk=chunk, tile=tile, transpose_out=transpose_out),
        grid=(batch, nt),
        in_specs=[row, row, row, row, _full(st0.shape)],
        out_specs=[row, pl.BlockSpec((1, B_HEADS, B_DV, B_DK), lambda b, t: (b, 0, 0, 0))],
        out_shape=[jax.ShapeDtypeStruct((batch * seq, B_WIDTH), F32),
                   jax.ShapeDtypeStruct((batch, B_HEADS, B_DV, B_DK), F32)],
        scratch_shapes=[pltpu.VMEM((B_HEADS, B_DV, B_DK), F32), pltpu.VMEM((B_HEADS, chunk, B_DK), F32),
                        pltpu.VMEM((B_HEADS, chunk, B_DK), F32)],
        compiler_params=_params(2), name="hgrn_chunks")(hq, hk, hv, hg, st0)


def _split2(x):
    hi = x.astype(BF16)
    return hi, (x - hi.astype(F32)).astype(BF16)


def _paged_kernel(pt_ref, lat_hbm, kr_hbm, qrow_ref, qr_ref, qh_ref, kh_ref, latn_ref, wukt_ref,
                  gkrow_ref, gkr_ref, wuv_ref, o_ref, lhs, latbuf, krbuf, cbuf, kx_sc, sems, *, layer, n_pages):
    b = pl.program_id(0)
    gp = PAGES_PER_GROUP
    n_groups = n_pages // gp
    gpos = gp * PAGE
    nk = A_HEADS * A_NOPE
    log2e = 1.4426950408889634
    sc2 = (A_QK ** -0.5) * log2e

    def page_copies(bb, g, slot):
        out = []
        for j in range(gp):
            page = pt_ref[bb, g * gp + j]
            dst = pl.ds(j * PAGE, PAGE)
            out.append(pltpu.make_async_copy(lat_hbm.at[layer, page], latbuf.at[slot, dst], sems.at[0, slot]))
            out.append(pltpu.make_async_copy(kr_hbm.at[layer, page], krbuf.at[slot, j], sems.at[1, slot]))
        return out

    def start(bb, g, slot):
        for cp in page_copies(bb, g, slot):
            cp.start()

    def wait(g, slot):
        for cp in page_copies(b, g, slot):
            cp.wait()

    ring = DMA_SLOTS
    ahead = ring - 1
    assert n_groups % ring == 0 and ahead <= n_groups

    def fetch_ahead(g):
        gg = g + ahead
        wrap = gg >= n_groups
        bb = jnp.where(wrap, b + 1, b)
        g2 = jnp.where(wrap, gg - n_groups, gg)

        @pl.when(bb < pl.num_programs(0))
        def _():
            start(bb, g2, lax.rem(gg, ring))

    @pl.when(b == 0)
    def _():
        for d in range(ahead):
            start(0, d, d)

    @pl.when(b == 0)
    def _():
        lhs[0:nk, :] = wukt_ref[...]

    hrow = lax.broadcasted_iota(jnp.int32, (A_HEADS, nk), 0)
    hcol = lax.broadcasted_iota(jnp.int32, (A_HEADS, nk), 1) // A_NOPE
    qsel = jnp.where(hrow == hcol, qrow_ref[0] * gkrow_ref[...], 0.0)
    q_hi, q_lo = _split2(qsel)
    qabs = _dot(q_hi, wukt_ref[...]) + _dot(q_lo, wukt_ref[...])
    lhs[nk:nk + 2 * A_HEADS, :] = jnp.concatenate(_split2(qabs), axis=0)
    qrg = jnp.concatenate(_split2(qr_ref[0] * gkr_ref[...]), axis=0)
    ppb = POS_BLOCK // PAGE

    def scores(slot, cslot):
        parts = []
        for j in range(gpos // POS_BLOCK):
            rows = pl.ds(j * POS_BLOCK, POS_BLOCK)
            cb = latbuf[slot, rows, :].astype(BF16)
            cbuf[cslot, rows, :] = cb
            kx_sc[j] = _dot_nt(lhs[...], cb)
        for j in range(gpos // POS_BLOCK):
            kn = kx_sc[j, 0:nk, :]
            ss = jnp.sum((kn * kn).reshape(A_HEADS, A_NOPE, POS_BLOCK), axis=1)
            raw = kx_sc[j, nk:nk + A_HEADS, :] + kx_sc[j, nk + A_HEADS:nk + 2 * A_HEADS, :]
            krt = jnp.concatenate([krbuf[slot, j * ppb + t] for t in range(ppb)], axis=-1)
            rr = _dot(qrg, krt.astype(BF16))
            raw = raw + rr[:A_HEADS] + rr[A_HEADS:]
            ss = ss + jnp.sum(krt * krt, axis=0, keepdims=True)
            parts.append(raw * lax.rsqrt(ss * (1.0 / A_QK) + EPS) * sc2)
        return jnp.concatenate(parts, axis=-1)

    def absorb(s, slot, carry):
        m, l, acc = carry
        m_new = jnp.maximum(m, jnp.max(s, axis=-1, keepdims=True))
        a = jnp.exp2(m - m_new)
        p = jnp.exp2(s - m_new)
        l = l * a + jnp.sum(p, axis=-1, keepdims=True)
        acc = acc * a + _dot(p.astype(BF16), cbuf[slot])
        return m_new, l, acc

    wait(0, 0)
    fetch_ahead(0)
    s0 = scores(0, 0)
    init = (jnp.full((A_HEADS, 1), -jnp.inf, F32), jnp.zeros((A_HEADS, 1), F32),
            jnp.zeros((A_HEADS, KV_LORA), F32))

    def group(g, carry):
        s_prev, state = carry
        cslot = lax.rem(g, 2)
        wait(g, lax.rem(g, ring))
        fetch_ahead(g)
        s_cur = scores(lax.rem(g, ring), cslot)
        return s_cur, absorb(s_prev, 1 - cslot, state)

    s_last, state = lax.fori_loop(1, n_groups, group, (s0, init))
    m, l, acc = absorb(s_last, (n_groups - 1) % 2, state)

    s_new = jnp.sum(qh_ref[0].astype(F32) * kh_ref[0].astype(F32), axis=-1, keepdims=True) * sc2
    m_new = jnp.maximum(m, s_new)
    a = jnp.exp2(m - m_new)
    p = jnp.exp2(s_new - m_new)
    acc = acc * a + p * latn_ref[0]
    l = l * a + p
    o_lat = (acc / l).astype(BF16)
    full = _dot(o_lat, wuv_ref[...])
    vrow = lax.broadcasted_iota(jnp.int32, (A_HEADS, A_WIDTH), 0)
    vcol = lax.broadcasted_iota(jnp.int32, (A_HEADS, A_WIDTH), 1) // A_V
    o_ref[0] = jnp.sum(jnp.where(vrow == vcol, full, 0.0), axis=0, keepdims=True)


def _paged(page_table, cache_lat, cache_kr, qrow, qr, qh, kh, latn, wukt, gkrow, gkr, wuv, layer):
    nb, n_pages = page_table.shape
    gp = PAGES_PER_GROUP
    per_b = lambda shape: pl.BlockSpec((1,) + shape, lambda b, pt: (b,) + (0,) * len(shape))
    full = lambda a: pl.BlockSpec(a.shape, lambda b, pt: (0,) * a.ndim)
    grid_spec = pltpu.PrefetchScalarGridSpec(
        num_scalar_prefetch=1, grid=(nb,),
        in_specs=[pl.BlockSpec(memory_space=pl.ANY), pl.BlockSpec(memory_space=pl.ANY),
                  per_b((1, A_HEADS * A_NOPE)), per_b((A_HEADS, A_ROPE)), per_b((A_HEADS, HEAD_PAD)),
                  per_b((A_HEADS, HEAD_PAD)), per_b((1, KV_LORA)), full(wukt), full(gkrow), full(gkr), full(wuv)],
        out_specs=per_b((1, A_WIDTH)),
        scratch_shapes=[pltpu.VMEM((A_HEADS * A_NOPE + 2 * A_HEADS, KV_LORA), BF16),
                        pltpu.VMEM((DMA_SLOTS, gp * PAGE, KV_LORA), F32), pltpu.VMEM((DMA_SLOTS, gp, A_ROPE, PAGE), F32),
                        pltpu.VMEM((2, gp * PAGE, KV_LORA), BF16),
                        pltpu.VMEM((gp * PAGE // POS_BLOCK, A_HEADS * A_NOPE + 2 * A_HEADS, POS_BLOCK), F32),
                        pltpu.SemaphoreType.DMA((2, DMA_SLOTS))])
    return pl.pallas_call(
        functools.partial(_paged_kernel, layer=layer, n_pages=n_pages), grid_spec=grid_spec,
        out_shape=jax.ShapeDtypeStruct((nb, 1, A_WIDTH), F32),
        compiler_params=_params(1), name="paged_attn")(
            page_table, cache_lat, cache_kr, qrow, qr, qh, kh, latn, wukt, gkrow, gkr, wuv)


def _hstep_kernel(s_ref, q_ref, k_ref, g_ref, v_ref, o_ref, sn_ref, *, layer, rows):
    r = lax.broadcasted_iota(jnp.int32, (B_DK, B_DK), 0)
    c = lax.broadcasted_iota(jnp.int32, (B_DK, B_DK), 1)
    eye = r == c

    def column(row):
        return jnp.sum(jnp.where(eye, row, 0.0), axis=-1, keepdims=True)

    for i in range(rows):
        for h in range(B_HEADS):
            hs = slice(h * B_DK, (h + 1) * B_DK)
            sn = column(jnp.exp(g_ref[i, :, hs])) * s_ref[layer, i, h] + column(k_ref[i, :, hs]) * v_ref[i, :, hs]
            sn_ref[i, h] = sn
            o_ref[i, :, hs] = jnp.sum(column(q_ref[i, :, hs]) * sn, axis=0, keepdims=True)


def _hstep(state, q, k, g, v, layer):
    nb = state.shape[1]
    rows = HSTEP_ROWS
    st_in = pl.BlockSpec((state.shape[0], rows, B_HEADS, B_DK, B_DV), lambda b: (0, b, 0, 0, 0))
    st_out = pl.BlockSpec((rows, B_HEADS, B_DK, B_DV), lambda b: (b, 0, 0, 0))
    row = pl.BlockSpec((rows, 1, B_FDIM), lambda b: (b, 0, 0))
    return pl.pallas_call(
        functools.partial(_hstep_kernel, layer=layer, rows=rows), grid=(nb // rows,),
        in_specs=[st_in, row, row, row, row], out_specs=[row, st_out],
        out_shape=[jax.ShapeDtypeStruct((nb, 1, B_WIDTH), F32), jax.ShapeDtypeStruct(state.shape[1:], F32)],
        compiler_params=_params(1), name="hgrn_step")(state, q, k, g, v)


def _merge_kernel(x_ref, ng_ref, wg_ref, oa_ref, ob_ref, gbn_ref, woa_ref, wob_ref, wo_ref, y_ref):
    x = x_ref[...]
    xn = _rms(x, ng_ref[...]).astype(BF16)
    gates = _dot(xn, wg_ref[...])
    ga, gb = gates[:, :A_WIDTH], gates[:, A_WIDTH:A_WIDTH + B_WIDTH]
    ma = gates[:, A_WIDTH + B_WIDTH:A_WIDTH + B_WIDTH + D_MODEL]
    mb = gates[:, A_WIDTH + B_WIDTH + D_MODEL:]
    ya = _dot((oa_ref[...] * (ga * jax.nn.sigmoid(ga))).astype(BF16), woa_ref[...])
    gbn = gbn_ref[...]
    obn = jnp.concatenate([_rms(ob_ref[:, h * B_DV:(h + 1) * B_DV], gbn) for h in range(B_HEADS)], axis=-1)
    yb = _dot((obn * (gb * jax.nn.sigmoid(gb))).astype(BF16), wob_ref[...])
    mix = jax.nn.sigmoid(ma) * ya + jax.nn.sigmoid(mb) * yb
    y_ref[...] = x + _dot(mix.astype(BF16), wo_ref[...])


def _merge(x, oa, ob, w, tm):
    rows = x.shape[0]
    row = lambda n: pl.BlockSpec((tm, n), lambda i: (i, 0))
    ins = [x, w["norm_g"], w["wg"], oa, ob, w["g_bn"], w["w_oa"], w["w_ob"], w["w_o"]]
    in_specs = [row(D_MODEL), _full(ins[1].shape), _full(ins[2].shape), row(A_WIDTH), row(B_WIDTH)] + \
               [_full(a.shape) for a in ins[5:]]
    return pl.pallas_call(
        _merge_kernel, grid=(rows // tm,), in_specs=in_specs, out_specs=row(D_MODEL),
        out_shape=jax.ShapeDtypeStruct((rows, D_MODEL), F32),
        compiler_params=_params(1), name="merge_out")(*ins)


def _head_pad_cols(w3, width):
    pad = jnp.zeros(w3.shape[:2] + (HEAD_PAD - width,), w3.dtype)
    return jnp.concatenate([w3, pad], axis=-1).reshape(w3.shape[0], -1)


def _rope_tables(pos):
    half = A_ROPE // 2
    inv = ROPE_THETA ** (-jnp.arange(half, dtype=F32) / half)
    ang = pos.astype(F32)[:, None] * inv
    cos, sin = jnp.cos(ang), jnp.sin(ang)
    n = pos.shape[0]
    one, zero = jnp.ones((n, A_NOPE), F32), jnp.zeros((n, A_NOPE), F32)
    tail = jnp.zeros((n, HEAD_PAD - A_QK), F32)
    z16 = jnp.zeros((n, half), F32)
    c = jnp.concatenate([one, cos, cos, tail], axis=-1)
    s1 = jnp.concatenate([zero, -sin, z16, tail], axis=-1)
    s2 = jnp.concatenate([zero, z16, sin, tail], axis=-1)
    return c, s1, s2


def _lane_gain(g):
    return jnp.concatenate([g, jnp.zeros((HEAD_PAD - A_QK,), g.dtype)])[None, :]


def kernel(x_prompt, x_sample, cache_latent, cache_krope, state_hgrn, page_table, meta_tokens,
           norm_g, w_in, g_cq, w_uq, g_ckv, w_uk, w_uv, g_qn, g_kn, lb_logits, g_bn, w_oa, w_ob, w_o):
    batch, seq, _ = x_prompt.shape
    dec_batch, dec_seq, _ = x_sample.shape
    depth = w_in.shape[0]
    assert depth == 1 and dec_seq == 1
    past_len = page_table.shape[1] * PAGE
    l = 0

    o0 = Q_LORA
    o1 = o0 + KV_LORA
    o2 = o1 + A_ROPE
    o3 = o2 + 3 * B_FDIM
    wi = w_in[l]
    wkr = jnp.zeros((D_MODEL, HEAD_PAD), F32).at[:, A_NOPE:A_QK].set(wi[:, o1:o2])
    w = {
        "norm_g": norm_g[l][None, :],
        "wa": jnp.concatenate([wi[:, :o1], wkr, wi[:, o2:o3]], axis=1).astype(BF16), "wg": wi[:, o3:].astype(BF16),
        "g_cq": g_cq[l][None, :], "g_ckv": g_ckv[l][None, :],
        "w_uq": _head_pad_cols(w_uq[l].reshape(Q_LORA, A_HEADS, A_QK), A_QK).astype(BF16),
        "w_uk": _head_pad_cols(w_uk[l], A_NOPE).astype(BF16),
        "w_uv": w_uv[l].reshape(KV_LORA, A_WIDTH).astype(BF16),
        "w_uvt": w_uv[l].reshape(KV_LORA, A_WIDTH).T.astype(BF16),
        "g_qn": _lane_gain(g_qn[l]), "g_kn": _lane_gain(g_kn[l]),
        "lb_logits": lb_logits, "g_bn": g_bn[l][None, :],
        "w_oa": w_oa[l].astype(BF16), "w_ob": w_ob[l].astype(BF16), "w_o": w_o[l].astype(BF16),
    }

    xp = x_prompt.reshape(batch * seq, D_MODEL)
    tabs_p = _rope_tables(N_META + jnp.arange(seq))
    qscale = (A_QK ** -0.5) * 1.4426950408889634
    lat_p, krp_p, q_p, k_p, vt_p, hq, hk, hv, hg = _pre(xp, tabs_p, w, PRE_TILE, seq // PRE_TILE, l, qscale)
    tabs_m = _rope_tables(jnp.arange(N_META))
    lat_m, krp_m, _, k_m, vt_m, mq, mk, mv, mg = _pre(meta_tokens, tabs_m, w, N_META, 1, l)

    oa = _attn(q_p, k_p, vt_p, k_m, vt_m, batch, seq)
    zero_state = jnp.zeros((B_HEADS, B_DV, B_DK), F32)
    _, st_meta = _hgrn(mq, mk, mv, mg, zero_state, 1, N_META, N_META, N_META, False)
    ob, st_fin = _hgrn(hq, hk, hv, hg, st_meta[0], batch, seq, HGRN_CHUNK, HGRN_TILE, True)
    y_prompt = _merge(xp, oa, ob, w, ROW_TILE).reshape(batch, seq, D_MODEL)

    lat_all = jnp.concatenate([jnp.broadcast_to(lat_m[None], (batch, N_META, KV_LORA)),
                               lat_p.reshape(batch, seq, KV_LORA)], axis=1)
    kr_all = jnp.concatenate([jnp.broadcast_to(krp_m[None, :, A_NOPE:A_QK], (batch, N_META, A_ROPE)),
                              krp_p[:, A_NOPE:A_QK].reshape(batch, seq, A_ROPE)], axis=1)

    xs = x_sample.reshape(dec_batch, D_MODEL)
    tabs_s = _rope_tables(jnp.full((dec_batch,), past_len, jnp.int32))
    lat_s, krp_s, q_s, k_s, _, sq, sk, sv, sg = _pre(xs, tabs_s, w, dec_batch, 1, l)
    q_s3 = q_s.reshape(dec_batch, A_HEADS, HEAD_PAD)
    k_s3 = k_s.reshape(dec_batch, A_HEADS, HEAD_PAD)
    qrow = q_s3[:, :, :A_NOPE].astype(F32).reshape(dec_batch, 1, A_HEADS * A_NOPE)
    qrope = q_s3[:, :, A_NOPE:A_QK].astype(F32)
    gkrow = jnp.tile(g_kn[l][:A_NOPE], A_HEADS)[None, :]
    gkr = g_kn[l][None, A_NOPE:]
    wukt = w_uk[l].reshape(KV_LORA, A_HEADS * A_NOPE).T.astype(BF16)
    oa_s = _paged(page_table, cache_latent, jnp.swapaxes(cache_krope, 2, 3), qrow, qrope, q_s3, k_s3,
                  lat_s[:, None, :], wukt, gkrow, gkr, w["w_uv"], l)
    row3 = lambda a: a.reshape(dec_batch, 1, B_FDIM)
    ob_s, st_s = _hstep(state_hgrn, row3(sq), row3(sk), row3(sg), row3(sv), l)
    y_sample = _merge(xs, oa_s.reshape(dec_batch, A_WIDTH), ob_s.reshape(dec_batch, B_WIDTH), w, dec_batch)

    return (y_prompt, y_sample.reshape(dec_batch, dec_seq, D_MODEL),
            lat_all[None], kr_all[None], st_fin[None],
            lat_s.reshape(1, dec_batch, dec_seq, KV_LORA),
            krp_s[:, A_NOPE:A_QK].reshape(1, dec_batch, dec_seq, A_ROPE),
            st_s[None])
```

```python
import functools

import jax
import jax.numpy as jnp
from jax import lax
from jax.experimental import pallas as pl
from jax.experimental.pallas import tpu as pltpu

F32 = jnp.float32
BF16 = jnp.bfloat16

D_MODEL = 1024
N_META = 16
A_HEADS = 8
A_NOPE = 64
A_ROPE = 32
A_QK = A_NOPE + A_ROPE
A_V = 64
A_WIDTH = A_HEADS * A_V
Q_LORA = 384
KV_LORA = 256
ROPE_THETA = 10000.0
B_HEADS = 4
B_DK = 128
B_DV = 128
B_FDIM = B_HEADS * B_DK
B_WIDTH = B_HEADS * B_DV
EPS = 1e-6
PAGE = 128

LANES = 128
LOG2E = 1.4426950408889634
HEAD_PAD = LANES
VMEM_LIMIT = 56 * 1024 * 1024

PRE_TILE = 256
ROW_TILE = 256
ATTN_TQ = 256
ATTN_TK = 256
HGRN_CHUNK = 128
HGRN_TILE = 256
PAGES_PER_GROUP = 32
HSTEP_ROWS = 4
POS_BLOCK = 2 * PAGE
DMA_SLOTS = 4


def _full(shape):
    return pl.BlockSpec(shape, lambda *_: (0,) * len(shape))


def _params(n_axes):
    return pltpu.CompilerParams(dimension_semantics=("arbitrary",) * n_axes,
                                vmem_limit_bytes=VMEM_LIMIT)


def _dot(a, b):
    return jnp.dot(a, b, preferred_element_type=F32)


def _dot_nt(a, b, precision=None):
    return lax.dot_general(a, b, (((1,), (1,)), ((), ())), precision=precision,
                           preferred_element_type=F32)


def _rms(x, g):
    r = lax.rsqrt(jnp.mean(x * x, axis=-1, keepdims=True) + EPS)
    return (x * r) * g


def _rope_lanes(t, c, s1, s2):
    return t * c + pltpu.roll(t, LANES - A_ROPE // 2, axis=1) * s1 + pltpu.roll(t, A_ROPE // 2, axis=1) * s2


def _pre_kernel(x_ref, ng_ref, wa_ref, gcq_ref, wuq_ref, gckv_ref,
                wuk_ref, wuv_ref, gqn_ref, gkn_ref, c_ref, s1_ref, s2_ref, lb_ref,
                lat_ref, krp_ref, q_ref, k_ref, vt_ref, hq_ref, hk_ref, hv_ref, hg_ref, *, layer, qscale):
    x = x_ref[...]
    xn = _rms(x, ng_ref[...]).astype(BF16)
    c, s1, s2 = c_ref[...], s1_ref[...], s2_ref[...]

    o1, o2, o3 = Q_LORA, Q_LORA + KV_LORA, Q_LORA + KV_LORA + LANES
    y = _dot(xn, wa_ref[:, :o3])
    cq = y[:, :o1]
    cqn = _rms(cq, gcq_ref[...]).astype(BF16)
    qraw = _dot(cqn, wuq_ref[...])
    gqn = gqn_ref[...]
    for h in range(A_HEADS):
        t = _rope_lanes(qraw[:, h * HEAD_PAD:(h + 1) * HEAD_PAD], c, s1, s2)
        r = lax.rsqrt(jnp.sum(t * t, axis=-1, keepdims=True) * (1.0 / A_QK) + EPS)
        q_ref[:, h * HEAD_PAD:(h + 1) * HEAD_PAD] = (((t * r) * gqn) * qscale).astype(BF16)

    ckv = _rms(y[:, o1:o2], gckv_ref[...])
    lat_ref[...] = ckv
    ckvb = ckv.astype(BF16)
    krp = _rope_lanes(y[:, o2:o3], c, s1, s2)
    krp_ref[...] = krp
    kraw = _dot(ckvb, wuk_ref[...])
    gkn = gkn_ref[...]
    for h in range(A_HEADS):
        t = kraw[:, h * HEAD_PAD:(h + 1) * HEAD_PAD] + krp
        r = lax.rsqrt(jnp.sum(t * t, axis=-1, keepdims=True) * (1.0 / A_QK) + EPS)
        k_ref[:, h * HEAD_PAD:(h + 1) * HEAD_PAD] = ((t * r) * gkn).astype(BF16)
    vt_ref[...] = _dot_nt(wuv_ref[...], ckvb).astype(BF16)

    lbl = lb_ref[...]
    e = jnp.exp(lbl - jnp.max(lbl, axis=0, keepdims=True))
    lb = jnp.sum(e[:layer + 1], axis=0, keepdims=True) / jnp.sum(e, axis=0, keepdims=True)
    b = _dot(xn, wa_ref[:, o3:])
    bq, z, bi = b[:, :B_FDIM], b[:, B_FDIM:2 * B_FDIM], b[:, 2 * B_FDIM:]
    hq_ref[...] = bq * jax.nn.sigmoid(bq)
    hg_ref[...] = jnp.log(lb + (1.0 - lb) * jax.nn.sigmoid(z))
    hk_ref[...] = (1.0 - lb) * jax.nn.sigmoid(-z)
    hv_ref[...] = bi


def _pre(x, tabs, w, tm, tab_blocks, layer, qscale=1.0):
    rows = x.shape[0]
    grid = (rows // tm,)
    row = lambda n: pl.BlockSpec((tm, n), lambda i: (i, 0))
    tab = pl.BlockSpec((tm, LANES), lambda i: (i % tab_blocks, 0))
    ins = [x, w["norm_g"], w["wa"], w["g_cq"], w["w_uq"], w["g_ckv"],
           w["w_uk"], w["w_uvt"], w["g_qn"], w["g_kn"], tabs[0], tabs[1], tabs[2], w["lb_logits"]]
    in_specs = [row(D_MODEL)] + [_full(a.shape) for a in ins[1:10]] + [tab, tab, tab] + [_full(ins[13].shape)]
    outs = [(KV_LORA, F32), (LANES, F32), (A_HEADS * HEAD_PAD, BF16), (A_HEADS * HEAD_PAD, BF16),
            None, (B_FDIM, F32), (B_FDIM, F32), (B_WIDTH, F32), (B_FDIM, F32)]
    out_specs = [pl.BlockSpec((A_WIDTH, tm), lambda i: (0, i)) if o is None else row(o[0]) for o in outs]
    out_shape = [jax.ShapeDtypeStruct((A_WIDTH, rows), BF16) if o is None else
                 jax.ShapeDtypeStruct((rows, o[0]), o[1]) for o in outs]
    return pl.pallas_call(
        functools.partial(_pre_kernel, layer=layer, qscale=qscale),
        grid=grid, in_specs=in_specs, out_specs=out_specs, out_shape=out_shape,
        compiler_params=_params(1), name="pre_proj")(*ins)


def _attn_kernel(q_ref, k_ref, vt_ref, km_ref, vtm_ref, o_ref, ot_sc, s_sc, sm_sc, m_sc, l_sc):
    i = pl.program_id(1)
    tq, tk = ATTN_TQ, ATTN_TK
    assert tq == tk
    key = lax.broadcasted_iota(jnp.int32, (tk, tq), 0)
    qry = lax.broadcasted_iota(jnp.int32, (tk, tq), 1)
    hsl = [slice(h * HEAD_PAD, (h + 1) * HEAD_PAD) for h in range(A_HEADS)]
    vsl = [slice(h * A_V, (h + 1) * A_V) for h in range(A_HEADS)]

    d0 = pl.multiple_of(i * tq, tq)
    for h in range(A_HEADS):
        s_sc[h, 0:tk, :] = _dot_nt(k_ref[pl.ds(d0, tk), hsl[h]], q_ref[:, hsl[h]])
        sm_sc[h] = _dot_nt(km_ref[:, hsl[h]], q_ref[:, hsl[h]])
    for h in range(A_HEADS):
        hr = slice(h, h + 1)
        s = jnp.where(key <= qry, s_sc[h, 0:tk, :], -jnp.inf)
        sm = sm_sc[h]
        m_new = jnp.maximum(jnp.max(s, axis=0, keepdims=True), jnp.max(sm, axis=0, keepdims=True))
        p = jnp.exp2(s - m_new)
        pm = jnp.exp2(sm - m_new)
        m_sc[hr, :] = m_new
        l_sc[hr, :] = jnp.sum(p, axis=0, keepdims=True) + jnp.sum(pm, axis=0, keepdims=True)
        ot_sc[vsl[h], :] = (_dot(vt_ref[vsl[h], pl.ds(d0, tk)], p.astype(BF16)) +
                            _dot(vtm_ref[vsl[h], :], pm.astype(BF16)))

    def visible(r0, keys):
        for h in range(A_HEADS):
            s_sc[h, 0:keys, :] = _dot_nt(k_ref[pl.ds(r0, keys), hsl[h]], q_ref[:, hsl[h]])
        for h in range(A_HEADS):
            hr = slice(h, h + 1)
            s = s_sc[h, 0:keys, :]
            m_new = jnp.maximum(m_sc[hr, :], jnp.max(s, axis=0, keepdims=True))
            a = jnp.exp2(m_sc[hr, :] - m_new)
            p = jnp.exp2(s - m_new)
            m_sc[hr, :] = m_new
            l_sc[hr, :] = l_sc[hr, :] * a + jnp.sum(p, axis=0, keepdims=True)
            ot_sc[vsl[h], :] = ot_sc[vsl[h], :] * a + _dot(vt_ref[vsl[h], pl.ds(r0, keys)], p.astype(BF16))

    def body(kb, _):
        visible(pl.multiple_of(kb * 2 * tk, 2 * tk), 2 * tk)
        return 0

    lax.fori_loop(0, i // 2, body, 0)

    @pl.when(i % 2 == 1)
    def _():
        visible(pl.multiple_of((i - 1) * tk, tk), tk)

    for h in range(A_HEADS):
        ot_sc[vsl[h], :] = ot_sc[vsl[h], :] / l_sc[h:h + 1, :]
    o_ref[...] = ot_sc[...].T


def _attn(q, k, vt, km, vtm, batch, seq):
    nq = seq // ATTN_TQ
    return pl.pallas_call(
        _attn_kernel, grid=(batch, nq),
        in_specs=[pl.BlockSpec((ATTN_TQ, A_HEADS * HEAD_PAD), lambda b, i: (b * nq + i, 0)),
                  pl.BlockSpec((seq, A_HEADS * HEAD_PAD), lambda b, i: (b, 0)),
                  pl.BlockSpec((A_WIDTH, seq), lambda b, i: (0, b)),
                  _full(km.shape), _full(vtm.shape)],
        out_specs=pl.BlockSpec((ATTN_TQ, A_WIDTH), lambda b, i: (b * nq + i, 0)),
        out_shape=jax.ShapeDtypeStruct((batch * seq, A_WIDTH), F32),
        scratch_shapes=[pltpu.VMEM((A_WIDTH, ATTN_TQ), F32), pltpu.VMEM((A_HEADS, 2 * ATTN_TK, ATTN_TQ), F32),
                        pltpu.VMEM((A_HEADS, km.shape[0], ATTN_TQ), F32),
                        pltpu.VMEM((A_HEADS, ATTN_TQ), F32), pltpu.VMEM((A_HEADS, ATTN_TQ), F32)],
        compiler_params=_params(2), name="prompt_attn")(q, k, vt, km, vtm)


def _split3(x):
    a = x.astype(BF16)
    r = x - a.astype(F32)
    b = r.astype(BF16)
    c = (r - b.astype(F32)).astype(BF16)
    return a, b, c


def _hgrn_chunk(q, kk, v, g, st, tril, chunk, cum_ref, krow):
    g1, g2, g3 = _split3(g)
    cum = (_dot(tril, g1) + _dot(tril, g2) + _dot(tril, g3)) * LOG2E
    cum_ref[...] = cum
    yield None
    nblk = chunk // 8
    cb = [cum[8 * j:8 * j + 8] for j in range(nblk)]
    qb = [q[8 * j:8 * j + 8] for j in range(nblk)]
    lane = lax.broadcasted_iota(jnp.int32, (8, chunk), 1)
    sub = lax.broadcasted_iota(jnp.int32, (8, chunk), 0)
    ab = []
    for j in range(nblk):
        a = jnp.zeros((8, chunk), F32)
        for s in range(8 * j, 8 * j + 8):
            col = jnp.sum(jnp.exp2(cb[j] - cum_ref[s:s + 1, :]) * qb[j] * krow(s), axis=-1, keepdims=True)
            a = jnp.where(lane == s, col, a)
        ab.append(jnp.where(lane <= sub + 8 * j, a, 0.0))
        yield None
    attn = jnp.concatenate(ab, axis=0)
    row = lax.broadcasted_iota(jnp.int32, (chunk, chunk), 0)
    col = lax.broadcasted_iota(jnp.int32, (chunk, chunk), 1)
    dead = jnp.full((8, B_DK), -jnp.inf, F32)
    w = 8
    while w < chunk:
        eq, ek = [], []
        for j in range(nblk):
            blk = (8 * j) // w
            if blk % 2:
                eq.append(cb[j] - cum_ref[blk * w - 1:blk * w, :])
                ek.append(dead)
            else:
                eq.append(dead)
                ek.append(cum_ref[blk * w + w - 1:blk * w + w, :] - cb[j])
        qa = (q * jnp.exp2(jnp.concatenate(eq, axis=0))).astype(BF16)
        ka = (kk * jnp.exp2(jnp.concatenate(ek, axis=0))).astype(BF16)
        sh = w.bit_length() - 1
        rb, cbk = row >> sh, col >> sh
        attn = attn + jnp.where((rb == cbk + 1) & ((cbk & 1) == 0), _dot_nt(qa, ka), 0.0)
        w *= 2
        yield None
    last = cum_ref[chunk - 1:chunk, :]
    o = _dot(attn.astype(BF16), v.astype(BF16)) + _dot_nt((q * jnp.exp2(cum)).astype(BF16), st.astype(BF16))
    kdec = (kk * jnp.exp2(last - cum)).astype(BF16)
    upd = lax.dot_general(v.astype(BF16), kdec, (((0,), (0,)), ((), ())), preferred_element_type=F32)
    yield o, jnp.exp2(last) * st + upd


def _hgrn_kernel(q_ref, k_ref, v_ref, g_ref, st0_ref, o_ref, st_ref, st_sc, cum_sc, kk_sc, *, chunk, tile,
                 transpose_out):
    t = pl.program_id(1)

    @pl.when(t == 0)
    def _():
        st_sc[...] = st0_ref[...]

    r = lax.broadcasted_iota(jnp.int32, (chunk, chunk), 0)
    c = lax.broadcasted_iota(jnp.int32, (chunk, chunk), 1)
    tril = (c <= r).astype(BF16)

    def body(ci, _):
        r0 = pl.multiple_of(ci * chunk, chunk)
        hsl = [slice(h * B_DK, (h + 1) * B_DK) for h in range(B_HEADS)]
        stages = {}
        for h in range(B_HEADS):
            kk = k_ref[pl.ds(r0, chunk), hsl[h]]
            kk_sc[h] = kk
            krow = lambda s, h=h: kk_sc[h, s:s + 1, :]
            stages[h] = _hgrn_chunk(q_ref[pl.ds(r0, chunk), hsl[h]], kk,
                                    v_ref[pl.ds(r0, chunk), hsl[h]], g_ref[pl.ds(r0, chunk), hsl[h]],
                                    st_sc[h], tril, chunk, cum_sc.at[h], krow)
        while stages:
            for h in list(stages):
                out = next(stages[h])
                if out is not None:
                    o_ref[pl.ds(r0, chunk), hsl[h]] = out[0]
                    st_sc[h] = out[1]
                    del stages[h]
        return 0

    lax.fori_loop(0, tile // chunk, body, 0)

    @pl.when(t == pl.num_programs(1) - 1)
    def _():
        for h in range(B_HEADS):
            st_ref[0, h] = st_sc[h].T if transpose_out else st_sc[h]


def _hgrn(hq, hk, hv, hg, st0, batch, seq, chunk, tile, transpose_out):
    nt = seq // tile
    row = pl.BlockSpec((tile, B_FDIM), lambda b, t: (b * nt + t, 0))
    return pl.pallas_call(
        functools.partial(_hgrn_kernel, chunk=chunk, tile=tile, transpose_out=transpose_out),
        grid=(batch, nt),
        in_specs=[row, row, row, row, _full(st0.shape)],
        out_specs=[row, pl.BlockSpec((1, B_HEADS, B_DV, B_DK), lambda b, t: (b, 0, 0, 0))],
        out_shape=[jax.ShapeDtypeStruct((batch * seq, B_WIDTH), F32),
                   jax.ShapeDtypeStruct((batch, B_HEADS, B_DV, B_DK), F32)],
        scratch_shapes=[pltpu.VMEM((B_HEADS, B_DV, B_DK), F32), pltpu.VMEM((B_HEADS, chunk, B_DK), F32),
                        pltpu.VMEM((B_HEADS, chunk, B_DK), F32)],
        compiler_params=_params(2), name="hgrn_chunks")(hq, hk, hv, hg, st0)


def _split2(x):
    hi = x.astype(BF16)
    return hi, (x - hi.astype(F32)).astype(BF16)


def _paged_kernel(pt_ref, lat_hbm, kr_hbm, qrow_ref, qr_ref, qh_ref, kh_ref, latn_ref, wukt_ref,
                  gkrow_ref, gkr_ref, wuv_ref, o_ref, lhs, latbuf, krbuf, cbuf, kx_sc, sems, *, layer, n_pages):
    b = pl.program_id(0)
    gp = PAGES_PER_GROUP
    n_groups = n_pages // gp
    gpos = gp * PAGE
    nk = A_HEADS * A_NOPE
    log2e = 1.4426950408889634
    sc2 = (A_QK ** -0.5) * log2e

    def page_copies(bb, g, slot):
        out = []
        for j in range(gp):
            page = pt_ref[bb, g * gp + j]
            dst = pl.ds(j * PAGE, PAGE)
            out.append(pltpu.make_async_copy(lat_hbm.at[layer, page], latbuf.at[slot, dst], sems.at[0, slot]))
            out.append(pltpu.make_async_copy(kr_hbm.at[layer, page], krbuf.at[slot, j], sems.at[1, slot]))
        return out

    def start(bb, g, slot):
        for cp in page_copies(bb, g, slot):
            cp.start()

    def wait(g, slot):
        for cp in page_copies(b, g, slot):
            cp.wait()

    ring = DMA_SLOTS
    ahead = ring - 1
    assert n_groups % ring == 0 and ahead <= n_groups

    def fetch_ahead(g):
        gg = g + ahead
        wrap = gg >= n_groups
        bb = jnp.where(wrap, b + 1, b)
        g2 = jnp.where(wrap, gg - n_groups, gg)

        @pl.when(bb < pl.num_programs(0))
        def _():
            start(bb, g2, lax.rem(gg, ring))

    @pl.when(b == 0)
    def _():
        for d in range(ahead):
            start(0, d, d)

    @pl.when(b == 0)
    def _():
        lhs[0:nk, :] = wukt_ref[...]

    hrow = lax.broadcasted_iota(jnp.int32, (A_HEADS, nk), 0)
    hcol = lax.broadcasted_iota(jnp.int32, (A_HEADS, nk), 1) // A_NOPE
    qsel = jnp.where(hrow == hcol, qrow_ref[0] * gkrow_ref[...], 0.0)
    q_hi, q_lo = _split2(qsel)
    qabs = _dot(q_hi, wukt_ref[...]) + _dot(q_lo, wukt_ref[...])
    lhs[nk:nk + 2 * A_HEADS, :] = jnp.concatenate(_split2(qabs), axis=0)
    qrg = jnp.concatenate(_split2(qr_ref[0] * gkr_ref[...]), axis=0)
    ppb = POS_BLOCK // PAGE

    def scores(slot, cslot, between=None):
        parts = []
        nblk = gpos // POS_BLOCK

        def project(j):
            rows = pl.ds(j * POS_BLOCK, POS_BLOCK)
            cb = latbuf[slot, rows, :].astype(BF16)
            cbuf[cslot, rows, :] = cb
            kx_sc[j] = _dot_nt(lhs[...], cb)

        for j in range(nblk):
            project(j)
        if between is not None:
            between()
        for j in range(nblk):
            kn = kx_sc[j, 0:nk, :]
            ss = jnp.sum((kn * kn).reshape(A_HEADS, A_NOPE, POS_BLOCK), axis=1)
            raw = kx_sc[j, nk:nk + A_HEADS, :] + kx_sc[j, nk + A_HEADS:nk + 2 * A_HEADS, :]
            krt = jnp.concatenate([krbuf[slot, j * ppb + t] for t in range(ppb)], axis=-1)
            rr = _dot(qrg, krt.astype(BF16))
            raw = raw + rr[:A_HEADS] + rr[A_HEADS:]
            ss = ss + jnp.sum(krt * krt, axis=0, keepdims=True)
            parts.append(raw * lax.rsqrt(ss * (1.0 / A_QK) + EPS) * sc2)
        return jnp.concatenate(parts, axis=-1)

    def absorb(s, slot, carry):
        m, l, acc = carry
        m_new = jnp.maximum(m, jnp.max(s, axis=-1, keepdims=True))
        a = jnp.exp2(m - m_new)
        p = jnp.exp2(s - m_new)
        l = l * a + jnp.sum(p, axis=-1, keepdims=True)
        acc = acc * a + _dot(p.astype(BF16), cbuf[slot])
        return m_new, l, acc

    wait(0, 0)
    fetch_ahead(0)
    s0 = scores(0, 0)
    init = (jnp.full((A_HEADS, 1), -jnp.inf, F32), jnp.zeros((A_HEADS, 1), F32),
            jnp.zeros((A_HEADS, KV_LORA), F32))

    def group(g, carry):
        s_prev, state = carry
        cslot = lax.rem(g, 2)
        wait(g, lax.rem(g, ring))
        fetch_ahead(g)
        box = []
        s_cur = scores(lax.rem(g, ring), cslot, lambda: box.append(absorb(s_prev, 1 - cslot, state)))
        return s_cur, box[0]

    s_last, state = lax.fori_loop(1, n_groups, group, (s0, init))
    m, l, acc = absorb(s_last, (n_groups - 1) % 2, state)

    s_new = jnp.sum(qh_ref[0].astype(F32) * kh_ref[0].astype(F32), axis=-1, keepdims=True) * sc2
    m_new = jnp.maximum(m, s_new)
    a = jnp.exp2(m - m_new)
    p = jnp.exp2(s_new - m_new)
    acc = acc * a + p * latn_ref[0]
    l = l * a + p
    o_lat = (acc / l).astype(BF16)
    full = _dot(o_lat, wuv_ref[...])
    vrow = lax.broadcasted_iota(jnp.int32, (A_HEADS, A_WIDTH), 0)
    vcol = lax.broadcasted_iota(jnp.int32, (A_HEADS, A_WIDTH), 1) // A_V
    o_ref[0] = jnp.sum(jnp.where(vrow == vcol, full, 0.0), axis=0, keepdims=True)


def _paged(page_table, cache_lat, cache_kr, qrow, qr, qh, kh, latn, wukt, gkrow, gkr, wuv, layer):
    nb, n_pages = page_table.shape
    gp = PAGES_PER_GROUP
    per_b = lambda shape: pl.BlockSpec((1,) + shape, lambda b, pt: (b,) + (0,) * len(shape))
    full = lambda a: pl.BlockSpec(a.shape, lambda b, pt: (0,) * a.ndim)
    grid_spec = pltpu.PrefetchScalarGridSpec(
        num_scalar_prefetch=1, grid=(nb,),
        in_specs=[pl.BlockSpec(memory_space=pl.ANY), pl.BlockSpec(memory_space=pl.ANY),
                  per_b((1, A_HEADS * A_NOPE)), per_b((A_HEADS, A_ROPE)), per_b((A_HEADS, HEAD_PAD)),
                  per_b((A_HEADS, HEAD_PAD)), per_b((1, KV_LORA)), full(wukt), full(gkrow), full(gkr), full(wuv)],
        out_specs=per_b((1, A_WIDTH)),
        scratch_shapes=[pltpu.VMEM((A_HEADS * A_NOPE + 2 * A_HEADS, KV_LORA), BF16),
                        pltpu.VMEM((DMA_SLOTS, gp * PAGE, KV_LORA), F32), pltpu.VMEM((DMA_SLOTS, gp, A_ROPE, PAGE), F32),
                        pltpu.VMEM((2, gp * PAGE, KV_LORA), BF16),
                        pltpu.VMEM((gp * PAGE // POS_BLOCK, A_HEADS * A_NOPE + 2 * A_HEADS, POS_BLOCK), F32),
                        pltpu.SemaphoreType.DMA((2, DMA_SLOTS))])
    return pl.pallas_call(
        functools.partial(_paged_kernel, layer=layer, n_pages=n_pages), grid_spec=grid_spec,
        out_shape=jax.ShapeDtypeStruct((nb, 1, A_WIDTH), F32),
        compiler_params=_params(1), name="paged_attn")(
            page_table, cache_lat, cache_kr, qrow, qr, qh, kh, latn, wukt, gkrow, gkr, wuv)


def _hstep_kernel(s_ref, q_ref, k_ref, g_ref, v_ref, o_ref, sn_ref, *, layer, rows):
    r = lax.broadcasted_iota(jnp.int32, (B_DK, B_DK), 0)
    c = lax.broadcasted_iota(jnp.int32, (B_DK, B_DK), 1)
    eye = r == c

    def column(row):
        return jnp.sum(jnp.where(eye, row, 0.0), axis=-1, keepdims=True)

    for i in range(rows):
        for h in range(B_HEADS):
            hs = slice(h * B_DK, (h + 1) * B_DK)
            sn = column(jnp.exp(g_ref[i, :, hs])) * s_ref[layer, i, h] + column(k_ref[i, :, hs]) * v_ref[i, :, hs]
            sn_ref[i, h] = sn
            o_ref[i, :, hs] = jnp.sum(column(q_ref[i, :, hs]) * sn, axis=0, keepdims=True)


def _hstep(state, q, k, g, v, layer):
    nb = state.shape[1]
    rows = HSTEP_ROWS
    st_in = pl.BlockSpec((state.shape[0], rows, B_HEADS, B_DK, B_DV), lambda b: (0, b, 0, 0, 0))
    st_out = pl.BlockSpec((rows, B_HEADS, B_DK, B_DV), lambda b: (b, 0, 0, 0))
    row = pl.BlockSpec((rows, 1, B_FDIM), lambda b: (b, 0, 0))
    return pl.pallas_call(
        functools.partial(_hstep_kernel, layer=layer, rows=rows), grid=(nb // rows,),
        in_specs=[st_in, row, row, row, row], out_specs=[row, st_out],
        out_shape=[jax.ShapeDtypeStruct((nb, 1, B_WIDTH), F32), jax.ShapeDtypeStruct(state.shape[1:], F32)],
        compiler_params=_params(1), name="hgrn_step")(state, q, k, g, v)


def _merge_kernel(x_ref, ng_ref, wg_ref, oa_ref, ob_ref, gbn_ref, woa_ref, wob_ref, wo_ref, y_ref):
    x = x_ref[...]
    xn = _rms(x, ng_ref[...]).astype(BF16)
    gates = _dot(xn, wg_ref[...])
    ga, gb = gates[:, :A_WIDTH], gates[:, A_WIDTH:A_WIDTH + B_WIDTH]
    ma = gates[:, A_WIDTH + B_WIDTH:A_WIDTH + B_WIDTH + D_MODEL]
    mb = gates[:, A_WIDTH + B_WIDTH + D_MODEL:]
    ya = _dot((oa_ref[...] * (ga * jax.nn.sigmoid(ga))).astype(BF16), woa_ref[...])
    gbn = gbn_ref[...]
    obn = jnp.concatenate([_rms(ob_ref[:, h * B_DV:(h + 1) * B_DV], gbn) for h in range(B_HEADS)], axis=-1)
    yb = _dot((obn * (gb * jax.nn.sigmoid(gb))).astype(BF16), wob_ref[...])
    mix = jax.nn.sigmoid(ma) * ya + jax.nn.sigmoid(mb) * yb
    y_ref[...] = x + _dot(mix.astype(BF16), wo_ref[...])


def _merge(x, oa, ob, w, tm):
    rows = x.shape[0]
    row = lambda n: pl.BlockSpec((tm, n), lambda i: (i, 0))
    ins = [x, w["norm_g"], w["wg"], oa, ob, w["g_bn"], w["w_oa"], w["w_ob"], w["w_o"]]
    in_specs = [row(D_MODEL), _full(ins[1].shape), _full(ins[2].shape), row(A_WIDTH), row(B_WIDTH)] + \
               [_full(a.shape) for a in ins[5:]]
    return pl.pallas_call(
        _merge_kernel, grid=(rows // tm,), in_specs=in_specs, out_specs=row(D_MODEL),
        out_shape=jax.ShapeDtypeStruct((rows, D_MODEL), F32),
        compiler_params=_params(1), name="merge_out")(*ins)


def _head_pad_cols(w3, width):
    pad = jnp.zeros(w3.shape[:2] + (HEAD_PAD - width,), w3.dtype)
    return jnp.concatenate([w3, pad], axis=-1).reshape(w3.shape[0], -1)


def _rope_tables(pos):
    half = A_ROPE // 2
    inv = ROPE_THETA ** (-jnp.arange(half, dtype=F32) / half)
    ang = pos.astype(F32)[:, None] * inv
    cos, sin = jnp.cos(ang), jnp.sin(ang)
    n = pos.shape[0]
    one, zero = jnp.ones((n, A_NOPE), F32), jnp.zeros((n, A_NOPE), F32)
    tail = jnp.zeros((n, HEAD_PAD - A_QK), F32)
    z16 = jnp.zeros((n, half), F32)
    c = jnp.concatenate([one, cos, cos, tail], axis=-1)
    s1 = jnp.concatenate([zero, -sin, z16, tail], axis=-1)
    s2 = jnp.concatenate([zero, z16, sin, tail], axis=-1)
    return c, s1, s2


def _lane_gain(g):
    return jnp.concatenate([g, jnp.zeros((HEAD_PAD - A_QK,), g.dtype)])[None, :]


def kernel(x_prompt, x_sample, cache_latent, cache_krope, state_hgrn, page_table, meta_tokens,
           norm_g, w_in, g_cq, w_uq, g_ckv, w_uk, w_uv, g_qn, g_kn, lb_logits, g_bn, w_oa, w_ob, w_o):
    batch, seq, _ = x_prompt.shape
    dec_batch, dec_seq, _ = x_sample.shape
    depth = w_in.shape[0]
    assert depth == 1 and dec_seq == 1
    past_len = page_table.shape[1] * PAGE
    l = 0

    o0 = Q_LORA
    o1 = o0 + KV_LORA
    o2 = o1 + A_ROPE
    o3 = o2 + 3 * B_FDIM
    wi = w_in[l]
    wkr = jnp.zeros((D_MODEL, HEAD_PAD), F32).at[:, A_NOPE:A_QK].set(wi[:, o1:o2])
    w = {
        "norm_g": norm_g[l][None, :],
        "wa": jnp.concatenate([wi[:, :o1], wkr, wi[:, o2:o3]], axis=1).astype(BF16), "wg": wi[:, o3:].astype(BF16),
        "g_cq": g_cq[l][None, :], "g_ckv": g_ckv[l][None, :],
        "w_uq": _head_pad_cols(w_uq[l].reshape(Q_LORA, A_HEADS, A_QK), A_QK).astype(BF16),
        "w_uk": _head_pad_cols(w_uk[l], A_NOPE).astype(BF16),
        "w_uv": w_uv[l].reshape(KV_LORA, A_WIDTH).astype(BF16),
        "w_uvt": w_uv[l].reshape(KV_LORA, A_WIDTH).T.astype(BF16),
        "g_qn": _lane_gain(g_qn[l]), "g_kn": _lane_gain(g_kn[l]),
        "lb_logits": lb_logits, "g_bn": g_bn[l][None, :],
        "w_oa": w_oa[l].astype(BF16), "w_ob": w_ob[l].astype(BF16), "w_o": w_o[l].astype(BF16),
    }

    xp = x_prompt.reshape(batch * seq, D_MODEL)
    tabs_p = _rope_tables(N_META + jnp.arange(seq))
    qscale = (A_QK ** -0.5) * 1.4426950408889634
    lat_p, krp_p, q_p, k_p, vt_p, hq, hk, hv, hg = _pre(xp, tabs_p, w, PRE_TILE, seq // PRE_TILE, l, qscale)
    tabs_m = _rope_tables(jnp.arange(N_META))
    lat_m, krp_m, _, k_m, vt_m, mq, mk, mv, mg = _pre(meta_tokens, tabs_m, w, N_META, 1, l)

    oa = _attn(q_p, k_p, vt_p, k_m, vt_m, batch, seq)
    zero_state = jnp.zeros((B_HEADS, B_DV, B_DK), F32)
    _, st_meta = _hgrn(mq, mk, mv, mg, zero_state, 1, N_META, N_META, N_META, False)
    ob, st_fin = _hgrn(hq, hk, hv, hg, st_meta[0], batch, seq, HGRN_CHUNK, HGRN_TILE, True)
    y_prompt = _merge(xp, oa, ob, w, ROW_TILE).reshape(batch, seq, D_MODEL)

    lat_all = jnp.concatenate([jnp.broadcast_to(lat_m[None], (batch, N_META, KV_LORA)),
                               lat_p.reshape(batch, seq, KV_LORA)], axis=1)
    kr_all = jnp.concatenate([jnp.broadcast_to(krp_m[None, :, A_NOPE:A_QK], (batch, N_META, A_ROPE)),
                              krp_p[:, A_NOPE:A_QK].reshape(batch, seq, A_ROPE)], axis=1)

    xs = x_sample.reshape(dec_batch, D_MODEL)
    tabs_s = _rope_tables(jnp.full((dec_batch,), past_len, jnp.int32))
    lat_s, krp_s, q_s, k_s, _, sq, sk, sv, sg = _pre(xs, tabs_s, w, dec_batch, 1, l)
    q_s3 = q_s.reshape(dec_batch, A_HEADS, HEAD_PAD)
    k_s3 = k_s.reshape(dec_batch, A_HEADS, HEAD_PAD)
    qrow = q_s3[:, :, :A_NOPE].astype(F32).reshape(dec_batch, 1, A_HEADS * A_NOPE)
    qrope = q_s3[:, :, A_NOPE:A_QK].astype(F32)
    gkrow = jnp.tile(g_kn[l][:A_NOPE], A_HEADS)[None, :]
    gkr = g_kn[l][None, A_NOPE:]
    wukt = w_uk[l].reshape(KV_LORA, A_HEADS * A_NOPE).T.astype(BF16)
    oa_s = _paged(page_table, cache_latent, jnp.swapaxes(cache_krope, 2, 3), qrow, qrope, q_s3, k_s3,
                  lat_s[:, None, :], wukt, gkrow, gkr, w["w_uv"], l)
    row3 = lambda a: a.reshape(dec_batch, 1, B_FDIM)
    ob_s, st_s = _hstep(state_hgrn, row3(sq), row3(sk), row3(sg), row3(sv), l)
    y_sample = _merge(xs, oa_s.reshape(dec_batch, A_WIDTH), ob_s.reshape(dec_batch, B_WIDTH), w, dec_batch)

    return (y_prompt, y_sample.reshape(dec_batch, dec_seq, D_MODEL),
            lat_all[None], kr_all[None], st_fin[None],
            lat_s.reshape(1, dec_batch, dec_seq, KV_LORA),
            krp_s[:, A_NOPE:A_QK].reshape(1, dec_batch, dec_seq, A_ROPE),
            st_s[None])
```

```python
import functools

import jax
import jax.numpy as jnp
from jax import lax
from jax.experimental import pallas as pl
from jax.experimental.pallas import tpu as pltpu

F32 = jnp.float32
BF16 = jnp.bfloat16

D_MODEL = 1024
N_META = 16
A_HEADS = 8
A_NOPE = 64
A_ROPE = 32
A_QK = A_NOPE + A_ROPE
A_V = 64
A_WIDTH = A_HEADS * A_V
Q_LORA = 384
KV_LORA = 256
ROPE_THETA = 10000.0
B_HEADS = 4
B_DK = 128
B_DV = 128
B_FDIM = B_HEADS * B_DK
B_WIDTH = B_HEADS * B_DV
EPS = 1e-6
PAGE = 128

LANES = 128
LOG2E = 1.4426950408889634
HEAD_PAD = LANES
VMEM_LIMIT = 56 * 1024 * 1024

PRE_TILE = 256
ROW_TILE = 256
ATTN_TQ = 256
ATTN_TK = 256
HGRN_CHUNK = 128
HGRN_TILE = 256
PAGES_PER_GROUP = 32
HSTEP_ROWS = 4
POS_BLOCK = 2 * PAGE
DMA_SLOTS = 4


def _full(shape):
    return pl.BlockSpec(shape, lambda *_: (0,) * len(shape))


def _params(n_axes):
    return pltpu.CompilerParams(dimension_semantics=("arbitrary",) * n_axes,
                                vmem_limit_bytes=VMEM_LIMIT)


def _dot(a, b):
    return jnp.dot(a, b, preferred_element_type=F32)


def _dot_nt(a, b, precision=None):
    return lax.dot_general(a, b, (((1,), (1,)), ((), ())), precision=precision,
                           preferred_element_type=F32)


def _rms(x, g):
    r = lax.rsqrt(jnp.mean(x * x, axis=-1, keepdims=True) + EPS)
    return (x * r) * g


def _rope_lanes(t, c, s1, s2):
    return t * c + pltpu.roll(t, LANES - A_ROPE // 2, axis=1) * s1 + pltpu.roll(t, A_ROPE // 2, axis=1) * s2


def _row_writer(latm_ref, krm_ref, latall_hbm, krall_hbm, latw, krw, sems, tiles_per_batch):
    i = pl.program_id(0)
    tm = latw.shape[1]
    slot = lax.rem(i, 2)
    bb = i // tiles_per_batch
    dst = pl.ds(N_META + lax.rem(i, tiles_per_batch) * tm, tm)

    def copies(s):
        return (pltpu.make_async_copy(latw.at[s], latall_hbm.at[bb, dst], sems.at[0, s]),
                pltpu.make_async_copy(krw.at[s], krall_hbm.at[bb, dst], sems.at[1, s]))

    def begin():
        @pl.when(i >= 2)
        def _():
            for cp in copies(slot):
                cp.wait()

    def finish():
        for cp in copies(slot):
            cp.start()

        @pl.when(lax.rem(i, tiles_per_batch) == 0)
        def _():
            head = pl.ds(0, N_META)
            meta = (pltpu.make_async_copy(latm_ref, latall_hbm.at[bb, head], sems.at[0, 2]),
                    pltpu.make_async_copy(krm_ref, krall_hbm.at[bb, head], sems.at[1, 2]))
            for cp in meta:
                cp.start()
            for cp in meta:
                cp.wait()

        @pl.when(i == pl.num_programs(0) - 1)
        def _():
            for cp in copies(slot) + copies(1 - slot):
                cp.wait()

    return slot, begin, finish


def _pre_kernel(*refs, layer, qscale, direct):
    (x_ref, ng_ref, wa_ref, gcq_ref, wuq_ref, gckv_ref, wuk_ref, wuv_ref, gqn_ref, gkn_ref,
     c_ref, s1_ref, s2_ref, lb_ref) = refs[:14]
    if direct:
        latm_ref, krm_ref = refs[14:16]
        (latall_hbm, krall_hbm, q_ref, k_ref, vt_ref, hq_ref, hk_ref, hv_ref, hg_ref, latw, krw, sems) = refs[16:]
        slot, begin, finish = _row_writer(latm_ref, krm_ref, latall_hbm, krall_hbm, latw, krw, sems, direct)
        begin()
    else:
        lat_ref, krp_ref, q_ref, k_ref, vt_ref, hq_ref, hk_ref, hv_ref, hg_ref = refs[14:]
    x = x_ref[...]
    xn = _rms(x, ng_ref[...]).astype(BF16)
    c, s1, s2 = c_ref[...], s1_ref[...], s2_ref[...]

    o1, o2, o3 = Q_LORA, Q_LORA + KV_LORA, Q_LORA + KV_LORA + LANES
    y = _dot(xn, wa_ref[:, :o3])
    cq = y[:, :o1]
    cqn = _rms(cq, gcq_ref[...]).astype(BF16)
    qraw = _dot(cqn, wuq_ref[...])
    gqn = gqn_ref[...]
    for h in range(A_HEADS):
        t = _rope_lanes(qraw[:, h * HEAD_PAD:(h + 1) * HEAD_PAD], c, s1, s2)
        r = lax.rsqrt(jnp.sum(t * t, axis=-1, keepdims=True) * (1.0 / A_QK) + EPS)
        q_ref[:, h * HEAD_PAD:(h + 1) * HEAD_PAD] = (((t * r) * gqn) * qscale).astype(BF16)

    ckv = _rms(y[:, o1:o2], gckv_ref[...])
    ckvb = ckv.astype(BF16)
    krp = _rope_lanes(y[:, o2:o3], c, s1, s2)
    if direct:
        latw[slot] = ckv
        krw[slot] = krp[:, A_NOPE:A_QK]
    else:
        lat_ref[...] = ckv
        krp_ref[...] = krp
    kraw = _dot(ckvb, wuk_ref[...])
    gkn = gkn_ref[...]
    for h in range(A_HEADS):
        t = kraw[:, h * HEAD_PAD:(h + 1) * HEAD_PAD] + krp
        r = lax.rsqrt(jnp.sum(t * t, axis=-1, keepdims=True) * (1.0 / A_QK) + EPS)
        k_ref[:, h * HEAD_PAD:(h + 1) * HEAD_PAD] = ((t * r) * gkn).astype(BF16)
    vt_ref[...] = _dot_nt(wuv_ref[...], ckvb).astype(BF16)

    lbl = lb_ref[...]
    e = jnp.exp(lbl - jnp.max(lbl, axis=0, keepdims=True))
    lb = jnp.sum(e[:layer + 1], axis=0, keepdims=True) / jnp.sum(e, axis=0, keepdims=True)
    b = _dot(xn, wa_ref[:, o3:])
    bq, z, bi = b[:, :B_FDIM], b[:, B_FDIM:2 * B_FDIM], b[:, 2 * B_FDIM:]
    hq_ref[...] = bq * jax.nn.sigmoid(bq)
    hg_ref[...] = jnp.log(lb + (1.0 - lb) * jax.nn.sigmoid(z))
    hk_ref[...] = (1.0 - lb) * jax.nn.sigmoid(-z)
    hv_ref[...] = bi
    if direct:
        finish()


def _pre(x, tabs, w, tm, tab_blocks, layer, qscale=1.0, direct=None):
    rows = x.shape[0]
    grid = (rows // tm,)
    row = lambda n: pl.BlockSpec((tm, n), lambda i: (i, 0))
    tab = pl.BlockSpec((tm, LANES), lambda i: (i % tab_blocks, 0))
    ins = [x, w["norm_g"], w["wa"], w["g_cq"], w["w_uq"], w["g_ckv"],
           w["w_uk"], w["w_uvt"], w["g_qn"], w["g_kn"], tabs[0], tabs[1], tabs[2], w["lb_logits"]]
    in_specs = [row(D_MODEL)] + [_full(a.shape) for a in ins[1:10]] + [tab, tab, tab] + [_full(ins[13].shape)]
    outs = [(A_HEADS * HEAD_PAD, BF16), (A_HEADS * HEAD_PAD, BF16),
            None, (B_FDIM, F32), (B_FDIM, F32), (B_WIDTH, F32), (B_FDIM, F32)]
    out_specs = [pl.BlockSpec((A_WIDTH, tm), lambda i: (0, i)) if o is None else row(o[0]) for o in outs]
    out_shape = [jax.ShapeDtypeStruct((A_WIDTH, rows), BF16) if o is None else
                 jax.ShapeDtypeStruct((rows, o[0]), o[1]) for o in outs]
    scratch = []
    if direct is None:
        out_specs = [row(KV_LORA), row(LANES)] + out_specs
        out_shape = [jax.ShapeDtypeStruct((rows, KV_LORA), F32), jax.ShapeDtypeStruct((rows, LANES), F32)] + out_shape
        tiles_per_batch = 0
    else:
        batch, seq, lat_m, kr_m = direct
        assert grid[0] >= 2 and seq % tm == 0
        tiles_per_batch = seq // tm
        ins += [lat_m, kr_m]
        in_specs += [_full(lat_m.shape), _full(kr_m.shape)]
        anyspec = pl.BlockSpec(memory_space=pl.ANY)
        out_specs = [anyspec, anyspec] + out_specs
        out_shape = [jax.ShapeDtypeStruct((batch, N_META + seq, KV_LORA), F32),
                     jax.ShapeDtypeStruct((batch, N_META + seq, A_ROPE), F32)] + out_shape
        scratch = [pltpu.VMEM((2, tm, KV_LORA), F32), pltpu.VMEM((2, tm, A_ROPE), F32),
                   pltpu.SemaphoreType.DMA((2, 3))]
    return pl.pallas_call(
        functools.partial(_pre_kernel, layer=layer, qscale=qscale, direct=tiles_per_batch),
        grid=grid, in_specs=in_specs, out_specs=out_specs, out_shape=out_shape, scratch_shapes=scratch,
        compiler_params=_params(1), name="pre_proj")(*ins)


def _attn_kernel(q_ref, k_ref, vt_ref, km_ref, vtm_ref, o_ref, ot_sc, s_sc, sm_sc, m_sc, l_sc):
    i = pl.program_id(1)
    tq, tk = ATTN_TQ, ATTN_TK
    assert tq == tk
    key = lax.broadcasted_iota(jnp.int32, (tk, tq), 0)
    qry = lax.broadcasted_iota(jnp.int32, (tk, tq), 1)
    hsl = [slice(h * HEAD_PAD, (h + 1) * HEAD_PAD) for h in range(A_HEADS)]
    vsl = [slice(h * A_V, (h + 1) * A_V) for h in range(A_HEADS)]

    d0 = pl.multiple_of(i * tq, tq)
    for h in range(A_HEADS):
        s_sc[h, 0:tk, :] = _dot_nt(k_ref[pl.ds(d0, tk), hsl[h]], q_ref[:, hsl[h]])
        sm_sc[h] = _dot_nt(km_ref[:, hsl[h]], q_ref[:, hsl[h]])
    for h in range(A_HEADS):
        hr = slice(h, h + 1)
        s = jnp.where(key <= qry, s_sc[h, 0:tk, :], -jnp.inf)
        sm = sm_sc[h]
        m_new = jnp.maximum(jnp.max(s, axis=0, keepdims=True), jnp.max(sm, axis=0, keepdims=True))
        p = jnp.exp2(s - m_new)
        pm = jnp.exp2(sm - m_new)
        m_sc[hr, :] = m_new
        l_sc[hr, :] = jnp.sum(p, axis=0, keepdims=True) + jnp.sum(pm, axis=0, keepdims=True)
        ot_sc[vsl[h], :] = (_dot(vt_ref[vsl[h], pl.ds(d0, tk)], p.astype(BF16)) +
                            _dot(vtm_ref[vsl[h], :], pm.astype(BF16)))

    def visible(r0, keys):
        for h in range(A_HEADS):
            s_sc[h, 0:keys, :] = _dot_nt(k_ref[pl.ds(r0, keys), hsl[h]], q_ref[:, hsl[h]])
        for h in range(A_HEADS):
            hr = slice(h, h + 1)
            s = s_sc[h, 0:keys, :]
            m_new = jnp.maximum(m_sc[hr, :], jnp.max(s, axis=0, keepdims=True))
            a = jnp.exp2(m_sc[hr, :] - m_new)
            p = jnp.exp2(s - m_new)
            m_sc[hr, :] = m_new
            l_sc[hr, :] = l_sc[hr, :] * a + jnp.sum(p, axis=0, keepdims=True)
            ot_sc[vsl[h], :] = ot_sc[vsl[h], :] * a + _dot(vt_ref[vsl[h], pl.ds(r0, keys)], p.astype(BF16))

    def body(kb, _):
        visible(pl.multiple_of(kb * 2 * tk, 2 * tk), 2 * tk)
        return 0

    lax.fori_loop(0, i // 2, body, 0)

    @pl.when(i % 2 == 1)
    def _():
        visible(pl.multiple_of((i - 1) * tk, tk), tk)

    for h in range(A_HEADS):
        ot_sc[vsl[h], :] = ot_sc[vsl[h], :] / l_sc[h:h + 1, :]
    o_ref[...] = ot_sc[...].T


def _attn(q, k, vt, km, vtm, batch, seq):
    nq = seq // ATTN_TQ
    return pl.pallas_call(
        _attn_kernel, grid=(batch, nq),
        in_specs=[pl.BlockSpec((ATTN_TQ, A_HEADS * HEAD_PAD), lambda b, i: (b * nq + i, 0)),
                  pl.BlockSpec((seq, A_HEADS * HEAD_PAD), lambda b, i: (b, 0)),
                  pl.BlockSpec((A_WIDTH, seq), lambda b, i: (0, b)),
                  _full(km.shape), _full(vtm.shape)],
        out_specs=pl.BlockSpec((ATTN_TQ, A_WIDTH), lambda b, i: (b * nq + i, 0)),
        out_shape=jax.ShapeDtypeStruct((batch * seq, A_WIDTH), F32),
        scratch_shapes=[pltpu.VMEM((A_WIDTH, ATTN_TQ), F32), pltpu.VMEM((A_HEADS, 2 * ATTN_TK, ATTN_TQ), F32),
                        pltpu.VMEM((A_HEADS, km.shape[0], ATTN_TQ), F32),
                        pltpu.VMEM((A_HEADS, ATTN_TQ), F32), pltpu.VMEM((A_HEADS, ATTN_TQ), F32)],
        compiler_params=_params(2), name="prompt_attn")(q, k, vt, km, vtm)


def _split3(x):
    a = x.astype(BF16)
    r = x - a.astype(F32)
    b = r.astype(BF16)
    c = (r - b.astype(F32)).astype(BF16)
    return a, b, c


def _hgrn_chunk(q, kk, v, g, st, tril, chunk, cum_ref, krow):
    g1, g2, g3 = _split3(g)
    cum = (_dot(tril, g1) + _dot(tril, g2) + _dot(tril, g3)) * LOG2E
    cum_ref[...] = cum
    yield None
    nblk = chunk // 8
    cb = [cum[8 * j:8 * j + 8] for j in range(nblk)]
    qb = [q[8 * j:8 * j + 8] for j in range(nblk)]
    lane = lax.broadcasted_iota(jnp.int32, (8, chunk), 1)
    sub = lax.broadcasted_iota(jnp.int32, (8, chunk), 0)
    ab = []
    for j in range(nblk):
        a = jnp.zeros((8, chunk), F32)
        for s in range(8 * j, 8 * j + 8):
            col = jnp.sum(jnp.exp2(cb[j] - cum_ref[s:s + 1, :]) * qb[j] * krow(s), axis=-1, keepdims=True)
            a = jnp.where(lane == s, col, a)
        ab.append(jnp.where(lane <= sub + 8 * j, a, 0.0))
        yield None
    attn = jnp.concatenate(ab, axis=0)
    row = lax.broadcasted_iota(jnp.int32, (chunk, chunk), 0)
    col = lax.broadcasted_iota(jnp.int32, (chunk, chunk), 1)
    dead = jnp.full((8, B_DK), -jnp.inf, F32)
    w = 8
    while w < chunk:
        eq, ek = [], []
        for j in range(nblk):
            blk = (8 * j) // w
            if blk % 2:
                eq.append(cb[j] - cum_ref[blk * w - 1:blk * w, :])
                ek.append(dead)
            else:
                eq.append(dead)
                ek.append(cum_ref[blk * w + w - 1:blk * w + w, :] - cb[j])
        qa = (q * jnp.exp2(jnp.concatenate(eq, axis=0))).astype(BF16)
        ka = (kk * jnp.exp2(jnp.concatenate(ek, axis=0))).astype(BF16)
        sh = w.bit_length() - 1
        rb, cbk = row >> sh, col >> sh
        attn = attn + jnp.where((rb == cbk + 1) & ((cbk & 1) == 0), _dot_nt(qa, ka), 0.0)
        w *= 2
        yield None
    last = cum_ref[chunk - 1:chunk, :]
    o = _dot(attn.astype(BF16), v.astype(BF16)) + _dot_nt((q * jnp.exp2(cum)).astype(BF16), st.astype(BF16))
    kdec = (kk * jnp.exp2(last - cum)).astype(BF16)
    upd = lax.dot_general(v.astype(BF16), kdec, (((0,), (0,)), ((), ())), preferred_element_type=F32)
    yield o, jnp.exp2(last) * st + upd


def _hgrn_kernel(q_ref, k_ref, v_ref, g_ref, st0_ref, o_ref, st_ref, st_sc, cum_sc, kk_sc, *, chunk, tile,
                 transpose_out):
    t = pl.program_id(1)

    @pl.when(t == 0)
    def _():
        st_sc[...] = st0_ref[...]

    r = lax.broadcasted_iota(jnp.int32, (chunk, chunk), 0)
    c = lax.broadcasted_iota(jnp.int32, (chunk, chunk), 1)
    tril = (c <= r).astype(BF16)

    def body(ci, _):
        r0 = pl.multiple_of(ci * chunk, chunk)
        hsl = [slice(h * B_DK, (h + 1) * B_DK) for h in range(B_HEADS)]
        stages = {}
        for h in range(B_HEADS):
            kk = k_ref[pl.ds(r0, chunk), hsl[h]]
            kk_sc[h] = kk
            krow = lambda s, h=h: kk_sc[h, s:s + 1, :]
            stages[h] = _hgrn_chunk(q_ref[pl.ds(r0, chunk), hsl[h]], kk,
                                    v_ref[pl.ds(r0, chunk), hsl[h]], g_ref[pl.ds(r0, chunk), hsl[h]],
                                    st_sc[h], tril, chunk, cum_sc.at[h], krow)
        while stages:
            for h in list(stages):
                out = next(stages[h])
                if out is not None:
                    o_ref[pl.ds(r0, chunk), hsl[h]] = out[0]
                    st_sc[h] = out[1]
                    del stages[h]
        return 0

    lax.fori_loop(0, tile // chunk, body, 0)

    @pl.when(t == pl.num_programs(1) - 1)
    def _():
        for h in range(B_HEADS):
            st_ref[0, h] = st_sc[h].T if transpose_out else st_sc[h]


def _hgrn(hq, hk, hv, hg, st0, batch, seq, chunk, tile, transpose_out):
    nt = seq // tile
    row = pl.BlockSpec((tile, B_FDIM), lambda b, t: (b * nt + t, 0))
    return pl.pallas_call(
        functools.partial(_hgrn_kernel, chunk=chunk, tile=tile, transpose_out=transpose_out),
        grid=(batch, nt),
        in_specs=[row, row, row, row, _full(st0.shape)],
        out_specs=[row, pl.BlockSpec((1, B_HEADS, B_DV, B_DK), lambda b, t: (b, 0, 0, 0))],
        out_shape=[jax.ShapeDtypeStruct((batch * seq, B_WIDTH), F32),
                   jax.ShapeDtypeStruct((batch, B_HEADS, B_DV, B_DK), F32)],
        scratch_shapes=[pltpu.VMEM((B_HEADS, B_DV, B_DK), F32), pltpu.VMEM((B_HEADS, chunk, B_DK), F32),
                        pltpu.VMEM((B_HEADS, chunk, B_DK), F32)],
        compiler_params=_params(2), name="hgrn_chunks")(hq, hk, hv, hg, st0)


def _split2(x):
    hi = x.astype(BF16)
    return hi, (x - hi.astype(F32)).astype(BF16)


def _paged_kernel(pt_ref, lat_hbm, kr_hbm, qrow_ref, qr_ref, qh_ref, kh_ref, latn_ref, wukt_ref,
                  gkrow_ref, gkr_ref, wuv_ref, o_ref, lhs, latbuf, krbuf, cbuf, kx_sc, sems, *, layer, n_pages):
    b = pl.program_id(0)
    gp = PAGES_PER_GROUP
    n_groups = n_pages // gp
    gpos = gp * PAGE
    nk = A_HEADS * A_NOPE
    log2e = 1.4426950408889634
    sc2 = (A_QK ** -0.5) * log2e

    def page_copies(bb, g, slot):
        out = []
        for j in range(gp):
            page = pt_ref[bb, g * gp + j]
            dst = pl.ds(j * PAGE, PAGE)
            out.append(pltpu.make_async_copy(lat_hbm.at[layer, page], latbuf.at[slot, dst], sems.at[0, slot]))
            out.append(pltpu.make_async_copy(kr_hbm.at[layer, page], krbuf.at[slot, j], sems.at[1, slot]))
        return out

    def start(bb, g, slot):
        for cp in page_copies(bb, g, slot):
            cp.start()

    def wait(g, slot):
        for cp in page_copies(b, g, slot):
            cp.wait()

    ring = DMA_SLOTS
    ahead = ring - 1
    assert n_groups % ring == 0 and ahead <= n_groups

    def fetch_ahead(g):
        gg = g + ahead
        wrap = gg >= n_groups
        bb = jnp.where(wrap, b + 1, b)
        g2 = jnp.where(wrap, gg - n_groups, gg)

        @pl.when(bb < pl.num_programs(0))
        def _():
            start(bb, g2, lax.rem(gg, ring))

    @pl.when(b == 0)
    def _():
        for d in range(ahead):
            start(0, d, d)

    @pl.when(b == 0)
    def _():
        lhs[0:nk, :] = wukt_ref[...]

    hrow = lax.broadcasted_iota(jnp.int32, (A_HEADS, nk), 0)
    hcol = lax.broadcasted_iota(jnp.int32, (A_HEADS, nk), 1) // A_NOPE
    qsel = jnp.where(hrow == hcol, qrow_ref[0] * gkrow_ref[...], 0.0)
    q_hi, q_lo = _split2(qsel)
    qabs = _dot(q_hi, wukt_ref[...]) + _dot(q_lo, wukt_ref[...])
    lhs[nk:nk + 2 * A_HEADS, :] = jnp.concatenate(_split2(qabs), axis=0)
    qrg = jnp.concatenate(_split2(qr_ref[0] * gkr_ref[...]), axis=0)
    ppb = POS_BLOCK // PAGE

    def scores(slot, cslot, between=None):
        parts = []
        nblk = gpos // POS_BLOCK

        def project(j):
            rows = pl.ds(j * POS_BLOCK, POS_BLOCK)
            cb = latbuf[slot, rows, :].astype(BF16)
            cbuf[cslot, rows, :] = cb
            kx_sc[j] = _dot_nt(lhs[...], cb)

        for j in range(nblk):
            project(j)
        if between is not None:
            between()
        for j in range(nblk):
            kn = kx_sc[j, 0:nk, :]
            ss = jnp.sum((kn * kn).reshape(A_HEADS, A_NOPE, POS_BLOCK), axis=1)
            raw = kx_sc[j, nk:nk + A_HEADS, :] + kx_sc[j, nk + A_HEADS:nk + 2 * A_HEADS, :]
            krt = jnp.concatenate([krbuf[slot, j * ppb + t] for t in range(ppb)], axis=-1)
            rr = _dot(qrg, krt.astype(BF16))
            raw = raw + rr[:A_HEADS] + rr[A_HEADS:]
            ss = ss + jnp.sum(krt * krt, axis=0, keepdims=True)
            parts.append(raw * lax.rsqrt(ss * (1.0 / A_QK) + EPS) * sc2)
        return jnp.concatenate(parts, axis=-1)

    def absorb(s, slot, carry):
        m, l, acc = carry
        m_new = jnp.maximum(m, jnp.max(s, axis=-1, keepdims=True))
        a = jnp.exp2(m - m_new)
        p = jnp.exp2(s - m_new)
        l = l * a + jnp.sum(p, axis=-1, keepdims=True)
        acc = acc * a + _dot(p.astype(BF16), cbuf[slot])
        return m_new, l, acc

    wait(0, 0)
    fetch_ahead(0)
    s0 = scores(0, 0)
    init = (jnp.full((A_HEADS, 1), -jnp.inf, F32), jnp.zeros((A_HEADS, 1), F32),
            jnp.zeros((A_HEADS, KV_LORA), F32))

    def group(g, carry):
        s_prev, state = carry
        cslot = lax.rem(g, 2)
        wait(g, lax.rem(g, ring))
        fetch_ahead(g)
        box = []
        s_cur = scores(lax.rem(g, ring), cslot, lambda: box.append(absorb(s_prev, 1 - cslot, state)))
        return s_cur, box[0]

    s_last, state = lax.fori_loop(1, n_groups, group, (s0, init))
    m, l, acc = absorb(s_last, (n_groups - 1) % 2, state)

    s_new = jnp.sum(qh_ref[0].astype(F32) * kh_ref[0].astype(F32), axis=-1, keepdims=True) * sc2
    m_new = jnp.maximum(m, s_new)
    a = jnp.exp2(m - m_new)
    p = jnp.exp2(s_new - m_new)
    acc = acc * a + p * latn_ref[0]
    l = l * a + p
    o_lat = (acc / l).astype(BF16)
    full = _dot(o_lat, wuv_ref[...])
    vrow = lax.broadcasted_iota(jnp.int32, (A_HEADS, A_WIDTH), 0)
    vcol = lax.broadcasted_iota(jnp.int32, (A_HEADS, A_WIDTH), 1) // A_V
    o_ref[0] = jnp.sum(jnp.where(vrow == vcol, full, 0.0), axis=0, keepdims=True)


def _paged(page_table, cache_lat, cache_kr, qrow, qr, qh, kh, latn, wukt, gkrow, gkr, wuv, layer):
    nb, n_pages = page_table.shape
    gp = PAGES_PER_GROUP
    per_b = lambda shape: pl.BlockSpec((1,) + shape, lambda b, pt: (b,) + (0,) * len(shape))
    full = lambda a: pl.BlockSpec(a.shape, lambda b, pt: (0,) * a.ndim)
    grid_spec = pltpu.PrefetchScalarGridSpec(
        num_scalar_prefetch=1, grid=(nb,),
        in_specs=[pl.BlockSpec(memory_space=pl.ANY), pl.BlockSpec(memory_space=pl.ANY),
                  per_b((1, A_HEADS * A_NOPE)), per_b((A_HEADS, A_ROPE)), per_b((A_HEADS, HEAD_PAD)),
                  per_b((A_HEADS, HEAD_PAD)), per_b((1, KV_LORA)), full(wukt), full(gkrow), full(gkr), full(wuv)],
        out_specs=per_b((1, A_WIDTH)),
        scratch_shapes=[pltpu.VMEM((A_HEADS * A_NOPE + 2 * A_HEADS, KV_LORA), BF16),
                        pltpu.VMEM((DMA_SLOTS, gp * PAGE, KV_LORA), F32), pltpu.VMEM((DMA_SLOTS, gp, A_ROPE, PAGE), F32),
                        pltpu.VMEM((2, gp * PAGE, KV_LORA), BF16),
                        pltpu.VMEM((gp * PAGE // POS_BLOCK, A_HEADS * A_NOPE + 2 * A_HEADS, POS_BLOCK), F32),
                        pltpu.SemaphoreType.DMA((2, DMA_SLOTS))])
    return pl.pallas_call(
        functools.partial(_paged_kernel, layer=layer, n_pages=n_pages), grid_spec=grid_spec,
        out_shape=jax.ShapeDtypeStruct((nb, 1, A_WIDTH), F32),
        compiler_params=_params(1), name="paged_attn")(
            page_table, cache_lat, cache_kr, qrow, qr, qh, kh, latn, wukt, gkrow, gkr, wuv)


def _hstep_kernel(s_ref, q_ref, k_ref, g_ref, v_ref, o_ref, sn_ref, *, layer, rows):
    r = lax.broadcasted_iota(jnp.int32, (B_DK, B_DK), 0)
    c = lax.broadcasted_iota(jnp.int32, (B_DK, B_DK), 1)
    eye = r == c

    def column(row):
        return jnp.sum(jnp.where(eye, row, 0.0), axis=-1, keepdims=True)

    for i in range(rows):
        for h in range(B_HEADS):
            hs = slice(h * B_DK, (h + 1) * B_DK)
            sn = column(jnp.exp(g_ref[i, :, hs])) * s_ref[layer, i, h] + column(k_ref[i, :, hs]) * v_ref[i, :, hs]
            sn_ref[i, h] = sn
            o_ref[i, :, hs] = jnp.sum(column(q_ref[i, :, hs]) * sn, axis=0, keepdims=True)


def _hstep(state, q, k, g, v, layer):
    nb = state.shape[1]
    rows = HSTEP_ROWS
    st_in = pl.BlockSpec((state.shape[0], rows, B_HEADS, B_DK, B_DV), lambda b: (0, b, 0, 0, 0))
    st_out = pl.BlockSpec((rows, B_HEADS, B_DK, B_DV), lambda b: (b, 0, 0, 0))
    row = pl.BlockSpec((rows, 1, B_FDIM), lambda b: (b, 0, 0))
    return pl.pallas_call(
        functools.partial(_hstep_kernel, layer=layer, rows=rows), grid=(nb // rows,),
        in_specs=[st_in, row, row, row, row], out_specs=[row, st_out],
        out_shape=[jax.ShapeDtypeStruct((nb, 1, B_WIDTH), F32), jax.ShapeDtypeStruct(state.shape[1:], F32)],
        compiler_params=_params(1), name="hgrn_step")(state, q, k, g, v)


def _merge_kernel(x_ref, ng_ref, wg_ref, oa_ref, ob_ref, gbn_ref, woa_ref, wob_ref, wo_ref, y_ref):
    x = x_ref[...]
    xn = _rms(x, ng_ref[...]).astype(BF16)
    gates = _dot(xn, wg_ref[...])
    ga, gb = gates[:, :A_WIDTH], gates[:, A_WIDTH:A_WIDTH + B_WIDTH]
    ma = gates[:, A_WIDTH + B_WIDTH:A_WIDTH + B_WIDTH + D_MODEL]
    mb = gates[:, A_WIDTH + B_WIDTH + D_MODEL:]
    ya = _dot((oa_ref[...] * (ga * jax.nn.sigmoid(ga))).astype(BF16), woa_ref[...])
    gbn = gbn_ref[...]
    obn = jnp.concatenate([_rms(ob_ref[:, h * B_DV:(h + 1) * B_DV], gbn) for h in range(B_HEADS)], axis=-1)
    yb = _dot((obn * (gb * jax.nn.sigmoid(gb))).astype(BF16), wob_ref[...])
    mix = jax.nn.sigmoid(ma) * ya + jax.nn.sigmoid(mb) * yb
    y_ref[...] = x + _dot(mix.astype(BF16), wo_ref[...])


def _merge(x, oa, ob, w, tm):
    rows = x.shape[0]
    row = lambda n: pl.BlockSpec((tm, n), lambda i: (i, 0))
    ins = [x, w["norm_g"], w["wg"], oa, ob, w["g_bn"], w["w_oa"], w["w_ob"], w["w_o"]]
    in_specs = [row(D_MODEL), _full(ins[1].shape), _full(ins[2].shape), row(A_WIDTH), row(B_WIDTH)] + \
               [_full(a.shape) for a in ins[5:]]
    return pl.pallas_call(
        _merge_kernel, grid=(rows // tm,), in_specs=in_specs, out_specs=row(D_MODEL),
        out_shape=jax.ShapeDtypeStruct((rows, D_MODEL), F32),
        compiler_params=_params(1), name="merge_out")(*ins)


def _head_pad_cols(w3, width):
    pad = jnp.zeros(w3.shape[:2] + (HEAD_PAD - width,), w3.dtype)
    return jnp.concatenate([w3, pad], axis=-1).reshape(w3.shape[0], -1)


def _rope_tables(pos):
    half = A_ROPE // 2
    inv = ROPE_THETA ** (-jnp.arange(half, dtype=F32) / half)
    ang = pos.astype(F32)[:, None] * inv
    cos, sin = jnp.cos(ang), jnp.sin(ang)
    n = pos.shape[0]
    one, zero = jnp.ones((n, A_NOPE), F32), jnp.zeros((n, A_NOPE), F32)
    tail = jnp.zeros((n, HEAD_PAD - A_QK), F32)
    z16 = jnp.zeros((n, half), F32)
    c = jnp.concatenate([one, cos, cos, tail], axis=-1)
    s1 = jnp.concatenate([zero, -sin, z16, tail], axis=-1)
    s2 = jnp.concatenate([zero, z16, sin, tail], axis=-1)
    return c, s1, s2


def _lane_gain(g):
    return jnp.concatenate([g, jnp.zeros((HEAD_PAD - A_QK,), g.dtype)])[None, :]


def kernel(x_prompt, x_sample, cache_latent, cache_krope, state_hgrn, page_table, meta_tokens,
           norm_g, w_in, g_cq, w_uq, g_ckv, w_uk, w_uv, g_qn, g_kn, lb_logits, g_bn, w_oa, w_ob, w_o):
    batch, seq, _ = x_prompt.shape
    dec_batch, dec_seq, _ = x_sample.shape
    depth = w_in.shape[0]
    assert depth == 1 and dec_seq == 1
    past_len = page_table.shape[1] * PAGE
    l = 0

    o0 = Q_LORA
    o1 = o0 + KV_LORA
    o2 = o1 + A_ROPE
    o3 = o2 + 3 * B_FDIM
    wi = w_in[l]
    wkr = jnp.zeros((D_MODEL, HEAD_PAD), F32).at[:, A_NOPE:A_QK].set(wi[:, o1:o2])
    w = {
        "norm_g": norm_g[l][None, :],
        "wa": jnp.concatenate([wi[:, :o1], wkr, wi[:, o2:o3]], axis=1).astype(BF16), "wg": wi[:, o3:].astype(BF16),
        "g_cq": g_cq[l][None, :], "g_ckv": g_ckv[l][None, :],
        "w_uq": _head_pad_cols(w_uq[l].reshape(Q_LORA, A_HEADS, A_QK), A_QK).astype(BF16),
        "w_uk": _head_pad_cols(w_uk[l], A_NOPE).astype(BF16),
        "w_uv": w_uv[l].reshape(KV_LORA, A_WIDTH).astype(BF16),
        "w_uvt": w_uv[l].reshape(KV_LORA, A_WIDTH).T.astype(BF16),
        "g_qn": _lane_gain(g_qn[l]), "g_kn": _lane_gain(g_kn[l]),
        "lb_logits": lb_logits, "g_bn": g_bn[l][None, :],
        "w_oa": w_oa[l].astype(BF16), "w_ob": w_ob[l].astype(BF16), "w_o": w_o[l].astype(BF16),
    }

    xs = x_sample.reshape(dec_batch, D_MODEL)
    tabs_m = _rope_tables(jnp.arange(N_META))
    tabs_s = _rope_tables(jnp.full((dec_batch,), past_len, jnp.int32))
    tabs_ms = tuple(jnp.concatenate([a, b], axis=0) for a, b in zip(tabs_m, tabs_s))
    small = _pre(jnp.concatenate([meta_tokens, xs], axis=0), tabs_ms, w, N_META + dec_batch, 1, l)
    lat_m, krp_m, _, k_m, _, mq, mk, mv, mg = [a[:N_META] for a in small]
    lat_s, krp_s, q_s, k_s, _, sq, sk, sv, sg = [a[N_META:] for a in small]
    vt_m = small[4][:, :N_META]

    xp = x_prompt.reshape(batch * seq, D_MODEL)
    tabs_p = _rope_tables(N_META + jnp.arange(seq))
    qscale = (A_QK ** -0.5) * 1.4426950408889634
    lat_all, kr_all, q_p, k_p, vt_p, hq, hk, hv, hg = _pre(
        xp, tabs_p, w, PRE_TILE, seq // PRE_TILE, l, qscale, direct=(batch, seq, lat_m, krp_m[:, A_NOPE:A_QK]))

    oa = _attn(q_p, k_p, vt_p, k_m, vt_m, batch, seq)
    zero_state = jnp.zeros((B_HEADS, B_DV, B_DK), F32)
    _, st_meta = _hgrn(mq, mk, mv, mg, zero_state, 1, N_META, N_META, N_META, False)
    ob, st_fin = _hgrn(hq, hk, hv, hg, st_meta[0], batch, seq, HGRN_CHUNK, HGRN_TILE, True)
    y_prompt = _merge(xp, oa, ob, w, ROW_TILE).reshape(batch, seq, D_MODEL)

    q_s3 = q_s.reshape(dec_batch, A_HEADS, HEAD_PAD)
    k_s3 = k_s.reshape(dec_batch, A_HEADS, HEAD_PAD)
    qrow = q_s3[:, :, :A_NOPE].astype(F32).reshape(dec_batch, 1, A_HEADS * A_NOPE)
    qrope = q_s3[:, :, A_NOPE:A_QK].astype(F32)
    gkrow = jnp.tile(g_kn[l][:A_NOPE], A_HEADS)[None, :]
    gkr = g_kn[l][None, A_NOPE:]
    wukt = w_uk[l].reshape(KV_LORA, A_HEADS * A_NOPE).T.astype(BF16)
    oa_s = _paged(page_table, cache_latent, jnp.swapaxes(cache_krope, 2, 3), qrow, qrope, q_s3, k_s3,
                  lat_s[:, None, :], wukt, gkrow, gkr, w["w_uv"], l)
    row3 = lambda a: a.reshape(dec_batch, 1, B_FDIM)
    ob_s, st_s = _hstep(state_hgrn, row3(sq), row3(sk), row3(sg), row3(sv), l)
    y_sample = _merge(xs, oa_s.reshape(dec_batch, A_WIDTH), ob_s.reshape(dec_batch, B_WIDTH), w, dec_batch)

    return (y_prompt, y_sample.reshape(dec_batch, dec_seq, D_MODEL),
            lat_all[None], kr_all[None], st_fin[None],
            lat_s.reshape(1, dec_batch, dec_seq, KV_LORA),
            krp_s[:, A_NOPE:A_QK].reshape(1, dec_batch, dec_seq, A_ROPE),
            st_s[None])
```

```python
import functools

import jax
import jax.numpy as jnp
from jax import lax
from jax.experimental import pallas as pl
from jax.experimental.pallas import tpu as pltpu

F32 = jnp.float32
BF16 = jnp.bfloat16

D_MODEL = 1024
N_META = 16
A_HEADS = 8
A_NOPE = 64
A_ROPE = 32
A_QK = A_NOPE + A_ROPE
A_V = 64
A_WIDTH = A_HEADS * A_V
Q_LORA = 384
KV_LORA = 256
ROPE_THETA = 10000.0
B_HEADS = 4
B_DK = 128
B_DV = 128
B_FDIM = B_HEADS * B_DK
B_WIDTH = B_HEADS * B_DV
EPS = 1e-6
PAGE = 128

LANES = 128
LOG2E = 1.4426950408889634
HEAD_PAD = LANES
VMEM_LIMIT = 56 * 1024 * 1024

PRE_TILE = 256
ROW_TILE = 512
ATTN_TQ = 256
ATTN_TK = 256
HGRN_CHUNK = 128
HGRN_TILE = 256
PAGES_PER_GROUP = 32
HSTEP_ROWS = 4
POS_BLOCK = 2 * PAGE
DMA_SLOTS = 4


def _full(shape):
    return pl.BlockSpec(shape, lambda *_: (0,) * len(shape))


def _params(n_axes):
    return pltpu.CompilerParams(dimension_semantics=("arbitrary",) * n_axes,
                                vmem_limit_bytes=VMEM_LIMIT)


def _dot(a, b):
    return jnp.dot(a, b, preferred_element_type=F32)


def _dot_nt(a, b, precision=None):
    return lax.dot_general(a, b, (((1,), (1,)), ((), ())), precision=precision,
                           preferred_element_type=F32)


def _rms(x, g):
    r = lax.rsqrt(jnp.mean(x * x, axis=-1, keepdims=True) + EPS)
    return (x * r) * g


def _rope_lanes(t, c, s1, s2):
    return t * c + pltpu.roll(t, LANES - A_ROPE // 2, axis=1) * s1 + pltpu.roll(t, A_ROPE // 2, axis=1) * s2


def _pre_kernel(x_ref, ng_ref, wa_ref, gcq_ref, wuq_ref, gckv_ref,
                wuk_ref, wuv_ref, gqn_ref, gkn_ref, c_ref, s1_ref, s2_ref, lb_ref,
                lat_ref, krp_ref, q_ref, k_ref, vt_ref, hq_ref, hk_ref, hv_ref, hg_ref, *, layer, qscale):
    x = x_ref[...]
    xn = _rms(x, ng_ref[...]).astype(BF16)
    c, s1, s2 = c_ref[...], s1_ref[...], s2_ref[...]

    o1, o2, o3 = Q_LORA, Q_LORA + KV_LORA, Q_LORA + KV_LORA + LANES
    y = _dot(xn, wa_ref[:, :o3])
    cq = y[:, :o1]
    cqn = _rms(cq, gcq_ref[...]).astype(BF16)
    qraw = _dot(cqn, wuq_ref[...])
    gqn = gqn_ref[...]
    for h in range(A_HEADS):
        t = _rope_lanes(qraw[:, h * HEAD_PAD:(h + 1) * HEAD_PAD], c, s1, s2)
        r = lax.rsqrt(jnp.sum(t * t, axis=-1, keepdims=True) * (1.0 / A_QK) + EPS)
        q_ref[:, h * HEAD_PAD:(h + 1) * HEAD_PAD] = (((t * r) * gqn) * qscale).astype(BF16)

    ckv = _rms(y[:, o1:o2], gckv_ref[...])
    lat_ref[...] = ckv
    ckvb = ckv.astype(BF16)
    krp = _rope_lanes(y[:, o2:o3], c, s1, s2)
    krp_ref[...] = krp
    kraw = _dot(ckvb, wuk_ref[...])
    gkn = gkn_ref[...]
    for h in range(A_HEADS):
        t = kraw[:, h * HEAD_PAD:(h + 1) * HEAD_PAD] + krp
        r = lax.rsqrt(jnp.sum(t * t, axis=-1, keepdims=True) * (1.0 / A_QK) + EPS)
        k_ref[:, h * HEAD_PAD:(h + 1) * HEAD_PAD] = ((t * r) * gkn).astype(BF16)
    vt_ref[...] = _dot_nt(wuv_ref[...], ckvb).astype(BF16)

    lbl = lb_ref[...]
    e = jnp.exp(lbl - jnp.max(lbl, axis=0, keepdims=True))
    lb = jnp.sum(e[:layer + 1], axis=0, keepdims=True) / jnp.sum(e, axis=0, keepdims=True)
    b = _dot(xn, wa_ref[:, o3:])
    bq, z, bi = b[:, :B_FDIM], b[:, B_FDIM:2 * B_FDIM], b[:, 2 * B_FDIM:]
    hq_ref[...] = bq * jax.nn.sigmoid(bq)
    sz = jax.nn.sigmoid(z)
    hg_ref[...] = jnp.log(lb + (1.0 - lb) * sz)
    hk_ref[...] = (1.0 - lb) * (1.0 - sz)
    hv_ref[...] = bi


def _pre(x, tabs, w, tm, tab_blocks, layer, qscale=1.0):
    rows = x.shape[0]
    grid = (rows // tm,)
    row = lambda n: pl.BlockSpec((tm, n), lambda i: (i, 0))
    tab = pl.BlockSpec((tm, LANES), lambda i: (i % tab_blocks, 0))
    ins = [x, w["norm_g"], w["wa"], w["g_cq"], w["w_uq"], w["g_ckv"],
           w["w_uk"], w["w_uvt"], w["g_qn"], w["g_kn"], tabs[0], tabs[1], tabs[2], w["lb_logits"]]
    in_specs = [row(D_MODEL)] + [_full(a.shape) for a in ins[1:10]] + [tab, tab, tab] + [_full(ins[13].shape)]
    outs = [(KV_LORA, F32), (LANES, F32), (A_HEADS * HEAD_PAD, BF16), (A_HEADS * HEAD_PAD, BF16),
            None, (B_FDIM, F32), (B_FDIM, F32), (B_WIDTH, F32), (B_FDIM, F32)]
    out_specs = [pl.BlockSpec((A_WIDTH, tm), lambda i: (0, i)) if o is None else row(o[0]) for o in outs]
    out_shape = [jax.ShapeDtypeStruct((A_WIDTH, rows), BF16) if o is None else
                 jax.ShapeDtypeStruct((rows, o[0]), o[1]) for o in outs]
    return pl.pallas_call(
        functools.partial(_pre_kernel, layer=layer, qscale=qscale),
        grid=grid, in_specs=in_specs, out_specs=out_specs, out_shape=out_shape,
        compiler_params=_params(1), name="pre_proj")(*ins)


def _attn_kernel(q_ref, k_ref, vt_ref, km_ref, vtm_ref, o_ref, ot_sc, s_sc, sm_sc, m_sc, l_sc):
    i = pl.program_id(1)
    tq, tk = ATTN_TQ, ATTN_TK
    assert tq == tk
    key = lax.broadcasted_iota(jnp.int32, (tk, tq), 0)
    qry = lax.broadcasted_iota(jnp.int32, (tk, tq), 1)
    hsl = [slice(h * HEAD_PAD, (h + 1) * HEAD_PAD) for h in range(A_HEADS)]
    vsl = [slice(h * A_V, (h + 1) * A_V) for h in range(A_HEADS)]

    d0 = pl.multiple_of(i * tq, tq)
    for h in range(A_HEADS):
        s_sc[h, 0:tk, :] = _dot_nt(k_ref[pl.ds(d0, tk), hsl[h]], q_ref[:, hsl[h]])
        sm_sc[h] = _dot_nt(km_ref[:, hsl[h]], q_ref[:, hsl[h]])
    for h in range(A_HEADS):
        hr = slice(h, h + 1)
        s = jnp.where(key <= qry, s_sc[h, 0:tk, :], -jnp.inf)
        sm = sm_sc[h]
        m_new = jnp.maximum(jnp.max(s, axis=0, keepdims=True), jnp.max(sm, axis=0, keepdims=True))
        p = jnp.exp2(s - m_new)
        pm = jnp.exp2(sm - m_new)
        m_sc[hr, :] = m_new
        l_sc[hr, :] = jnp.sum(p, axis=0, keepdims=True) + jnp.sum(pm, axis=0, keepdims=True)
        ot_sc[vsl[h], :] = (_dot(vt_ref[vsl[h], pl.ds(d0, tk)], p.astype(BF16)) +
                            _dot(vtm_ref[vsl[h], :], pm.astype(BF16)))

    def visible(r0, keys):
        for h in range(A_HEADS):
            s_sc[h, 0:keys, :] = _dot_nt(k_ref[pl.ds(r0, keys), hsl[h]], q_ref[:, hsl[h]])
        for h in range(A_HEADS):
            hr = slice(h, h + 1)
            s = s_sc[h, 0:keys, :]
            m_new = jnp.maximum(m_sc[hr, :], jnp.max(s, axis=0, keepdims=True))
            a = jnp.exp2(m_sc[hr, :] - m_new)
            p = jnp.exp2(s - m_new)
            m_sc[hr, :] = m_new
            l_sc[hr, :] = l_sc[hr, :] * a + jnp.sum(p, axis=0, keepdims=True)
            ot_sc[vsl[h], :] = ot_sc[vsl[h], :] * a + _dot(vt_ref[vsl[h], pl.ds(r0, keys)], p.astype(BF16))

    def body(kb, _):
        visible(pl.multiple_of(kb * 2 * tk, 2 * tk), 2 * tk)
        return 0

    lax.fori_loop(0, i // 2, body, 0)

    @pl.when(i % 2 == 1)
    def _():
        visible(pl.multiple_of((i - 1) * tk, tk), tk)

    for h in range(A_HEADS):
        ot_sc[vsl[h], :] = ot_sc[vsl[h], :] / l_sc[h:h + 1, :]
    o_ref[...] = ot_sc[...].T


def _attn(q, k, vt, km, vtm, batch, seq):
    nq = seq // ATTN_TQ
    return pl.pallas_call(
        _attn_kernel, grid=(batch, nq),
        in_specs=[pl.BlockSpec((ATTN_TQ, A_HEADS * HEAD_PAD), lambda b, i: (b * nq + i, 0)),
                  pl.BlockSpec((seq, A_HEADS * HEAD_PAD), lambda b, i: (b, 0)),
                  pl.BlockSpec((A_WIDTH, seq), lambda b, i: (0, b)),
                  _full(km.shape), _full(vtm.shape)],
        out_specs=pl.BlockSpec((ATTN_TQ, A_WIDTH), lambda b, i: (b * nq + i, 0)),
        out_shape=jax.ShapeDtypeStruct((batch * seq, A_WIDTH), F32),
        scratch_shapes=[pltpu.VMEM((A_WIDTH, ATTN_TQ), F32), pltpu.VMEM((A_HEADS, 2 * ATTN_TK, ATTN_TQ), F32),
                        pltpu.VMEM((A_HEADS, km.shape[0], ATTN_TQ), F32),
                        pltpu.VMEM((A_HEADS, ATTN_TQ), F32), pltpu.VMEM((A_HEADS, ATTN_TQ), F32)],
        compiler_params=_params(2), name="prompt_attn")(q, k, vt, km, vtm)


def _split3(x):
    a = x.astype(BF16)
    r = x - a.astype(F32)
    b = r.astype(BF16)
    c = (r - b.astype(F32)).astype(BF16)
    return a, b, c


def _hgrn_chunk(q, kk, v, g, st, tril, chunk, cum_ref, krow):
    g1, g2, g3 = _split3(g)
    cum = (_dot(tril, g1) + _dot(tril, g2) + _dot(tril, g3)) * LOG2E
    cum_ref[...] = cum
    yield None
    nblk = chunk // 8
    cb = [cum[8 * j:8 * j + 8] for j in range(nblk)]
    qb = [q[8 * j:8 * j + 8] for j in range(nblk)]
    lane = lax.broadcasted_iota(jnp.int32, (8, chunk), 1)
    sub = lax.broadcasted_iota(jnp.int32, (8, chunk), 0)
    ab = []
    for j in range(nblk):
        a = jnp.zeros((8, chunk), F32)
        for s in range(8 * j, 8 * j + 8):
            col = jnp.sum(jnp.exp2(cb[j] - cum_ref[s:s + 1, :]) * qb[j] * krow(s), axis=-1, keepdims=True)
            a = jnp.where(lane == s, col, a)
        ab.append(jnp.where(lane <= sub + 8 * j, a, 0.0))
        yield None
    attn = jnp.concatenate(ab, axis=0)
    row = lax.broadcasted_iota(jnp.int32, (chunk, chunk), 0)
    col = lax.broadcasted_iota(jnp.int32, (chunk, chunk), 1)
    kb = [kk[8 * j:8 * j + 8] for j in range(nblk)]
    dead = jnp.zeros((8, B_DK), F32)
    w = 8
    while w < chunk:
        qs, ks = [], []
        for j in range(nblk):
            blk = (8 * j) // w
            if blk % 2:
                qs.append(qb[j] * jnp.exp2(cb[j] - cum_ref[blk * w - 1:blk * w, :]))
                ks.append(dead)
            else:
                qs.append(dead)
                ks.append(kb[j] * jnp.exp2(cum_ref[blk * w + w - 1:blk * w + w, :] - cb[j]))
        qa = jnp.concatenate(qs, axis=0).astype(BF16)
        ka = jnp.concatenate(ks, axis=0).astype(BF16)
        sh = w.bit_length() - 1
        rb, cbk = row >> sh, col >> sh
        attn = attn + jnp.where((rb == cbk + 1) & ((cbk & 1) == 0), _dot_nt(qa, ka), 0.0)
        w *= 2
        yield None
    last = cum_ref[chunk - 1:chunk, :]
    o = _dot(attn.astype(BF16), v.astype(BF16)) + _dot_nt((q * jnp.exp2(cum)).astype(BF16), st.astype(BF16))
    kdec = (kk * jnp.exp2(last - cum)).astype(BF16)
    upd = lax.dot_general(v.astype(BF16), kdec, (((0,), (0,)), ((), ())), preferred_element_type=F32)
    yield o, jnp.exp2(last) * st + upd


def _hgrn_kernel(q_ref, k_ref, v_ref, g_ref, st0_ref, o_ref, st_ref, st_sc, cum_sc, kk_sc, *, chunk, tile,
                 transpose_out):
    t = pl.program_id(1)

    @pl.when(t == 0)
    def _():
        st_sc[...] = st0_ref[...]

    r = lax.broadcasted_iota(jnp.int32, (chunk, chunk), 0)
    c = lax.broadcasted_iota(jnp.int32, (chunk, chunk), 1)
    tril = (c <= r).astype(BF16)

    def body(ci, _):
        r0 = pl.multiple_of(ci * chunk, chunk)
        hsl = [slice(h * B_DK, (h + 1) * B_DK) for h in range(B_HEADS)]
        stages = {}
        for h in range(B_HEADS):
            kk = k_ref[pl.ds(r0, chunk), hsl[h]]
            kk_sc[h] = kk
            krow = lambda s, h=h: kk_sc[h, s:s + 1, :]
            stages[h] = _hgrn_chunk(q_ref[pl.ds(r0, chunk), hsl[h]], kk,
                                    v_ref[pl.ds(r0, chunk), hsl[h]], g_ref[pl.ds(r0, chunk), hsl[h]],
                                    st_sc[h], tril, chunk, cum_sc.at[h], krow)
        while stages:
            for h in list(stages):
                out = next(stages[h])
                if out is not None:
                    o_ref[pl.ds(r0, chunk), hsl[h]] = out[0]
                    st_sc[h] = out[1]
                    del stages[h]
        return 0

    lax.fori_loop(0, tile // chunk, body, 0)

    @pl.when(t == pl.num_programs(1) - 1)
    def _():
        for h in range(B_HEADS):
            st_ref[0, h] = st_sc[h].T if transpose_out else st_sc[h]


def _hgrn(hq, hk, hv, hg, st0, batch, seq, chunk, tile, transpose_out):
    nt = seq // tile
    row = pl.BlockSpec((tile, B_FDIM), lambda b, t: (b * nt + t, 0))
    return pl.pallas_call(
        functools.partial(_hgrn_kernel, chunk=chunk, tile=tile, transpose_out=transpose_out),
        grid=(batch, nt),
        in_specs=[row, row, row, row, _full(st0.shape)],
        out_specs=[row, pl.BlockSpec((1, B_HEADS, B_DV, B_DK), lambda b, t: (b, 0, 0, 0))],
        out_shape=[jax.ShapeDtypeStruct((batch * seq, B_WIDTH), F32),
                   jax.ShapeDtypeStruct((batch, B_HEADS, B_DV, B_DK), F32)],
        scratch_shapes=[pltpu.VMEM((B_HEADS, B_DV, B_DK), F32), pltpu.VMEM((B_HEADS, chunk, B_DK), F32),
                        pltpu.VMEM((B_HEADS, chunk, B_DK), F32)],
        compiler_params=_params(2), name="hgrn_chunks")(hq, hk, hv, hg, st0)


def _split2(x):
    hi = x.astype(BF16)
    return hi, (x - hi.astype(F32)).astype(BF16)


def _paged_kernel(pt_ref, lat_hbm, kr_hbm, qrow_ref, qr_ref, qh_ref, kh_ref, latn_ref, wukt_ref,
                  gkrow_ref, gkr_ref, wuv_ref, o_ref, lhs, latbuf, krbuf, cbuf, kx_sc, sems, *, layer, n_pages):
    b = pl.program_id(0)
    gp = PAGES_PER_GROUP
    n_groups = n_pages // gp
    gpos = gp * PAGE
    nk = A_HEADS * A_NOPE
    log2e = 1.4426950408889634
    sc2 = (A_QK ** -0.5) * log2e

    def page_copies(bb, g, slot):
        out = []
        for j in range(gp):
            page = pt_ref[bb, g * gp + j]
            dst = pl.ds(j * PAGE, PAGE)
            out.append(pltpu.make_async_copy(lat_hbm.at[layer, page], latbuf.at[slot, dst], sems.at[0, slot]))
            out.append(pltpu.make_async_copy(kr_hbm.at[layer, page], krbuf.at[slot, j], sems.at[1, slot]))
        return out

    def start(bb, g, slot):
        for cp in page_copies(bb, g, slot):
            cp.start()

    def wait(g, slot):
        for cp in page_copies(b, g, slot):
            cp.wait()

    ring = DMA_SLOTS
    ahead = ring - 1
    assert n_groups % ring == 0 and ahead <= n_groups

    def fetch_ahead(g):
        gg = g + ahead
        wrap = gg >= n_groups
        bb = jnp.where(wrap, b + 1, b)
        g2 = jnp.where(wrap, gg - n_groups, gg)

        @pl.when(bb < pl.num_programs(0))
        def _():
            start(bb, g2, lax.rem(gg, ring))

    @pl.when(b == 0)
    def _():
        for d in range(ahead):
            start(0, d, d)

    @pl.when(b == 0)
    def _():
        lhs[0:nk, :] = wukt_ref[...]

    hrow = lax.broadcasted_iota(jnp.int32, (A_HEADS, nk), 0)
    hcol = lax.broadcasted_iota(jnp.int32, (A_HEADS, nk), 1) // A_NOPE
    qsel = jnp.where(hrow == hcol, qrow_ref[0] * gkrow_ref[...], 0.0)
    q_hi, q_lo = _split2(qsel)
    qabs = _dot(q_hi, wukt_ref[...]) + _dot(q_lo, wukt_ref[...])
    lhs[nk:nk + 2 * A_HEADS, :] = jnp.concatenate(_split2(qabs), axis=0)
    qrg = jnp.concatenate(_split2(qr_ref[0] * gkr_ref[...]), axis=0)
    ppb = POS_BLOCK // PAGE

    def scores(slot, cslot, between=None):
        parts = []
        nblk = gpos // POS_BLOCK

        def project(j):
            rows = pl.ds(j * POS_BLOCK, POS_BLOCK)
            cb = latbuf[slot, rows, :].astype(BF16)
            cbuf[cslot, rows, :] = cb
            kx_sc[j] = _dot_nt(lhs[...], cb)

        for j in range(nblk):
            project(j)
        if between is not None:
            between()
        for j in range(nblk):
            kn = kx_sc[j, 0:nk, :]
            ss = jnp.sum((kn * kn).reshape(A_HEADS, A_NOPE, POS_BLOCK), axis=1)
            raw = kx_sc[j, nk:nk + A_HEADS, :] + kx_sc[j, nk + A_HEADS:nk + 2 * A_HEADS, :]
            krt = jnp.concatenate([krbuf[slot, j * ppb + t] for t in range(ppb)], axis=-1)
            rr = _dot(qrg, krt.astype(BF16))
            raw = raw + rr[:A_HEADS] + rr[A_HEADS:]
            ss = ss + jnp.sum(krt * krt, axis=0, keepdims=True)
            parts.append(raw * lax.rsqrt(ss * (1.0 / A_QK) + EPS) * sc2)
        return jnp.concatenate(parts, axis=-1)

    def absorb(s, slot, carry):
        m, l, acc = carry
        m_new = jnp.maximum(m, jnp.max(s, axis=-1, keepdims=True))
        a = jnp.exp2(m - m_new)
        p = jnp.exp2(s - m_new)
        l = l * a + jnp.sum(p, axis=-1, keepdims=True)
        acc = acc * a + _dot(p.astype(BF16), cbuf[slot])
        return m_new, l, acc

    wait(0, 0)
    fetch_ahead(0)
    s0 = scores(0, 0)
    init = (jnp.full((A_HEADS, 1), -jnp.inf, F32), jnp.zeros((A_HEADS, 1), F32),
            jnp.zeros((A_HEADS, KV_LORA), F32))

    def group(g, carry):
        s_prev, state = carry
        cslot = lax.rem(g, 2)
        wait(g, lax.rem(g, ring))
        fetch_ahead(g)
        box = []
        s_cur = scores(lax.rem(g, ring), cslot, lambda: box.append(absorb(s_prev, 1 - cslot, state)))
        return s_cur, box[0]

    s_last, state = lax.fori_loop(1, n_groups, group, (s0, init))
    m, l, acc = absorb(s_last, (n_groups - 1) % 2, state)

    s_new = jnp.sum(qh_ref[0].astype(F32) * kh_ref[0].astype(F32), axis=-1, keepdims=True) * sc2
    m_new = jnp.maximum(m, s_new)
    a = jnp.exp2(m - m_new)
    p = jnp.exp2(s_new - m_new)
    acc = acc * a + p * latn_ref[0]
    l = l * a + p
    o_lat = (acc / l).astype(BF16)
    full = _dot(o_lat, wuv_ref[...])
    vrow = lax.broadcasted_iota(jnp.int32, (A_HEADS, A_WIDTH), 0)
    vcol = lax.broadcasted_iota(jnp.int32, (A_HEADS, A_WIDTH), 1) // A_V
    o_ref[0] = jnp.sum(jnp.where(vrow == vcol, full, 0.0), axis=0, keepdims=True)


def _paged(page_table, cache_lat, cache_kr, qrow, qr, qh, kh, latn, wukt, gkrow, gkr, wuv, layer):
    nb, n_pages = page_table.shape
    gp = PAGES_PER_GROUP
    per_b = lambda shape: pl.BlockSpec((1,) + shape, lambda b, pt: (b,) + (0,) * len(shape))
    full = lambda a: pl.BlockSpec(a.shape, lambda b, pt: (0,) * a.ndim)
    grid_spec = pltpu.PrefetchScalarGridSpec(
        num_scalar_prefetch=1, grid=(nb,),
        in_specs=[pl.BlockSpec(memory_space=pl.ANY), pl.BlockSpec(memory_space=pl.ANY),
                  per_b((1, A_HEADS * A_NOPE)), per_b((A_HEADS, A_ROPE)), per_b((A_HEADS, HEAD_PAD)),
                  per_b((A_HEADS, HEAD_PAD)), per_b((1, KV_LORA)), full(wukt), full(gkrow), full(gkr), full(wuv)],
        out_specs=per_b((1, A_WIDTH)),
        scratch_shapes=[pltpu.VMEM((A_HEADS * A_NOPE + 2 * A_HEADS, KV_LORA), BF16),
                        pltpu.VMEM((DMA_SLOTS, gp * PAGE, KV_LORA), F32), pltpu.VMEM((DMA_SLOTS, gp, A_ROPE, PAGE), F32),
                        pltpu.VMEM((2, gp * PAGE, KV_LORA), BF16),
                        pltpu.VMEM((gp * PAGE // POS_BLOCK, A_HEADS * A_NOPE + 2 * A_HEADS, POS_BLOCK), F32),
                        pltpu.SemaphoreType.DMA((2, DMA_SLOTS))])
    return pl.pallas_call(
        functools.partial(_paged_kernel, layer=layer, n_pages=n_pages), grid_spec=grid_spec,
        out_shape=jax.ShapeDtypeStruct((nb, 1, A_WIDTH), F32),
        compiler_params=_params(1), name="paged_attn")(
            page_table, cache_lat, cache_kr, qrow, qr, qh, kh, latn, wukt, gkrow, gkr, wuv)


def _hstep_kernel(s_ref, q_ref, k_ref, g_ref, v_ref, o_ref, sn_ref, *, layer, rows):
    r = lax.broadcasted_iota(jnp.int32, (B_DK, B_DK), 0)
    c = lax.broadcasted_iota(jnp.int32, (B_DK, B_DK), 1)
    eye = r == c

    def column(row):
        return jnp.sum(jnp.where(eye, row, 0.0), axis=-1, keepdims=True)

    for i in range(rows):
        for h in range(B_HEADS):
            hs = slice(h * B_DK, (h + 1) * B_DK)
            sn = column(jnp.exp(g_ref[i, :, hs])) * s_ref[layer, i, h] + column(k_ref[i, :, hs]) * v_ref[i, :, hs]
            sn_ref[i, h] = sn
            o_ref[i, :, hs] = jnp.sum(column(q_ref[i, :, hs]) * sn, axis=0, keepdims=True)


def _hstep(state, q, k, g, v, layer):
    nb = state.shape[1]
    rows = HSTEP_ROWS
    st_in = pl.BlockSpec((state.shape[0], rows, B_HEADS, B_DK, B_DV), lambda b: (0, b, 0, 0, 0))
    st_out = pl.BlockSpec((rows, B_HEADS, B_DK, B_DV), lambda b: (b, 0, 0, 0))
    row = pl.BlockSpec((rows, 1, B_FDIM), lambda b: (b, 0, 0))
    return pl.pallas_call(
        functools.partial(_hstep_kernel, layer=layer, rows=rows), grid=(nb // rows,),
        in_specs=[st_in, row, row, row, row], out_specs=[row, st_out],
        out_shape=[jax.ShapeDtypeStruct((nb, 1, B_WIDTH), F32), jax.ShapeDtypeStruct(state.shape[1:], F32)],
        compiler_params=_params(1), name="hgrn_step")(state, q, k, g, v)


def _merge_kernel(x_ref, ng_ref, wg_ref, oa_ref, ob_ref, gbn_ref, woa_ref, wob_ref, wo_ref, y_ref):
    x = x_ref[...]
    xn = _rms(x, ng_ref[...]).astype(BF16)
    gates = _dot(xn, wg_ref[...])
    ga, gb = gates[:, :A_WIDTH], gates[:, A_WIDTH:A_WIDTH + B_WIDTH]
    ma = gates[:, A_WIDTH + B_WIDTH:A_WIDTH + B_WIDTH + D_MODEL]
    mb = gates[:, A_WIDTH + B_WIDTH + D_MODEL:]
    ya = _dot((oa_ref[...] * (ga * jax.nn.sigmoid(ga))).astype(BF16), woa_ref[...])
    gbn = gbn_ref[...]
    obn = jnp.concatenate([_rms(ob_ref[:, h * B_DV:(h + 1) * B_DV], gbn) for h in range(B_HEADS)], axis=-1)
    yb = _dot((obn * (gb * jax.nn.sigmoid(gb))).astype(BF16), wob_ref[...])
    mix = jax.nn.sigmoid(ma) * ya + jax.nn.sigmoid(mb) * yb
    y_ref[...] = x + _dot(mix.astype(BF16), wo_ref[...])


def _merge(x, oa, ob, w, tm):
    rows = x.shape[0]
    row = lambda n: pl.BlockSpec((tm, n), lambda i: (i, 0))
    ins = [x, w["norm_g"], w["wg"], oa, ob, w["g_bn"], w["w_oa"], w["w_ob"], w["w_o"]]
    in_specs = [row(D_MODEL), _full(ins[1].shape), _full(ins[2].shape), row(A_WIDTH), row(B_WIDTH)] + \
               [_full(a.shape) for a in ins[5:]]
    return pl.pallas_call(
        _merge_kernel, grid=(rows // tm,), in_specs=in_specs, out_specs=row(D_MODEL),
        out_shape=jax.ShapeDtypeStruct((rows, D_MODEL), F32),
        compiler_params=_params(1), name="merge_out")(*ins)


def _head_pad_cols(w3, width):
    pad = jnp.zeros(w3.shape[:2] + (HEAD_PAD - width,), w3.dtype)
    return jnp.concatenate([w3, pad], axis=-1).reshape(w3.shape[0], -1)


def _rope_tables(pos):
    half = A_ROPE // 2
    inv = ROPE_THETA ** (-jnp.arange(half, dtype=F32) / half)
    ang = pos.astype(F32)[:, None] * inv
    cos, sin = jnp.cos(ang), jnp.sin(ang)
    n = pos.shape[0]
    one, zero = jnp.ones((n, A_NOPE), F32), jnp.zeros((n, A_NOPE), F32)
    tail = jnp.zeros((n, HEAD_PAD - A_QK), F32)
    z16 = jnp.zeros((n, half), F32)
    c = jnp.concatenate([one, cos, cos, tail], axis=-1)
    s1 = jnp.concatenate([zero, -sin, z16, tail], axis=-1)
    s2 = jnp.concatenate([zero, z16, sin, tail], axis=-1)
    return c, s1, s2


def _lane_gain(g):
    return jnp.concatenate([g, jnp.zeros((HEAD_PAD - A_QK,), g.dtype)])[None, :]


def kernel(x_prompt, x_sample, cache_latent, cache_krope, state_hgrn, page_table, meta_tokens,
           norm_g, w_in, g_cq, w_uq, g_ckv, w_uk, w_uv, g_qn, g_kn, lb_logits, g_bn, w_oa, w_ob, w_o):
    batch, seq, _ = x_prompt.shape
    dec_batch, dec_seq, _ = x_sample.shape
    depth = w_in.shape[0]
    assert depth == 1 and dec_seq == 1
    past_len = page_table.shape[1] * PAGE
    l = 0

    o0 = Q_LORA
    o1 = o0 + KV_LORA
    o2 = o1 + A_ROPE
    o3 = o2 + 3 * B_FDIM
    wi = w_in[l]
    wkr = jnp.zeros((D_MODEL, HEAD_PAD), F32).at[:, A_NOPE:A_QK].set(wi[:, o1:o2])
    w = {
        "norm_g": norm_g[l][None, :],
        "wa": jnp.concatenate([wi[:, :o1], wkr, wi[:, o2:o3]], axis=1).astype(BF16), "wg": wi[:, o3:].astype(BF16),
        "g_cq": g_cq[l][None, :], "g_ckv": g_ckv[l][None, :],
        "w_uq": _head_pad_cols(w_uq[l].reshape(Q_LORA, A_HEADS, A_QK), A_QK).astype(BF16),
        "w_uk": _head_pad_cols(w_uk[l], A_NOPE).astype(BF16),
        "w_uv": w_uv[l].reshape(KV_LORA, A_WIDTH).astype(BF16),
        "w_uvt": w_uv[l].reshape(KV_LORA, A_WIDTH).T.astype(BF16),
        "g_qn": _lane_gain(g_qn[l]), "g_kn": _lane_gain(g_kn[l]),
        "lb_logits": lb_logits, "g_bn": g_bn[l][None, :],
        "w_oa": w_oa[l].astype(BF16), "w_ob": w_ob[l].astype(BF16), "w_o": w_o[l].astype(BF16),
    }

    xs = x_sample.reshape(dec_batch, D_MODEL)
    tabs_m = _rope_tables(jnp.arange(N_META))
    tabs_s = _rope_tables(jnp.full((dec_batch,), past_len, jnp.int32))
    tabs_ms = tuple(jnp.concatenate([a, b], axis=0) for a, b in zip(tabs_m, tabs_s))
    small = _pre(jnp.concatenate([meta_tokens, xs], axis=0), tabs_ms, w, N_META + dec_batch, 1, l)
    lat_m, krp_m, _, k_m, _, mq, mk, mv, mg = [a[:N_META] for a in small]
    lat_s, krp_s, q_s, k_s, _, sq, sk, sv, sg = [a[N_META:] for a in small]
    vt_m = small[4][:, :N_META]

    xp = x_prompt.reshape(batch * seq, D_MODEL)
    tabs_p = _rope_tables(N_META + jnp.arange(seq))
    qscale = (A_QK ** -0.5) * 1.4426950408889634
    lat_p, krp_p, q_p, k_p, vt_p, hq, hk, hv, hg = _pre(xp, tabs_p, w, PRE_TILE, seq // PRE_TILE, l, qscale)
    lat_all = jnp.concatenate([jnp.broadcast_to(lat_m[None], (batch, N_META, KV_LORA)),
                               lat_p.reshape(batch, seq, KV_LORA)], axis=1)
    kr_all = jnp.concatenate([jnp.broadcast_to(krp_m[None, :, A_NOPE:A_QK], (batch, N_META, A_ROPE)),
                              krp_p[:, A_NOPE:A_QK].reshape(batch, seq, A_ROPE)], axis=1)

    oa = _attn(q_p, k_p, vt_p, k_m, vt_m, batch, seq)
    zero_state = jnp.zeros((B_HEADS, B_DV, B_DK), F32)
    _, st_meta = _hgrn(mq, mk, mv, mg, zero_state, 1, N_META, N_META, N_META, False)
    ob, st_fin = _hgrn(hq, hk, hv, hg, st_meta[0], batch, seq, HGRN_CHUNK, HGRN_TILE, True)
    y_prompt = _merge(xp, oa, ob, w, ROW_TILE).reshape(batch, seq, D_MODEL)

    q_s3 = q_s.reshape(dec_batch, A_HEADS, HEAD_PAD)
    k_s3 = k_s.reshape(dec_batch, A_HEADS, HEAD_PAD)
    qrow = q_s3[:, :, :A_NOPE].astype(F32).reshape(dec_batch, 1, A_HEADS * A_NOPE)
    qrope = q_s3[:, :, A_NOPE:A_QK].astype(F32)
    gkrow = jnp.tile(g_kn[l][:A_NOPE], A_HEADS)[None, :]
    gkr = g_kn[l][None, A_NOPE:]
    wukt = w_uk[l].reshape(KV_LORA, A_HEADS * A_NOPE).T.astype(BF16)
    oa_s = _paged(page_table, cache_latent, jnp.swapaxes(cache_krope, 2, 3), qrow, qrope, q_s3, k_s3,
                  lat_s[:, None, :], wukt, gkrow, gkr, w["w_uv"], l)
    row3 = lambda a: a.reshape(dec_batch, 1, B_FDIM)
    ob_s, st_s = _hstep(state_hgrn, row3(sq), row3(sk), row3(sg), row3(sv), l)
    y_sample = _merge(xs, oa_s.reshape(dec_batch, A_WIDTH), ob_s.reshape(dec_batch, B_WIDTH), w, dec_batch)

    return (y_prompt, y_sample.reshape(dec_batch, dec_seq, D_MODEL),
            lat_all[None], kr_all[None], st_fin[None],
            lat_s.reshape(1, dec_batch, dec_seq, KV_LORA),
            krp_s[:, A_NOPE:A_QK].reshape(1, dec_batch, dec_seq, A_ROPE),
            st_s[None])
```

```python
import functools

import jax
import jax.numpy as jnp
from jax import lax
from jax.experimental import pallas as pl
from jax.experimental.pallas import tpu as pltpu

F32 = jnp.float32
BF16 = jnp.bfloat16

D_MODEL = 1024
N_META = 16
A_HEADS = 8
A_NOPE = 64
A_ROPE = 32
A_QK = A_NOPE + A_ROPE
A_V = 64
A_WIDTH = A_HEADS * A_V
Q_LORA = 384
KV_LORA = 256
ROPE_THETA = 10000.0
B_HEADS = 4
B_DK = 128
B_DV = 128
B_FDIM = B_HEADS * B_DK
B_WIDTH = B_HEADS * B_DV
EPS = 1e-6
PAGE = 128

LANES = 128
LOG2E = 1.4426950408889634
HEAD_PAD = LANES
VMEM_LIMIT = 56 * 1024 * 1024

PRE_TILE = 256
ROW_TILE = 1024
ATTN_TQ = 256
ATTN_TK = 256
HGRN_CHUNK = 128
HGRN_TILE = 256
PAGES_PER_GROUP = 32
HSTEP_ROWS = 8
POS_BLOCK = 2 * PAGE
DMA_SLOTS = 4


def _full(shape):
    return pl.BlockSpec(shape, lambda *_: (0,) * len(shape))


def _params(n_axes):
    return pltpu.CompilerParams(dimension_semantics=("arbitrary",) * n_axes,
                                vmem_limit_bytes=VMEM_LIMIT)


def _dot(a, b):
    return jnp.dot(a, b, preferred_element_type=F32)


def _dot_nt(a, b, precision=None):
    return lax.dot_general(a, b, (((1,), (1,)), ((), ())), precision=precision,
                           preferred_element_type=F32)


def _rms(x, g):
    r = lax.rsqrt(jnp.mean(x * x, axis=-1, keepdims=True) + EPS)
    return (x * r) * g


def _rope_lanes(t, c, s1, s2):
    return t * c + pltpu.roll(t, LANES - A_ROPE // 2, axis=1) * s1 + pltpu.roll(t, A_ROPE // 2, axis=1) * s2


def _pre_kernel(x_ref, ng_ref, wa_ref, gcq_ref, wuq_ref, gckv_ref,
                wuk_ref, wuv_ref, gqn_ref, gkn_ref, c_ref, s1_ref, s2_ref, lb_ref,
                lat_ref, krp_ref, q_ref, k_ref, vt_ref, hq_ref, hk_ref, hv_ref, hg_ref, *, layer, qscale):
    x = x_ref[...]
    xn = _rms(x, ng_ref[...]).astype(BF16)
    c, s1, s2 = c_ref[...], s1_ref[...], s2_ref[...]

    o1, o2, o3 = Q_LORA, Q_LORA + KV_LORA, Q_LORA + KV_LORA + LANES
    y = _dot(xn, wa_ref[:, :o3])
    cq = y[:, :o1]
    cqn = _rms(cq, gcq_ref[...]).astype(BF16)
    qraw = _dot(cqn, wuq_ref[...])
    gqn = gqn_ref[...]
    for h in range(A_HEADS):
        t = _rope_lanes(qraw[:, h * HEAD_PAD:(h + 1) * HEAD_PAD], c, s1, s2)
        r = lax.rsqrt(jnp.sum(t * t, axis=-1, keepdims=True) * (1.0 / A_QK) + EPS)
        q_ref[:, h * HEAD_PAD:(h + 1) * HEAD_PAD] = (((t * r) * gqn) * qscale).astype(BF16)

    ckv = _rms(y[:, o1:o2], gckv_ref[...])
    lat_ref[...] = ckv
    ckvb = ckv.astype(BF16)
    krp = _rope_lanes(y[:, o2:o3], c, s1, s2)
    krp_ref[...] = krp
    kraw = _dot(ckvb, wuk_ref[...])
    gkn = gkn_ref[...]
    for h in range(A_HEADS):
        t = kraw[:, h * HEAD_PAD:(h + 1) * HEAD_PAD] + krp
        r = lax.rsqrt(jnp.sum(t * t, axis=-1, keepdims=True) * (1.0 / A_QK) + EPS)
        k_ref[:, h * HEAD_PAD:(h + 1) * HEAD_PAD] = ((t * r) * gkn).astype(BF16)
    vt_ref[...] = _dot_nt(wuv_ref[...], ckvb).astype(BF16)

    lbl = lb_ref[...]
    e = jnp.exp(lbl - jnp.max(lbl, axis=0, keepdims=True))
    lb = jnp.sum(e[:layer + 1], axis=0, keepdims=True) / jnp.sum(e, axis=0, keepdims=True)
    b = _dot(xn, wa_ref[:, o3:])
    bq, z, bi = b[:, :B_FDIM], b[:, B_FDIM:2 * B_FDIM], b[:, 2 * B_FDIM:]
    hq_ref[...] = bq * jax.nn.sigmoid(bq)
    sz = jax.nn.sigmoid(z)
    hg_ref[...] = jnp.log(lb + (1.0 - lb) * sz)
    hk_ref[...] = (1.0 - lb) * (1.0 - sz)
    hv_ref[...] = bi


def _pre(x, tabs, w, tm, tab_blocks, layer, qscale=1.0):
    rows = x.shape[0]
    grid = (rows // tm,)
    row = lambda n: pl.BlockSpec((tm, n), lambda i: (i, 0))
    tab = pl.BlockSpec((tm, LANES), lambda i: (i % tab_blocks, 0))
    ins = [x, w["norm_g"], w["wa"], w["g_cq"], w["w_uq"], w["g_ckv"],
           w["w_uk"], w["w_uvt"], w["g_qn"], w["g_kn"], tabs[0], tabs[1], tabs[2], w["lb_logits"]]
    in_specs = [row(D_MODEL)] + [_full(a.shape) for a in ins[1:10]] + [tab, tab, tab] + [_full(ins[13].shape)]
    outs = [(KV_LORA, F32), (LANES, F32), (A_HEADS * HEAD_PAD, BF16), (A_HEADS * HEAD_PAD, BF16),
            None, (B_FDIM, F32), (B_FDIM, F32), (B_WIDTH, F32), (B_FDIM, F32)]
    out_specs = [pl.BlockSpec((A_WIDTH, tm), lambda i: (0, i)) if o is None else row(o[0]) for o in outs]
    out_shape = [jax.ShapeDtypeStruct((A_WIDTH, rows), BF16) if o is None else
                 jax.ShapeDtypeStruct((rows, o[0]), o[1]) for o in outs]
    return pl.pallas_call(
        functools.partial(_pre_kernel, layer=layer, qscale=qscale),
        grid=grid, in_specs=in_specs, out_specs=out_specs, out_shape=out_shape,
        compiler_params=_params(1), name="pre_proj")(*ins)


def _attn_kernel(q_ref, k_ref, vt_ref, km_ref, vtm_ref, o_ref, ot_sc, s_sc, sm_sc, m_sc, l_sc):
    i = pl.program_id(1)
    tq, tk = ATTN_TQ, ATTN_TK
    assert tq == tk
    key = lax.broadcasted_iota(jnp.int32, (tk, tq), 0)
    qry = lax.broadcasted_iota(jnp.int32, (tk, tq), 1)
    hsl = [slice(h * HEAD_PAD, (h + 1) * HEAD_PAD) for h in range(A_HEADS)]
    vsl = [slice(h * A_V, (h + 1) * A_V) for h in range(A_HEADS)]

    d0 = pl.multiple_of(i * tq, tq)
    for h in range(A_HEADS):
        s_sc[h, 0:tk, :] = _dot_nt(k_ref[pl.ds(d0, tk), hsl[h]], q_ref[:, hsl[h]])
        sm_sc[h] = _dot_nt(km_ref[:, hsl[h]], q_ref[:, hsl[h]])
    for h in range(A_HEADS):
        hr = slice(h, h + 1)
        s = jnp.where(key <= qry, s_sc[h, 0:tk, :], -jnp.inf)
        sm = sm_sc[h]
        m_new = jnp.maximum(jnp.max(s, axis=0, keepdims=True), jnp.max(sm, axis=0, keepdims=True))
        p = jnp.exp2(s - m_new)
        pm = jnp.exp2(sm - m_new)
        m_sc[hr, :] = m_new
        l_sc[hr, :] = jnp.sum(p, axis=0, keepdims=True) + jnp.sum(pm, axis=0, keepdims=True)
        ot_sc[vsl[h], :] = (_dot(vt_ref[vsl[h], pl.ds(d0, tk)], p.astype(BF16)) +
                            _dot(vtm_ref[vsl[h], :], pm.astype(BF16)))

    def visible(r0, keys):
        for h in range(A_HEADS):
            s_sc[h, 0:keys, :] = _dot_nt(k_ref[pl.ds(r0, keys), hsl[h]], q_ref[:, hsl[h]])
        for h in range(A_HEADS):
            hr = slice(h, h + 1)
            s = s_sc[h, 0:keys, :]
            m_new = jnp.maximum(m_sc[hr, :], jnp.max(s, axis=0, keepdims=True))
            a = jnp.exp2(m_sc[hr, :] - m_new)
            p = jnp.exp2(s - m_new)
            m_sc[hr, :] = m_new
            l_sc[hr, :] = l_sc[hr, :] * a + jnp.sum(p, axis=0, keepdims=True)
            ot_sc[vsl[h], :] = ot_sc[vsl[h], :] * a + _dot(vt_ref[vsl[h], pl.ds(r0, keys)], p.astype(BF16))

    def body(kb, _):
        visible(pl.multiple_of(kb * 4 * tk, 4 * tk), 4 * tk)
        return 0

    lax.fori_loop(0, i // 4, body, 0)

    @pl.when(i % 4 >= 2)
    def _():
        visible(pl.multiple_of((i // 4) * 4 * tk, 2 * tk), 2 * tk)

    @pl.when(i % 2 == 1)
    def _():
        visible(pl.multiple_of((i - 1) * tk, tk), tk)

    for h in range(A_HEADS):
        ot_sc[vsl[h], :] = ot_sc[vsl[h], :] / l_sc[h:h + 1, :]
    o_ref[...] = ot_sc[...].T


def _attn(q, k, vt, km, vtm, batch, seq):
    nq = seq // ATTN_TQ
    return pl.pallas_call(
        _attn_kernel, grid=(batch, nq),
        in_specs=[pl.BlockSpec((ATTN_TQ, A_HEADS * HEAD_PAD), lambda b, i: (b * nq + i, 0)),
                  pl.BlockSpec((seq, A_HEADS * HEAD_PAD), lambda b, i: (b, 0)),
                  pl.BlockSpec((A_WIDTH, seq), lambda b, i: (0, b)),
                  _full(km.shape), _full(vtm.shape)],
        out_specs=pl.BlockSpec((ATTN_TQ, A_WIDTH), lambda b, i: (b * nq + i, 0)),
        out_shape=jax.ShapeDtypeStruct((batch * seq, A_WIDTH), F32),
        scratch_shapes=[pltpu.VMEM((A_WIDTH, ATTN_TQ), F32), pltpu.VMEM((A_HEADS, 4 * ATTN_TK, ATTN_TQ), F32),
                        pltpu.VMEM((A_HEADS, km.shape[0], ATTN_TQ), F32),
                        pltpu.VMEM((A_HEADS, ATTN_TQ), F32), pltpu.VMEM((A_HEADS, ATTN_TQ), F32)],
        compiler_params=_params(2), name="prompt_attn")(q, k, vt, km, vtm)


def _split3(x):
    a = x.astype(BF16)
    r = x - a.astype(F32)
    b = r.astype(BF16)
    c = (r - b.astype(F32)).astype(BF16)
    return a, b, c


def _hgrn_chunk(q, kk, v, g, st, tril, chunk, cum_ref, krow):
    g1, g2, g3 = _split3(g)
    cum = (_dot(tril, g1) + _dot(tril, g2) + _dot(tril, g3)) * LOG2E
    cum_ref[...] = cum
    yield None
    nblk = chunk // 8
    cb = [cum[8 * j:8 * j + 8] for j in range(nblk)]
    qb = [q[8 * j:8 * j + 8] for j in range(nblk)]
    lane = lax.broadcasted_iota(jnp.int32, (8, chunk), 1)
    sub = lax.broadcasted_iota(jnp.int32, (8, chunk), 0)
    ab = []
    for j in range(nblk):
        a = jnp.zeros((8, chunk), F32)
        for s in range(8 * j, 8 * j + 8):
            col = jnp.sum(jnp.exp2(cb[j] - cum_ref[s:s + 1, :]) * qb[j] * krow(s), axis=-1, keepdims=True)
            a = jnp.where(lane == s, col, a)
        ab.append(jnp.where(lane <= sub + 8 * j, a, 0.0))
        yield None
    attn = jnp.concatenate(ab, axis=0)
    row = lax.broadcasted_iota(jnp.int32, (chunk, chunk), 0)
    col = lax.broadcasted_iota(jnp.int32, (chunk, chunk), 1)
    kb = [kk[8 * j:8 * j + 8] for j in range(nblk)]
    dead = jnp.zeros((8, B_DK), F32)
    w = 8
    while w < chunk:
        qs, ks = [], []
        for j in range(nblk):
            blk = (8 * j) // w
            if blk % 2:
                qs.append(qb[j] * jnp.exp2(cb[j] - cum_ref[blk * w - 1:blk * w, :]))
                ks.append(dead)
            else:
                qs.append(dead)
                ks.append(kb[j] * jnp.exp2(cum_ref[blk * w + w - 1:blk * w + w, :] - cb[j]))
        qa = jnp.concatenate(qs, axis=0).astype(BF16)
        ka = jnp.concatenate(ks, axis=0).astype(BF16)
        sh = w.bit_length() - 1
        rb, cbk = row >> sh, col >> sh
        attn = attn + jnp.where((rb == cbk + 1) & ((cbk & 1) == 0), _dot_nt(qa, ka), 0.0)
        w *= 2
        yield None
    last = cum_ref[chunk - 1:chunk, :]
    o = _dot(attn.astype(BF16), v.astype(BF16)) + _dot_nt((q * jnp.exp2(cum)).astype(BF16), st.astype(BF16))
    kdec = (kk * jnp.exp2(last - cum)).astype(BF16)
    upd = lax.dot_general(v.astype(BF16), kdec, (((0,), (0,)), ((), ())), preferred_element_type=F32)
    yield o, jnp.exp2(last) * st + upd


def _hgrn_kernel(q_ref, k_ref, v_ref, g_ref, st0_ref, o_ref, st_ref, st_sc, cum_sc, kk_sc, *, chunk, tile,
                 transpose_out):
    t = pl.program_id(1)

    @pl.when(t == 0)
    def _():
        st_sc[...] = st0_ref[...]

    r = lax.broadcasted_iota(jnp.int32, (chunk, chunk), 0)
    c = lax.broadcasted_iota(jnp.int32, (chunk, chunk), 1)
    tril = (c <= r).astype(BF16)

    def body(ci, _):
        r0 = pl.multiple_of(ci * chunk, chunk)
        hsl = [slice(h * B_DK, (h + 1) * B_DK) for h in range(B_HEADS)]
        stages = {}
        for h in range(B_HEADS):
            kk = k_ref[pl.ds(r0, chunk), hsl[h]]
            kk_sc[h] = kk
            krow = lambda s, h=h: kk_sc[h, s:s + 1, :]
            stages[h] = _hgrn_chunk(q_ref[pl.ds(r0, chunk), hsl[h]], kk,
                                    v_ref[pl.ds(r0, chunk), hsl[h]], g_ref[pl.ds(r0, chunk), hsl[h]],
                                    st_sc[h], tril, chunk, cum_sc.at[h], krow)
        while stages:
            for h in list(stages):
                out = next(stages[h])
                if out is not None:
                    o_ref[pl.ds(r0, chunk), hsl[h]] = out[0]
                    st_sc[h] = out[1]
                    del stages[h]
        return 0

    lax.fori_loop(0, tile // chunk, body, 0)

    @pl.when(t == pl.num_programs(1) - 1)
    def _():
        for h in range(B_HEADS):
            st_ref[0, h] = st_sc[h].T if transpose_out else st_sc[h]


def _hgrn(hq, hk, hv, hg, st0, batch, seq, chunk, tile, transpose_out):
    nt = seq // tile
    row = pl.BlockSpec((tile, B_FDIM), lambda b, t: (b * nt + t, 0))
    return pl.pallas_call(
        functools.partial(_hgrn_kernel, chunk=chunk, tile=tile, transpose_out=transpose_out),
        grid=(batch, nt),
        in_specs=[row, row, row, row, _full(st0.shape)],
        out_specs=[row, pl.BlockSpec((1, B_HEADS, B_DV, B_DK), lambda b, t: (b, 0, 0, 0))],
        out_shape=[jax.ShapeDtypeStruct((batch * seq, B_WIDTH), F32),
                   jax.ShapeDtypeStruct((batch, B_HEADS, B_DV, B_DK), F32)],
        scratch_shapes=[pltpu.VMEM((B_HEADS, B_DV, B_DK), F32), pltpu.VMEM((B_HEADS, chunk, B_DK), F32),
                        pltpu.VMEM((B_HEADS, chunk, B_DK), F32)],
        compiler_params=_params(2), name="hgrn_chunks")(hq, hk, hv, hg, st0)


def _split2(x):
    hi = x.astype(BF16)
    return hi, (x - hi.astype(F32)).astype(BF16)


def _paged_kernel(pt_ref, lat_hbm, kr_hbm, qrow_ref, qr_ref, qh_ref, kh_ref, latn_ref, wukt_ref,
                  gkrow_ref, gkr_ref, wuv_ref, o_ref, lhs, latbuf, krbuf, cbuf, kx_sc, sems, *, layer, n_pages):
    b = pl.program_id(0)
    gp = PAGES_PER_GROUP
    n_groups = n_pages // gp
    gpos = gp * PAGE
    nk = A_HEADS * A_NOPE
    log2e = 1.4426950408889634
    sc2 = (A_QK ** -0.5) * log2e

    def page_copies(bb, g, slot):
        out = []
        for j in range(gp):
            page = pt_ref[bb, g * gp + j]
            dst = pl.ds(j * PAGE, PAGE)
            out.append(pltpu.make_async_copy(lat_hbm.at[layer, page], latbuf.at[slot, dst], sems.at[0, slot]))
            out.append(pltpu.make_async_copy(kr_hbm.at[layer, page], krbuf.at[slot, j], sems.at[1, slot]))
        return out

    def start(bb, g, slot):
        for cp in page_copies(bb, g, slot):
            cp.start()

    def wait(g, slot):
        for cp in page_copies(b, g, slot):
            cp.wait()

    ring = DMA_SLOTS
    ahead = ring - 1
    assert n_groups % ring == 0 and ahead <= n_groups

    def fetch_ahead(g):
        gg = g + ahead
        wrap = gg >= n_groups
        bb = jnp.where(wrap, b + 1, b)
        g2 = jnp.where(wrap, gg - n_groups, gg)

        @pl.when(bb < pl.num_programs(0))
        def _():
            start(bb, g2, lax.rem(gg, ring))

    @pl.when(b == 0)
    def _():
        for d in range(ahead):
            start(0, d, d)

    @pl.when(b == 0)
    def _():
        lhs[0:nk, :] = wukt_ref[...]

    hrow = lax.broadcasted_iota(jnp.int32, (A_HEADS, nk), 0)
    hcol = lax.broadcasted_iota(jnp.int32, (A_HEADS, nk), 1) // A_NOPE
    qsel = jnp.where(hrow == hcol, qrow_ref[0] * gkrow_ref[...], 0.0)
    q_hi, q_lo = _split2(qsel)
    qabs = _dot(q_hi, wukt_ref[...]) + _dot(q_lo, wukt_ref[...])
    lhs[nk:nk + 2 * A_HEADS, :] = jnp.concatenate(_split2(qabs), axis=0)
    qrg = jnp.concatenate(_split2(qr_ref[0] * gkr_ref[...]), axis=0)
    ppb = POS_BLOCK // PAGE

    def scores(slot, cslot, between=None):
        parts = []
        nblk = gpos // POS_BLOCK

        def project(j):
            rows = pl.ds(j * POS_BLOCK, POS_BLOCK)
            cb = latbuf[slot, rows, :].astype(BF16)
            cbuf[cslot, rows, :] = cb
            kx_sc[j] = _dot_nt(lhs[...], cb)

        for j in range(nblk):
            project(j)
        if between is not None:
            between()
        for j in range(nblk):
            kn = kx_sc[j, 0:nk, :]
            ss = jnp.sum((kn * kn).reshape(A_HEADS, A_NOPE, POS_BLOCK), axis=1)
            raw = kx_sc[j, nk:nk + A_HEADS, :] + kx_sc[j, nk + A_HEADS:nk + 2 * A_HEADS, :]
            krt = jnp.concatenate([krbuf[slot, j * ppb + t] for t in range(ppb)], axis=-1)
            rr = _dot(qrg, krt.astype(BF16))
            raw = raw + rr[:A_HEADS] + rr[A_HEADS:]
            ss = ss + jnp.sum(krt * krt, axis=0, keepdims=True)
            parts.append(raw * lax.rsqrt(ss * (1.0 / A_QK) + EPS) * sc2)
        return jnp.concatenate(parts, axis=-1)

    def absorb(s, slot, carry):
        m, l, acc = carry
        m_new = jnp.maximum(m, jnp.max(s, axis=-1, keepdims=True))
        a = jnp.exp2(m - m_new)
        p = jnp.exp2(s - m_new)
        l = l * a + jnp.sum(p, axis=-1, keepdims=True)
        acc = acc * a + _dot(p.astype(BF16), cbuf[slot])
        return m_new, l, acc

    wait(0, 0)
    fetch_ahead(0)
    s0 = scores(0, 0)
    init = (jnp.full((A_HEADS, 1), -jnp.inf, F32), jnp.zeros((A_HEADS, 1), F32),
            jnp.zeros((A_HEADS, KV_LORA), F32))

    def group(g, carry):
        s_prev, state = carry
        cslot = lax.rem(g, 2)
        wait(g, lax.rem(g, ring))
        fetch_ahead(g)
        box = []
        s_cur = scores(lax.rem(g, ring), cslot, lambda: box.append(absorb(s_prev, 1 - cslot, state)))
        return s_cur, box[0]

    s_last, state = lax.fori_loop(1, n_groups, group, (s0, init))
    m, l, acc = absorb(s_last, (n_groups - 1) % 2, state)

    s_new = jnp.sum(qh_ref[0].astype(F32) * kh_ref[0].astype(F32), axis=-1, keepdims=True) * sc2
    m_new = jnp.maximum(m, s_new)
    a = jnp.exp2(m - m_new)
    p = jnp.exp2(s_new - m_new)
    acc = acc * a + p * latn_ref[0]
    l = l * a + p
    o_lat = (acc / l).astype(BF16)
    full = _dot(o_lat, wuv_ref[...])
    vrow = lax.broadcasted_iota(jnp.int32, (A_HEADS, A_WIDTH), 0)
    vcol = lax.broadcasted_iota(jnp.int32, (A_HEADS, A_WIDTH), 1) // A_V
    o_ref[0] = jnp.sum(jnp.where(vrow == vcol, full, 0.0), axis=0, keepdims=True)


def _paged(page_table, cache_lat, cache_kr, qrow, qr, qh, kh, latn, wukt, gkrow, gkr, wuv, layer):
    nb, n_pages = page_table.shape
    gp = PAGES_PER_GROUP
    per_b = lambda shape: pl.BlockSpec((1,) + shape, lambda b, pt: (b,) + (0,) * len(shape))
    full = lambda a: pl.BlockSpec(a.shape, lambda b, pt: (0,) * a.ndim)
    grid_spec = pltpu.PrefetchScalarGridSpec(
        num_scalar_prefetch=1, grid=(nb,),
        in_specs=[pl.BlockSpec(memory_space=pl.ANY), pl.BlockSpec(memory_space=pl.ANY),
                  per_b((1, A_HEADS * A_NOPE)), per_b((A_HEADS, A_ROPE)), per_b((A_HEADS, HEAD_PAD)),
                  per_b((A_HEADS, HEAD_PAD)), per_b((1, KV_LORA)), full(wukt), full(gkrow), full(gkr), full(wuv)],
        out_specs=per_b((1, A_WIDTH)),
        scratch_shapes=[pltpu.VMEM((A_HEADS * A_NOPE + 2 * A_HEADS, KV_LORA), BF16),
                        pltpu.VMEM((DMA_SLOTS, gp * PAGE, KV_LORA), F32), pltpu.VMEM((DMA_SLOTS, gp, A_ROPE, PAGE), F32),
                        pltpu.VMEM((2, gp * PAGE, KV_LORA), BF16),
                        pltpu.VMEM((gp * PAGE // POS_BLOCK, A_HEADS * A_NOPE + 2 * A_HEADS, POS_BLOCK), F32),
                        pltpu.SemaphoreType.DMA((2, DMA_SLOTS))])
    return pl.pallas_call(
        functools.partial(_paged_kernel, layer=layer, n_pages=n_pages), grid_spec=grid_spec,
        out_shape=jax.ShapeDtypeStruct((nb, 1, A_WIDTH), F32),
        compiler_params=_params(1), name="paged_attn")(
            page_table, cache_lat, cache_kr, qrow, qr, qh, kh, latn, wukt, gkrow, gkr, wuv)


def _hstep_kernel(s_ref, q_ref, k_ref, g_ref, v_ref, o_ref, sn_ref, *, layer, rows):
    r = lax.broadcasted_iota(jnp.int32, (B_DK, B_DK), 0)
    c = lax.broadcasted_iota(jnp.int32, (B_DK, B_DK), 1)
    eye = r == c

    def column(row):
        return jnp.sum(jnp.where(eye, row, 0.0), axis=-1, keepdims=True)

    for i in range(rows):
        for h in range(B_HEADS):
            hs = slice(h * B_DK, (h + 1) * B_DK)
            sn = column(jnp.exp(g_ref[i, :, hs])) * s_ref[layer, i, h] + column(k_ref[i, :, hs]) * v_ref[i, :, hs]
            sn_ref[i, h] = sn
            o_ref[i, :, hs] = jnp.sum(column(q_ref[i, :, hs]) * sn, axis=0, keepdims=True)


def _hstep(state, q, k, g, v, layer):
    nb = state.shape[1]
    rows = HSTEP_ROWS
    st_in = pl.BlockSpec((state.shape[0], rows, B_HEADS, B_DK, B_DV), lambda b: (0, b, 0, 0, 0))
    st_out = pl.BlockSpec((rows, B_HEADS, B_DK, B_DV), lambda b: (b, 0, 0, 0))
    row = pl.BlockSpec((rows, 1, B_FDIM), lambda b: (b, 0, 0))
    return pl.pallas_call(
        functools.partial(_hstep_kernel, layer=layer, rows=rows), grid=(nb // rows,),
        in_specs=[st_in, row, row, row, row], out_specs=[row, st_out],
        out_shape=[jax.ShapeDtypeStruct((nb, 1, B_WIDTH), F32), jax.ShapeDtypeStruct(state.shape[1:], F32)],
        compiler_params=_params(1), name="hgrn_step")(state, q, k, g, v)


def _merge_kernel(x_ref, ng_ref, wg_ref, oa_ref, ob_ref, gbn_ref, woa_ref, wob_ref, wo_ref, y_ref):
    x = x_ref[...]
    xn = _rms(x, ng_ref[...]).astype(BF16)
    gates = _dot(xn, wg_ref[...])
    ga, gb = gates[:, :A_WIDTH], gates[:, A_WIDTH:A_WIDTH + B_WIDTH]
    ma = gates[:, A_WIDTH + B_WIDTH:A_WIDTH + B_WIDTH + D_MODEL]
    mb = gates[:, A_WIDTH + B_WIDTH + D_MODEL:]
    ya = _dot((oa_ref[...] * (ga * jax.nn.sigmoid(ga))).astype(BF16), woa_ref[...])
    gbn = gbn_ref[...]
    obn = jnp.concatenate([_rms(ob_ref[:, h * B_DV:(h + 1) * B_DV], gbn) for h in range(B_HEADS)], axis=-1)
    yb = _dot((obn * (gb * jax.nn.sigmoid(gb))).astype(BF16), wob_ref[...])
    mix = jax.nn.sigmoid(ma) * ya + jax.nn.sigmoid(mb) * yb
    y_ref[...] = x + _dot(mix.astype(BF16), wo_ref[...])


def _merge(x, oa, ob, w, tm):
    rows = x.shape[0]
    row = lambda n: pl.BlockSpec((tm, n), lambda i: (i, 0))
    ins = [x, w["norm_g"], w["wg"], oa, ob, w["g_bn"], w["w_oa"], w["w_ob"], w["w_o"]]
    in_specs = [row(D_MODEL), _full(ins[1].shape), _full(ins[2].shape), row(A_WIDTH), row(B_WIDTH)] + \
               [_full(a.shape) for a in ins[5:]]
    return pl.pallas_call(
        _merge_kernel, grid=(rows // tm,), in_specs=in_specs, out_specs=row(D_MODEL),
        out_shape=jax.ShapeDtypeStruct((rows, D_MODEL), F32),
        compiler_params=_params(1), name="merge_out")(*ins)


def _head_pad_cols(w3, width):
    pad = jnp.zeros(w3.shape[:2] + (HEAD_PAD - width,), w3.dtype)
    return jnp.concatenate([w3, pad], axis=-1).reshape(w3.shape[0], -1)


def _rope_tables(pos):
    half = A_ROPE // 2
    inv = ROPE_THETA ** (-jnp.arange(half, dtype=F32) / half)
    ang = pos.astype(F32)[:, None] * inv
    cos, sin = jnp.cos(ang), jnp.sin(ang)
    n = pos.shape[0]
    one, zero = jnp.ones((n, A_NOPE), F32), jnp.zeros((n, A_NOPE), F32)
    tail = jnp.zeros((n, HEAD_PAD - A_QK), F32)
    z16 = jnp.zeros((n, half), F32)
    c = jnp.concatenate([one, cos, cos, tail], axis=-1)
    s1 = jnp.concatenate([zero, -sin, z16, tail], axis=-1)
    s2 = jnp.concatenate([zero, z16, sin, tail], axis=-1)
    return c, s1, s2


def _lane_gain(g):
    return jnp.concatenate([g, jnp.zeros((HEAD_PAD - A_QK,), g.dtype)])[None, :]


def kernel(x_prompt, x_sample, cache_latent, cache_krope, state_hgrn, page_table, meta_tokens,
           norm_g, w_in, g_cq, w_uq, g_ckv, w_uk, w_uv, g_qn, g_kn, lb_logits, g_bn, w_oa, w_ob, w_o):
    batch, seq, _ = x_prompt.shape
    dec_batch, dec_seq, _ = x_sample.shape
    depth = w_in.shape[0]
    assert depth == 1 and dec_seq == 1
    past_len = page_table.shape[1] * PAGE
    l = 0

    o0 = Q_LORA
    o1 = o0 + KV_LORA
    o2 = o1 + A_ROPE
    o3 = o2 + 3 * B_FDIM
    wi = w_in[l]
    wkr = jnp.zeros((D_MODEL, HEAD_PAD), F32).at[:, A_NOPE:A_QK].set(wi[:, o1:o2])
    w = {
        "norm_g": norm_g[l][None, :],
        "wa": jnp.concatenate([wi[:, :o1], wkr, wi[:, o2:o3]], axis=1).astype(BF16), "wg": wi[:, o3:].astype(BF16),
        "g_cq": g_cq[l][None, :], "g_ckv": g_ckv[l][None, :],
        "w_uq": _head_pad_cols(w_uq[l].reshape(Q_LORA, A_HEADS, A_QK), A_QK).astype(BF16),
        "w_uk": _head_pad_cols(w_uk[l], A_NOPE).astype(BF16),
        "w_uv": w_uv[l].reshape(KV_LORA, A_WIDTH).astype(BF16),
        "w_uvt": w_uv[l].reshape(KV_LORA, A_WIDTH).T.astype(BF16),
        "g_qn": _lane_gain(g_qn[l]), "g_kn": _lane_gain(g_kn[l]),
        "lb_logits": lb_logits, "g_bn": g_bn[l][None, :],
        "w_oa": w_oa[l].astype(BF16), "w_ob": w_ob[l].astype(BF16), "w_o": w_o[l].astype(BF16),
    }

    xs = x_sample.reshape(dec_batch, D_MODEL)
    tabs_m = _rope_tables(jnp.arange(N_META))
    tabs_s = _rope_tables(jnp.full((dec_batch,), past_len, jnp.int32))
    tabs_ms = tuple(jnp.concatenate([a, b], axis=0) for a, b in zip(tabs_m, tabs_s))
    small = _pre(jnp.concatenate([meta_tokens, xs], axis=0), tabs_ms, w, N_META + dec_batch, 1, l)
    lat_m, krp_m, _, k_m, _, mq, mk, mv, mg = [a[:N_META] for a in small]
    lat_s, krp_s, q_s, k_s, _, sq, sk, sv, sg = [a[N_META:] for a in small]
    vt_m = small[4][:, :N_META]

    xp = x_prompt.reshape(batch * seq, D_MODEL)
    tabs_p = _rope_tables(N_META + jnp.arange(seq))
    qscale = (A_QK ** -0.5) * 1.4426950408889634
    lat_p, krp_p, q_p, k_p, vt_p, hq, hk, hv, hg = _pre(xp, tabs_p, w, PRE_TILE, seq // PRE_TILE, l, qscale)
    lat_all = jnp.concatenate([jnp.broadcast_to(lat_m[None], (batch, N_META, KV_LORA)),
                               lat_p.reshape(batch, seq, KV_LORA)], axis=1)
    kr_all = jnp.concatenate([jnp.broadcast_to(krp_m[None, :, A_NOPE:A_QK], (batch, N_META, A_ROPE)),
                              krp_p[:, A_NOPE:A_QK].reshape(batch, seq, A_ROPE)], axis=1)

    oa = _attn(q_p, k_p, vt_p, k_m, vt_m, batch, seq)
    zero_state = jnp.zeros((B_HEADS, B_DV, B_DK), F32)
    _, st_meta = _hgrn(mq, mk, mv, mg, zero_state, 1, N_META, N_META, N_META, False)
    ob, st_fin = _hgrn(hq, hk, hv, hg, st_meta[0], batch, seq, HGRN_CHUNK, HGRN_TILE, True)
    y_prompt = _merge(xp, oa, ob, w, ROW_TILE).reshape(batch, seq, D_MODEL)

    q_s3 = q_s.reshape(dec_batch, A_HEADS, HEAD_PAD)
    k_s3 = k_s.reshape(dec_batch, A_HEADS, HEAD_PAD)
    qrow = q_s3[:, :, :A_NOPE].astype(F32).reshape(dec_batch, 1, A_HEADS * A_NOPE)
    qrope = q_s3[:, :, A_NOPE:A_QK].astype(F32)
    gkrow = jnp.tile(g_kn[l][:A_NOPE], A_HEADS)[None, :]
    gkr = g_kn[l][None, A_NOPE:]
    wukt = w_uk[l].reshape(KV_LORA, A_HEADS * A_NOPE).T.astype(BF16)
    oa_s = _paged(page_table, cache_latent, jnp.swapaxes(cache_krope, 2, 3), qrow, qrope, q_s3, k_s3,
                  lat_s[:, None, :], wukt, gkrow, gkr, w["w_uv"], l)
    row3 = lambda a: a.reshape(dec_batch, 1, B_FDIM)
    ob_s, st_s = _hstep(state_hgrn, row3(sq), row3(sk), row3(sg), row3(sv), l)
    y_sample = _merge(xs, oa_s.reshape(dec_batch, A_WIDTH), ob_s.reshape(dec_batch, B_WIDTH), w, dec_batch)

    return (y_prompt, y_sample.reshape(dec_batch, dec_seq, D_MODEL),
            lat_all[None], kr_all[None], st_fin[None],
            lat_s.reshape(1, dec_batch, dec_seq, KV_LORA),
            krp_s[:, A_NOPE:A_QK].reshape(1, dec_batch, dec_seq, A_ROPE),
            st_s[None])
```

```python
import functools

import jax
import jax.numpy as jnp
from jax import lax
from jax.experimental import pallas as pl
from jax.experimental.pallas import tpu as pltpu

F32 = jnp.float32
BF16 = jnp.bfloat16

D_MODEL = 1024
N_META = 16
A_HEADS = 8
A_NOPE = 64
A_ROPE = 32
A_QK = A_NOPE + A_ROPE
A_V = 64
A_WIDTH = A_HEADS * A_V
Q_LORA = 384
KV_LORA = 256
ROPE_THETA = 10000.0
B_HEADS = 4
B_DK = 128
B_DV = 128
B_FDIM = B_HEADS * B_DK
B_WIDTH = B_HEADS * B_DV
EPS = 1e-6
PAGE = 128

LANES = 128
LOG2E = 1.4426950408889634
HEAD_PAD = LANES
VMEM_LIMIT = 56 * 1024 * 1024

PRE_TILE = 1024
PRE_SUB = 256
ROW_TILE = 1024
ATTN_TQ = 256
ATTN_TK = 256
ATTN_QT = 2
HGRN_CHUNK = 128
HGRN_TILE = 1024
PAGES_PER_GROUP = 32
HSTEP_ROWS = 8
POS_BLOCK = 2 * PAGE
DMA_SLOTS = 4


def _full(shape):
    return pl.BlockSpec(shape, lambda *_: (0,) * len(shape))


def _params(n_axes):
    return pltpu.CompilerParams(dimension_semantics=("arbitrary",) * n_axes,
                                vmem_limit_bytes=VMEM_LIMIT)


def _dot(a, b):
    return jnp.dot(a, b, preferred_element_type=F32)


def _dot_nt(a, b, precision=None):
    return lax.dot_general(a, b, (((1,), (1,)), ((), ())), precision=precision,
                           preferred_element_type=F32)


def _rms(x, g):
    r = lax.rsqrt(jnp.mean(x * x, axis=-1, keepdims=True) + EPS)
    return (x * r) * g


def _rope_lanes(t, c, s1, s2):
    return t * c + pltpu.roll(t, LANES - A_ROPE // 2, axis=1) * s1 + pltpu.roll(t, A_ROPE // 2, axis=1) * s2


def _pre_kernel(x_ref, ng_ref, wa_ref, gcq_ref, wuq_ref, gckv_ref,
                wuk_ref, wuv_ref, gqn_ref, gkn_ref, c_ref, s1_ref, s2_ref, lb_ref,
                lat_ref, krp_ref, q_ref, k_ref, vt_ref, hq_ref, hk_ref, hv_ref, hg_ref, *, layer, qscale, sub):
    lbl = lb_ref[...]
    e = jnp.exp(lbl - jnp.max(lbl, axis=0, keepdims=True))
    lb = jnp.sum(e[:layer + 1], axis=0, keepdims=True) / jnp.sum(e, axis=0, keepdims=True)
    gqn, gkn = gqn_ref[...], gkn_ref[...]
    o1, o2, o3 = Q_LORA, Q_LORA + KV_LORA, Q_LORA + KV_LORA + LANES

    def rows_step(rows):
        x = x_ref[rows, :]
        xn = _rms(x, ng_ref[...]).astype(BF16)
        c, s1, s2 = c_ref[rows, :], s1_ref[rows, :], s2_ref[rows, :]

        y = _dot(xn, wa_ref[:, :o3])
        cqn = _rms(y[:, :o1], gcq_ref[...]).astype(BF16)
        qraw = _dot(cqn, wuq_ref[...])
        for h in range(A_HEADS):
            hs = slice(h * HEAD_PAD, (h + 1) * HEAD_PAD)
            t = _rope_lanes(qraw[:, hs], c, s1, s2)
            r = lax.rsqrt(jnp.sum(t * t, axis=-1, keepdims=True) * (1.0 / A_QK) + EPS)
            q_ref[rows, hs] = (((t * r) * gqn) * qscale).astype(BF16)

        ckv = _rms(y[:, o1:o2], gckv_ref[...])
        lat_ref[rows, :] = ckv
        ckvb = ckv.astype(BF16)
        krp = _rope_lanes(y[:, o2:o3], c, s1, s2)
        krp_ref[rows, :] = krp
        kraw = _dot(ckvb, wuk_ref[...])
        for h in range(A_HEADS):
            hs = slice(h * HEAD_PAD, (h + 1) * HEAD_PAD)
            t = kraw[:, hs] + krp
            r = lax.rsqrt(jnp.sum(t * t, axis=-1, keepdims=True) * (1.0 / A_QK) + EPS)
            k_ref[rows, hs] = ((t * r) * gkn).astype(BF16)
        vt_ref[:, rows] = _dot_nt(wuv_ref[...], ckvb).astype(BF16)

        b = _dot(xn, wa_ref[:, o3:])
        bq, z, bi = b[:, :B_FDIM], b[:, B_FDIM:2 * B_FDIM], b[:, 2 * B_FDIM:]
        hq_ref[rows, :] = bq * jax.nn.sigmoid(bq)
        sz = jax.nn.sigmoid(z)
        hg_ref[rows, :] = jnp.log(lb + (1.0 - lb) * sz)
        hk_ref[rows, :] = (1.0 - lb) * (1.0 - sz)
        hv_ref[rows, :] = bi

    tm = x_ref.shape[0]
    if tm == sub:
        rows_step(slice(None))
    else:
        def body(r, _):
            rows_step(pl.ds(pl.multiple_of(r * sub, sub), sub))
            return 0
        lax.fori_loop(0, tm // sub, body, 0)


def _pre(x, tabs, w, tm, tab_blocks, layer, qscale=1.0, sub=None):
    rows = x.shape[0]
    grid = (rows // tm,)
    row = lambda n: pl.BlockSpec((tm, n), lambda i: (i, 0))
    tab = pl.BlockSpec((tm, LANES), lambda i: (i % tab_blocks, 0))
    ins = [x, w["norm_g"], w["wa"], w["g_cq"], w["w_uq"], w["g_ckv"],
           w["w_uk"], w["w_uvt"], w["g_qn"], w["g_kn"], tabs[0], tabs[1], tabs[2], w["lb_logits"]]
    in_specs = [row(D_MODEL)] + [_full(a.shape) for a in ins[1:10]] + [tab, tab, tab] + [_full(ins[13].shape)]
    outs = [(KV_LORA, F32), (LANES, F32), (A_HEADS * HEAD_PAD, BF16), (A_HEADS * HEAD_PAD, BF16),
            None, (B_FDIM, F32), (B_FDIM, F32), (B_WIDTH, F32), (B_FDIM, F32)]
    out_specs = [pl.BlockSpec((A_WIDTH, tm), lambda i: (0, i)) if o is None else row(o[0]) for o in outs]
    out_shape = [jax.ShapeDtypeStruct((A_WIDTH, rows), BF16) if o is None else
                 jax.ShapeDtypeStruct((rows, o[0]), o[1]) for o in outs]
    return pl.pallas_call(
        functools.partial(_pre_kernel, layer=layer, qscale=qscale, sub=sub or tm),
        grid=grid, in_specs=in_specs, out_specs=out_specs, out_shape=out_shape,
        compiler_params=_params(1), name="pre_proj")(*ins)


def _attn_kernel(q_ref, k_ref, vt_ref, km_ref, vtm_ref, o_ref, ot_sc, s_sc, sm_sc, m_sc, l_sc):
    def tile(qt, _):
        i = pl.program_id(1) * ATTN_QT + qt
        qrows = pl.ds(pl.multiple_of(qt * ATTN_TQ, ATTN_TQ), ATTN_TQ)
        _attn_tile(i, qrows, q_ref, k_ref, vt_ref, km_ref, vtm_ref, o_ref, ot_sc, s_sc, sm_sc, m_sc, l_sc)
        return 0

    lax.fori_loop(0, ATTN_QT, tile, 0)


def _attn_tile(i, qrows, q_ref, k_ref, vt_ref, km_ref, vtm_ref, o_ref, ot_sc, s_sc, sm_sc, m_sc, l_sc):
    tq, tk = ATTN_TQ, ATTN_TK
    assert tq == tk
    key = lax.broadcasted_iota(jnp.int32, (tk, tq), 0)
    qry = lax.broadcasted_iota(jnp.int32, (tk, tq), 1)
    hsl = [slice(h * HEAD_PAD, (h + 1) * HEAD_PAD) for h in range(A_HEADS)]
    vsl = [slice(h * A_V, (h + 1) * A_V) for h in range(A_HEADS)]

    d0 = pl.multiple_of(i * tq, tq)
    for h in range(A_HEADS):
        s_sc[h, 0:tk, :] = _dot_nt(k_ref[pl.ds(d0, tk), hsl[h]], q_ref[qrows, hsl[h]])
        sm_sc[h] = _dot_nt(km_ref[:, hsl[h]], q_ref[qrows, hsl[h]])
    for h in range(A_HEADS):
        hr = slice(h, h + 1)
        s = jnp.where(key <= qry, s_sc[h, 0:tk, :], -jnp.inf)
        sm = sm_sc[h]
        m_new = jnp.maximum(jnp.max(s, axis=0, keepdims=True), jnp.max(sm, axis=0, keepdims=True))
        p = jnp.exp2(s - m_new)
        pm = jnp.exp2(sm - m_new)
        m_sc[hr, :] = m_new
        l_sc[hr, :] = jnp.sum(p, axis=0, keepdims=True) + jnp.sum(pm, axis=0, keepdims=True)
        ot_sc[vsl[h], :] = (_dot(vt_ref[vsl[h], pl.ds(d0, tk)], p.astype(BF16)) +
                            _dot(vtm_ref[vsl[h], :], pm.astype(BF16)))

    def visible(r0, keys):
        for h in range(A_HEADS):
            s_sc[h, 0:keys, :] = _dot_nt(k_ref[pl.ds(r0, keys), hsl[h]], q_ref[qrows, hsl[h]])
        for h in range(A_HEADS):
            hr = slice(h, h + 1)
            s = s_sc[h, 0:keys, :]
            m_new = jnp.maximum(m_sc[hr, :], jnp.max(s, axis=0, keepdims=True))
            a = jnp.exp2(m_sc[hr, :] - m_new)
            p = jnp.exp2(s - m_new)
            m_sc[hr, :] = m_new
            l_sc[hr, :] = l_sc[hr, :] * a + jnp.sum(p, axis=0, keepdims=True)
            ot_sc[vsl[h], :] = ot_sc[vsl[h], :] * a + _dot(vt_ref[vsl[h], pl.ds(r0, keys)], p.astype(BF16))

    def body(kb, _):
        visible(pl.multiple_of(kb * 4 * tk, 4 * tk), 4 * tk)
        return 0

    lax.fori_loop(0, i // 4, body, 0)

    @pl.when(i % 4 >= 2)
    def _():
        visible(pl.multiple_of((i // 4) * 4 * tk, 2 * tk), 2 * tk)

    @pl.when(i % 2 == 1)
    def _():
        visible(pl.multiple_of((i - 1) * tk, tk), tk)

    for h in range(A_HEADS):
        ot_sc[vsl[h], :] = ot_sc[vsl[h], :] / l_sc[h:h + 1, :]
    o_ref[qrows, :] = ot_sc[...].T


def _attn(q, k, vt, km, vtm, batch, seq):
    rows = ATTN_QT * ATTN_TQ
    nq = seq // rows
    return pl.pallas_call(
        _attn_kernel, grid=(batch, nq),
        in_specs=[pl.BlockSpec((rows, A_HEADS * HEAD_PAD), lambda b, i: (b * nq + i, 0)),
                  pl.BlockSpec((seq, A_HEADS * HEAD_PAD), lambda b, i: (b, 0)),
                  pl.BlockSpec((A_WIDTH, seq), lambda b, i: (0, b)),
                  _full(km.shape), _full(vtm.shape)],
        out_specs=pl.BlockSpec((rows, A_WIDTH), lambda b, i: (b * nq + i, 0)),
        out_shape=jax.ShapeDtypeStruct((batch * seq, A_WIDTH), F32),
        scratch_shapes=[pltpu.VMEM((A_WIDTH, ATTN_TQ), F32), pltpu.VMEM((A_HEADS, 4 * ATTN_TK, ATTN_TQ), F32),
                        pltpu.VMEM((A_HEADS, km.shape[0], ATTN_TQ), F32),
                        pltpu.VMEM((A_HEADS, ATTN_TQ), F32), pltpu.VMEM((A_HEADS, ATTN_TQ), F32)],
        compiler_params=_params(2), name="prompt_attn")(q, k, vt, km, vtm)


def _split3(x):
    a = x.astype(BF16)
    r = x - a.astype(F32)
    b = r.astype(BF16)
    c = (r - b.astype(F32)).astype(BF16)
    return a, b, c


def _hgrn_chunk(q, kk, v, g, st, tril, chunk, cum_ref, krow):
    g1, g2, g3 = _split3(g)
    cum = (_dot(tril, g1) + _dot(tril, g2) + _dot(tril, g3)) * LOG2E
    cum_ref[...] = cum
    yield None
    nblk = chunk // 8
    cb = [cum[8 * j:8 * j + 8] for j in range(nblk)]
    qb = [q[8 * j:8 * j + 8] for j in range(nblk)]
    lane = lax.broadcasted_iota(jnp.int32, (8, chunk), 1)
    sub = lax.broadcasted_iota(jnp.int32, (8, chunk), 0)
    ab = []
    for j in range(nblk):
        a = jnp.zeros((8, chunk), F32)
        for s in range(8 * j, 8 * j + 8):
            col = jnp.sum(jnp.exp2(cb[j] - cum_ref[s:s + 1, :]) * qb[j] * krow(s), axis=-1, keepdims=True)
            a = jnp.where(lane == s, col, a)
        ab.append(jnp.where(lane <= sub + 8 * j, a, 0.0))
        yield None
    attn = jnp.concatenate(ab, axis=0)
    row = lax.broadcasted_iota(jnp.int32, (chunk, chunk), 0)
    col = lax.broadcasted_iota(jnp.int32, (chunk, chunk), 1)
    kb = [kk[8 * j:8 * j + 8] for j in range(nblk)]
    dead = jnp.zeros((8, B_DK), F32)
    w = 8
    while w < chunk:
        qs, ks = [], []
        for j in range(nblk):
            blk = (8 * j) // w
            if blk % 2:
                qs.append(qb[j] * jnp.exp2(cb[j] - cum_ref[blk * w - 1:blk * w, :]))
                ks.append(dead)
            else:
                qs.append(dead)
                ks.append(kb[j] * jnp.exp2(cum_ref[blk * w + w - 1:blk * w + w, :] - cb[j]))
        qa = jnp.concatenate(qs, axis=0).astype(BF16)
        ka = jnp.concatenate(ks, axis=0).astype(BF16)
        sh = w.bit_length() - 1
        rb, cbk = row >> sh, col >> sh
        attn = attn + jnp.where((rb == cbk + 1) & ((cbk & 1) == 0), _dot_nt(qa, ka), 0.0)
        w *= 2
        yield None
    last = cum_ref[chunk - 1:chunk, :]
    o = _dot(attn.astype(BF16), v.astype(BF16)) + _dot_nt((q * jnp.exp2(cum)).astype(BF16), st.astype(BF16))
    kdec = (kk * jnp.exp2(last - cum)).astype(BF16)
    upd = lax.dot_general(v.astype(BF16), kdec, (((0,), (0,)), ((), ())), preferred_element_type=F32)
    yield o, jnp.exp2(last) * st + upd


def _hgrn_kernel(q_ref, k_ref, v_ref, g_ref, st0_ref, o_ref, st_ref, st_sc, cum_sc, kk_sc, *, chunk, tile,
                 transpose_out):
    t = pl.program_id(1)

    @pl.when(t == 0)
    def _():
        st_sc[...] = st0_ref[...]

    r = lax.broadcasted_iota(jnp.int32, (chunk, chunk), 0)
    c = lax.broadcasted_iota(jnp.int32, (chunk, chunk), 1)
    tril = (c <= r).astype(BF16)

    def body(ci, _):
        r0 = pl.multiple_of(ci * chunk, chunk)
        hsl = [slice(h * B_DK, (h + 1) * B_DK) for h in range(B_HEADS)]
        stages = {}
        for h in range(B_HEADS):
            kk = k_ref[pl.ds(r0, chunk), hsl[h]]
            kk_sc[h] = kk
            krow = lambda s, h=h: kk_sc[h, s:s + 1, :]
            stages[h] = _hgrn_chunk(q_ref[pl.ds(r0, chunk), hsl[h]], kk,
                                    v_ref[pl.ds(r0, chunk), hsl[h]], g_ref[pl.ds(r0, chunk), hsl[h]],
                                    st_sc[h], tril, chunk, cum_sc.at[h], krow)
        while stages:
            for h in list(stages):
                out = next(stages[h])
                if out is not None:
                    o_ref[pl.ds(r0, chunk), hsl[h]] = out[0]
                    st_sc[h] = out[1]
                    del stages[h]
        return 0

    lax.fori_loop(0, tile // chunk, body, 0)

    @pl.when(t == pl.num_programs(1) - 1)
    def _():
        for h in range(B_HEADS):
            st_ref[0, h] = st_sc[h].T if transpose_out else st_sc[h]


def _hgrn(hq, hk, hv, hg, st0, batch, seq, chunk, tile, transpose_out):
    nt = seq // tile
    row = pl.BlockSpec((tile, B_FDIM), lambda b, t: (b * nt + t, 0))
    return pl.pallas_call(
        functools.partial(_hgrn_kernel, chunk=chunk, tile=tile, transpose_out=transpose_out),
        grid=(batch, nt),
        in_specs=[row, row, row, row, _full(st0.shape)],
        out_specs=[row, pl.BlockSpec((1, B_HEADS, B_DV, B_DK), lambda b, t: (b, 0, 0, 0))],
        out_shape=[jax.ShapeDtypeStruct((batch * seq, B_WIDTH), F32),
                   jax.ShapeDtypeStruct((batch, B_HEADS, B_DV, B_DK), F32)],
        scratch_shapes=[pltpu.VMEM((B_HEADS, B_DV, B_DK), F32), pltpu.VMEM((B_HEADS, chunk, B_DK), F32),
                        pltpu.VMEM((B_HEADS, chunk, B_DK), F32)],
        compiler_params=_params(2), name="hgrn_chunks")(hq, hk, hv, hg, st0)


def _split2(x):
    hi = x.astype(BF16)
    return hi, (x - hi.astype(F32)).astype(BF16)


def _paged_kernel(pt_ref, lat_hbm, kr_hbm, qrow_ref, qr_ref, qh_ref, kh_ref, latn_ref, wukt_ref,
                  gkrow_ref, gkr_ref, wuv_ref, o_ref, lhs, latbuf, krbuf, cbuf, kx_sc, sems, *, layer, n_pages):
    b = pl.program_id(0)
    gp = PAGES_PER_GROUP
    n_groups = n_pages // gp
    gpos = gp * PAGE
    nk = A_HEADS * A_NOPE
    log2e = 1.4426950408889634
    sc2 = (A_QK ** -0.5) * log2e

    def page_copies(bb, g, slot):
        out = []
        for j in range(gp):
            page = pt_ref[bb, g * gp + j]
            dst = pl.ds(j * PAGE, PAGE)
            out.append(pltpu.make_async_copy(lat_hbm.at[layer, page], latbuf.at[slot, dst], sems.at[0, slot]))
            out.append(pltpu.make_async_copy(kr_hbm.at[layer, page], krbuf.at[slot, j], sems.at[1, slot]))
        return out

    def start(bb, g, slot):
        for cp in page_copies(bb, g, slot):
            cp.start()

    def wait(g, slot):
        for cp in page_copies(b, g, slot):
            cp.wait()

    ring = DMA_SLOTS
    ahead = ring - 1
    assert n_groups % ring == 0 and ahead <= n_groups

    def fetch_ahead(g):
        gg = g + ahead
        wrap = gg >= n_groups
        bb = jnp.where(wrap, b + 1, b)
        g2 = jnp.where(wrap, gg - n_groups, gg)

        @pl.when(bb < pl.num_programs(0))
        def _():
            start(bb, g2, lax.rem(gg, ring))

    @pl.when(b == 0)
    def _():
        for d in range(ahead):
            start(0, d, d)

    @pl.when(b == 0)
    def _():
        lhs[0:nk, :] = wukt_ref[...]

    hrow = lax.broadcasted_iota(jnp.int32, (A_HEADS, nk), 0)
    hcol = lax.broadcasted_iota(jnp.int32, (A_HEADS, nk), 1) // A_NOPE
    qsel = jnp.where(hrow == hcol, qrow_ref[0] * gkrow_ref[...], 0.0)
    q_hi, q_lo = _split2(qsel)
    qabs = _dot(q_hi, wukt_ref[...]) + _dot(q_lo, wukt_ref[...])
    lhs[nk:nk + 2 * A_HEADS, :] = jnp.concatenate(_split2(qabs), axis=0)
    qrg = jnp.concatenate(_split2(qr_ref[0] * gkr_ref[...]), axis=0)
    ppb = POS_BLOCK // PAGE

    def scores(slot, cslot, between=None):
        parts = []
        nblk = gpos // POS_BLOCK

        def project(j):
            rows = pl.ds(j * POS_BLOCK, POS_BLOCK)
            cb = latbuf[slot, rows, :].astype(BF16)
            cbuf[cslot, rows, :] = cb
            kx_sc[j] = _dot_nt(lhs[...], cb)

        for j in range(nblk):
            project(j)
        if between is not None:
            between()
        for j in range(nblk):
            kn = kx_sc[j, 0:nk, :]
            ss = jnp.sum((kn * kn).reshape(A_HEADS, A_NOPE, POS_BLOCK), axis=1)
            raw = kx_sc[j, nk:nk + A_HEADS, :] + kx_sc[j, nk + A_HEADS:nk + 2 * A_HEADS, :]
            krt = jnp.concatenate([krbuf[slot, j * ppb + t] for t in range(ppb)], axis=-1)
            rr = _dot(qrg, krt.astype(BF16))
            raw = raw + rr[:A_HEADS] + rr[A_HEADS:]
            ss = ss + jnp.sum(krt * krt, axis=0, keepdims=True)
            parts.append(raw * lax.rsqrt(ss * (1.0 / A_QK) + EPS) * sc2)
        return jnp.concatenate(parts, axis=-1)

    def absorb(s, slot, carry):
        m, l, acc = carry
        m_new = jnp.maximum(m, jnp.max(s, axis=-1, keepdims=True))
        a = jnp.exp2(m - m_new)
        p = jnp.exp2(s - m_new)
        l = l * a + jnp.sum(p, axis=-1, keepdims=True)
        acc = acc * a + _dot(p.astype(BF16), cbuf[slot])
        return m_new, l, acc

    wait(0, 0)
    fetch_ahead(0)
    s0 = scores(0, 0)
    init = (jnp.full((A_HEADS, 1), -jnp.inf, F32), jnp.zeros((A_HEADS, 1), F32),
            jnp.zeros((A_HEADS, KV_LORA), F32))

    def group(g, carry):
        s_prev, state = carry
        cslot = lax.rem(g, 2)
        wait(g, lax.rem(g, ring))
        fetch_ahead(g)
        box = []
        s_cur = scores(lax.rem(g, ring), cslot, lambda: box.append(absorb(s_prev, 1 - cslot, state)))
        return s_cur, box[0]

    s_last, state = lax.fori_loop(1, n_groups, group, (s0, init))
    m, l, acc = absorb(s_last, (n_groups - 1) % 2, state)

    s_new = jnp.sum(qh_ref[0].astype(F32) * kh_ref[0].astype(F32), axis=-1, keepdims=True) * sc2
    m_new = jnp.maximum(m, s_new)
    a = jnp.exp2(m - m_new)
    p = jnp.exp2(s_new - m_new)
    acc = acc * a + p * latn_ref[0]
    l = l * a + p
    o_lat = (acc / l).astype(BF16)
    full = _dot(o_lat, wuv_ref[...])
    vrow = lax.broadcasted_iota(jnp.int32, (A_HEADS, A_WIDTH), 0)
    vcol = lax.broadcasted_iota(jnp.int32, (A_HEADS, A_WIDTH), 1) // A_V
    o_ref[0] = jnp.sum(jnp.where(vrow == vcol, full, 0.0), axis=0, keepdims=True)


def _paged(page_table, cache_lat, cache_kr, qrow, qr, qh, kh, latn, wukt, gkrow, gkr, wuv, layer):
    nb, n_pages = page_table.shape
    gp = PAGES_PER_GROUP
    per_b = lambda shape: pl.BlockSpec((1,) + shape, lambda b, pt: (b,) + (0,) * len(shape))
    full = lambda a: pl.BlockSpec(a.shape, lambda b, pt: (0,) * a.ndim)
    grid_spec = pltpu.PrefetchScalarGridSpec(
        num_scalar_prefetch=1, grid=(nb,),
        in_specs=[pl.BlockSpec(memory_space=pl.ANY), pl.BlockSpec(memory_space=pl.ANY),
                  per_b((1, A_HEADS * A_NOPE)), per_b((A_HEADS, A_ROPE)), per_b((A_HEADS, HEAD_PAD)),
                  per_b((A_HEADS, HEAD_PAD)), per_b((1, KV_LORA)), full(wukt), full(gkrow), full(gkr), full(wuv)],
        out_specs=per_b((1, A_WIDTH)),
        scratch_shapes=[pltpu.VMEM((A_HEADS * A_NOPE + 2 * A_HEADS, KV_LORA), BF16),
                        pltpu.VMEM((DMA_SLOTS, gp * PAGE, KV_LORA), F32), pltpu.VMEM((DMA_SLOTS, gp, A_ROPE, PAGE), F32),
                        pltpu.VMEM((2, gp * PAGE, KV_LORA), BF16),
                        pltpu.VMEM((gp * PAGE // POS_BLOCK, A_HEADS * A_NOPE + 2 * A_HEADS, POS_BLOCK), F32),
                        pltpu.SemaphoreType.DMA((2, DMA_SLOTS))])
    return pl.pallas_call(
        functools.partial(_paged_kernel, layer=layer, n_pages=n_pages), grid_spec=grid_spec,
        out_shape=jax.ShapeDtypeStruct((nb, 1, A_WIDTH), F32),
        compiler_params=_params(1), name="paged_attn")(
            page_table, cache_lat, cache_kr, qrow, qr, qh, kh, latn, wukt, gkrow, gkr, wuv)


def _hstep_kernel(s_ref, q_ref, k_ref, g_ref, v_ref, o_ref, sn_ref, *, layer, rows):
    r = lax.broadcasted_iota(jnp.int32, (B_DK, B_DK), 0)
    c = lax.broadcasted_iota(jnp.int32, (B_DK, B_DK), 1)
    eye = r == c

    def column(row):
        return jnp.sum(jnp.where(eye, row, 0.0), axis=-1, keepdims=True)

    for i in range(rows):
        for h in range(B_HEADS):
            hs = slice(h * B_DK, (h + 1) * B_DK)
            sn = column(jnp.exp(g_ref[i, :, hs])) * s_ref[layer, i, h] + column(k_ref[i, :, hs]) * v_ref[i, :, hs]
            sn_ref[i, h] = sn
            o_ref[i, :, hs] = jnp.sum(column(q_ref[i, :, hs]) * sn, axis=0, keepdims=True)


def _hstep(state, q, k, g, v, layer):
    nb = state.shape[1]
    rows = HSTEP_ROWS
    st_in = pl.BlockSpec((state.shape[0], rows, B_HEADS, B_DK, B_DV), lambda b: (0, b, 0, 0, 0))
    st_out = pl.BlockSpec((rows, B_HEADS, B_DK, B_DV), lambda b: (b, 0, 0, 0))
    row = pl.BlockSpec((rows, 1, B_FDIM), lambda b: (b, 0, 0))
    return pl.pallas_call(
        functools.partial(_hstep_kernel, layer=layer, rows=rows), grid=(nb // rows,),
        in_specs=[st_in, row, row, row, row], out_specs=[row, st_out],
        out_shape=[jax.ShapeDtypeStruct((nb, 1, B_WIDTH), F32), jax.ShapeDtypeStruct(state.shape[1:], F32)],
        compiler_params=_params(1), name="hgrn_step")(state, q, k, g, v)


def _merge_kernel(x_ref, ng_ref, wg_ref, oa_ref, ob_ref, gbn_ref, woa_ref, wob_ref, wo_ref, y_ref):
    x = x_ref[...]
    xn = _rms(x, ng_ref[...]).astype(BF16)
    gates = _dot(xn, wg_ref[...])
    ga, gb = gates[:, :A_WIDTH], gates[:, A_WIDTH:A_WIDTH + B_WIDTH]
    ma = gates[:, A_WIDTH + B_WIDTH:A_WIDTH + B_WIDTH + D_MODEL]
    mb = gates[:, A_WIDTH + B_WIDTH + D_MODEL:]
    ya = _dot((oa_ref[...] * (ga * jax.nn.sigmoid(ga))).astype(BF16), woa_ref[...])
    gbn = gbn_ref[...]
    obn = jnp.concatenate([_rms(ob_ref[:, h * B_DV:(h + 1) * B_DV], gbn) for h in range(B_HEADS)], axis=-1)
    yb = _dot((obn * (gb * jax.nn.sigmoid(gb))).astype(BF16), wob_ref[...])
    mix = jax.nn.sigmoid(ma) * ya + jax.nn.sigmoid(mb) * yb
    y_ref[...] = x + _dot(mix.astype(BF16), wo_ref[...])


def _merge(x, oa, ob, w, tm):
    rows = x.shape[0]
    row = lambda n: pl.BlockSpec((tm, n), lambda i: (i, 0))
    ins = [x, w["norm_g"], w["wg"], oa, ob, w["g_bn"], w["w_oa"], w["w_ob"], w["w_o"]]
    in_specs = [row(D_MODEL), _full(ins[1].shape), _full(ins[2].shape), row(A_WIDTH), row(B_WIDTH)] + \
               [_full(a.shape) for a in ins[5:]]
    return pl.pallas_call(
        _merge_kernel, grid=(rows // tm,), in_specs=in_specs, out_specs=row(D_MODEL),
        out_shape=jax.ShapeDtypeStruct((rows, D_MODEL), F32),
        compiler_params=_params(1), name="merge_out")(*ins)


def _head_pad_cols(w3, width):
    pad = jnp.zeros(w3.shape[:2] + (HEAD_PAD - width,), w3.dtype)
    return jnp.concatenate([w3, pad], axis=-1).reshape(w3.shape[0], -1)


def _rope_tables(pos):
    half = A_ROPE // 2
    inv = ROPE_THETA ** (-jnp.arange(half, dtype=F32) / half)
    ang = pos.astype(F32)[:, None] * inv
    cos, sin = jnp.cos(ang), jnp.sin(ang)
    n = pos.shape[0]
    one, zero = jnp.ones((n, A_NOPE), F32), jnp.zeros((n, A_NOPE), F32)
    tail = jnp.zeros((n, HEAD_PAD - A_QK), F32)
    z16 = jnp.zeros((n, half), F32)
    c = jnp.concatenate([one, cos, cos, tail], axis=-1)
    s1 = jnp.concatenate([zero, -sin, z16, tail], axis=-1)
    s2 = jnp.concatenate([zero, z16, sin, tail], axis=-1)
    return c, s1, s2


def _lane_gain(g):
    return jnp.concatenate([g, jnp.zeros((HEAD_PAD - A_QK,), g.dtype)])[None, :]


def kernel(x_prompt, x_sample, cache_latent, cache_krope, state_hgrn, page_table, meta_tokens,
           norm_g, w_in, g_cq, w_uq, g_ckv, w_uk, w_uv, g_qn, g_kn, lb_logits, g_bn, w_oa, w_ob, w_o):
    batch, seq, _ = x_prompt.shape
    dec_batch, dec_seq, _ = x_sample.shape
    depth = w_in.shape[0]
    assert depth == 1 and dec_seq == 1
    past_len = page_table.shape[1] * PAGE
    l = 0

    o0 = Q_LORA
    o1 = o0 + KV_LORA
    o2 = o1 + A_ROPE
    o3 = o2 + 3 * B_FDIM
    wi = w_in[l]
    wkr = jnp.zeros((D_MODEL, HEAD_PAD), F32).at[:, A_NOPE:A_QK].set(wi[:, o1:o2])
    w = {
        "norm_g": norm_g[l][None, :],
        "wa": jnp.concatenate([wi[:, :o1], wkr, wi[:, o2:o3]], axis=1).astype(BF16), "wg": wi[:, o3:].astype(BF16),
        "g_cq": g_cq[l][None, :], "g_ckv": g_ckv[l][None, :],
        "w_uq": _head_pad_cols(w_uq[l].reshape(Q_LORA, A_HEADS, A_QK), A_QK).astype(BF16),
        "w_uk": _head_pad_cols(w_uk[l], A_NOPE).astype(BF16),
        "w_uv": w_uv[l].reshape(KV_LORA, A_WIDTH).astype(BF16),
        "w_uvt": w_uv[l].reshape(KV_LORA, A_WIDTH).T.astype(BF16),
        "g_qn": _lane_gain(g_qn[l]), "g_kn": _lane_gain(g_kn[l]),
        "lb_logits": lb_logits, "g_bn": g_bn[l][None, :],
        "w_oa": w_oa[l].astype(BF16), "w_ob": w_ob[l].astype(BF16), "w_o": w_o[l].astype(BF16),
    }

    xs = x_sample.reshape(dec_batch, D_MODEL)
    tabs_m = _rope_tables(jnp.arange(N_META))
    tabs_s = _rope_tables(jnp.full((dec_batch,), past_len, jnp.int32))
    tabs_ms = tuple(jnp.concatenate([a, b], axis=0) for a, b in zip(tabs_m, tabs_s))
    small = _pre(jnp.concatenate([meta_tokens, xs], axis=0), tabs_ms, w, N_META + dec_batch, 1, l)
    lat_m, krp_m, _, k_m, _, mq, mk, mv, mg = [a[:N_META] for a in small]
    lat_s, krp_s, q_s, k_s, _, sq, sk, sv, sg = [a[N_META:] for a in small]
    vt_m = small[4][:, :N_META]

    xp = x_prompt.reshape(batch * seq, D_MODEL)
    tabs_p = _rope_tables(N_META + jnp.arange(seq))
    qscale = (A_QK ** -0.5) * 1.4426950408889634
    lat_p, krp_p, q_p, k_p, vt_p, hq, hk, hv, hg = _pre(xp, tabs_p, w, PRE_TILE, seq // PRE_TILE, l, qscale, PRE_SUB)
    lat_all = jnp.concatenate([jnp.broadcast_to(lat_m[None], (batch, N_META, KV_LORA)),
                               lat_p.reshape(batch, seq, KV_LORA)], axis=1)
    kr_all = jnp.concatenate([jnp.broadcast_to(krp_m[None, :, A_NOPE:A_QK], (batch, N_META, A_ROPE)),
                              krp_p[:, A_NOPE:A_QK].reshape(batch, seq, A_ROPE)], axis=1)

    oa = _attn(q_p, k_p, vt_p, k_m, vt_m, batch, seq)
    zero_state = jnp.zeros((B_HEADS, B_DV, B_DK), F32)
    _, st_meta = _hgrn(mq, mk, mv, mg, zero_state, 1, N_META, N_META, N_META, False)
    ob, st_fin = _hgrn(hq, hk, hv, hg, st_meta[0], batch, seq, HGRN_CHUNK, HGRN_TILE, True)
    y_prompt = _merge(xp, oa, ob, w, ROW_TILE).reshape(batch, seq, D_MODEL)

    q_s3 = q_s.reshape(dec_batch, A_HEADS, HEAD_PAD)
    k_s3 = k_s.reshape(dec_batch, A_HEADS, HEAD_PAD)
    qrow = q_s3[:, :, :A_NOPE].astype(F32).reshape(dec_batch, 1, A_HEADS * A_NOPE)
    qrope = q_s3[:, :, A_NOPE:A_QK].astype(F32)
    gkrow = jnp.tile(g_kn[l][:A_NOPE], A_HEADS)[None, :]
    gkr = g_kn[l][None, A_NOPE:]
    wukt = w_uk[l].reshape(KV_LORA, A_HEADS * A_NOPE).T.astype(BF16)
    oa_s = _paged(page_table, cache_latent, jnp.swapaxes(cache_krope, 2, 3), qrow, qrope, q_s3, k_s3,
                  lat_s[:, None, :], wukt, gkrow, gkr, w["w_uv"], l)
    row3 = lambda a: a.reshape(dec_batch, 1, B_FDIM)
    ob_s, st_s = _hstep(state_hgrn, row3(sq), row3(sk), row3(sg), row3(sv), l)
    y_sample = _merge(xs, oa_s.reshape(dec_batch, A_WIDTH), ob_s.reshape(dec_batch, B_WIDTH), w, dec_batch)

    return (y_prompt, y_sample.reshape(dec_batch, dec_seq, D_MODEL),
            lat_all[None], kr_all[None], st_fin[None],
            lat_s.reshape(1, dec_batch, dec_seq, KV_LORA),
            krp_s[:, A_NOPE:A_QK].reshape(1, dec_batch, dec_seq, A_ROPE),
            st_s[None])
```

```python
import functools

import jax
import jax.numpy as jnp
from jax import lax
from jax.experimental import pallas as pl
from jax.experimental.pallas import tpu as pltpu

F32 = jnp.float32
BF16 = jnp.bfloat16

D_MODEL = 1024
N_META = 16
A_HEADS = 8
A_NOPE = 64
A_ROPE = 32
A_QK = A_NOPE + A_ROPE
A_V = 64
A_WIDTH = A_HEADS * A_V
Q_LORA = 384
KV_LORA = 256
ROPE_THETA = 10000.0
B_HEADS = 4
B_DK = 128
B_DV = 128
B_FDIM = B_HEADS * B_DK
B_WIDTH = B_HEADS * B_DV
EPS = 1e-6
PAGE = 128

LANES = 128
LOG2E = 1.4426950408889634
HEAD_PAD = LANES
VMEM_LIMIT = 56 * 1024 * 1024

PRE_TILE = 256
PRE_SUB = 256
ROW_TILE = 1024
ATTN_TQ = 256
ATTN_TK = 256
ATTN_QT = 2
HGRN_CHUNK = 128
HGRN_TILE = 1024
PAGES_PER_GROUP = 32
HSTEP_ROWS = 8
POS_BLOCK = 2 * PAGE
DMA_SLOTS = 4


def _full(shape):
    return pl.BlockSpec(shape, lambda *_: (0,) * len(shape))


def _params(n_axes):
    return pltpu.CompilerParams(dimension_semantics=("arbitrary",) * n_axes,
                                vmem_limit_bytes=VMEM_LIMIT)


def _dot(a, b):
    return jnp.dot(a, b, preferred_element_type=F32)


def _dot_nt(a, b, precision=None):
    return lax.dot_general(a, b, (((1,), (1,)), ((), ())), precision=precision,
                           preferred_element_type=F32)


def _rms(x, g):
    r = lax.rsqrt(jnp.mean(x * x, axis=-1, keepdims=True) + EPS)
    return (x * r) * g


def _rope_lanes(t, c, s1, s2):
    return t * c + pltpu.roll(t, LANES - A_ROPE // 2, axis=1) * s1 + pltpu.roll(t, A_ROPE // 2, axis=1) * s2


def _pre_kernel(x_ref, ng_ref, wa_ref, gcq_ref, wuq_ref, gckv_ref,
                wuk_ref, wuv_ref, gqn_ref, gkn_ref, c_ref, s1_ref, s2_ref, lb_ref,
                lat_ref, krp_ref, q_ref, k_ref, vt_ref, hq_ref, hk_ref, hv_ref, hg_ref, *, layer, qscale, sub):
    lbl = lb_ref[...]
    e = jnp.exp(lbl - jnp.max(lbl, axis=0, keepdims=True))
    lb = jnp.sum(e[:layer + 1], axis=0, keepdims=True) / jnp.sum(e, axis=0, keepdims=True)
    gqn, gkn = gqn_ref[...], gkn_ref[...]
    o1, o2, o3 = Q_LORA, Q_LORA + KV_LORA, Q_LORA + KV_LORA + LANES

    def rows_step(rows):
        x = x_ref[rows, :]
        xn = _rms(x, ng_ref[...]).astype(BF16)
        c, s1, s2 = c_ref[rows, :], s1_ref[rows, :], s2_ref[rows, :]

        y = _dot(xn, wa_ref[:, :o3])
        cqn = _rms(y[:, :o1], gcq_ref[...]).astype(BF16)
        qraw = _dot(cqn, wuq_ref[...])
        for h in range(A_HEADS):
            hs = slice(h * HEAD_PAD, (h + 1) * HEAD_PAD)
            t = _rope_lanes(qraw[:, hs], c, s1, s2)
            r = lax.rsqrt(jnp.sum(t * t, axis=-1, keepdims=True) * (1.0 / A_QK) + EPS)
            q_ref[rows, hs] = (((t * r) * gqn) * qscale).astype(BF16)

        ckv = _rms(y[:, o1:o2], gckv_ref[...])
        lat_ref[rows, :] = ckv
        ckvb = ckv.astype(BF16)
        krp = _rope_lanes(y[:, o2:o3], c, s1, s2)
        krp_ref[rows, :] = krp
        kraw = _dot(ckvb, wuk_ref[...])
        for h in range(A_HEADS):
            hs = slice(h * HEAD_PAD, (h + 1) * HEAD_PAD)
            t = kraw[:, hs] + krp
            r = lax.rsqrt(jnp.sum(t * t, axis=-1, keepdims=True) * (1.0 / A_QK) + EPS)
            k_ref[rows, hs] = ((t * r) * gkn).astype(BF16)
        vt_ref[:, rows] = _dot_nt(wuv_ref[...], ckvb).astype(BF16)

        b = _dot(xn, wa_ref[:, o3:])
        bq, z, bi = b[:, :B_FDIM], b[:, B_FDIM:2 * B_FDIM], b[:, 2 * B_FDIM:]
        hq_ref[rows, :] = bq * jax.nn.sigmoid(bq)
        sz = jax.nn.sigmoid(z)
        hg_ref[rows, :] = jnp.log(lb + (1.0 - lb) * sz)
        hk_ref[rows, :] = (1.0 - lb) * (1.0 - sz)
        hv_ref[rows, :] = bi

    tm = x_ref.shape[0]
    if tm == sub:
        rows_step(slice(None))
    else:
        def body(r, _):
            rows_step(pl.ds(pl.multiple_of(r * sub, sub), sub))
            return 0
        lax.fori_loop(0, tm // sub, body, 0)


def _pre(x, tabs, w, tm, tab_blocks, layer, qscale=1.0, sub=None):
    rows = x.shape[0]
    grid = (rows // tm,)
    row = lambda n: pl.BlockSpec((tm, n), lambda i: (i, 0))
    tab = pl.BlockSpec((tm, LANES), lambda i: (i % tab_blocks, 0))
    ins = [x, w["norm_g"], w["wa"], w["g_cq"], w["w_uq"], w["g_ckv"],
           w["w_uk"], w["w_uvt"], w["g_qn"], w["g_kn"], tabs[0], tabs[1], tabs[2], w["lb_logits"]]
    in_specs = [row(D_MODEL)] + [_full(a.shape) for a in ins[1:10]] + [tab, tab, tab] + [_full(ins[13].shape)]
    outs = [(KV_LORA, F32), (LANES, F32), (A_HEADS * HEAD_PAD, BF16), (A_HEADS * HEAD_PAD, BF16),
            None, (B_FDIM, F32), (B_FDIM, F32), (B_WIDTH, F32), (B_FDIM, F32)]
    out_specs = [pl.BlockSpec((A_WIDTH, tm), lambda i: (0, i)) if o is None else row(o[0]) for o in outs]
    out_shape = [jax.ShapeDtypeStruct((A_WIDTH, rows), BF16) if o is None else
                 jax.ShapeDtypeStruct((rows, o[0]), o[1]) for o in outs]
    return pl.pallas_call(
        functools.partial(_pre_kernel, layer=layer, qscale=qscale, sub=sub or tm),
        grid=grid, in_specs=in_specs, out_specs=out_specs, out_shape=out_shape,
        compiler_params=_params(1), name="pre_proj")(*ins)


def _attn_kernel(q_ref, k_ref, vt_ref, km_ref, vtm_ref, o_ref, ot_sc, s_sc, sm_sc, m_sc, l_sc):
    def tile(qt, _):
        i = pl.program_id(1) * ATTN_QT + qt
        qrows = pl.ds(pl.multiple_of(qt * ATTN_TQ, ATTN_TQ), ATTN_TQ)
        _attn_tile(i, qrows, q_ref, k_ref, vt_ref, km_ref, vtm_ref, o_ref, ot_sc, s_sc, sm_sc, m_sc, l_sc)
        return 0

    lax.fori_loop(0, ATTN_QT, tile, 0)


def _attn_tile(i, qrows, q_ref, k_ref, vt_ref, km_ref, vtm_ref, o_ref, ot_sc, s_sc, sm_sc, m_sc, l_sc):
    tq, tk = ATTN_TQ, ATTN_TK
    assert tq == tk
    key = lax.broadcasted_iota(jnp.int32, (tk, tq), 0)
    qry = lax.broadcasted_iota(jnp.int32, (tk, tq), 1)
    hsl = [slice(h * HEAD_PAD, (h + 1) * HEAD_PAD) for h in range(A_HEADS)]
    vsl = [slice(h * A_V, (h + 1) * A_V) for h in range(A_HEADS)]

    d0 = pl.multiple_of(i * tq, tq)
    for h in range(A_HEADS):
        s_sc[h, 0:tk, :] = _dot_nt(k_ref[pl.ds(d0, tk), hsl[h]], q_ref[qrows, hsl[h]])
        sm_sc[h] = _dot_nt(km_ref[:, hsl[h]], q_ref[qrows, hsl[h]])
    for h in range(A_HEADS):
        hr = slice(h, h + 1)
        s = jnp.where(key <= qry, s_sc[h, 0:tk, :], -jnp.inf)
        sm = sm_sc[h]
        m_new = jnp.maximum(jnp.max(s, axis=0, keepdims=True), jnp.max(sm, axis=0, keepdims=True))
        p = jnp.exp2(s - m_new)
        pm = jnp.exp2(sm - m_new)
        m_sc[hr, :] = m_new
        l_sc[hr, :] = jnp.sum(p, axis=0, keepdims=True) + jnp.sum(pm, axis=0, keepdims=True)
        ot_sc[vsl[h], :] = (_dot(vt_ref[vsl[h], pl.ds(d0, tk)], p.astype(BF16)) +
                            _dot(vtm_ref[vsl[h], :], pm.astype(BF16)))

    def visible(r0, keys):
        for h in range(A_HEADS):
            s_sc[h, 0:keys, :] = _dot_nt(k_ref[pl.ds(r0, keys), hsl[h]], q_ref[qrows, hsl[h]])
        for h in range(A_HEADS):
            hr = slice(h, h + 1)
            s = s_sc[h, 0:keys, :]
            m_new = jnp.maximum(m_sc[hr, :], jnp.max(s, axis=0, keepdims=True))
            a = jnp.exp2(m_sc[hr, :] - m_new)
            p = jnp.exp2(s - m_new)
            m_sc[hr, :] = m_new
            l_sc[hr, :] = l_sc[hr, :] * a + jnp.sum(p, axis=0, keepdims=True)
            ot_sc[vsl[h], :] = ot_sc[vsl[h], :] * a + _dot(vt_ref[vsl[h], pl.ds(r0, keys)], p.astype(BF16))

    def body(kb, _):
        visible(pl.multiple_of(kb * 4 * tk, 4 * tk), 4 * tk)
        return 0

    lax.fori_loop(0, i // 4, body, 0)

    @pl.when(i % 4 >= 2)
    def _():
        visible(pl.multiple_of((i // 4) * 4 * tk, 2 * tk), 2 * tk)

    @pl.when(i % 2 == 1)
    def _():
        visible(pl.multiple_of((i - 1) * tk, tk), tk)

    for h in range(A_HEADS):
        ot_sc[vsl[h], :] = ot_sc[vsl[h], :] / l_sc[h:h + 1, :]
    o_ref[qrows, :] = ot_sc[...].T


def _attn(q, k, vt, km, vtm, batch, seq):
    rows = ATTN_QT * ATTN_TQ
    nq = seq // rows
    return pl.pallas_call(
        _attn_kernel, grid=(batch, nq),
        in_specs=[pl.BlockSpec((rows, A_HEADS * HEAD_PAD), lambda b, i: (b * nq + i, 0)),
                  pl.BlockSpec((seq, A_HEADS * HEAD_PAD), lambda b, i: (b, 0)),
                  pl.BlockSpec((A_WIDTH, seq), lambda b, i: (0, b)),
                  _full(km.shape), _full(vtm.shape)],
        out_specs=pl.BlockSpec((rows, A_WIDTH), lambda b, i: (b * nq + i, 0)),
        out_shape=jax.ShapeDtypeStruct((batch * seq, A_WIDTH), F32),
        scratch_shapes=[pltpu.VMEM((A_WIDTH, ATTN_TQ), F32), pltpu.VMEM((A_HEADS, 4 * ATTN_TK, ATTN_TQ), F32),
                        pltpu.VMEM((A_HEADS, km.shape[0], ATTN_TQ), F32),
                        pltpu.VMEM((A_HEADS, ATTN_TQ), F32), pltpu.VMEM((A_HEADS, ATTN_TQ), F32)],
        compiler_params=_params(2), name="prompt_attn")(q, k, vt, km, vtm)


def _split3(x):
    a = x.astype(BF16)
    r = x - a.astype(F32)
    b = r.astype(BF16)
    c = (r - b.astype(F32)).astype(BF16)
    return a, b, c


def _hgrn_chunk(q, kk, v, g, st, tril, chunk, cum_ref, krow):
    g1, g2, g3 = _split3(g)
    cum = (_dot(tril, g1) + _dot(tril, g2) + _dot(tril, g3)) * LOG2E
    cum_ref[...] = cum
    yield None
    nblk = chunk // 8
    cb = [cum[8 * j:8 * j + 8] for j in range(nblk)]
    qb = [q[8 * j:8 * j + 8] for j in range(nblk)]
    lane = lax.broadcasted_iota(jnp.int32, (8, chunk), 1)
    sub = lax.broadcasted_iota(jnp.int32, (8, chunk), 0)
    ab = []
    for j in range(nblk):
        a = jnp.zeros((8, chunk), F32)
        for s in range(8 * j, 8 * j + 8):
            col = jnp.sum(jnp.exp2(cb[j] - cum_ref[s:s + 1, :]) * qb[j] * krow(s), axis=-1, keepdims=True)
            a = jnp.where(lane == s, col, a)
        ab.append(jnp.where(lane <= sub + 8 * j, a, 0.0))
        yield None
    attn = jnp.concatenate(ab, axis=0)
    row = lax.broadcasted_iota(jnp.int32, (chunk, chunk), 0)
    col = lax.broadcasted_iota(jnp.int32, (chunk, chunk), 1)
    kb = [kk[8 * j:8 * j + 8] for j in range(nblk)]
    dead = jnp.zeros((8, B_DK), F32)
    w = 8
    while w < chunk:
        qs, ks = [], []
        for j in range(nblk):
            blk = (8 * j) // w
            if blk % 2:
                qs.append(qb[j] * jnp.exp2(cb[j] - cum_ref[blk * w - 1:blk * w, :]))
                ks.append(dead)
            else:
                qs.append(dead)
                ks.append(kb[j] * jnp.exp2(cum_ref[blk * w + w - 1:blk * w + w, :] - cb[j]))
        qa = jnp.concatenate(qs, axis=0).astype(BF16)
        ka = jnp.concatenate(ks, axis=0).astype(BF16)
        sh = w.bit_length() - 1
        rb, cbk = row >> sh, col >> sh
        attn = attn + jnp.where((rb == cbk + 1) & ((cbk & 1) == 0), _dot_nt(qa, ka), 0.0)
        w *= 2
        yield None
    last = cum_ref[chunk - 1:chunk, :]
    o = _dot(attn.astype(BF16), v.astype(BF16)) + _dot_nt((q * jnp.exp2(cum)).astype(BF16), st.astype(BF16))
    kdec = (kk * jnp.exp2(last - cum)).astype(BF16)
    upd = lax.dot_general(v.astype(BF16), kdec, (((0,), (0,)), ((), ())), preferred_element_type=F32)
    yield o, jnp.exp2(last) * st + upd


def _hgrn_kernel(q_ref, k_ref, v_ref, g_ref, st0_ref, o_ref, st_ref, st_sc, cum_sc, kk_sc, *, chunk, tile,
                 transpose_out):
    t = pl.program_id(1)

    @pl.when(t == 0)
    def _():
        st_sc[...] = st0_ref[...]

    r = lax.broadcasted_iota(jnp.int32, (chunk, chunk), 0)
    c = lax.broadcasted_iota(jnp.int32, (chunk, chunk), 1)
    tril = (c <= r).astype(BF16)

    def body(ci, _):
        r0 = pl.multiple_of(ci * chunk, chunk)
        hsl = [slice(h * B_DK, (h + 1) * B_DK) for h in range(B_HEADS)]
        stages = {}
        for h in range(B_HEADS):
            kk = k_ref[pl.ds(r0, chunk), hsl[h]]
            kk_sc[h] = kk
            krow = lambda s, h=h: kk_sc[h, s:s + 1, :]
            stages[h] = _hgrn_chunk(q_ref[pl.ds(r0, chunk), hsl[h]], kk,
                                    v_ref[pl.ds(r0, chunk), hsl[h]], g_ref[pl.ds(r0, chunk), hsl[h]],
                                    st_sc[h], tril, chunk, cum_sc.at[h], krow)
        while stages:
            for h in list(stages):
                out = next(stages[h])
                if out is not None:
                    o_ref[pl.ds(r0, chunk), hsl[h]] = out[0]
                    st_sc[h] = out[1]
                    del stages[h]
        return 0

    lax.fori_loop(0, tile // chunk, body, 0)

    @pl.when(t == pl.num_programs(1) - 1)
    def _():
        for h in range(B_HEADS):
            st_ref[0, h] = st_sc[h].T if transpose_out else st_sc[h]


def _hgrn(hq, hk, hv, hg, st0, batch, seq, chunk, tile, transpose_out):
    nt = seq // tile
    row = pl.BlockSpec((tile, B_FDIM), lambda b, t: (b * nt + t, 0))
    return pl.pallas_call(
        functools.partial(_hgrn_kernel, chunk=chunk, tile=tile, transpose_out=transpose_out),
        grid=(batch, nt),
        in_specs=[row, row, row, row, _full(st0.shape)],
        out_specs=[row, pl.BlockSpec((1, B_HEADS, B_DV, B_DK), lambda b, t: (b, 0, 0, 0))],
        out_shape=[jax.ShapeDtypeStruct((batch * seq, B_WIDTH), F32),
                   jax.ShapeDtypeStruct((batch, B_HEADS, B_DV, B_DK), F32)],
        scratch_shapes=[pltpu.VMEM((B_HEADS, B_DV, B_DK), F32), pltpu.VMEM((B_HEADS, chunk, B_DK), F32),
                        pltpu.VMEM((B_HEADS, chunk, B_DK), F32)],
        compiler_params=_params(2), name="hgrn_chunks")(hq, hk, hv, hg, st0)


def _split2(x):
    hi = x.astype(BF16)
    return hi, (x - hi.astype(F32)).astype(BF16)


def _paged_kernel(pt_ref, lat_hbm, kr_hbm, qrow_ref, qr_ref, qh_ref, kh_ref, latn_ref, wukt_ref,
                  gkrow_ref, gkr_ref, wuv_ref, o_ref, lhs, latbuf, krbuf, cbuf, kx_sc, sems, *, layer, n_pages):
    b = pl.program_id(0)
    gp = PAGES_PER_GROUP
    n_groups = n_pages // gp
    gpos = gp * PAGE
    nk = A_HEADS * A_NOPE
    log2e = 1.4426950408889634
    sc2 = (A_QK ** -0.5) * log2e

    def page_copies(bb, g, slot):
        out = []
        for j in range(gp):
            page = pt_ref[bb, g * gp + j]
            dst = pl.ds(j * PAGE, PAGE)
            out.append(pltpu.make_async_copy(lat_hbm.at[layer, page], latbuf.at[slot, dst], sems.at[0, slot]))
            out.append(pltpu.make_async_copy(kr_hbm.at[layer, page], krbuf.at[slot, j], sems.at[1, slot]))
        return out

    def start(bb, g, slot):
        for cp in page_copies(bb, g, slot):
            cp.start()

    def wait(g, slot):
        for cp in page_copies(b, g, slot):
            cp.wait()

    ring = DMA_SLOTS
    ahead = ring - 1
    assert n_groups % ring == 0 and ahead <= n_groups

    def fetch_ahead(g):
        gg = g + ahead
        wrap = gg >= n_groups
        bb = jnp.where(wrap, b + 1, b)
        g2 = jnp.where(wrap, gg - n_groups, gg)

        @pl.when(bb < pl.num_programs(0))
        def _():
            start(bb, g2, lax.rem(gg, ring))

    @pl.when(b == 0)
    def _():
        for d in range(ahead):
            start(0, d, d)

    @pl.when(b == 0)
    def _():
        lhs[0:nk, :] = wukt_ref[...]

    hrow = lax.broadcasted_iota(jnp.int32, (A_HEADS, nk), 0)
    hcol = lax.broadcasted_iota(jnp.int32, (A_HEADS, nk), 1) // A_NOPE
    qsel = jnp.where(hrow == hcol, qrow_ref[0] * gkrow_ref[...], 0.0)
    q_hi, q_lo = _split2(qsel)
    qabs = _dot(q_hi, wukt_ref[...]) + _dot(q_lo, wukt_ref[...])
    lhs[nk:nk + 2 * A_HEADS, :] = jnp.concatenate(_split2(qabs), axis=0)
    qrg = jnp.concatenate(_split2(qr_ref[0] * gkr_ref[...]), axis=0)
    ppb = POS_BLOCK // PAGE

    def scores(slot, cslot, between=None):
        parts = []
        nblk = gpos // POS_BLOCK

        def project(j):
            rows = pl.ds(j * POS_BLOCK, POS_BLOCK)
            cb = latbuf[slot, rows, :].astype(BF16)
            cbuf[cslot, rows, :] = cb
            kx_sc[j] = _dot_nt(lhs[...], cb)

        for j in range(nblk):
            project(j)
        if between is not None:
            between()
        for j in range(nblk):
            kn = kx_sc[j, 0:nk, :]
            ss = jnp.sum((kn * kn).reshape(A_HEADS, A_NOPE, POS_BLOCK), axis=1)
            raw = kx_sc[j, nk:nk + A_HEADS, :] + kx_sc[j, nk + A_HEADS:nk + 2 * A_HEADS, :]
            krt = jnp.concatenate([krbuf[slot, j * ppb + t] for t in range(ppb)], axis=-1)
            rr = _dot(qrg, krt.astype(BF16))
            raw = raw + rr[:A_HEADS] + rr[A_HEADS:]
            ss = ss + jnp.sum(krt * krt, axis=0, keepdims=True)
            parts.append(raw * lax.rsqrt(ss * (1.0 / A_QK) + EPS) * sc2)
        return jnp.concatenate(parts, axis=-1)

    def absorb(s, slot, carry):
        m, l, acc = carry
        m_new = jnp.maximum(m, jnp.max(s, axis=-1, keepdims=True))
        a = jnp.exp2(m - m_new)
        p = jnp.exp2(s - m_new)
        l = l * a + jnp.sum(p, axis=-1, keepdims=True)
        acc = acc * a + _dot(p.astype(BF16), cbuf[slot])
        return m_new, l, acc

    wait(0, 0)
    fetch_ahead(0)
    s0 = scores(0, 0)
    init = (jnp.full((A_HEADS, 1), -jnp.inf, F32), jnp.zeros((A_HEADS, 1), F32),
            jnp.zeros((A_HEADS, KV_LORA), F32))

    def group(g, carry):
        s_prev, state = carry
        cslot = lax.rem(g, 2)
        wait(g, lax.rem(g, ring))
        fetch_ahead(g)
        box = []
        s_cur = scores(lax.rem(g, ring), cslot, lambda: box.append(absorb(s_prev, 1 - cslot, state)))
        return s_cur, box[0]

    s_last, state = lax.fori_loop(1, n_groups, group, (s0, init))
    m, l, acc = absorb(s_last, (n_groups - 1) % 2, state)

    s_new = jnp.sum(qh_ref[0].astype(F32) * kh_ref[0].astype(F32), axis=-1, keepdims=True) * sc2
    m_new = jnp.maximum(m, s_new)
    a = jnp.exp2(m - m_new)
    p = jnp.exp2(s_new - m_new)
    acc = acc * a + p * latn_ref[0]
    l = l * a + p
    o_lat = (acc / l).astype(BF16)
    full = _dot(o_lat, wuv_ref[...])
    vrow = lax.broadcasted_iota(jnp.int32, (A_HEADS, A_WIDTH), 0)
    vcol = lax.broadcasted_iota(jnp.int32, (A_HEADS, A_WIDTH), 1) // A_V
    o_ref[0] = jnp.sum(jnp.where(vrow == vcol, full, 0.0), axis=0, keepdims=True)


def _paged(page_table, cache_lat, cache_kr, qrow, qr, qh, kh, latn, wukt, gkrow, gkr, wuv, layer):
    nb, n_pages = page_table.shape
    gp = PAGES_PER_GROUP
    per_b = lambda shape: pl.BlockSpec((1,) + shape, lambda b, pt: (b,) + (0,) * len(shape))
    full = lambda a: pl.BlockSpec(a.shape, lambda b, pt: (0,) * a.ndim)
    grid_spec = pltpu.PrefetchScalarGridSpec(
        num_scalar_prefetch=1, grid=(nb,),
        in_specs=[pl.BlockSpec(memory_space=pl.ANY), pl.BlockSpec(memory_space=pl.ANY),
                  per_b((1, A_HEADS * A_NOPE)), per_b((A_HEADS, A_ROPE)), per_b((A_HEADS, HEAD_PAD)),
                  per_b((A_HEADS, HEAD_PAD)), per_b((1, KV_LORA)), full(wukt), full(gkrow), full(gkr), full(wuv)],
        out_specs=per_b((1, A_WIDTH)),
        scratch_shapes=[pltpu.VMEM((A_HEADS * A_NOPE + 2 * A_HEADS, KV_LORA), BF16),
                        pltpu.VMEM((DMA_SLOTS, gp * PAGE, KV_LORA), F32), pltpu.VMEM((DMA_SLOTS, gp, A_ROPE, PAGE), F32),
                        pltpu.VMEM((2, gp * PAGE, KV_LORA), BF16),
                        pltpu.VMEM((gp * PAGE // POS_BLOCK, A_HEADS * A_NOPE + 2 * A_HEADS, POS_BLOCK), F32),
                        pltpu.SemaphoreType.DMA((2, DMA_SLOTS))])
    return pl.pallas_call(
        functools.partial(_paged_kernel, layer=layer, n_pages=n_pages), grid_spec=grid_spec,
        out_shape=jax.ShapeDtypeStruct((nb, 1, A_WIDTH), F32),
        compiler_params=_params(1), name="paged_attn")(
            page_table, cache_lat, cache_kr, qrow, qr, qh, kh, latn, wukt, gkrow, gkr, wuv)


def _hstep_kernel(s_ref, q_ref, k_ref, g_ref, v_ref, o_ref, sn_ref, *, layer, rows):
    r = lax.broadcasted_iota(jnp.int32, (B_DK, B_DK), 0)
    c = lax.broadcasted_iota(jnp.int32, (B_DK, B_DK), 1)
    eye = r == c

    def column(row):
        return jnp.sum(jnp.where(eye, row, 0.0), axis=-1, keepdims=True)

    for i in range(rows):
        for h in range(B_HEADS):
            hs = slice(h * B_DK, (h + 1) * B_DK)
            sn = column(jnp.exp(g_ref[i, :, hs])) * s_ref[layer, i, h] + column(k_ref[i, :, hs]) * v_ref[i, :, hs]
            sn_ref[i, h] = sn
            o_ref[i, :, hs] = jnp.sum(column(q_ref[i, :, hs]) * sn, axis=0, keepdims=True)


def _hstep(state, q, k, g, v, layer):
    nb = state.shape[1]
    rows = HSTEP_ROWS
    st_in = pl.BlockSpec((state.shape[0], rows, B_HEADS, B_DK, B_DV), lambda b: (0, b, 0, 0, 0))
    st_out = pl.BlockSpec((rows, B_HEADS, B_DK, B_DV), lambda b: (b, 0, 0, 0))
    row = pl.BlockSpec((rows, 1, B_FDIM), lambda b: (b, 0, 0))
    return pl.pallas_call(
        functools.partial(_hstep_kernel, layer=layer, rows=rows), grid=(nb // rows,),
        in_specs=[st_in, row, row, row, row], out_specs=[row, st_out],
        out_shape=[jax.ShapeDtypeStruct((nb, 1, B_WIDTH), F32), jax.ShapeDtypeStruct(state.shape[1:], F32)],
        compiler_params=_params(1), name="hgrn_step")(state, q, k, g, v)


def _merge_kernel(x_ref, ng_ref, wg_ref, oa_ref, ob_ref, gbn_ref, woa_ref, wob_ref, wo_ref, y_ref):
    x = x_ref[...]
    xn = _rms(x, ng_ref[...]).astype(BF16)
    gates = _dot(xn, wg_ref[...])
    ga, gb = gates[:, :A_WIDTH], gates[:, A_WIDTH:A_WIDTH + B_WIDTH]
    ma = gates[:, A_WIDTH + B_WIDTH:A_WIDTH + B_WIDTH + D_MODEL]
    mb = gates[:, A_WIDTH + B_WIDTH + D_MODEL:]
    ya = _dot((oa_ref[...] * (ga * jax.nn.sigmoid(ga))).astype(BF16), woa_ref[...])
    gbn = gbn_ref[...]
    obn = jnp.concatenate([_rms(ob_ref[:, h * B_DV:(h + 1) * B_DV], gbn) for h in range(B_HEADS)], axis=-1)
    yb = _dot((obn * (gb * jax.nn.sigmoid(gb))).astype(BF16), wob_ref[...])
    mix = jax.nn.sigmoid(ma) * ya + jax.nn.sigmoid(mb) * yb
    y_ref[...] = x + _dot(mix.astype(BF16), wo_ref[...])


def _merge(x, oa, ob, w, tm):
    rows = x.shape[0]
    row = lambda n: pl.BlockSpec((tm, n), lambda i: (i, 0))
    ins = [x, w["norm_g"], w["wg"], oa, ob, w["g_bn"], w["w_oa"], w["w_ob"], w["w_o"]]
    in_specs = [row(D_MODEL), _full(ins[1].shape), _full(ins[2].shape), row(A_WIDTH), row(B_WIDTH)] + \
               [_full(a.shape) for a in ins[5:]]
    return pl.pallas_call(
        _merge_kernel, grid=(rows // tm,), in_specs=in_specs, out_specs=row(D_MODEL),
        out_shape=jax.ShapeDtypeStruct((rows, D_MODEL), F32),
        compiler_params=_params(1), name="merge_out")(*ins)


def _head_pad_cols(w3, width):
    pad = jnp.zeros(w3.shape[:2] + (HEAD_PAD - width,), w3.dtype)
    return jnp.concatenate([w3, pad], axis=-1).reshape(w3.shape[0], -1)


def _rope_tables(pos):
    half = A_ROPE // 2
    inv = ROPE_THETA ** (-jnp.arange(half, dtype=F32) / half)
    ang = pos.astype(F32)[:, None] * inv
    cos, sin = jnp.cos(ang), jnp.sin(ang)
    n = pos.shape[0]
    one, zero = jnp.ones((n, A_NOPE), F32), jnp.zeros((n, A_NOPE), F32)
    tail = jnp.zeros((n, HEAD_PAD - A_QK), F32)
    z16 = jnp.zeros((n, half), F32)
    c = jnp.concatenate([one, cos, cos, tail], axis=-1)
    s1 = jnp.concatenate([zero, -sin, z16, tail], axis=-1)
    s2 = jnp.concatenate([zero, z16, sin, tail], axis=-1)
    return c, s1, s2


def _lane_gain(g):
    return jnp.concatenate([g, jnp.zeros((HEAD_PAD - A_QK,), g.dtype)])[None, :]


def kernel(x_prompt, x_sample, cache_latent, cache_krope, state_hgrn, page_table, meta_tokens,
           norm_g, w_in, g_cq, w_uq, g_ckv, w_uk, w_uv, g_qn, g_kn, lb_logits, g_bn, w_oa, w_ob, w_o):
    batch, seq, _ = x_prompt.shape
    dec_batch, dec_seq, _ = x_sample.shape
    depth = w_in.shape[0]
    assert depth == 1 and dec_seq == 1
    past_len = page_table.shape[1] * PAGE
    l = 0

    o0 = Q_LORA
    o1 = o0 + KV_LORA
    o2 = o1 + A_ROPE
    o3 = o2 + 3 * B_FDIM
    wi = w_in[l]
    wkr = jnp.zeros((D_MODEL, HEAD_PAD), F32).at[:, A_NOPE:A_QK].set(wi[:, o1:o2])
    w = {
        "norm_g": norm_g[l][None, :],
        "wa": jnp.concatenate([wi[:, :o1], wkr, wi[:, o2:o3]], axis=1).astype(BF16), "wg": wi[:, o3:].astype(BF16),
        "g_cq": g_cq[l][None, :], "g_ckv": g_ckv[l][None, :],
        "w_uq": _head_pad_cols(w_uq[l].reshape(Q_LORA, A_HEADS, A_QK), A_QK).astype(BF16),
        "w_uk": _head_pad_cols(w_uk[l], A_NOPE).astype(BF16),
        "w_uv": w_uv[l].reshape(KV_LORA, A_WIDTH).astype(BF16),
        "w_uvt": w_uv[l].reshape(KV_LORA, A_WIDTH).T.astype(BF16),
        "g_qn": _lane_gain(g_qn[l]), "g_kn": _lane_gain(g_kn[l]),
        "lb_logits": lb_logits, "g_bn": g_bn[l][None, :],
        "w_oa": w_oa[l].astype(BF16), "w_ob": w_ob[l].astype(BF16), "w_o": w_o[l].astype(BF16),
    }

    xs = x_sample.reshape(dec_batch, D_MODEL)
    tabs_m = _rope_tables(jnp.arange(N_META))
    tabs_s = _rope_tables(jnp.full((dec_batch,), past_len, jnp.int32))
    tabs_ms = tuple(jnp.concatenate([a, b], axis=0) for a, b in zip(tabs_m, tabs_s))
    small = _pre(jnp.concatenate([meta_tokens, xs], axis=0), tabs_ms, w, N_META + dec_batch, 1, l)
    lat_m, krp_m, _, k_m, _, mq, mk, mv, mg = [a[:N_META] for a in small]
    lat_s, krp_s, q_s, k_s, _, sq, sk, sv, sg = [a[N_META:] for a in small]
    vt_m = small[4][:, :N_META]

    xp = x_prompt.reshape(batch * seq, D_MODEL)
    tabs_p = _rope_tables(N_META + jnp.arange(seq))
    qscale = (A_QK ** -0.5) * 1.4426950408889634
    lat_p, krp_p, q_p, k_p, vt_p, hq, hk, hv, hg = _pre(xp, tabs_p, w, PRE_TILE, seq // PRE_TILE, l, qscale, PRE_SUB)
    lat_all = jnp.concatenate([jnp.broadcast_to(lat_m[None], (batch, N_META, KV_LORA)),
                               lat_p.reshape(batch, seq, KV_LORA)], axis=1)
    kr_all = jnp.concatenate([jnp.broadcast_to(krp_m[None, :, A_NOPE:A_QK], (batch, N_META, A_ROPE)),
                              krp_p[:, A_NOPE:A_QK].reshape(batch, seq, A_ROPE)], axis=1)

    oa = _attn(q_p, k_p, vt_p, k_m, vt_m, batch, seq)
    zero_state = jnp.zeros((B_HEADS, B_DV, B_DK), F32)
    _, st_meta = _hgrn(mq, mk, mv, mg, zero_state, 1, N_META, N_META, N_META, False)
    ob, st_fin = _hgrn(hq, hk, hv, hg, st_meta[0], batch, seq, HGRN_CHUNK, HGRN_TILE, True)
    y_prompt = _merge(xp, oa, ob, w, ROW_TILE).reshape(batch, seq, D_MODEL)

    q_s3 = q_s.reshape(dec_batch, A_HEADS, HEAD_PAD)
    k_s3 = k_s.reshape(dec_batch, A_HEADS, HEAD_PAD)
    qrow = q_s3[:, :, :A_NOPE].astype(F32).reshape(dec_batch, 1, A_HEADS * A_NOPE)
    qrope = q_s3[:, :, A_NOPE:A_QK].astype(F32)
    gkrow = jnp.tile(g_kn[l][:A_NOPE], A_HEADS)[None, :]
    gkr = g_kn[l][None, A_NOPE:]
    wukt = w_uk[l].reshape(KV_LORA, A_HEADS * A_NOPE).T.astype(BF16)
    oa_s = _paged(page_table, cache_latent, jnp.swapaxes(cache_krope, 2, 3), qrow, qrope, q_s3, k_s3,
                  lat_s[:, None, :], wukt, gkrow, gkr, w["w_uv"], l)
    row3 = lambda a: a.reshape(dec_batch, 1, B_FDIM)
    ob_s, st_s = _hstep(state_hgrn, row3(sq), row3(sk), row3(sg), row3(sv), l)
    y_sample = _merge(xs, oa_s.reshape(dec_batch, A_WIDTH), ob_s.reshape(dec_batch, B_WIDTH), w, dec_batch)

    return (y_prompt, y_sample.reshape(dec_batch, dec_seq, D_MODEL),
            lat_all[None], kr_all[None], st_fin[None],
            lat_s.reshape(1, dec_batch, dec_seq, KV_LORA),
            krp_s[:, A_NOPE:A_QK].reshape(1, dec_batch, dec_seq, A_ROPE),
            st_s[None])
```

```python
import functools

import jax
import jax.numpy as jnp
from jax import lax
from jax.experimental import pallas as pl
from jax.experimental.pallas import tpu as pltpu

F32 = jnp.float32
BF16 = jnp.bfloat16

D_MODEL = 1024
N_META = 16
A_HEADS = 8
A_NOPE = 64
A_ROPE = 32
A_QK = A_NOPE + A_ROPE
A_V = 64
A_WIDTH = A_HEADS * A_V
Q_LORA = 384
KV_LORA = 256
ROPE_THETA = 10000.0
B_HEADS = 4
B_DK = 128
B_DV = 128
B_FDIM = B_HEADS * B_DK
B_WIDTH = B_HEADS * B_DV
EPS = 1e-6
PAGE = 128

LANES = 128
LOG2E = 1.4426950408889634
HEAD_PAD = LANES
VMEM_LIMIT = 56 * 1024 * 1024

PRE_TILE = 256
PRE_SUB = 256
ROW_TILE = 1024
ATTN_TQ = 256
ATTN_TK = 256
ATTN_QT = 2
HGRN_CHUNK = 128
HGRN_TILE = 1024
PAGES_PER_GROUP = 32
HSTEP_ROWS = 8
POS_BLOCK = 2 * PAGE
DMA_SLOTS = 4


def _full(shape):
    return pl.BlockSpec(shape, lambda *_: (0,) * len(shape))


def _params(n_axes):
    return pltpu.CompilerParams(dimension_semantics=("arbitrary",) * n_axes,
                                vmem_limit_bytes=VMEM_LIMIT)


def _dot(a, b):
    return jnp.dot(a, b, preferred_element_type=F32)


def _dot_nt(a, b, precision=None):
    return lax.dot_general(a, b, (((1,), (1,)), ((), ())), precision=precision,
                           preferred_element_type=F32)


def _rms(x, g):
    r = lax.rsqrt(jnp.mean(x * x, axis=-1, keepdims=True) + EPS)
    return (x * r) * g


def _rope_lanes(t, c, s1, s2):
    return t * c + pltpu.roll(t, LANES - A_ROPE // 2, axis=1) * s1 + pltpu.roll(t, A_ROPE // 2, axis=1) * s2


def _pre_kernel(x_ref, ng_ref, wa_ref, gcq_ref, wuq_ref, gckv_ref,
                wuk_ref, wuv_ref, gqn_ref, gkn_ref, c_ref, s1_ref, s2_ref, lb_ref,
                lat_ref, krp_ref, q_ref, k_ref, vt_ref, hq_ref, hk_ref, hv_ref, hg_ref, *, layer, qscale, sub):
    lbl = lb_ref[...]
    e = jnp.exp(lbl - jnp.max(lbl, axis=0, keepdims=True))
    lb = jnp.sum(e[:layer + 1], axis=0, keepdims=True) / jnp.sum(e, axis=0, keepdims=True)
    gqn, gkn = gqn_ref[...], gkn_ref[...]
    o1, o2, o3 = Q_LORA, Q_LORA + KV_LORA, Q_LORA + KV_LORA + LANES

    def rows_step(rows):
        x = x_ref[rows, :]
        xn = _rms(x, ng_ref[...]).astype(BF16)
        c, s1, s2 = c_ref[rows, :], s1_ref[rows, :], s2_ref[rows, :]

        y = _dot(xn, wa_ref[:, :o3])
        cqn = _rms(y[:, :o1], gcq_ref[...]).astype(BF16)
        qraw = _dot(cqn, wuq_ref[...])
        for h in range(A_HEADS):
            hs = slice(h * HEAD_PAD, (h + 1) * HEAD_PAD)
            t = _rope_lanes(qraw[:, hs], c, s1, s2)
            r = lax.rsqrt(jnp.sum(t * t, axis=-1, keepdims=True) * (1.0 / A_QK) + EPS)
            q_ref[rows, hs] = (((t * r) * gqn) * qscale).astype(BF16)

        ckv = _rms(y[:, o1:o2], gckv_ref[...])
        lat_ref[rows, :] = ckv
        ckvb = ckv.astype(BF16)
        krp = _rope_lanes(y[:, o2:o3], c, s1, s2)
        krp_ref[rows, :] = krp
        kraw = _dot(ckvb, wuk_ref[...])
        for h in range(A_HEADS):
            hs = slice(h * HEAD_PAD, (h + 1) * HEAD_PAD)
            t = kraw[:, hs] + krp
            r = lax.rsqrt(jnp.sum(t * t, axis=-1, keepdims=True) * (1.0 / A_QK) + EPS)
            k_ref[rows, hs] = ((t * r) * gkn).astype(BF16)
        vt_ref[:, rows] = _dot_nt(wuv_ref[...], ckvb).astype(BF16)

        b = _dot(xn, wa_ref[:, o3:])
        bq, z, bi = b[:, :B_FDIM], b[:, B_FDIM:2 * B_FDIM], b[:, 2 * B_FDIM:]
        hq_ref[rows, :] = bq * jax.nn.sigmoid(bq)
        sz = jax.nn.sigmoid(z)
        hg_ref[rows, :] = jnp.log(lb + (1.0 - lb) * sz)
        hk_ref[rows, :] = (1.0 - lb) * (1.0 - sz)
        hv_ref[rows, :] = bi

    tm = x_ref.shape[0]
    if tm == sub:
        rows_step(slice(None))
    else:
        def body(r, _):
            rows_step(pl.ds(pl.multiple_of(r * sub, sub), sub))
            return 0
        lax.fori_loop(0, tm // sub, body, 0)


def _pre(x, tabs, w, tm, tab_blocks, layer, qscale=1.0, sub=None):
    rows = x.shape[0]
    grid = (rows // tm,)
    row = lambda n: pl.BlockSpec((tm, n), lambda i: (i, 0))
    tab = pl.BlockSpec((tm, LANES), lambda i: (i % tab_blocks, 0))
    ins = [x, w["norm_g"], w["wa"], w["g_cq"], w["w_uq"], w["g_ckv"],
           w["w_uk"], w["w_uvt"], w["g_qn"], w["g_kn"], tabs[0], tabs[1], tabs[2], w["lb_logits"]]
    in_specs = [row(D_MODEL)] + [_full(a.shape) for a in ins[1:10]] + [tab, tab, tab] + [_full(ins[13].shape)]
    outs = [(KV_LORA, F32), (LANES, F32), (A_HEADS * HEAD_PAD, BF16), (A_HEADS * HEAD_PAD, BF16),
            None, (B_FDIM, F32), (B_FDIM, F32), (B_WIDTH, F32), (B_FDIM, F32)]
    out_specs = [pl.BlockSpec((A_WIDTH, tm), lambda i: (0, i)) if o is None else row(o[0]) for o in outs]
    out_shape = [jax.ShapeDtypeStruct((A_WIDTH, rows), BF16) if o is None else
                 jax.ShapeDtypeStruct((rows, o[0]), o[1]) for o in outs]
    return pl.pallas_call(
        functools.partial(_pre_kernel, layer=layer, qscale=qscale, sub=sub or tm),
        grid=grid, in_specs=in_specs, out_specs=out_specs, out_shape=out_shape,
        compiler_params=_params(1), name="pre_proj")(*ins)


def _attn_kernel(q_ref, k_ref, vt_ref, km_ref, vtm_ref, o_ref, ot_sc, s_sc, sm_sc, m_sc, l_sc):
    def tile(qt, _):
        i = pl.program_id(1) * ATTN_QT + qt
        qrows = pl.ds(pl.multiple_of(qt * ATTN_TQ, ATTN_TQ), ATTN_TQ)
        _attn_tile(i, qrows, q_ref, k_ref, vt_ref, km_ref, vtm_ref, o_ref, ot_sc, s_sc, sm_sc, m_sc, l_sc)
        return 0

    lax.fori_loop(0, ATTN_QT, tile, 0)


def _attn_tile(i, qrows, q_ref, k_ref, vt_ref, km_ref, vtm_ref, o_ref, ot_sc, s_sc, sm_sc, m_sc, l_sc):
    tq, tk = ATTN_TQ, ATTN_TK
    assert tq == tk
    key = lax.broadcasted_iota(jnp.int32, (tk, tq), 0)
    qry = lax.broadcasted_iota(jnp.int32, (tk, tq), 1)
    hsl = [slice(h * HEAD_PAD, (h + 1) * HEAD_PAD) for h in range(A_HEADS)]
    vsl = [slice(h * A_V, (h + 1) * A_V) for h in range(A_HEADS)]

    d0 = pl.multiple_of(i * tq, tq)
    for h in range(A_HEADS):
        s_sc[h, 0:tk, :] = _dot_nt(k_ref[pl.ds(d0, tk), hsl[h]], q_ref[qrows, hsl[h]])
        sm_sc[h] = _dot_nt(km_ref[:, hsl[h]], q_ref[qrows, hsl[h]])
    for h in range(A_HEADS):
        hr = slice(h, h + 1)
        s = jnp.where(key <= qry, s_sc[h, 0:tk, :], -jnp.inf)
        sm = sm_sc[h]
        m_new = jnp.maximum(jnp.max(s, axis=0, keepdims=True), jnp.max(sm, axis=0, keepdims=True))
        p = jnp.exp2(s - m_new)
        pm = jnp.exp2(sm - m_new)
        m_sc[hr, :] = m_new
        l_sc[hr, :] = jnp.sum(p, axis=0, keepdims=True) + jnp.sum(pm, axis=0, keepdims=True)
        ot_sc[vsl[h], :] = (_dot(vt_ref[vsl[h], pl.ds(d0, tk)], p.astype(BF16)) +
                            _dot(vtm_ref[vsl[h], :], pm.astype(BF16)))

    def visible(r0, keys):
        for h in range(A_HEADS):
            s_sc[h, 0:keys, :] = _dot_nt(k_ref[pl.ds(r0, keys), hsl[h]], q_ref[qrows, hsl[h]])
        for h in range(A_HEADS):
            hr = slice(h, h + 1)
            s = s_sc[h, 0:keys, :]
            m_new = jnp.maximum(m_sc[hr, :], jnp.max(s, axis=0, keepdims=True))
            a = jnp.exp2(m_sc[hr, :] - m_new)
            p = jnp.exp2(s - m_new)
            m_sc[hr, :] = m_new
            l_sc[hr, :] = l_sc[hr, :] * a + jnp.sum(p, axis=0, keepdims=True)
            ot_sc[vsl[h], :] = ot_sc[vsl[h], :] * a + _dot(vt_ref[vsl[h], pl.ds(r0, keys)], p.astype(BF16))

    def body(kb, _):
        visible(pl.multiple_of(kb * 4 * tk, 4 * tk), 4 * tk)
        return 0

    lax.fori_loop(0, i // 4, body, 0)

    @pl.when(i % 4 >= 2)
    def _():
        visible(pl.multiple_of((i // 4) * 4 * tk, 2 * tk), 2 * tk)

    @pl.when(i % 2 == 1)
    def _():
        visible(pl.multiple_of((i - 1) * tk, tk), tk)

    for h in range(A_HEADS):
        ot_sc[vsl[h], :] = ot_sc[vsl[h], :] / l_sc[h:h + 1, :]
    o_ref[qrows, :] = ot_sc[...].T


def _attn(q, k, vt, km, vtm, batch, seq):
    rows = ATTN_QT * ATTN_TQ
    nq = seq // rows
    return pl.pallas_call(
        _attn_kernel, grid=(batch, nq),
        in_specs=[pl.BlockSpec((rows, A_HEADS * HEAD_PAD), lambda b, i: (b * nq + i, 0)),
                  pl.BlockSpec((seq, A_HEADS * HEAD_PAD), lambda b, i: (b, 0)),
                  pl.BlockSpec((A_WIDTH, seq), lambda b, i: (0, b)),
                  _full(km.shape), _full(vtm.shape)],
        out_specs=pl.BlockSpec((rows, A_WIDTH), lambda b, i: (b * nq + i, 0)),
        out_shape=jax.ShapeDtypeStruct((batch * seq, A_WIDTH), F32),
        scratch_shapes=[pltpu.VMEM((A_WIDTH, ATTN_TQ), F32), pltpu.VMEM((A_HEADS, 4 * ATTN_TK, ATTN_TQ), F32),
                        pltpu.VMEM((A_HEADS, km.shape[0], ATTN_TQ), F32),
                        pltpu.VMEM((A_HEADS, ATTN_TQ), F32), pltpu.VMEM((A_HEADS, ATTN_TQ), F32)],
        compiler_params=_params(2), name="prompt_attn")(q, k, vt, km, vtm)


def _split3(x):
    a = x.astype(BF16)
    r = x - a.astype(F32)
    b = r.astype(BF16)
    c = (r - b.astype(F32)).astype(BF16)
    return a, b, c


def _hgrn_chunk(q, kk, v, g, st, tril, chunk, cum_ref, krow):
    g1, g2, g3 = _split3(g)
    cum = (_dot(tril, g1) + _dot(tril, g2) + _dot(tril, g3)) * LOG2E
    cum_ref[...] = cum
    yield None
    nblk = chunk // 8
    cb = [cum[8 * j:8 * j + 8] for j in range(nblk)]
    qb = [q[8 * j:8 * j + 8] for j in range(nblk)]
    lane = lax.broadcasted_iota(jnp.int32, (8, chunk), 1)
    sub = lax.broadcasted_iota(jnp.int32, (8, chunk), 0)
    ab = []
    for j in range(nblk):
        a = jnp.zeros((8, chunk), F32)
        for s in range(8 * j, 8 * j + 8):
            col = jnp.sum(jnp.exp2(cb[j] - cum_ref[s:s + 1, :]) * qb[j] * krow(s), axis=-1, keepdims=True)
            a = jnp.where(lane == s, col, a)
        ab.append(jnp.where(lane <= sub + 8 * j, a, 0.0))
        yield None
    attn = jnp.concatenate(ab, axis=0)
    row = lax.broadcasted_iota(jnp.int32, (chunk, chunk), 0)
    col = lax.broadcasted_iota(jnp.int32, (chunk, chunk), 1)
    kb = [kk[8 * j:8 * j + 8] for j in range(nblk)]
    dead = jnp.zeros((8, B_DK), F32)
    w = 8
    while w < chunk:
        qs, ks = [], []
        for j in range(nblk):
            blk = (8 * j) // w
            if blk % 2:
                qs.append(qb[j] * jnp.exp2(cb[j] - cum_ref[blk * w - 1:blk * w, :]))
                ks.append(dead)
            else:
                qs.append(dead)
                ks.append(kb[j] * jnp.exp2(cum_ref[blk * w + w - 1:blk * w + w, :] - cb[j]))
        qa = jnp.concatenate(qs, axis=0).astype(BF16)
        ka = jnp.concatenate(ks, axis=0).astype(BF16)
        sh = w.bit_length() - 1
        rb, cbk = row >> sh, col >> sh
        attn = attn + jnp.where((rb == cbk + 1) & ((cbk & 1) == 0), _dot_nt(qa, ka), 0.0)
        w *= 2
        yield None
    last = cum_ref[chunk - 1:chunk, :]
    o = _dot(attn.astype(BF16), v.astype(BF16)) + _dot_nt((q * jnp.exp2(cum)).astype(BF16), st.astype(BF16))
    kdec = (kk * jnp.exp2(last - cum)).astype(BF16)
    upd = lax.dot_general(v.astype(BF16), kdec, (((0,), (0,)), ((), ())), preferred_element_type=F32)
    yield o, jnp.exp2(last) * st + upd


def _hgrn_kernel(q_ref, k_ref, v_ref, g_ref, st0_ref, o_ref, st_ref, st_sc, cum_sc, kk_sc, *, chunk, tile,
                 transpose_out):
    t = pl.program_id(1)

    @pl.when(t == 0)
    def _():
        st_sc[...] = st0_ref[...]

    r = lax.broadcasted_iota(jnp.int32, (chunk, chunk), 0)
    c = lax.broadcasted_iota(jnp.int32, (chunk, chunk), 1)
    tril = (c <= r).astype(BF16)

    def body(ci, _):
        r0 = pl.multiple_of(ci * chunk, chunk)
        hsl = [slice(h * B_DK, (h + 1) * B_DK) for h in range(B_HEADS)]
        stages = {}
        for h in range(B_HEADS):
            kk = k_ref[pl.ds(r0, chunk), hsl[h]]
            kk_sc[h] = kk
            krow = lambda s, h=h: kk_sc[h, s:s + 1, :]
            stages[h] = _hgrn_chunk(q_ref[pl.ds(r0, chunk), hsl[h]], kk,
                                    v_ref[pl.ds(r0, chunk), hsl[h]], g_ref[pl.ds(r0, chunk), hsl[h]],
                                    st_sc[h], tril, chunk, cum_sc.at[h], krow)
        while stages:
            for h in list(stages):
                out = next(stages[h])
                if out is not None:
                    o_ref[pl.ds(r0, chunk), hsl[h]] = out[0]
                    st_sc[h] = out[1]
                    del stages[h]
        return 0

    lax.fori_loop(0, tile // chunk, body, 0, unroll=4 if tile // chunk > 1 else 1)

    @pl.when(t == pl.num_programs(1) - 1)
    def _():
        for h in range(B_HEADS):
            st_ref[0, h] = st_sc[h].T if transpose_out else st_sc[h]


def _hgrn(hq, hk, hv, hg, st0, batch, seq, chunk, tile, transpose_out):
    nt = seq // tile
    row = pl.BlockSpec((tile, B_FDIM), lambda b, t: (b * nt + t, 0))
    return pl.pallas_call(
        functools.partial(_hgrn_kernel, chunk=chunk, tile=tile, transpose_out=transpose_out),
        grid=(batch, nt),
        in_specs=[row, row, row, row, _full(st0.shape)],
        out_specs=[row, pl.BlockSpec((1, B_HEADS, B_DV, B_DK), lambda b, t: (b, 0, 0, 0))],
        out_shape=[jax.ShapeDtypeStruct((batch * seq, B_WIDTH), F32),
                   jax.ShapeDtypeStruct((batch, B_HEADS, B_DV, B_DK), F32)],
        scratch_shapes=[pltpu.VMEM((B_HEADS, B_DV, B_DK), F32), pltpu.VMEM((B_HEADS, chunk, B_DK), F32),
                        pltpu.VMEM((B_HEADS, chunk, B_DK), F32)],
        compiler_params=_params(2), name="hgrn_chunks")(hq, hk, hv, hg, st0)


def _split2(x):
    hi = x.astype(BF16)
    return hi, (x - hi.astype(F32)).astype(BF16)


def _paged_kernel(pt_ref, lat_hbm, kr_hbm, qrow_ref, qr_ref, qh_ref, kh_ref, latn_ref, wukt_ref,
                  gkrow_ref, gkr_ref, wuv_ref, o_ref, lhs, latbuf, krbuf, cbuf, kx_sc, sems, *, layer, n_pages):
    b = pl.program_id(0)
    gp = PAGES_PER_GROUP
    n_groups = n_pages // gp
    gpos = gp * PAGE
    nk = A_HEADS * A_NOPE
    log2e = 1.4426950408889634
    sc2 = (A_QK ** -0.5) * log2e

    def page_copies(bb, g, slot):
        out = []
        for j in range(gp):
            page = pt_ref[bb, g * gp + j]
            dst = pl.ds(j * PAGE, PAGE)
            out.append(pltpu.make_async_copy(lat_hbm.at[layer, page], latbuf.at[slot, dst], sems.at[0, slot]))
            out.append(pltpu.make_async_copy(kr_hbm.at[layer, page], krbuf.at[slot, j], sems.at[1, slot]))
        return out

    def start(bb, g, slot):
        for cp in page_copies(bb, g, slot):
            cp.start()

    def wait(g, slot):
        for cp in page_copies(b, g, slot):
            cp.wait()

    ring = DMA_SLOTS
    ahead = ring - 1
    assert n_groups % ring == 0 and ahead <= n_groups

    def fetch_ahead(g):
        gg = g + ahead
        wrap = gg >= n_groups
        bb = jnp.where(wrap, b + 1, b)
        g2 = jnp.where(wrap, gg - n_groups, gg)

        @pl.when(bb < pl.num_programs(0))
        def _():
            start(bb, g2, lax.rem(gg, ring))

    @pl.when(b == 0)
    def _():
        for d in range(ahead):
            start(0, d, d)

    @pl.when(b == 0)
    def _():
        lhs[0:nk, :] = wukt_ref[...]

    hrow = lax.broadcasted_iota(jnp.int32, (A_HEADS, nk), 0)
    hcol = lax.broadcasted_iota(jnp.int32, (A_HEADS, nk), 1) // A_NOPE
    qsel = jnp.where(hrow == hcol, qrow_ref[0] * gkrow_ref[...], 0.0)
    q_hi, q_lo = _split2(qsel)
    qabs = _dot(q_hi, wukt_ref[...]) + _dot(q_lo, wukt_ref[...])
    lhs[nk:nk + 2 * A_HEADS, :] = jnp.concatenate(_split2(qabs), axis=0)
    qrg = jnp.concatenate(_split2(qr_ref[0] * gkr_ref[...]), axis=0)
    ppb = POS_BLOCK // PAGE

    def scores(slot, cslot, between=None):
        parts = []
        nblk = gpos // POS_BLOCK

        def project(j):
            rows = pl.ds(j * POS_BLOCK, POS_BLOCK)
            cb = latbuf[slot, rows, :].astype(BF16)
            cbuf[cslot, rows, :] = cb
            kx_sc[j] = _dot_nt(lhs[...], cb)

        for j in range(nblk):
            project(j)
        if between is not None:
            between()
        for j in range(nblk):
            kn = kx_sc[j, 0:nk, :]
            ss = jnp.sum((kn * kn).reshape(A_HEADS, A_NOPE, POS_BLOCK), axis=1)
            raw = kx_sc[j, nk:nk + A_HEADS, :] + kx_sc[j, nk + A_HEADS:nk + 2 * A_HEADS, :]
            krt = jnp.concatenate([krbuf[slot, j * ppb + t] for t in range(ppb)], axis=-1)
            rr = _dot(qrg, krt.astype(BF16))
            raw = raw + rr[:A_HEADS] + rr[A_HEADS:]
            ss = ss + jnp.sum(krt * krt, axis=0, keepdims=True)
            parts.append(raw * lax.rsqrt(ss * (1.0 / A_QK) + EPS) * sc2)
        return jnp.concatenate(parts, axis=-1)

    def absorb(s, slot, carry):
        m, l, acc = carry
        m_new = jnp.maximum(m, jnp.max(s, axis=-1, keepdims=True))
        a = jnp.exp2(m - m_new)
        p = jnp.exp2(s - m_new)
        l = l * a + jnp.sum(p, axis=-1, keepdims=True)
        acc = acc * a + _dot(p.astype(BF16), cbuf[slot])
        return m_new, l, acc

    wait(0, 0)
    fetch_ahead(0)
    s0 = scores(0, 0)
    init = (jnp.full((A_HEADS, 1), -jnp.inf, F32), jnp.zeros((A_HEADS, 1), F32),
            jnp.zeros((A_HEADS, KV_LORA), F32))

    def group(g, carry):
        s_prev, state = carry
        cslot = lax.rem(g, 2)
        wait(g, lax.rem(g, ring))
        fetch_ahead(g)
        box = []
        s_cur = scores(lax.rem(g, ring), cslot, lambda: box.append(absorb(s_prev, 1 - cslot, state)))
        return s_cur, box[0]

    s_last, state = lax.fori_loop(1, n_groups, group, (s0, init))
    m, l, acc = absorb(s_last, (n_groups - 1) % 2, state)

    s_new = jnp.sum(qh_ref[0].astype(F32) * kh_ref[0].astype(F32), axis=-1, keepdims=True) * sc2
    m_new = jnp.maximum(m, s_new)
    a = jnp.exp2(m - m_new)
    p = jnp.exp2(s_new - m_new)
    acc = acc * a + p * latn_ref[0]
    l = l * a + p
    o_lat = (acc / l).astype(BF16)
    full = _dot(o_lat, wuv_ref[...])
    vrow = lax.broadcasted_iota(jnp.int32, (A_HEADS, A_WIDTH), 0)
    vcol = lax.broadcasted_iota(jnp.int32, (A_HEADS, A_WIDTH), 1) // A_V
    o_ref[0] = jnp.sum(jnp.where(vrow == vcol, full, 0.0), axis=0, keepdims=True)


def _paged(page_table, cache_lat, cache_kr, qrow, qr, qh, kh, latn, wukt, gkrow, gkr, wuv, layer):
    nb, n_pages = page_table.shape
    gp = PAGES_PER_GROUP
    per_b = lambda shape: pl.BlockSpec((1,) + shape, lambda b, pt: (b,) + (0,) * len(shape))
    full = lambda a: pl.BlockSpec(a.shape, lambda b, pt: (0,) * a.ndim)
    grid_spec = pltpu.PrefetchScalarGridSpec(
        num_scalar_prefetch=1, grid=(nb,),
        in_specs=[pl.BlockSpec(memory_space=pl.ANY), pl.BlockSpec(memory_space=pl.ANY),
                  per_b((1, A_HEADS * A_NOPE)), per_b((A_HEADS, A_ROPE)), per_b((A_HEADS, HEAD_PAD)),
                  per_b((A_HEADS, HEAD_PAD)), per_b((1, KV_LORA)), full(wukt), full(gkrow), full(gkr), full(wuv)],
        out_specs=per_b((1, A_WIDTH)),
        scratch_shapes=[pltpu.VMEM((A_HEADS * A_NOPE + 2 * A_HEADS, KV_LORA), BF16),
                        pltpu.VMEM((DMA_SLOTS, gp * PAGE, KV_LORA), F32), pltpu.VMEM((DMA_SLOTS, gp, A_ROPE, PAGE), F32),
                        pltpu.VMEM((2, gp * PAGE, KV_LORA), BF16),
                        pltpu.VMEM((gp * PAGE // POS_BLOCK, A_HEADS * A_NOPE + 2 * A_HEADS, POS_BLOCK), F32),
                        pltpu.SemaphoreType.DMA((2, DMA_SLOTS))])
    return pl.pallas_call(
        functools.partial(_paged_kernel, layer=layer, n_pages=n_pages), grid_spec=grid_spec,
        out_shape=jax.ShapeDtypeStruct((nb, 1, A_WIDTH), F32),
        compiler_params=_params(1), name="paged_attn")(
            page_table, cache_lat, cache_kr, qrow, qr, qh, kh, latn, wukt, gkrow, gkr, wuv)


def _hstep_kernel(s_ref, q_ref, k_ref, g_ref, v_ref, o_ref, sn_ref, *, layer, rows):
    r = lax.broadcasted_iota(jnp.int32, (B_DK, B_DK), 0)
    c = lax.broadcasted_iota(jnp.int32, (B_DK, B_DK), 1)
    eye = r == c

    def column(row):
        return jnp.sum(jnp.where(eye, row, 0.0), axis=-1, keepdims=True)

    for i in range(rows):
        for h in range(B_HEADS):
            hs = slice(h * B_DK, (h + 1) * B_DK)
            sn = column(jnp.exp(g_ref[i, :, hs])) * s_ref[layer, i, h] + column(k_ref[i, :, hs]) * v_ref[i, :, hs]
            sn_ref[i, h] = sn
            o_ref[i, :, hs] = jnp.sum(column(q_ref[i, :, hs]) * sn, axis=0, keepdims=True)


def _hstep(state, q, k, g, v, layer):
    nb = state.shape[1]
    rows = HSTEP_ROWS
    st_in = pl.BlockSpec((state.shape[0], rows, B_HEADS, B_DK, B_DV), lambda b: (0, b, 0, 0, 0))
    st_out = pl.BlockSpec((rows, B_HEADS, B_DK, B_DV), lambda b: (b, 0, 0, 0))
    row = pl.BlockSpec((rows, 1, B_FDIM), lambda b: (b, 0, 0))
    return pl.pallas_call(
        functools.partial(_hstep_kernel, layer=layer, rows=rows), grid=(nb // rows,),
        in_specs=[st_in, row, row, row, row], out_specs=[row, st_out],
        out_shape=[jax.ShapeDtypeStruct((nb, 1, B_WIDTH), F32), jax.ShapeDtypeStruct(state.shape[1:], F32)],
        compiler_params=_params(1), name="hgrn_step")(state, q, k, g, v)


def _merge_kernel(x_ref, ng_ref, wg_ref, oa_ref, ob_ref, gbn_ref, woa_ref, wob_ref, wo_ref, y_ref):
    x = x_ref[...]
    xn = _rms(x, ng_ref[...]).astype(BF16)
    gates = _dot(xn, wg_ref[...])
    ga, gb = gates[:, :A_WIDTH], gates[:, A_WIDTH:A_WIDTH + B_WIDTH]
    ma = gates[:, A_WIDTH + B_WIDTH:A_WIDTH + B_WIDTH + D_MODEL]
    mb = gates[:, A_WIDTH + B_WIDTH + D_MODEL:]
    ya = _dot((oa_ref[...] * (ga * jax.nn.sigmoid(ga))).astype(BF16), woa_ref[...])
    gbn = gbn_ref[...]
    obn = jnp.concatenate([_rms(ob_ref[:, h * B_DV:(h + 1) * B_DV], gbn) for h in range(B_HEADS)], axis=-1)
    yb = _dot((obn * (gb * jax.nn.sigmoid(gb))).astype(BF16), wob_ref[...])
    mix = jax.nn.sigmoid(ma) * ya + jax.nn.sigmoid(mb) * yb
    y_ref[...] = x + _dot(mix.astype(BF16), wo_ref[...])


def _merge(x, oa, ob, w, tm):
    rows = x.shape[0]
    row = lambda n: pl.BlockSpec((tm, n), lambda i: (i, 0))
    ins = [x, w["norm_g"], w["wg"], oa, ob, w["g_bn"], w["w_oa"], w["w_ob"], w["w_o"]]
    in_specs = [row(D_MODEL), _full(ins[1].shape), _full(ins[2].shape), row(A_WIDTH), row(B_WIDTH)] + \
               [_full(a.shape) for a in ins[5:]]
    return pl.pallas_call(
        _merge_kernel, grid=(rows // tm,), in_specs=in_specs, out_specs=row(D_MODEL),
        out_shape=jax.ShapeDtypeStruct((rows, D_MODEL), F32),
        compiler_params=_params(1), name="merge_out")(*ins)


def _head_pad_cols(w3, width):
    pad = jnp.zeros(w3.shape[:2] + (HEAD_PAD - width,), w3.dtype)
    return jnp.concatenate([w3, pad], axis=-1).reshape(w3.shape[0], -1)


def _rope_tables(pos):
    half = A_ROPE // 2
    inv = ROPE_THETA ** (-jnp.arange(half, dtype=F32) / half)
    ang = pos.astype(F32)[:, None] * inv
    cos, sin = jnp.cos(ang), jnp.sin(ang)
    n = pos.shape[0]
    one, zero = jnp.ones((n, A_NOPE), F32), jnp.zeros((n, A_NOPE), F32)
    tail = jnp.zeros((n, HEAD_PAD - A_QK), F32)
    z16 = jnp.zeros((n, half), F32)
    c = jnp.concatenate([one, cos, cos, tail], axis=-1)
    s1 = jnp.concatenate([zero, -sin, z16, tail], axis=-1)
    s2 = jnp.concatenate([zero, z16, sin, tail], axis=-1)
    return c, s1, s2


def _lane_gain(g):
    return jnp.concatenate([g, jnp.zeros((HEAD_PAD - A_QK,), g.dtype)])[None, :]


def kernel(x_prompt, x_sample, cache_latent, cache_krope, state_hgrn, page_table, meta_tokens,
           norm_g, w_in, g_cq, w_uq, g_ckv, w_uk, w_uv, g_qn, g_kn, lb_logits, g_bn, w_oa, w_ob, w_o):
    batch, seq, _ = x_prompt.shape
    dec_batch, dec_seq, _ = x_sample.shape
    depth = w_in.shape[0]
    assert depth == 1 and dec_seq == 1
    past_len = page_table.shape[1] * PAGE
    l = 0

    o0 = Q_LORA
    o1 = o0 + KV_LORA
    o2 = o1 + A_ROPE
    o3 = o2 + 3 * B_FDIM
    wi = w_in[l]
    wkr = jnp.zeros((D_MODEL, HEAD_PAD), F32).at[:, A_NOPE:A_QK].set(wi[:, o1:o2])
    w = {
        "norm_g": norm_g[l][None, :],
        "wa": jnp.concatenate([wi[:, :o1], wkr, wi[:, o2:o3]], axis=1).astype(BF16), "wg": wi[:, o3:].astype(BF16),
        "g_cq": g_cq[l][None, :], "g_ckv": g_ckv[l][None, :],
        "w_uq": _head_pad_cols(w_uq[l].reshape(Q_LORA, A_HEADS, A_QK), A_QK).astype(BF16),
        "w_uk": _head_pad_cols(w_uk[l], A_NOPE).astype(BF16),
        "w_uv": w_uv[l].reshape(KV_LORA, A_WIDTH).astype(BF16),
        "w_uvt": w_uv[l].reshape(KV_LORA, A_WIDTH).T.astype(BF16),
        "g_qn": _lane_gain(g_qn[l]), "g_kn": _lane_gain(g_kn[l]),
        "lb_logits": lb_logits, "g_bn": g_bn[l][None, :],
        "w_oa": w_oa[l].astype(BF16), "w_ob": w_ob[l].astype(BF16), "w_o": w_o[l].astype(BF16),
    }

    xs = x_sample.reshape(dec_batch, D_MODEL)
    tabs_m = _rope_tables(jnp.arange(N_META))
    tabs_s = _rope_tables(jnp.full((dec_batch,), past_len, jnp.int32))
    tabs_ms = tuple(jnp.concatenate([a, b], axis=0) for a, b in zip(tabs_m, tabs_s))
    small = _pre(jnp.concatenate([meta_tokens, xs], axis=0), tabs_ms, w, N_META + dec_batch, 1, l)
    lat_m, krp_m, _, k_m, _, mq, mk, mv, mg = [a[:N_META] for a in small]
    lat_s, krp_s, q_s, k_s, _, sq, sk, sv, sg = [a[N_META:] for a in small]
    vt_m = small[4][:, :N_META]

    xp = x_prompt.reshape(batch * seq, D_MODEL)
    tabs_p = _rope_tables(N_META + jnp.arange(seq))
    qscale = (A_QK ** -0.5) * 1.4426950408889634
    lat_p, krp_p, q_p, k_p, vt_p, hq, hk, hv, hg = _pre(xp, tabs_p, w, PRE_TILE, seq // PRE_TILE, l, qscale, PRE_SUB)
    lat_all = jnp.concatenate([jnp.broadcast_to(lat_m[None], (batch, N_META, KV_LORA)),
                               lat_p.reshape(batch, seq, KV_LORA)], axis=1)
    kr_all = jnp.concatenate([jnp.broadcast_to(krp_m[None, :, A_NOPE:A_QK], (batch, N_META, A_ROPE)),
                              krp_p[:, A_NOPE:A_QK].reshape(batch, seq, A_ROPE)], axis=1)

    oa = _attn(q_p, k_p, vt_p, k_m, vt_m, batch, seq)
    zero_state = jnp.zeros((B_HEADS, B_DV, B_DK), F32)
    _, st_meta = _hgrn(mq, mk, mv, mg, zero_state, 1, N_META, N_META, N_META, False)
    ob, st_fin = _hgrn(hq, hk, hv, hg, st_meta[0], batch, seq, HGRN_CHUNK, HGRN_TILE, True)
    y_prompt = _merge(xp, oa, ob, w, ROW_TILE).reshape(batch, seq, D_MODEL)

    q_s3 = q_s.reshape(dec_batch, A_HEADS, HEAD_PAD)
    k_s3 = k_s.reshape(dec_batch, A_HEADS, HEAD_PAD)
    qrow = q_s3[:, :, :A_NOPE].astype(F32).reshape(dec_batch, 1, A_HEADS * A_NOPE)
    qrope = q_s3[:, :, A_NOPE:A_QK].astype(F32)
    gkrow = jnp.tile(g_kn[l][:A_NOPE], A_HEADS)[None, :]
    gkr = g_kn[l][None, A_NOPE:]
    wukt = w_uk[l].reshape(KV_LORA, A_HEADS * A_NOPE).T.astype(BF16)
    oa_s = _paged(page_table, cache_latent, jnp.swapaxes(cache_krope, 2, 3), qrow, qrope, q_s3, k_s3,
                  lat_s[:, None, :], wukt, gkrow, gkr, w["w_uv"], l)
    row3 = lambda a: a.reshape(dec_batch, 1, B_FDIM)
    ob_s, st_s = _hstep(state_hgrn, row3(sq), row3(sk), row3(sg), row3(sv), l)
    y_sample = _merge(xs, oa_s.reshape(dec_batch, A_WIDTH), ob_s.reshape(dec_batch, B_WIDTH), w, dec_batch)

    return (y_prompt, y_sample.reshape(dec_batch, dec_seq, D_MODEL),
            lat_all[None], kr_all[None], st_fin[None],
            lat_s.reshape(1, dec_batch, dec_seq, KV_LORA),
            krp_s[:, A_NOPE:A_QK].reshape(1, dec_batch, dec_seq, A_ROPE),
            st_s[None])
```

```python
import functools

import jax
import jax.numpy as jnp
from jax import lax
from jax.experimental import pallas as pl
from jax.experimental.pallas import tpu as pltpu

F32 = jnp.float32
BF16 = jnp.bfloat16

D_MODEL = 1024
N_META = 16
A_HEADS = 8
A_NOPE = 64
A_ROPE = 32
A_QK = A_NOPE + A_ROPE
A_V = 64
A_WIDTH = A_HEADS * A_V
Q_LORA = 384
KV_LORA = 256
ROPE_THETA = 10000.0
B_HEADS = 4
B_DK = 128
B_DV = 128
B_FDIM = B_HEADS * B_DK
B_WIDTH = B_HEADS * B_DV
EPS = 1e-6
PAGE = 128

LANES = 128
LOG2E = 1.4426950408889634
HEAD_PAD = LANES
VMEM_LIMIT = 56 * 1024 * 1024

PRE_TILE = 256
PRE_SUB = 256
ROW_TILE = 1024
ATTN_TQ = 256
ATTN_TK = 256
ATTN_QT = 4
HGRN_CHUNK = 128
HGRN_TILE = 1024
HGRN_UNROLL = 4
PAGES_PER_GROUP = 32
HSTEP_ROWS = 16
POS_BLOCK = 2 * PAGE
DMA_SLOTS = 4


def _full(shape):
    return pl.BlockSpec(shape, lambda *_: (0,) * len(shape))


def _params(n_axes):
    return pltpu.CompilerParams(dimension_semantics=("arbitrary",) * n_axes,
                                vmem_limit_bytes=VMEM_LIMIT)


def _dot(a, b):
    return jnp.dot(a, b, preferred_element_type=F32)


def _dot_nt(a, b, precision=None):
    return lax.dot_general(a, b, (((1,), (1,)), ((), ())), precision=precision,
                           preferred_element_type=F32)


def _rms(x, g):
    r = lax.rsqrt(jnp.mean(x * x, axis=-1, keepdims=True) + EPS)
    return (x * r) * g


def _rope_lanes(t, c, s1, s2):
    return t * c + pltpu.roll(t, LANES - A_ROPE // 2, axis=1) * s1 + pltpu.roll(t, A_ROPE // 2, axis=1) * s2


def _pre_kernel(x_ref, ng_ref, wa_ref, gcq_ref, wuq_ref, gckv_ref,
                wuk_ref, wuv_ref, gqn_ref, gkn_ref, c_ref, s1_ref, s2_ref, lb_ref,
                lat_ref, krp_ref, q_ref, k_ref, vt_ref, hq_ref, hk_ref, hv_ref, hg_ref, *, layer, qscale, sub):
    lbl = lb_ref[...]
    e = jnp.exp(lbl - jnp.max(lbl, axis=0, keepdims=True))
    lb = jnp.sum(e[:layer + 1], axis=0, keepdims=True) / jnp.sum(e, axis=0, keepdims=True)
    gqn, gkn = gqn_ref[...], gkn_ref[...]
    o1, o2, o3 = Q_LORA, Q_LORA + KV_LORA, Q_LORA + KV_LORA + LANES

    def rows_step(rows):
        x = x_ref[rows, :]
        xn = _rms(x, ng_ref[...]).astype(BF16)
        c, s1, s2 = c_ref[rows, :], s1_ref[rows, :], s2_ref[rows, :]

        y = _dot(xn, wa_ref[:, :o3])
        cqn = _rms(y[:, :o1], gcq_ref[...]).astype(BF16)
        qraw = _dot(cqn, wuq_ref[...])
        for h in range(A_HEADS):
            hs = slice(h * HEAD_PAD, (h + 1) * HEAD_PAD)
            t = _rope_lanes(qraw[:, hs], c, s1, s2)
            r = lax.rsqrt(jnp.sum(t * t, axis=-1, keepdims=True) * (1.0 / A_QK) + EPS)
            q_ref[rows, hs] = (((t * r) * gqn) * qscale).astype(BF16)

        ckv = _rms(y[:, o1:o2], gckv_ref[...])
        lat_ref[rows, :] = ckv
        ckvb = ckv.astype(BF16)
        krp = _rope_lanes(y[:, o2:o3], c, s1, s2)
        krp_ref[rows, :] = krp
        kraw = _dot(ckvb, wuk_ref[...])
        for h in range(A_HEADS):
            hs = slice(h * HEAD_PAD, (h + 1) * HEAD_PAD)
            t = kraw[:, hs] + krp
            r = lax.rsqrt(jnp.sum(t * t, axis=-1, keepdims=True) * (1.0 / A_QK) + EPS)
            k_ref[rows, hs] = ((t * r) * gkn).astype(BF16)
        vt_ref[:, rows] = _dot_nt(wuv_ref[...], ckvb).astype(BF16)

        b = _dot(xn, wa_ref[:, o3:])
        bq, z, bi = b[:, :B_FDIM], b[:, B_FDIM:2 * B_FDIM], b[:, 2 * B_FDIM:]
        hq_ref[rows, :] = bq * jax.nn.sigmoid(bq)
        sz = jax.nn.sigmoid(z)
        hg_ref[rows, :] = jnp.log(lb + (1.0 - lb) * sz)
        hk_ref[rows, :] = (1.0 - lb) * (1.0 - sz)
        hv_ref[rows, :] = bi

    tm = x_ref.shape[0]
    if tm == sub:
        rows_step(slice(None))
    else:
        def body(r, _):
            rows_step(pl.ds(pl.multiple_of(r * sub, sub), sub))
            return 0
        lax.fori_loop(0, tm // sub, body, 0)


def _pre(x, tabs, w, tm, tab_blocks, layer, qscale=1.0, sub=None):
    rows = x.shape[0]
    grid = (rows // tm,)
    row = lambda n: pl.BlockSpec((tm, n), lambda i: (i, 0))
    tab = pl.BlockSpec((tm, LANES), lambda i: (i % tab_blocks, 0))
    ins = [x, w["norm_g"], w["wa"], w["g_cq"], w["w_uq"], w["g_ckv"],
           w["w_uk"], w["w_uvt"], w["g_qn"], w["g_kn"], tabs[0], tabs[1], tabs[2], w["lb_logits"]]
    in_specs = [row(D_MODEL)] + [_full(a.shape) for a in ins[1:10]] + [tab, tab, tab] + [_full(ins[13].shape)]
    outs = [(KV_LORA, F32), (LANES, F32), (A_HEADS * HEAD_PAD, BF16), (A_HEADS * HEAD_PAD, BF16),
            None, (B_FDIM, F32), (B_FDIM, F32), (B_WIDTH, F32), (B_FDIM, F32)]
    out_specs = [pl.BlockSpec((A_WIDTH, tm), lambda i: (0, i)) if o is None else row(o[0]) for o in outs]
    out_shape = [jax.ShapeDtypeStruct((A_WIDTH, rows), BF16) if o is None else
                 jax.ShapeDtypeStruct((rows, o[0]), o[1]) for o in outs]
    return pl.pallas_call(
        functools.partial(_pre_kernel, layer=layer, qscale=qscale, sub=sub or tm),
        grid=grid, in_specs=in_specs, out_specs=out_specs, out_shape=out_shape,
        compiler_params=_params(1), name="pre_proj")(*ins)


def _attn_kernel(q_ref, k_ref, vt_ref, km_ref, vtm_ref, o_ref, ot_sc, s_sc, sm_sc, m_sc, l_sc):
    def tile(qt, _):
        i = pl.program_id(1) * ATTN_QT + qt
        qrows = pl.ds(pl.multiple_of(qt * ATTN_TQ, ATTN_TQ), ATTN_TQ)
        _attn_tile(i, qrows, q_ref, k_ref, vt_ref, km_ref, vtm_ref, o_ref, ot_sc, s_sc, sm_sc, m_sc, l_sc)
        return 0

    lax.fori_loop(0, ATTN_QT, tile, 0)


def _attn_tile(i, qrows, q_ref, k_ref, vt_ref, km_ref, vtm_ref, o_ref, ot_sc, s_sc, sm_sc, m_sc, l_sc):
    tq, tk = ATTN_TQ, ATTN_TK
    assert tq == tk
    key = lax.broadcasted_iota(jnp.int32, (tk, tq), 0)
    qry = lax.broadcasted_iota(jnp.int32, (tk, tq), 1)
    hsl = [slice(h * HEAD_PAD, (h + 1) * HEAD_PAD) for h in range(A_HEADS)]
    vsl = [slice(h * A_V, (h + 1) * A_V) for h in range(A_HEADS)]

    d0 = pl.multiple_of(i * tq, tq)
    for h in range(A_HEADS):
        s_sc[h, 0:tk, :] = _dot_nt(k_ref[pl.ds(d0, tk), hsl[h]], q_ref[qrows, hsl[h]])
        sm_sc[h] = _dot_nt(km_ref[:, hsl[h]], q_ref[qrows, hsl[h]])
    for h in range(A_HEADS):
        hr = slice(h, h + 1)
        s = jnp.where(key <= qry, s_sc[h, 0:tk, :], -jnp.inf)
        sm = sm_sc[h]
        m_new = jnp.maximum(jnp.max(s, axis=0, keepdims=True), jnp.max(sm, axis=0, keepdims=True))
        p = jnp.exp2(s - m_new)
        pm = jnp.exp2(sm - m_new)
        m_sc[hr, :] = m_new
        l_sc[hr, :] = jnp.sum(p, axis=0, keepdims=True) + jnp.sum(pm, axis=0, keepdims=True)
        ot_sc[vsl[h], :] = (_dot(vt_ref[vsl[h], pl.ds(d0, tk)], p.astype(BF16)) +
                            _dot(vtm_ref[vsl[h], :], pm.astype(BF16)))

    def visible(r0, keys):
        for h in range(A_HEADS):
            s_sc[h, 0:keys, :] = _dot_nt(k_ref[pl.ds(r0, keys), hsl[h]], q_ref[qrows, hsl[h]])
        for h in range(A_HEADS):
            hr = slice(h, h + 1)
            s = s_sc[h, 0:keys, :]
            m_new = jnp.maximum(m_sc[hr, :], jnp.max(s, axis=0, keepdims=True))
            a = jnp.exp2(m_sc[hr, :] - m_new)
            p = jnp.exp2(s - m_new)
            m_sc[hr, :] = m_new
            l_sc[hr, :] = l_sc[hr, :] * a + jnp.sum(p, axis=0, keepdims=True)
            ot_sc[vsl[h], :] = ot_sc[vsl[h], :] * a + _dot(vt_ref[vsl[h], pl.ds(r0, keys)], p.astype(BF16))

    def body(kb, _):
        visible(pl.multiple_of(kb * 4 * tk, 4 * tk), 4 * tk)
        return 0

    lax.fori_loop(0, i // 4, body, 0)

    @pl.when(i % 4 >= 2)
    def _():
        visible(pl.multiple_of((i // 4) * 4 * tk, 2 * tk), 2 * tk)

    @pl.when(i % 2 == 1)
    def _():
        visible(pl.multiple_of((i - 1) * tk, tk), tk)

    for h in range(A_HEADS):
        ot_sc[vsl[h], :] = ot_sc[vsl[h], :] / l_sc[h:h + 1, :]
    o_ref[qrows, :] = ot_sc[...].T


def _attn(q, k, vt, km, vtm, batch, seq):
    rows = ATTN_QT * ATTN_TQ
    nq = seq // rows
    return pl.pallas_call(
        _attn_kernel, grid=(batch, nq),
        in_specs=[pl.BlockSpec((rows, A_HEADS * HEAD_PAD), lambda b, i: (b * nq + i, 0)),
                  pl.BlockSpec((seq, A_HEADS * HEAD_PAD), lambda b, i: (b, 0)),
                  pl.BlockSpec((A_WIDTH, seq), lambda b, i: (0, b)),
                  _full(km.shape), _full(vtm.shape)],
        out_specs=pl.BlockSpec((rows, A_WIDTH), lambda b, i: (b * nq + i, 0)),
        out_shape=jax.ShapeDtypeStruct((batch * seq, A_WIDTH), F32),
        scratch_shapes=[pltpu.VMEM((A_WIDTH, ATTN_TQ), F32), pltpu.VMEM((A_HEADS, 4 * ATTN_TK, ATTN_TQ), F32),
                        pltpu.VMEM((A_HEADS, km.shape[0], ATTN_TQ), F32),
                        pltpu.VMEM((A_HEADS, ATTN_TQ), F32), pltpu.VMEM((A_HEADS, ATTN_TQ), F32)],
        compiler_params=_params(2), name="prompt_attn")(q, k, vt, km, vtm)


def _split3(x):
    a = x.astype(BF16)
    r = x - a.astype(F32)
    b = r.astype(BF16)
    c = (r - b.astype(F32)).astype(BF16)
    return a, b, c


def _hgrn_chunk(q, kk, v, g, st, tril, chunk, cum_ref, krow):
    g1, g2, g3 = _split3(g)
    cum = (_dot(tril, g1) + _dot(tril, g2) + _dot(tril, g3)) * LOG2E
    cum_ref[...] = cum
    yield None
    nblk = chunk // 8
    cb = [cum[8 * j:8 * j + 8] for j in range(nblk)]
    qb = [q[8 * j:8 * j + 8] for j in range(nblk)]
    lane = lax.broadcasted_iota(jnp.int32, (8, chunk), 1)
    sub = lax.broadcasted_iota(jnp.int32, (8, chunk), 0)
    ab = []
    for j in range(nblk):
        a = jnp.zeros((8, chunk), F32)
        for s in range(8 * j, 8 * j + 8):
            col = jnp.sum(jnp.exp2(cb[j] - cum_ref[s:s + 1, :]) * qb[j] * krow(s), axis=-1, keepdims=True)
            a = jnp.where(lane == s, col, a)
        ab.append(jnp.where(lane <= sub + 8 * j, a, 0.0))
        yield None
    attn = jnp.concatenate(ab, axis=0)
    row = lax.broadcasted_iota(jnp.int32, (chunk, chunk), 0)
    col = lax.broadcasted_iota(jnp.int32, (chunk, chunk), 1)
    kb = [kk[8 * j:8 * j + 8] for j in range(nblk)]
    dead = jnp.zeros((8, B_DK), F32)
    w = 8
    while w < chunk:
        qs, ks = [], []
        for j in range(nblk):
            blk = (8 * j) // w
            if blk % 2:
                qs.append(qb[j] * jnp.exp2(cb[j] - cum_ref[blk * w - 1:blk * w, :]))
                ks.append(dead)
            else:
                qs.append(dead)
                ks.append(kb[j] * jnp.exp2(cum_ref[blk * w + w - 1:blk * w + w, :] - cb[j]))
        qa = jnp.concatenate(qs, axis=0).astype(BF16)
        ka = jnp.concatenate(ks, axis=0).astype(BF16)
        sh = w.bit_length() - 1
        rb, cbk = row >> sh, col >> sh
        attn = attn + jnp.where((rb == cbk + 1) & ((cbk & 1) == 0), _dot_nt(qa, ka), 0.0)
        w *= 2
        yield None
    last = cum_ref[chunk - 1:chunk, :]
    o = _dot(attn.astype(BF16), v.astype(BF16)) + _dot_nt((q * jnp.exp2(cum)).astype(BF16), st.astype(BF16))
    kdec = (kk * jnp.exp2(last - cum)).astype(BF16)
    upd = lax.dot_general(v.astype(BF16), kdec, (((0,), (0,)), ((), ())), preferred_element_type=F32)
    yield o, jnp.exp2(last) * st + upd


def _hgrn_kernel(q_ref, k_ref, v_ref, g_ref, st0_ref, o_ref, st_ref, st_sc, cum_sc, kk_sc, *, chunk, tile,
                 transpose_out):
    t = pl.program_id(1)

    @pl.when(t == 0)
    def _():
        st_sc[...] = st0_ref[...]

    r = lax.broadcasted_iota(jnp.int32, (chunk, chunk), 0)
    c = lax.broadcasted_iota(jnp.int32, (chunk, chunk), 1)
    tril = (c <= r).astype(BF16)

    def body(ci, _):
        r0 = pl.multiple_of(ci * chunk, chunk)
        hsl = [slice(h * B_DK, (h + 1) * B_DK) for h in range(B_HEADS)]
        stages = {}
        for h in range(B_HEADS):
            kk = k_ref[pl.ds(r0, chunk), hsl[h]]
            kk_sc[h] = kk
            krow = lambda s, h=h: kk_sc[h, s:s + 1, :]
            stages[h] = _hgrn_chunk(q_ref[pl.ds(r0, chunk), hsl[h]], kk,
                                    v_ref[pl.ds(r0, chunk), hsl[h]], g_ref[pl.ds(r0, chunk), hsl[h]],
                                    st_sc[h], tril, chunk, cum_sc.at[h], krow)
        while stages:
            for h in list(stages):
                out = next(stages[h])
                if out is not None:
                    o_ref[pl.ds(r0, chunk), hsl[h]] = out[0]
                    st_sc[h] = out[1]
                    del stages[h]
        return 0

    lax.fori_loop(0, tile // chunk, body, 0, unroll=min(HGRN_UNROLL, tile // chunk))

    @pl.when(t == pl.num_programs(1) - 1)
    def _():
        for h in range(B_HEADS):
            st_ref[0, h] = st_sc[h].T if transpose_out else st_sc[h]


def _hgrn(hq, hk, hv, hg, st0, batch, seq, chunk, tile, transpose_out):
    nt = seq // tile
    row = pl.BlockSpec((tile, B_FDIM), lambda b, t: (b * nt + t, 0))
    return pl.pallas_call(
        functools.partial(_hgrn_kernel, chunk=chunk, tile=tile, transpose_out=transpose_out),
        grid=(batch, nt),
        in_specs=[row, row, row, row, _full(st0.shape)],
        out_specs=[row, pl.BlockSpec((1, B_HEADS, B_DV, B_DK), lambda b, t: (b, 0, 0, 0))],
        out_shape=[jax.ShapeDtypeStruct((batch * seq, B_WIDTH), F32),
                   jax.ShapeDtypeStruct((batch, B_HEADS, B_DV, B_DK), F32)],
        scratch_shapes=[pltpu.VMEM((B_HEADS, B_DV, B_DK), F32), pltpu.VMEM((B_HEADS, chunk, B_DK), F32),
                        pltpu.VMEM((B_HEADS, chunk, B_DK), F32)],
        compiler_params=_params(2), name="hgrn_chunks")(hq, hk, hv, hg, st0)


def _split2(x):
    hi = x.astype(BF16)
    return hi, (x - hi.astype(F32)).astype(BF16)


def _paged_kernel(pt_ref, lat_hbm, kr_hbm, qrow_ref, qr_ref, qh_ref, kh_ref, latn_ref, wukt_ref,
                  gkrow_ref, gkr_ref, wuv_ref, o_ref, lhs, latbuf, krbuf, cbuf, kx_sc, sems, *, layer, n_pages):
    b = pl.program_id(0)
    gp = PAGES_PER_GROUP
    n_groups = n_pages // gp
    gpos = gp * PAGE
    nk = A_HEADS * A_NOPE
    log2e = 1.4426950408889634
    sc2 = (A_QK ** -0.5) * log2e

    def page_copies(bb, g, slot):
        out = []
        for j in range(gp):
            page = pt_ref[bb, g * gp + j]
            dst = pl.ds(j * PAGE, PAGE)
            out.append(pltpu.make_async_copy(lat_hbm.at[layer, page], latbuf.at[slot, dst], sems.at[0, slot]))
            out.append(pltpu.make_async_copy(kr_hbm.at[layer, page], krbuf.at[slot, j], sems.at[1, slot]))
        return out

    def start(bb, g, slot):
        for cp in page_copies(bb, g, slot):
            cp.start()

    def wait(g, slot):
        for cp in page_copies(b, g, slot):
            cp.wait()

    ring = DMA_SLOTS
    ahead = ring - 1
    assert n_groups % ring == 0 and ahead <= n_groups

    def fetch_ahead(g):
        gg = g + ahead
        wrap = gg >= n_groups
        bb = jnp.where(wrap, b + 1, b)
        g2 = jnp.where(wrap, gg - n_groups, gg)

        @pl.when(bb < pl.num_programs(0))
        def _():
            start(bb, g2, lax.rem(gg, ring))

    @pl.when(b == 0)
    def _():
        for d in range(ahead):
            start(0, d, d)

    @pl.when(b == 0)
    def _():
        lhs[0:nk, :] = wukt_ref[...]

    hrow = lax.broadcasted_iota(jnp.int32, (A_HEADS, nk), 0)
    hcol = lax.broadcasted_iota(jnp.int32, (A_HEADS, nk), 1) // A_NOPE
    qsel = jnp.where(hrow == hcol, qrow_ref[0] * gkrow_ref[...], 0.0)
    q_hi, q_lo = _split2(qsel)
    qabs = _dot(q_hi, wukt_ref[...]) + _dot(q_lo, wukt_ref[...])
    lhs[nk:nk + 2 * A_HEADS, :] = jnp.concatenate(_split2(qabs), axis=0)
    qrg = jnp.concatenate(_split2(qr_ref[0] * gkr_ref[...]), axis=0)
    ppb = POS_BLOCK // PAGE

    def scores(slot, cslot, between=None):
        parts = []
        nblk = gpos // POS_BLOCK

        def project(j):
            rows = pl.ds(j * POS_BLOCK, POS_BLOCK)
            cb = latbuf[slot, rows, :].astype(BF16)
            cbuf[cslot, rows, :] = cb
            kx_sc[j] = _dot_nt(lhs[...], cb)

        for j in range(nblk):
            project(j)
        if between is not None:
            between()
        for j in range(nblk):
            kn = kx_sc[j, 0:nk, :]
            ss = jnp.sum((kn * kn).reshape(A_HEADS, A_NOPE, POS_BLOCK), axis=1)
            raw = kx_sc[j, nk:nk + A_HEADS, :] + kx_sc[j, nk + A_HEADS:nk + 2 * A_HEADS, :]
            krt = jnp.concatenate([krbuf[slot, j * ppb + t] for t in range(ppb)], axis=-1)
            rr = _dot(qrg, krt.astype(BF16))
            raw = raw + rr[:A_HEADS] + rr[A_HEADS:]
            ss = ss + jnp.sum(krt * krt, axis=0, keepdims=True)
            parts.append(raw * lax.rsqrt(ss * (1.0 / A_QK) + EPS) * sc2)
        return jnp.concatenate(parts, axis=-1)

    def absorb(s, slot, carry):
        m, l, acc = carry
        m_new = jnp.maximum(m, jnp.max(s, axis=-1, keepdims=True))
        a = jnp.exp2(m - m_new)
        p = jnp.exp2(s - m_new)
        l = l * a + jnp.sum(p, axis=-1, keepdims=True)
        acc = acc * a + _dot(p.astype(BF16), cbuf[slot])
        return m_new, l, acc

    wait(0, 0)
    fetch_ahead(0)
    s0 = scores(0, 0)
    init = (jnp.full((A_HEADS, 1), -jnp.inf, F32), jnp.zeros((A_HEADS, 1), F32),
            jnp.zeros((A_HEADS, KV_LORA), F32))

    def group(g, carry):
        s_prev, state = carry
        cslot = lax.rem(g, 2)
        wait(g, lax.rem(g, ring))
        fetch_ahead(g)
        box = []
        s_cur = scores(lax.rem(g, ring), cslot, lambda: box.append(absorb(s_prev, 1 - cslot, state)))
        return s_cur, box[0]

    s_last, state = lax.fori_loop(1, n_groups, group, (s0, init))
    m, l, acc = absorb(s_last, (n_groups - 1) % 2, state)

    s_new = jnp.sum(qh_ref[0].astype(F32) * kh_ref[0].astype(F32), axis=-1, keepdims=True) * sc2
    m_new = jnp.maximum(m, s_new)
    a = jnp.exp2(m - m_new)
    p = jnp.exp2(s_new - m_new)
    acc = acc * a + p * latn_ref[0]
    l = l * a + p
    o_lat = (acc / l).astype(BF16)
    full = _dot(o_lat, wuv_ref[...])
    vrow = lax.broadcasted_iota(jnp.int32, (A_HEADS, A_WIDTH), 0)
    vcol = lax.broadcasted_iota(jnp.int32, (A_HEADS, A_WIDTH), 1) // A_V
    o_ref[0] = jnp.sum(jnp.where(vrow == vcol, full, 0.0), axis=0, keepdims=True)


def _paged(page_table, cache_lat, cache_kr, qrow, qr, qh, kh, latn, wukt, gkrow, gkr, wuv, layer):
    nb, n_pages = page_table.shape
    gp = PAGES_PER_GROUP
    per_b = lambda shape: pl.BlockSpec((1,) + shape, lambda b, pt: (b,) + (0,) * len(shape))
    full = lambda a: pl.BlockSpec(a.shape, lambda b, pt: (0,) * a.ndim)
    grid_spec = pltpu.PrefetchScalarGridSpec(
        num_scalar_prefetch=1, grid=(nb,),
        in_specs=[pl.BlockSpec(memory_space=pl.ANY), pl.BlockSpec(memory_space=pl.ANY),
                  per_b((1, A_HEADS * A_NOPE)), per_b((A_HEADS, A_ROPE)), per_b((A_HEADS, HEAD_PAD)),
                  per_b((A_HEADS, HEAD_PAD)), per_b((1, KV_LORA)), full(wukt), full(gkrow), full(gkr), full(wuv)],
        out_specs=per_b((1, A_WIDTH)),
        scratch_shapes=[pltpu.VMEM((A_HEADS * A_NOPE + 2 * A_HEADS, KV_LORA), BF16),
                        pltpu.VMEM((DMA_SLOTS, gp * PAGE, KV_LORA), F32), pltpu.VMEM((DMA_SLOTS, gp, A_ROPE, PAGE), F32),
                        pltpu.VMEM((2, gp * PAGE, KV_LORA), BF16),
                        pltpu.VMEM((gp * PAGE // POS_BLOCK, A_HEADS * A_NOPE + 2 * A_HEADS, POS_BLOCK), F32),
                        pltpu.SemaphoreType.DMA((2, DMA_SLOTS))])
    return pl.pallas_call(
        functools.partial(_paged_kernel, layer=layer, n_pages=n_pages), grid_spec=grid_spec,
        out_shape=jax.ShapeDtypeStruct((nb, 1, A_WIDTH), F32),
        compiler_params=_params(1), name="paged_attn")(
            page_table, cache_lat, cache_kr, qrow, qr, qh, kh, latn, wukt, gkrow, gkr, wuv)


def _hstep_kernel(s_ref, q_ref, k_ref, g_ref, v_ref, o_ref, sn_ref, *, layer, rows):
    r = lax.broadcasted_iota(jnp.int32, (B_DK, B_DK), 0)
    c = lax.broadcasted_iota(jnp.int32, (B_DK, B_DK), 1)
    eye = r == c

    def column(row):
        return jnp.sum(jnp.where(eye, row, 0.0), axis=-1, keepdims=True)

    for i in range(rows):
        for h in range(B_HEADS):
            hs = slice(h * B_DK, (h + 1) * B_DK)
            sn = column(jnp.exp(g_ref[i, :, hs])) * s_ref[layer, i, h] + column(k_ref[i, :, hs]) * v_ref[i, :, hs]
            sn_ref[i, h] = sn
            o_ref[i, :, hs] = jnp.sum(column(q_ref[i, :, hs]) * sn, axis=0, keepdims=True)


def _hstep(state, q, k, g, v, layer):
    nb = state.shape[1]
    rows = HSTEP_ROWS
    st_in = pl.BlockSpec((state.shape[0], rows, B_HEADS, B_DK, B_DV), lambda b: (0, b, 0, 0, 0))
    st_out = pl.BlockSpec((rows, B_HEADS, B_DK, B_DV), lambda b: (b, 0, 0, 0))
    row = pl.BlockSpec((rows, 1, B_FDIM), lambda b: (b, 0, 0))
    return pl.pallas_call(
        functools.partial(_hstep_kernel, layer=layer, rows=rows), grid=(nb // rows,),
        in_specs=[st_in, row, row, row, row], out_specs=[row, st_out],
        out_shape=[jax.ShapeDtypeStruct((nb, 1, B_WIDTH), F32), jax.ShapeDtypeStruct(state.shape[1:], F32)],
        compiler_params=_params(1), name="hgrn_step")(state, q, k, g, v)


def _merge_kernel(x_ref, ng_ref, wg_ref, oa_ref, ob_ref, gbn_ref, woa_ref, wob_ref, wo_ref, y_ref):
    x = x_ref[...]
    xn = _rms(x, ng_ref[...]).astype(BF16)
    gates = _dot(xn, wg_ref[...])
    ga, gb = gates[:, :A_WIDTH], gates[:, A_WIDTH:A_WIDTH + B_WIDTH]
    ma = gates[:, A_WIDTH + B_WIDTH:A_WIDTH + B_WIDTH + D_MODEL]
    mb = gates[:, A_WIDTH + B_WIDTH + D_MODEL:]
    ya = _dot((oa_ref[...] * (ga * jax.nn.sigmoid(ga))).astype(BF16), woa_ref[...])
    gbn = gbn_ref[...]
    obn = jnp.concatenate([_rms(ob_ref[:, h * B_DV:(h + 1) * B_DV], gbn) for h in range(B_HEADS)], axis=-1)
    yb = _dot((obn * (gb * jax.nn.sigmoid(gb))).astype(BF16), wob_ref[...])
    mix = jax.nn.sigmoid(ma) * ya + jax.nn.sigmoid(mb) * yb
    y_ref[...] = x + _dot(mix.astype(BF16), wo_ref[...])


def _merge(x, oa, ob, w, tm):
    rows = x.shape[0]
    row = lambda n: pl.BlockSpec((tm, n), lambda i: (i, 0))
    ins = [x, w["norm_g"], w["wg"], oa, ob, w["g_bn"], w["w_oa"], w["w_ob"], w["w_o"]]
    in_specs = [row(D_MODEL), _full(ins[1].shape), _full(ins[2].shape), row(A_WIDTH), row(B_WIDTH)] + \
               [_full(a.shape) for a in ins[5:]]
    return pl.pallas_call(
        _merge_kernel, grid=(rows // tm,), in_specs=in_specs, out_specs=row(D_MODEL),
        out_shape=jax.ShapeDtypeStruct((rows, D_MODEL), F32),
        compiler_params=_params(1), name="merge_out")(*ins)


def _head_pad_cols(w3, width):
    pad = jnp.zeros(w3.shape[:2] + (HEAD_PAD - width,), w3.dtype)
    return jnp.concatenate([w3, pad], axis=-1).reshape(w3.shape[0], -1)


def _rope_tables(pos):
    half = A_ROPE // 2
    inv = ROPE_THETA ** (-jnp.arange(half, dtype=F32) / half)
    ang = pos.astype(F32)[:, None] * inv
    cos, sin = jnp.cos(ang), jnp.sin(ang)
    n = pos.shape[0]
    one, zero = jnp.ones((n, A_NOPE), F32), jnp.zeros((n, A_NOPE), F32)
    tail = jnp.zeros((n, HEAD_PAD - A_QK), F32)
    z16 = jnp.zeros((n, half), F32)
    c = jnp.concatenate([one, cos, cos, tail], axis=-1)
    s1 = jnp.concatenate([zero, -sin, z16, tail], axis=-1)
    s2 = jnp.concatenate([zero, z16, sin, tail], axis=-1)
    return c, s1, s2


def _lane_gain(g):
    return jnp.concatenate([g, jnp.zeros((HEAD_PAD - A_QK,), g.dtype)])[None, :]


def kernel(x_prompt, x_sample, cache_latent, cache_krope, state_hgrn, page_table, meta_tokens,
           norm_g, w_in, g_cq, w_uq, g_ckv, w_uk, w_uv, g_qn, g_kn, lb_logits, g_bn, w_oa, w_ob, w_o):
    batch, seq, _ = x_prompt.shape
    dec_batch, dec_seq, _ = x_sample.shape
    depth = w_in.shape[0]
    assert depth == 1 and dec_seq == 1
    past_len = page_table.shape[1] * PAGE
    l = 0

    o0 = Q_LORA
    o1 = o0 + KV_LORA
    o2 = o1 + A_ROPE
    o3 = o2 + 3 * B_FDIM
    wi = w_in[l]
    wkr = jnp.zeros((D_MODEL, HEAD_PAD), F32).at[:, A_NOPE:A_QK].set(wi[:, o1:o2])
    w = {
        "norm_g": norm_g[l][None, :],
        "wa": jnp.concatenate([wi[:, :o1], wkr, wi[:, o2:o3]], axis=1).astype(BF16), "wg": wi[:, o3:].astype(BF16),
        "g_cq": g_cq[l][None, :], "g_ckv": g_ckv[l][None, :],
        "w_uq": _head_pad_cols(w_uq[l].reshape(Q_LORA, A_HEADS, A_QK), A_QK).astype(BF16),
        "w_uk": _head_pad_cols(w_uk[l], A_NOPE).astype(BF16),
        "w_uv": w_uv[l].reshape(KV_LORA, A_WIDTH).astype(BF16),
        "w_uvt": w_uv[l].reshape(KV_LORA, A_WIDTH).T.astype(BF16),
        "g_qn": _lane_gain(g_qn[l]), "g_kn": _lane_gain(g_kn[l]),
        "lb_logits": lb_logits, "g_bn": g_bn[l][None, :],
        "w_oa": w_oa[l].astype(BF16), "w_ob": w_ob[l].astype(BF16), "w_o": w_o[l].astype(BF16),
    }

    xs = x_sample.reshape(dec_batch, D_MODEL)
    tabs_m = _rope_tables(jnp.arange(N_META))
    tabs_s = _rope_tables(jnp.full((dec_batch,), past_len, jnp.int32))
    tabs_ms = tuple(jnp.concatenate([a, b], axis=0) for a, b in zip(tabs_m, tabs_s))
    small = _pre(jnp.concatenate([meta_tokens, xs], axis=0), tabs_ms, w, N_META + dec_batch, 1, l)
    lat_m, krp_m, _, k_m, _, mq, mk, mv, mg = [a[:N_META] for a in small]
    lat_s, krp_s, q_s, k_s, _, sq, sk, sv, sg = [a[N_META:] for a in small]
    vt_m = small[4][:, :N_META]

    xp = x_prompt.reshape(batch * seq, D_MODEL)
    tabs_p = _rope_tables(N_META + jnp.arange(seq))
    qscale = (A_QK ** -0.5) * 1.4426950408889634
    lat_p, krp_p, q_p, k_p, vt_p, hq, hk, hv, hg = _pre(xp, tabs_p, w, PRE_TILE, seq // PRE_TILE, l, qscale, PRE_SUB)
    lat_all = jnp.concatenate([jnp.broadcast_to(lat_m[None], (batch, N_META, KV_LORA)),
                               lat_p.reshape(batch, seq, KV_LORA)], axis=1)
    kr_all = jnp.concatenate([jnp.broadcast_to(krp_m[None, :, A_NOPE:A_QK], (batch, N_META, A_ROPE)),
                              krp_p[:, A_NOPE:A_QK].reshape(batch, seq, A_ROPE)], axis=1)

    oa = _attn(q_p, k_p, vt_p, k_m, vt_m, batch, seq)
    zero_state = jnp.zeros((B_HEADS, B_DV, B_DK), F32)
    _, st_meta = _hgrn(mq, mk, mv, mg, zero_state, 1, N_META, N_META, N_META, False)
    ob, st_fin = _hgrn(hq, hk, hv, hg, st_meta[0], batch, seq, HGRN_CHUNK, HGRN_TILE, True)
    y_prompt = _merge(xp, oa, ob, w, ROW_TILE).reshape(batch, seq, D_MODEL)

    q_s3 = q_s.reshape(dec_batch, A_HEADS, HEAD_PAD)
    k_s3 = k_s.reshape(dec_batch, A_HEADS, HEAD_PAD)
    qrow = q_s3[:, :, :A_NOPE].astype(F32).reshape(dec_batch, 1, A_HEADS * A_NOPE)
    qrope = q_s3[:, :, A_NOPE:A_QK].astype(F32)
    gkrow = jnp.tile(g_kn[l][:A_NOPE], A_HEADS)[None, :]
    gkr = g_kn[l][None, A_NOPE:]
    wukt = w_uk[l].reshape(KV_LORA, A_HEADS * A_NOPE).T.astype(BF16)
    oa_s = _paged(page_table, cache_latent, jnp.swapaxes(cache_krope, 2, 3), qrow, qrope, q_s3, k_s3,
                  lat_s[:, None, :], wukt, gkrow, gkr, w["w_uv"], l)
    row3 = lambda a: a.reshape(dec_batch, 1, B_FDIM)
    ob_s, st_s = _hstep(state_hgrn, row3(sq), row3(sk), row3(sg), row3(sv), l)
    y_sample = _merge(xs, oa_s.reshape(dec_batch, A_WIDTH), ob_s.reshape(dec_batch, B_WIDTH), w, dec_batch)

    return (y_prompt, y_sample.reshape(dec_batch, dec_seq, D_MODEL),
            lat_all[None], kr_all[None], st_fin[None],
            lat_s.reshape(1, dec_batch, dec_seq, KV_LORA),
            krp_s[:, A_NOPE:A_QK].reshape(1, dec_batch, dec_seq, A_ROPE),
            st_s[None])
```

```python
import functools

import jax
import jax.numpy as jnp
from jax import lax
from jax.experimental import pallas as pl
from jax.experimental.pallas import tpu as pltpu

F32 = jnp.float32
BF16 = jnp.bfloat16

D_MODEL = 1024
N_META = 16
A_HEADS = 8
A_NOPE = 64
A_ROPE = 32
A_QK = A_NOPE + A_ROPE
A_V = 64
A_WIDTH = A_HEADS * A_V
Q_LORA = 384
KV_LORA = 256
ROPE_THETA = 10000.0
B_HEADS = 4
B_DK = 128
B_DV = 128
B_FDIM = B_HEADS * B_DK
B_WIDTH = B_HEADS * B_DV
EPS = 1e-6
PAGE = 128

LANES = 128
LOG2E = 1.4426950408889634
HEAD_PAD = LANES
VMEM_LIMIT = 56 * 1024 * 1024

PRE_TILE = 256
PRE_SUB = 256
ROW_TILE = 1024
ATTN_TQ = 256
ATTN_TK = 256
ATTN_QT = 2
HGRN_CHUNK = 128
HGRN_TILE = 1024
PAGES_PER_GROUP = 32
HSTEP_ROWS = 8
POS_BLOCK = 2 * PAGE
DMA_SLOTS = 4


def _full(shape):
    return pl.BlockSpec(shape, lambda *_: (0,) * len(shape))


def _params(n_axes):
    return pltpu.CompilerParams(dimension_semantics=("arbitrary",) * n_axes,
                                vmem_limit_bytes=VMEM_LIMIT)


def _dot(a, b):
    return jnp.dot(a, b, preferred_element_type=F32)


def _dot_nt(a, b, precision=None):
    return lax.dot_general(a, b, (((1,), (1,)), ((), ())), precision=precision,
                           preferred_element_type=F32)


def _rms(x, g):
    r = lax.rsqrt(jnp.mean(x * x, axis=-1, keepdims=True) + EPS)
    return (x * r) * g


def _rope_lanes(t, c, s1, s2):
    return t * c + pltpu.roll(t, LANES - A_ROPE // 2, axis=1) * s1 + pltpu.roll(t, A_ROPE // 2, axis=1) * s2


def _pre_kernel(x_ref, ng_ref, wa_ref, gcq_ref, wuq_ref, gckv_ref,
                wuk_ref, wuv_ref, gqn_ref, gkn_ref, c_ref, s1_ref, s2_ref, lb_ref,
                lat_ref, krp_ref, q_ref, k_ref, vt_ref, hq_ref, hk_ref, hv_ref, hg_ref, *, layer, qscale, sub):
    lbl = lb_ref[...]
    e = jnp.exp(lbl - jnp.max(lbl, axis=0, keepdims=True))
    lb = jnp.sum(e[:layer + 1], axis=0, keepdims=True) / jnp.sum(e, axis=0, keepdims=True)
    gqn, gkn = gqn_ref[...], gkn_ref[...]
    o1, o2, o3 = Q_LORA, Q_LORA + KV_LORA, Q_LORA + KV_LORA + LANES

    def rows_step(rows):
        x = x_ref[rows, :]
        xn = _rms(x, ng_ref[...]).astype(BF16)
        c, s1, s2 = c_ref[rows, :], s1_ref[rows, :], s2_ref[rows, :]

        y = _dot(xn, wa_ref[:, :o3])
        cqn = _rms(y[:, :o1], gcq_ref[...]).astype(BF16)
        qraw = _dot(cqn, wuq_ref[...])
        for h in range(A_HEADS):
            hs = slice(h * HEAD_PAD, (h + 1) * HEAD_PAD)
            t = _rope_lanes(qraw[:, hs], c, s1, s2)
            r = lax.rsqrt(jnp.sum(t * t, axis=-1, keepdims=True) * (1.0 / A_QK) + EPS)
            q_ref[rows, hs] = (((t * r) * gqn) * qscale).astype(BF16)

        ckv = _rms(y[:, o1:o2], gckv_ref[...])
        lat_ref[rows, :] = ckv
        ckvb = ckv.astype(BF16)
        krp = _rope_lanes(y[:, o2:o3], c, s1, s2)
        krp_ref[rows, :] = krp
        kraw = _dot(ckvb, wuk_ref[...])
        for h in range(A_HEADS):
            hs = slice(h * HEAD_PAD, (h + 1) * HEAD_PAD)
            t = kraw[:, hs] + krp
            r = lax.rsqrt(jnp.sum(t * t, axis=-1, keepdims=True) * (1.0 / A_QK) + EPS)
            k_ref[rows, hs] = ((t * r) * gkn).astype(BF16)
        vt_ref[:, rows] = _dot_nt(wuv_ref[...], ckvb).astype(BF16)

        b = _dot(xn, wa_ref[:, o3:])
        bq, z, bi = b[:, :B_FDIM], b[:, B_FDIM:2 * B_FDIM], b[:, 2 * B_FDIM:]
        hq_ref[rows, :] = bq * jax.nn.sigmoid(bq)
        sz = jax.nn.sigmoid(z)
        hg_ref[rows, :] = jnp.log(lb + (1.0 - lb) * sz)
        hk_ref[rows, :] = (1.0 - lb) * (1.0 - sz)
        hv_ref[rows, :] = bi

    tm = x_ref.shape[0]
    if tm == sub:
        rows_step(slice(None))
    else:
        def body(r, _):
            rows_step(pl.ds(pl.multiple_of(r * sub, sub), sub))
            return 0
        lax.fori_loop(0, tm // sub, body, 0)


def _pre(x, tabs, w, tm, tab_blocks, layer, qscale=1.0, sub=None):
    rows = x.shape[0]
    grid = (rows // tm,)
    row = lambda n: pl.BlockSpec((tm, n), lambda i: (i, 0))
    tab = pl.BlockSpec((tm, LANES), lambda i: (i % tab_blocks, 0))
    ins = [x, w["norm_g"], w["wa"], w["g_cq"], w["w_uq"], w["g_ckv"],
           w["w_uk"], w["w_uvt"], w["g_qn"], w["g_kn"], tabs[0], tabs[1], tabs[2], w["lb_logits"]]
    in_specs = [row(D_MODEL)] + [_full(a.shape) for a in ins[1:10]] + [tab, tab, tab] + [_full(ins[13].shape)]
    outs = [(KV_LORA, F32), (LANES, F32), (A_HEADS * HEAD_PAD, BF16), (A_HEADS * HEAD_PAD, BF16),
            None, (B_FDIM, F32), (B_FDIM, F32), (B_WIDTH, F32), (B_FDIM, F32)]
    out_specs = [pl.BlockSpec((A_WIDTH, tm), lambda i: (0, i)) if o is None else row(o[0]) for o in outs]
    out_shape = [jax.ShapeDtypeStruct((A_WIDTH, rows), BF16) if o is None else
                 jax.ShapeDtypeStruct((rows, o[0]), o[1]) for o in outs]
    return pl.pallas_call(
        functools.partial(_pre_kernel, layer=layer, qscale=qscale, sub=sub or tm),
        grid=grid, in_specs=in_specs, out_specs=out_specs, out_shape=out_shape,
        compiler_params=_params(1), name="pre_proj")(*ins)


def _attn_kernel(q_ref, k_ref, vt_ref, km_ref, vtm_ref, o_ref, ot_sc, s_sc, sm_sc, m_sc, l_sc):
    def tile(qt, _):
        i = pl.program_id(1) * ATTN_QT + qt
        qrows = pl.ds(pl.multiple_of(qt * ATTN_TQ, ATTN_TQ), ATTN_TQ)
        _attn_tile(i, qrows, q_ref, k_ref, vt_ref, km_ref, vtm_ref, o_ref, ot_sc, s_sc, sm_sc, m_sc, l_sc)
        return 0

    lax.fori_loop(0, ATTN_QT, tile, 0)


def _attn_tile(i, qrows, q_ref, k_ref, vt_ref, km_ref, vtm_ref, o_ref, ot_sc, s_sc, sm_sc, m_sc, l_sc):
    tq, tk = ATTN_TQ, ATTN_TK
    assert tq == tk
    key = lax.broadcasted_iota(jnp.int32, (tk, tq), 0)
    qry = lax.broadcasted_iota(jnp.int32, (tk, tq), 1)
    hsl = [slice(h * HEAD_PAD, (h + 1) * HEAD_PAD) for h in range(A_HEADS)]
    vsl = [slice(h * A_V, (h + 1) * A_V) for h in range(A_HEADS)]

    d0 = pl.multiple_of(i * tq, tq)
    for h in range(A_HEADS):
        s_sc[h, 0:tk, :] = _dot_nt(k_ref[pl.ds(d0, tk), hsl[h]], q_ref[qrows, hsl[h]])
        sm_sc[h] = _dot_nt(km_ref[:, hsl[h]], q_ref[qrows, hsl[h]])
    for h in range(A_HEADS):
        hr = slice(h, h + 1)
        s = jnp.where(key <= qry, s_sc[h, 0:tk, :], -jnp.inf)
        sm = sm_sc[h]
        m_new = jnp.maximum(jnp.max(s, axis=0, keepdims=True), jnp.max(sm, axis=0, keepdims=True))
        p = jnp.exp2(s - m_new)
        pm = jnp.exp2(sm - m_new)
        m_sc[hr, :] = m_new
        l_sc[hr, :] = jnp.sum(p, axis=0, keepdims=True) + jnp.sum(pm, axis=0, keepdims=True)
        ot_sc[vsl[h], :] = (_dot(vt_ref[vsl[h], pl.ds(d0, tk)], p.astype(BF16)) +
                            _dot(vtm_ref[vsl[h], :], pm.astype(BF16)))

    def visible(r0, keys):
        for h in range(A_HEADS):
            s_sc[h, 0:keys, :] = _dot_nt(k_ref[pl.ds(r0, keys), hsl[h]], q_ref[qrows, hsl[h]])
        for h in range(A_HEADS):
            hr = slice(h, h + 1)
            s = s_sc[h, 0:keys, :]
            m_new = jnp.maximum(m_sc[hr, :], jnp.max(s, axis=0, keepdims=True))
            a = jnp.exp2(m_sc[hr, :] - m_new)
            p = jnp.exp2(s - m_new)
            m_sc[hr, :] = m_new
            l_sc[hr, :] = l_sc[hr, :] * a + jnp.sum(p, axis=0, keepdims=True)
            ot_sc[vsl[h], :] = ot_sc[vsl[h], :] * a + _dot(vt_ref[vsl[h], pl.ds(r0, keys)], p.astype(BF16))

    def body(kb, _):
        visible(pl.multiple_of(kb * 4 * tk, 4 * tk), 4 * tk)
        return 0

    lax.fori_loop(0, i // 4, body, 0)

    @pl.when(i % 4 >= 2)
    def _():
        visible(pl.multiple_of((i // 4) * 4 * tk, 2 * tk), 2 * tk)

    @pl.when(i % 2 == 1)
    def _():
        visible(pl.multiple_of((i - 1) * tk, tk), tk)

    for h in range(A_HEADS):
        ot_sc[vsl[h], :] = ot_sc[vsl[h], :] / l_sc[h:h + 1, :]
    o_ref[qrows, :] = ot_sc[...].T


def _attn(q, k, vt, km, vtm, batch, seq):
    rows = ATTN_QT * ATTN_TQ
    nq = seq // rows
    return pl.pallas_call(
        _attn_kernel, grid=(batch, nq),
        in_specs=[pl.BlockSpec((rows, A_HEADS * HEAD_PAD), lambda b, i: (b * nq + i, 0)),
                  pl.BlockSpec((seq, A_HEADS * HEAD_PAD), lambda b, i: (b, 0)),
                  pl.BlockSpec((A_WIDTH, seq), lambda b, i: (0, b)),
                  _full(km.shape), _full(vtm.shape)],
        out_specs=pl.BlockSpec((rows, A_WIDTH), lambda b, i: (b * nq + i, 0)),
        out_shape=jax.ShapeDtypeStruct((batch * seq, A_WIDTH), F32),
        scratch_shapes=[pltpu.VMEM((A_WIDTH, ATTN_TQ), F32), pltpu.VMEM((A_HEADS, 4 * ATTN_TK, ATTN_TQ), F32),
                        pltpu.VMEM((A_HEADS, km.shape[0], ATTN_TQ), F32),
                        pltpu.VMEM((A_HEADS, ATTN_TQ), F32), pltpu.VMEM((A_HEADS, ATTN_TQ), F32)],
        compiler_params=_params(2), name="prompt_attn")(q, k, vt, km, vtm)


def _split3(x):
    a = x.astype(BF16)
    r = x - a.astype(F32)
    b = r.astype(BF16)
    c = (r - b.astype(F32)).astype(BF16)
    return a, b, c


def _hgrn_chunk(q, kk, v, g, st, tril, chunk, cum_ref, krow):
    g1, g2, g3 = _split3(g)
    cum = (_dot(tril, g1) + _dot(tril, g2) + _dot(tril, g3)) * LOG2E
    cum_ref[...] = cum
    yield None
    nblk = chunk // 8
    cb = [cum[8 * j:8 * j + 8] for j in range(nblk)]
    qb = [q[8 * j:8 * j + 8] for j in range(nblk)]
    lane = lax.broadcasted_iota(jnp.int32, (8, chunk), 1)
    sub = lax.broadcasted_iota(jnp.int32, (8, chunk), 0)
    ab = []
    for j in range(nblk):
        a = jnp.zeros((8, chunk), F32)
        for s in range(8 * j, 8 * j + 8):
            col = jnp.sum(jnp.exp2(cb[j] - cum_ref[s:s + 1, :]) * qb[j] * krow(s), axis=-1, keepdims=True)
            a = jnp.where(lane == s, col, a)
        ab.append(jnp.where(lane <= sub + 8 * j, a, 0.0))
        yield None
    attn = jnp.concatenate(ab, axis=0)
    row = lax.broadcasted_iota(jnp.int32, (chunk, chunk), 0)
    col = lax.broadcasted_iota(jnp.int32, (chunk, chunk), 1)
    kb = [kk[8 * j:8 * j + 8] for j in range(nblk)]
    dead = jnp.zeros((8, B_DK), F32)
    w = 8
    while w < chunk:
        qs, ks = [], []
        for j in range(nblk):
            blk = (8 * j) // w
            if blk % 2:
                qs.append(qb[j] * jnp.exp2(cb[j] - cum_ref[blk * w - 1:blk * w, :]))
                ks.append(dead)
            else:
                qs.append(dead)
                ks.append(kb[j] * jnp.exp2(cum_ref[blk * w + w - 1:blk * w + w, :] - cb[j]))
        qa = jnp.concatenate(qs, axis=0).astype(BF16)
        ka = jnp.concatenate(ks, axis=0).astype(BF16)
        sh = w.bit_length() - 1
        rb, cbk = row >> sh, col >> sh
        attn = attn + jnp.where((rb == cbk + 1) & ((cbk & 1) == 0), _dot_nt(qa, ka), 0.0)
        w *= 2
        yield None
    last = cum_ref[chunk - 1:chunk, :]
    o = _dot(attn.astype(BF16), v.astype(BF16)) + _dot_nt((q * jnp.exp2(cum)).astype(BF16), st.astype(BF16))
    kdec = (kk * jnp.exp2(last - cum)).astype(BF16)
    upd = lax.dot_general(v.astype(BF16), kdec, (((0,), (0,)), ((), ())), preferred_element_type=F32)
    yield o, jnp.exp2(last) * st + upd


def _hgrn_kernel(q_ref, k_ref, v_ref, g_ref, st0_ref, o_ref, st_ref, st_sc, cum_sc, kk_sc, *, chunk, tile,
                 transpose_out):
    t = pl.program_id(1)

    @pl.when(t == 0)
    def _():
        st_sc[...] = st0_ref[...]

    r = lax.broadcasted_iota(jnp.int32, (chunk, chunk), 0)
    c = lax.broadcasted_iota(jnp.int32, (chunk, chunk), 1)
    tril = (c <= r).astype(BF16)

    def body(ci, _):
        r0 = pl.multiple_of(ci * chunk, chunk)
        hsl = [slice(h * B_DK, (h + 1) * B_DK) for h in range(B_HEADS)]
        stages = {}
        for h in range(B_HEADS):
            kk = k_ref[pl.ds(r0, chunk), hsl[h]]
            kk_sc[h] = kk
            krow = lambda s, h=h: kk_sc[h, s:s + 1, :]
            stages[h] = _hgrn_chunk(q_ref[pl.ds(r0, chunk), hsl[h]], kk,
                                    v_ref[pl.ds(r0, chunk), hsl[h]], g_ref[pl.ds(r0, chunk), hsl[h]],
                                    st_sc[h], tril, chunk, cum_sc.at[h], krow)
        while stages:
            for h in list(stages):
                out = next(stages[h])
                if out is not None:
                    o_ref[pl.ds(r0, chunk), hsl[h]] = out[0]
                    st_sc[h] = out[1]
                    del stages[h]
        return 0

    lax.fori_loop(0, tile // chunk, body, 0, unroll=4 if tile // chunk > 1 else 1)

    @pl.when(t == pl.num_programs(1) - 1)
    def _():
        for h in range(B_HEADS):
            st_ref[0, h] = st_sc[h].T if transpose_out else st_sc[h]


def _hgrn(hq, hk, hv, hg, st0, batch, seq, chunk, tile, transpose_out):
    nt = seq // tile
    row = pl.BlockSpec((tile, B_FDIM), lambda b, t: (b * nt + t, 0))
    return pl.pallas_call(
        functools.partial(_hgrn_kernel, chunk=chunk, tile=tile, transpose_out=transpose_out),
        grid=(batch, nt),
        in_specs=[row, row, row, row, _full(st0.shape)],
        out_specs=[row, pl.BlockSpec((1, B_HEADS, B_DV, B_DK), lambda b, t: (b, 0, 0, 0))],
        out_shape=[jax.ShapeDtypeStruct((batch * seq, B_WIDTH), F32),
                   jax.ShapeDtypeStruct((batch, B_HEADS, B_DV, B_DK), F32)],
        scratch_shapes=[pltpu.VMEM((B_HEADS, B_DV, B_DK), F32), pltpu.VMEM((B_HEADS, chunk, B_DK), F32),
                        pltpu.VMEM((B_HEADS, chunk, B_DK), F32)],
        compiler_params=_params(2), name="hgrn_chunks")(hq, hk, hv, hg, st0)


def _split2(x):
    hi = x.astype(BF16)
    return hi, (x - hi.astype(F32)).astype(BF16)


def _paged_kernel(pt_ref, lat_hbm, kr_hbm, qrow_ref, qr_ref, qh_ref, kh_ref, latn_ref, wukt_ref,
                  gkrow_ref, gkr_ref, wuv_ref, o_ref, lhs, latbuf, krbuf, cbuf, kx_sc, sems, *, layer, n_pages):
    b = pl.program_id(0)
    gp = PAGES_PER_GROUP
    n_groups = n_pages // gp
    gpos = gp * PAGE
    nk = A_HEADS * A_NOPE
    log2e = 1.4426950408889634
    sc2 = (A_QK ** -0.5) * log2e

    def page_copies(bb, g, slot):
        out = []
        for j in range(gp):
            page = pt_ref[bb, g * gp + j]
            dst = pl.ds(j * PAGE, PAGE)
            out.append(pltpu.make_async_copy(lat_hbm.at[layer, page], latbuf.at[slot, dst], sems.at[0, slot]))
            out.append(pltpu.make_async_copy(kr_hbm.at[layer, page], krbuf.at[slot, j], sems.at[1, slot]))
        return out

    def start(bb, g, slot):
        for cp in page_copies(bb, g, slot):
            cp.start()

    def wait(g, slot):
        for cp in page_copies(b, g, slot):
            cp.wait()

    ring = DMA_SLOTS
    ahead = ring - 1
    assert n_groups % ring == 0 and ahead <= n_groups

    def fetch_ahead(g):
        gg = g + ahead
        wrap = gg >= n_groups
        bb = jnp.where(wrap, b + 1, b)
        g2 = jnp.where(wrap, gg - n_groups, gg)

        @pl.when(bb < pl.num_programs(0))
        def _():
            start(bb, g2, lax.rem(gg, ring))

    @pl.when(b == 0)
    def _():
        for d in range(ahead):
            start(0, d, d)

    @pl.when(b == 0)
    def _():
        lhs[0:nk, :] = wukt_ref[...]

    hrow = lax.broadcasted_iota(jnp.int32, (A_HEADS, nk), 0)
    hcol = lax.broadcasted_iota(jnp.int32, (A_HEADS, nk), 1) // A_NOPE
    qsel = jnp.where(hrow == hcol, qrow_ref[0] * gkrow_ref[...], 0.0)
    q_hi, q_lo = _split2(qsel)
    qabs = _dot(q_hi, wukt_ref[...]) + _dot(q_lo, wukt_ref[...])
    lhs[nk:nk + 2 * A_HEADS, :] = jnp.concatenate(_split2(qabs), axis=0)
    qrg = jnp.concatenate(_split2(qr_ref[0] * gkr_ref[...]), axis=0)
    ppb = POS_BLOCK // PAGE

    def scores(slot, cslot, between=None):
        parts = []
        nblk = gpos // POS_BLOCK

        def project(j):
            rows = pl.ds(j * POS_BLOCK, POS_BLOCK)
            cb = latbuf[slot, rows, :].astype(BF16)
            cbuf[cslot, rows, :] = cb
            kx_sc[j] = _dot_nt(lhs[...], cb)

        for j in range(nblk):
            project(j)
        if between is not None:
            between()
        for j in range(nblk):
            kn = kx_sc[j, 0:nk, :]
            ss = jnp.sum((kn * kn).reshape(A_HEADS, A_NOPE, POS_BLOCK), axis=1)
            raw = kx_sc[j, nk:nk + A_HEADS, :] + kx_sc[j, nk + A_HEADS:nk + 2 * A_HEADS, :]
            krt = jnp.concatenate([krbuf[slot, j * ppb + t] for t in range(ppb)], axis=-1)
            rr = _dot(qrg, krt.astype(BF16))
            raw = raw + rr[:A_HEADS] + rr[A_HEADS:]
            ss = ss + jnp.sum(krt * krt, axis=0, keepdims=True)
            parts.append(raw * lax.rsqrt(ss * (1.0 / A_QK) + EPS) * sc2)
        return jnp.concatenate(parts, axis=-1)

    def absorb(s, slot, carry):
        m, l, acc = carry
        m_new = jnp.maximum(m, jnp.max(s, axis=-1, keepdims=True))
        a = jnp.exp2(m - m_new)
        p = jnp.exp2(s - m_new)
        l = l * a + jnp.sum(p, axis=-1, keepdims=True)
        acc = acc * a + _dot(p.astype(BF16), cbuf[slot])
        return m_new, l, acc

    wait(0, 0)
    fetch_ahead(0)
    s0 = scores(0, 0)
    init = (jnp.full((A_HEADS, 1), -jnp.inf, F32), jnp.zeros((A_HEADS, 1), F32),
            jnp.zeros((A_HEADS, KV_LORA), F32))

    def group(g, carry):
        s_prev, state = carry
        cslot = lax.rem(g, 2)
        wait(g, lax.rem(g, ring))
        fetch_ahead(g)
        box = []
        s_cur = scores(lax.rem(g, ring), cslot, lambda: box.append(absorb(s_prev, 1 - cslot, state)))
        return s_cur, box[0]

    s_last, state = lax.fori_loop(1, n_groups, group, (s0, init), unroll=True)
    m, l, acc = absorb(s_last, (n_groups - 1) % 2, state)

    s_new = jnp.sum(qh_ref[0].astype(F32) * kh_ref[0].astype(F32), axis=-1, keepdims=True) * sc2
    m_new = jnp.maximum(m, s_new)
    a = jnp.exp2(m - m_new)
    p = jnp.exp2(s_new - m_new)
    acc = acc * a + p * latn_ref[0]
    l = l * a + p
    o_lat = (acc / l).astype(BF16)
    full = _dot(o_lat, wuv_ref[...])
    vrow = lax.broadcasted_iota(jnp.int32, (A_HEADS, A_WIDTH), 0)
    vcol = lax.broadcasted_iota(jnp.int32, (A_HEADS, A_WIDTH), 1) // A_V
    o_ref[0] = jnp.sum(jnp.where(vrow == vcol, full, 0.0), axis=0, keepdims=True)


def _paged(page_table, cache_lat, cache_kr, qrow, qr, qh, kh, latn, wukt, gkrow, gkr, wuv, layer):
    nb, n_pages = page_table.shape
    gp = PAGES_PER_GROUP
    per_b = lambda shape: pl.BlockSpec((1,) + shape, lambda b, pt: (b,) + (0,) * len(shape))
    full = lambda a: pl.BlockSpec(a.shape, lambda b, pt: (0,) * a.ndim)
    grid_spec = pltpu.PrefetchScalarGridSpec(
        num_scalar_prefetch=1, grid=(nb,),
        in_specs=[pl.BlockSpec(memory_space=pl.ANY), pl.BlockSpec(memory_space=pl.ANY),
                  per_b((1, A_HEADS * A_NOPE)), per_b((A_HEADS, A_ROPE)), per_b((A_HEADS, HEAD_PAD)),
                  per_b((A_HEADS, HEAD_PAD)), per_b((1, KV_LORA)), full(wukt), full(gkrow), full(gkr), full(wuv)],
        out_specs=per_b((1, A_WIDTH)),
        scratch_shapes=[pltpu.VMEM((A_HEADS * A_NOPE + 2 * A_HEADS, KV_LORA), BF16),
                        pltpu.VMEM((DMA_SLOTS, gp * PAGE, KV_LORA), F32), pltpu.VMEM((DMA_SLOTS, gp, A_ROPE, PAGE), F32),
                        pltpu.VMEM((2, gp * PAGE, KV_LORA), BF16),
                        pltpu.VMEM((gp * PAGE // POS_BLOCK, A_HEADS * A_NOPE + 2 * A_HEADS, POS_BLOCK), F32),
                        pltpu.SemaphoreType.DMA((2, DMA_SLOTS))])
    return pl.pallas_call(
        functools.partial(_paged_kernel, layer=layer, n_pages=n_pages), grid_spec=grid_spec,
        out_shape=jax.ShapeDtypeStruct((nb, 1, A_WIDTH), F32),
        compiler_params=_params(1), name="paged_attn")(
            page_table, cache_lat, cache_kr, qrow, qr, qh, kh, latn, wukt, gkrow, gkr, wuv)


def _hstep_kernel(s_ref, q_ref, k_ref, g_ref, v_ref, o_ref, sn_ref, *, layer, rows):
    r = lax.broadcasted_iota(jnp.int32, (B_DK, B_DK), 0)
    c = lax.broadcasted_iota(jnp.int32, (B_DK, B_DK), 1)
    eye = r == c

    def column(row):
        return jnp.sum(jnp.where(eye, row, 0.0), axis=-1, keepdims=True)

    for i in range(rows):
        for h in range(B_HEADS):
            hs = slice(h * B_DK, (h + 1) * B_DK)
            sn = column(jnp.exp(g_ref[i, :, hs])) * s_ref[layer, i, h] + column(k_ref[i, :, hs]) * v_ref[i, :, hs]
            sn_ref[i, h] = sn
            o_ref[i, :, hs] = jnp.sum(column(q_ref[i, :, hs]) * sn, axis=0, keepdims=True)


def _hstep(state, q, k, g, v, layer):
    nb = state.shape[1]
    rows = HSTEP_ROWS
    st_in = pl.BlockSpec((state.shape[0], rows, B_HEADS, B_DK, B_DV), lambda b: (0, b, 0, 0, 0))
    st_out = pl.BlockSpec((rows, B_HEADS, B_DK, B_DV), lambda b: (b, 0, 0, 0))
    row = pl.BlockSpec((rows, 1, B_FDIM), lambda b: (b, 0, 0))
    return pl.pallas_call(
        functools.partial(_hstep_kernel, layer=layer, rows=rows), grid=(nb // rows,),
        in_specs=[st_in, row, row, row, row], out_specs=[row, st_out],
        out_shape=[jax.ShapeDtypeStruct((nb, 1, B_WIDTH), F32), jax.ShapeDtypeStruct(state.shape[1:], F32)],
        compiler_params=_params(1), name="hgrn_step")(state, q, k, g, v)


def _merge_kernel(x_ref, ng_ref, wg_ref, oa_ref, ob_ref, gbn_ref, woa_ref, wob_ref, wo_ref, y_ref):
    x = x_ref[...]
    xn = _rms(x, ng_ref[...]).astype(BF16)
    gates = _dot(xn, wg_ref[...])
    ga, gb = gates[:, :A_WIDTH], gates[:, A_WIDTH:A_WIDTH + B_WIDTH]
    ma = gates[:, A_WIDTH + B_WIDTH:A_WIDTH + B_WIDTH + D_MODEL]
    mb = gates[:, A_WIDTH + B_WIDTH + D_MODEL:]
    ya = _dot((oa_ref[...] * (ga * jax.nn.sigmoid(ga))).astype(BF16), woa_ref[...])
    gbn = gbn_ref[...]
    obn = jnp.concatenate([_rms(ob_ref[:, h * B_DV:(h + 1) * B_DV], gbn) for h in range(B_HEADS)], axis=-1)
    yb = _dot((obn * (gb * jax.nn.sigmoid(gb))).astype(BF16), wob_ref[...])
    mix = jax.nn.sigmoid(ma) * ya + jax.nn.sigmoid(mb) * yb
    y_ref[...] = x + _dot(mix.astype(BF16), wo_ref[...])


def _merge(x, oa, ob, w, tm):
    rows = x.shape[0]
    row = lambda n: pl.BlockSpec((tm, n), lambda i: (i, 0))
    ins = [x, w["norm_g"], w["wg"], oa, ob, w["g_bn"], w["w_oa"], w["w_ob"], w["w_o"]]
    in_specs = [row(D_MODEL), _full(ins[1].shape), _full(ins[2].shape), row(A_WIDTH), row(B_WIDTH)] + \
               [_full(a.shape) for a in ins[5:]]
    return pl.pallas_call(
        _merge_kernel, grid=(rows // tm,), in_specs=in_specs, out_specs=row(D_MODEL),
        out_shape=jax.ShapeDtypeStruct((rows, D_MODEL), F32),
        compiler_params=_params(1), name="merge_out")(*ins)


def _head_pad_cols(w3, width):
    pad = jnp.zeros(w3.shape[:2] + (HEAD_PAD - width,), w3.dtype)
    return jnp.concatenate([w3, pad], axis=-1).reshape(w3.shape[0], -1)


def _rope_tables(pos):
    half = A_ROPE // 2
    inv = ROPE_THETA ** (-jnp.arange(half, dtype=F32) / half)
    ang = pos.astype(F32)[:, None] * inv
    cos, sin = jnp.cos(ang), jnp.sin(ang)
    n = pos.shape[0]
    one, zero = jnp.ones((n, A_NOPE), F32), jnp.zeros((n, A_NOPE), F32)
    tail = jnp.zeros((n, HEAD_PAD - A_QK), F32)
    z16 = jnp.zeros((n, half), F32)
    c = jnp.concatenate([one, cos, cos, tail], axis=-1)
    s1 = jnp.concatenate([zero, -sin, z16, tail], axis=-1)
    s2 = jnp.concatenate([zero, z16, sin, tail], axis=-1)
    return c, s1, s2


def _lane_gain(g):
    return jnp.concatenate([g, jnp.zeros((HEAD_PAD - A_QK,), g.dtype)])[None, :]


def kernel(x_prompt, x_sample, cache_latent, cache_krope, state_hgrn, page_table, meta_tokens,
           norm_g, w_in, g_cq, w_uq, g_ckv, w_uk, w_uv, g_qn, g_kn, lb_logits, g_bn, w_oa, w_ob, w_o):
    batch, seq, _ = x_prompt.shape
    dec_batch, dec_seq, _ = x_sample.shape
    depth = w_in.shape[0]
    assert depth == 1 and dec_seq == 1
    past_len = page_table.shape[1] * PAGE
    l = 0

    o0 = Q_LORA
    o1 = o0 + KV_LORA
    o2 = o1 + A_ROPE
    o3 = o2 + 3 * B_FDIM
    wi = w_in[l]
    wkr = jnp.zeros((D_MODEL, HEAD_PAD), F32).at[:, A_NOPE:A_QK].set(wi[:, o1:o2])
    w = {
        "norm_g": norm_g[l][None, :],
        "wa": jnp.concatenate([wi[:, :o1], wkr, wi[:, o2:o3]], axis=1).astype(BF16), "wg": wi[:, o3:].astype(BF16),
        "g_cq": g_cq[l][None, :], "g_ckv": g_ckv[l][None, :],
        "w_uq": _head_pad_cols(w_uq[l].reshape(Q_LORA, A_HEADS, A_QK), A_QK).astype(BF16),
        "w_uk": _head_pad_cols(w_uk[l], A_NOPE).astype(BF16),
        "w_uv": w_uv[l].reshape(KV_LORA, A_WIDTH).astype(BF16),
        "w_uvt": w_uv[l].reshape(KV_LORA, A_WIDTH).T.astype(BF16),
        "g_qn": _lane_gain(g_qn[l]), "g_kn": _lane_gain(g_kn[l]),
        "lb_logits": lb_logits, "g_bn": g_bn[l][None, :],
        "w_oa": w_oa[l].astype(BF16), "w_ob": w_ob[l].astype(BF16), "w_o": w_o[l].astype(BF16),
    }

    xs = x_sample.reshape(dec_batch, D_MODEL)
    tabs_m = _rope_tables(jnp.arange(N_META))
    tabs_s = _rope_tables(jnp.full((dec_batch,), past_len, jnp.int32))
    tabs_ms = tuple(jnp.concatenate([a, b], axis=0) for a, b in zip(tabs_m, tabs_s))
    small = _pre(jnp.concatenate([meta_tokens, xs], axis=0), tabs_ms, w, N_META + dec_batch, 1, l)
    lat_m, krp_m, _, k_m, _, mq, mk, mv, mg = [a[:N_META] for a in small]
    lat_s, krp_s, q_s, k_s, _, sq, sk, sv, sg = [a[N_META:] for a in small]
    vt_m = small[4][:, :N_META]

    xp = x_prompt.reshape(batch * seq, D_MODEL)
    tabs_p = _rope_tables(N_META + jnp.arange(seq))
    qscale = (A_QK ** -0.5) * 1.4426950408889634
    lat_p, krp_p, q_p, k_p, vt_p, hq, hk, hv, hg = _pre(xp, tabs_p, w, PRE_TILE, seq // PRE_TILE, l, qscale, PRE_SUB)
    lat_all = jnp.concatenate([jnp.broadcast_to(lat_m[None], (batch, N_META, KV_LORA)),
                               lat_p.reshape(batch, seq, KV_LORA)], axis=1)
    kr_all = jnp.concatenate([jnp.broadcast_to(krp_m[None, :, A_NOPE:A_QK], (batch, N_META, A_ROPE)),
                              krp_p[:, A_NOPE:A_QK].reshape(batch, seq, A_ROPE)], axis=1)

    oa = _attn(q_p, k_p, vt_p, k_m, vt_m, batch, seq)
    zero_state = jnp.zeros((B_HEADS, B_DV, B_DK), F32)
    _, st_meta = _hgrn(mq, mk, mv, mg, zero_state, 1, N_META, N_META, N_META, False)
    ob, st_fin = _hgrn(hq, hk, hv, hg, st_meta[0], batch, seq, HGRN_CHUNK, HGRN_TILE, True)
    y_prompt = _merge(xp, oa, ob, w, ROW_TILE).reshape(batch, seq, D_MODEL)

    q_s3 = q_s.reshape(dec_batch, A_HEADS, HEAD_PAD)
    k_s3 = k_s.reshape(dec_batch, A_HEADS, HEAD_PAD)
    qrow = q_s3[:, :, :A_NOPE].astype(F32).reshape(dec_batch, 1, A_HEADS * A_NOPE)
    qrope = q_s3[:, :, A_NOPE:A_QK].astype(F32)
    gkrow = jnp.tile(g_kn[l][:A_NOPE], A_HEADS)[None, :]
    gkr = g_kn[l][None, A_NOPE:]
    wukt = w_uk[l].reshape(KV_LORA, A_HEADS * A_NOPE).T.astype(BF16)
    oa_s = _paged(page_table, cache_latent, jnp.swapaxes(cache_krope, 2, 3), qrow, qrope, q_s3, k_s3,
                  lat_s[:, None, :], wukt, gkrow, gkr, w["w_uv"], l)
    row3 = lambda a: a.reshape(dec_batch, 1, B_FDIM)
    ob_s, st_s = _hstep(state_hgrn, row3(sq), row3(sk), row3(sg), row3(sv), l)
    y_sample = _merge(xs, oa_s.reshape(dec_batch, A_WIDTH), ob_s.reshape(dec_batch, B_WIDTH), w, dec_batch)

    return (y_prompt, y_sample.reshape(dec_batch, dec_seq, D_MODEL),
            lat_all[None], kr_all[None], st_fin[None],
            lat_s.reshape(1, dec_batch, dec_seq, KV_LORA),
            krp_s[:, A_NOPE:A_QK].reshape(1, dec_batch, dec_seq, A_ROPE),
            st_s[None])
```

```python
import functools

import jax
import jax.numpy as jnp
from jax import lax
from jax.experimental import pallas as pl
from jax.experimental.pallas import tpu as pltpu

F32 = jnp.float32
BF16 = jnp.bfloat16

D_MODEL = 1024
N_META = 16
A_HEADS = 8
A_NOPE = 64
A_ROPE = 32
A_QK = A_NOPE + A_ROPE
A_V = 64
A_WIDTH = A_HEADS * A_V
Q_LORA = 384
KV_LORA = 256
ROPE_THETA = 10000.0
B_HEADS = 4
B_DK = 128
B_DV = 128
B_FDIM = B_HEADS * B_DK
B_WIDTH = B_HEADS * B_DV
EPS = 1e-6
PAGE = 128

LANES = 128
LOG2E = 1.4426950408889634
HEAD_PAD = LANES
VMEM_LIMIT = 56 * 1024 * 1024

PRE_TILE = 256
PRE_SUB = 256
ROW_TILE = 1024
ATTN_TQ = 256
ATTN_TK = 256
ATTN_QT = 2
HGRN_CHUNK = 128
HGRN_TILE = 1024
PAGES_PER_GROUP = 32
HSTEP_ROWS = 8
POS_BLOCK = 2 * PAGE
DMA_SLOTS = 4


def _full(shape):
    return pl.BlockSpec(shape, lambda *_: (0,) * len(shape))


def _params(n_axes):
    return pltpu.CompilerParams(dimension_semantics=("arbitrary",) * n_axes,
                                vmem_limit_bytes=VMEM_LIMIT)


def _dot(a, b):
    return jnp.dot(a, b, preferred_element_type=F32)


def _dot_nt(a, b, precision=None):
    return lax.dot_general(a, b, (((1,), (1,)), ((), ())), precision=precision,
                           preferred_element_type=F32)


def _rms(x, g):
    r = lax.rsqrt(jnp.mean(x * x, axis=-1, keepdims=True) + EPS)
    return (x * r) * g


def _rope_lanes(t, c, s1, s2):
    return t * c + pltpu.roll(t, LANES - A_ROPE // 2, axis=1) * s1 + pltpu.roll(t, A_ROPE // 2, axis=1) * s2


def _pre_kernel(x_ref, ng_ref, wa_ref, gcq_ref, wuq_ref, gckv_ref,
                wuk_ref, wuv_ref, gqn_ref, gkn_ref, c_ref, s1_ref, s2_ref, lb_ref,
                lat_ref, krp_ref, q_ref, k_ref, vt_ref, hq_ref, hk_ref, hv_ref, hg_ref, *, layer, qscale, sub):
    lbl = lb_ref[...]
    e = jnp.exp(lbl - jnp.max(lbl, axis=0, keepdims=True))
    lb = jnp.sum(e[:layer + 1], axis=0, keepdims=True) / jnp.sum(e, axis=0, keepdims=True)
    gqn, gkn = gqn_ref[...], gkn_ref[...]
    o1, o2, o3 = Q_LORA, Q_LORA + KV_LORA, Q_LORA + KV_LORA + LANES

    def rows_step(rows):
        x = x_ref[rows, :]
        xn = _rms(x, ng_ref[...]).astype(BF16)
        c, s1, s2 = c_ref[rows, :], s1_ref[rows, :], s2_ref[rows, :]

        y = _dot(xn, wa_ref[:, :o3])
        cqn = _rms(y[:, :o1], gcq_ref[...]).astype(BF16)
        qraw = _dot(cqn, wuq_ref[...])
        for h in range(A_HEADS):
            hs = slice(h * HEAD_PAD, (h + 1) * HEAD_PAD)
            t = _rope_lanes(qraw[:, hs], c, s1, s2)
            r = lax.rsqrt(jnp.sum(t * t, axis=-1, keepdims=True) * (1.0 / A_QK) + EPS)
            q_ref[rows, hs] = (((t * r) * gqn) * qscale).astype(BF16)

        ckv = _rms(y[:, o1:o2], gckv_ref[...])
        lat_ref[rows, :] = ckv
        ckvb = ckv.astype(BF16)
        krp = _rope_lanes(y[:, o2:o3], c, s1, s2)
        krp_ref[rows, :] = krp
        kraw = _dot(ckvb, wuk_ref[...])
        for h in range(A_HEADS):
            hs = slice(h * HEAD_PAD, (h + 1) * HEAD_PAD)
            t = kraw[:, hs] + krp
            r = lax.rsqrt(jnp.sum(t * t, axis=-1, keepdims=True) * (1.0 / A_QK) + EPS)
            k_ref[rows, hs] = ((t * r) * gkn).astype(BF16)
        vt_ref[:, rows] = _dot_nt(wuv_ref[...], ckvb).astype(BF16)

        b = _dot(xn, wa_ref[:, o3:])
        bq, z, bi = b[:, :B_FDIM], b[:, B_FDIM:2 * B_FDIM], b[:, 2 * B_FDIM:]
        hq_ref[rows, :] = bq * jax.nn.sigmoid(bq)
        sz = jax.nn.sigmoid(z)
        hg_ref[rows, :] = jnp.log(lb + (1.0 - lb) * sz)
        hk_ref[rows, :] = (1.0 - lb) * (1.0 - sz)
        hv_ref[rows, :] = bi

    tm = x_ref.shape[0]
    if tm == sub:
        rows_step(slice(None))
    else:
        def body(r, _):
            rows_step(pl.ds(pl.multiple_of(r * sub, sub), sub))
            return 0
        lax.fori_loop(0, tm // sub, body, 0)


def _pre(x, tabs, w, tm, tab_blocks, layer, qscale=1.0, sub=None):
    rows = x.shape[0]
    grid = (rows // tm,)
    row = lambda n: pl.BlockSpec((tm, n), lambda i: (i, 0))
    tab = pl.BlockSpec((tm, LANES), lambda i: (i % tab_blocks, 0))
    ins = [x, w["norm_g"], w["wa"], w["g_cq"], w["w_uq"], w["g_ckv"],
           w["w_uk"], w["w_uvt"], w["g_qn"], w["g_kn"], tabs[0], tabs[1], tabs[2], w["lb_logits"]]
    in_specs = [row(D_MODEL)] + [_full(a.shape) for a in ins[1:10]] + [tab, tab, tab] + [_full(ins[13].shape)]
    outs = [(KV_LORA, F32), (LANES, F32), (A_HEADS * HEAD_PAD, BF16), (A_HEADS * HEAD_PAD, BF16),
            None, (B_FDIM, F32), (B_FDIM, F32), (B_WIDTH, F32), (B_FDIM, F32)]
    out_specs = [pl.BlockSpec((A_WIDTH, tm), lambda i: (0, i)) if o is None else row(o[0]) for o in outs]
    out_shape = [jax.ShapeDtypeStruct((A_WIDTH, rows), BF16) if o is None else
                 jax.ShapeDtypeStruct((rows, o[0]), o[1]) for o in outs]
    return pl.pallas_call(
        functools.partial(_pre_kernel, layer=layer, qscale=qscale, sub=sub or tm),
        grid=grid, in_specs=in_specs, out_specs=out_specs, out_shape=out_shape,
        compiler_params=_params(1), name="pre_proj")(*ins)


def _attn_kernel(q_ref, k_ref, vt_ref, km_ref, vtm_ref, o_ref, ot_sc, s_sc, sm_sc, m_sc, l_sc):
    def tile(qt, _):
        i = pl.program_id(1) * ATTN_QT + qt
        qrows = pl.ds(pl.multiple_of(qt * ATTN_TQ, ATTN_TQ), ATTN_TQ)
        _attn_tile(i, qrows, q_ref, k_ref, vt_ref, km_ref, vtm_ref, o_ref, ot_sc, s_sc, sm_sc, m_sc, l_sc)
        return 0

    lax.fori_loop(0, ATTN_QT, tile, 0, unroll=True)


def _attn_tile(i, qrows, q_ref, k_ref, vt_ref, km_ref, vtm_ref, o_ref, ot_sc, s_sc, sm_sc, m_sc, l_sc):
    tq, tk = ATTN_TQ, ATTN_TK
    assert tq == tk
    key = lax.broadcasted_iota(jnp.int32, (tk, tq), 0)
    qry = lax.broadcasted_iota(jnp.int32, (tk, tq), 1)
    hsl = [slice(h * HEAD_PAD, (h + 1) * HEAD_PAD) for h in range(A_HEADS)]
    vsl = [slice(h * A_V, (h + 1) * A_V) for h in range(A_HEADS)]

    d0 = pl.multiple_of(i * tq, tq)
    for h in range(A_HEADS):
        s_sc[h, 0:tk, :] = _dot_nt(k_ref[pl.ds(d0, tk), hsl[h]], q_ref[qrows, hsl[h]])
        sm_sc[h] = _dot_nt(km_ref[:, hsl[h]], q_ref[qrows, hsl[h]])
    for h in range(A_HEADS):
        hr = slice(h, h + 1)
        s = jnp.where(key <= qry, s_sc[h, 0:tk, :], -jnp.inf)
        sm = sm_sc[h]
        m_new = jnp.maximum(jnp.max(s, axis=0, keepdims=True), jnp.max(sm, axis=0, keepdims=True))
        p = jnp.exp2(s - m_new)
        pm = jnp.exp2(sm - m_new)
        m_sc[hr, :] = m_new
        l_sc[hr, :] = jnp.sum(p, axis=0, keepdims=True) + jnp.sum(pm, axis=0, keepdims=True)
        ot_sc[vsl[h], :] = (_dot(vt_ref[vsl[h], pl.ds(d0, tk)], p.astype(BF16)) +
                            _dot(vtm_ref[vsl[h], :], pm.astype(BF16)))

    def visible(r0, keys):
        for h in range(A_HEADS):
            s_sc[h, 0:keys, :] = _dot_nt(k_ref[pl.ds(r0, keys), hsl[h]], q_ref[qrows, hsl[h]])
        for h in range(A_HEADS):
            hr = slice(h, h + 1)
            s = s_sc[h, 0:keys, :]
            m_new = jnp.maximum(m_sc[hr, :], jnp.max(s, axis=0, keepdims=True))
            a = jnp.exp2(m_sc[hr, :] - m_new)
            p = jnp.exp2(s - m_new)
            m_sc[hr, :] = m_new
            l_sc[hr, :] = l_sc[hr, :] * a + jnp.sum(p, axis=0, keepdims=True)
            ot_sc[vsl[h], :] = ot_sc[vsl[h], :] * a + _dot(vt_ref[vsl[h], pl.ds(r0, keys)], p.astype(BF16))

    def body(kb, _):
        visible(pl.multiple_of(kb * 4 * tk, 4 * tk), 4 * tk)
        return 0

    lax.fori_loop(0, i // 4, body, 0)

    @pl.when(i % 4 >= 2)
    def _():
        visible(pl.multiple_of((i // 4) * 4 * tk, 2 * tk), 2 * tk)

    @pl.when(i % 2 == 1)
    def _():
        visible(pl.multiple_of((i - 1) * tk, tk), tk)

    for h in range(A_HEADS):
        ot_sc[vsl[h], :] = ot_sc[vsl[h], :] / l_sc[h:h + 1, :]
    o_ref[qrows, :] = ot_sc[...].T


def _attn(q, k, vt, km, vtm, batch, seq):
    rows = ATTN_QT * ATTN_TQ
    nq = seq // rows
    return pl.pallas_call(
        _attn_kernel, grid=(batch, nq),
        in_specs=[pl.BlockSpec((rows, A_HEADS * HEAD_PAD), lambda b, i: (b * nq + i, 0)),
                  pl.BlockSpec((seq, A_HEADS * HEAD_PAD), lambda b, i: (b, 0)),
                  pl.BlockSpec((A_WIDTH, seq), lambda b, i: (0, b)),
                  _full(km.shape), _full(vtm.shape)],
        out_specs=pl.BlockSpec((rows, A_WIDTH), lambda b, i: (b * nq + i, 0)),
        out_shape=jax.ShapeDtypeStruct((batch * seq, A_WIDTH), F32),
        scratch_shapes=[pltpu.VMEM((A_WIDTH, ATTN_TQ), F32), pltpu.VMEM((A_HEADS, 4 * ATTN_TK, ATTN_TQ), F32),
                        pltpu.VMEM((A_HEADS, km.shape[0], ATTN_TQ), F32),
                        pltpu.VMEM((A_HEADS, ATTN_TQ), F32), pltpu.VMEM((A_HEADS, ATTN_TQ), F32)],
        compiler_params=_params(2), name="prompt_attn")(q, k, vt, km, vtm)


def _split3(x):
    a = x.astype(BF16)
    r = x - a.astype(F32)
    b = r.astype(BF16)
    c = (r - b.astype(F32)).astype(BF16)
    return a, b, c


def _hgrn_chunk(q, kk, v, g, st, tril, chunk, cum_ref, krow):
    g1, g2, g3 = _split3(g)
    cum = (_dot(tril, g1) + _dot(tril, g2) + _dot(tril, g3)) * LOG2E
    cum_ref[...] = cum
    yield None
    nblk = chunk // 8
    cb = [cum[8 * j:8 * j + 8] for j in range(nblk)]
    qb = [q[8 * j:8 * j + 8] for j in range(nblk)]
    lane = lax.broadcasted_iota(jnp.int32, (8, chunk), 1)
    sub = lax.broadcasted_iota(jnp.int32, (8, chunk), 0)
    ab = []
    for j in range(nblk):
        a = jnp.zeros((8, chunk), F32)
        for s in range(8 * j, 8 * j + 8):
            col = jnp.sum(jnp.exp2(cb[j] - cum_ref[s:s + 1, :]) * qb[j] * krow(s), axis=-1, keepdims=True)
            a = jnp.where(lane == s, col, a)
        ab.append(jnp.where(lane <= sub + 8 * j, a, 0.0))
        yield None
    attn = jnp.concatenate(ab, axis=0)
    row = lax.broadcasted_iota(jnp.int32, (chunk, chunk), 0)
    col = lax.broadcasted_iota(jnp.int32, (chunk, chunk), 1)
    kb = [kk[8 * j:8 * j + 8] for j in range(nblk)]
    dead = jnp.zeros((8, B_DK), F32)
    w = 8
    while w < chunk:
        qs, ks = [], []
        for j in range(nblk):
            blk = (8 * j) // w
            if blk % 2:
                qs.append(qb[j] * jnp.exp2(cb[j] - cum_ref[blk * w - 1:blk * w, :]))
                ks.append(dead)
            else:
                qs.append(dead)
                ks.append(kb[j] * jnp.exp2(cum_ref[blk * w + w - 1:blk * w + w, :] - cb[j]))
        qa = jnp.concatenate(qs, axis=0).astype(BF16)
        ka = jnp.concatenate(ks, axis=0).astype(BF16)
        sh = w.bit_length() - 1
        rb, cbk = row >> sh, col >> sh
        attn = attn + jnp.where((rb == cbk + 1) & ((cbk & 1) == 0), _dot_nt(qa, ka), 0.0)
        w *= 2
        yield None
    last = cum_ref[chunk - 1:chunk, :]
    o = _dot(attn.astype(BF16), v.astype(BF16)) + _dot_nt((q * jnp.exp2(cum)).astype(BF16), st.astype(BF16))
    kdec = (kk * jnp.exp2(last - cum)).astype(BF16)
    upd = lax.dot_general(v.astype(BF16), kdec, (((0,), (0,)), ((), ())), preferred_element_type=F32)
    yield o, jnp.exp2(last) * st + upd


def _hgrn_kernel(q_ref, k_ref, v_ref, g_ref, st0_ref, o_ref, st_ref, st_sc, cum_sc, kk_sc, *, chunk, tile,
                 transpose_out):
    t = pl.program_id(1)

    @pl.when(t == 0)
    def _():
        st_sc[...] = st0_ref[...]

    r = lax.broadcasted_iota(jnp.int32, (chunk, chunk), 0)
    c = lax.broadcasted_iota(jnp.int32, (chunk, chunk), 1)
    tril = (c <= r).astype(BF16)

    def body(ci, _):
        r0 = pl.multiple_of(ci * chunk, chunk)
        hsl = [slice(h * B_DK, (h + 1) * B_DK) for h in range(B_HEADS)]
        stages = {}
        for h in range(B_HEADS):
            kk = k_ref[pl.ds(r0, chunk), hsl[h]]
            kk_sc[h] = kk
            krow = lambda s, h=h: kk_sc[h, s:s + 1, :]
            stages[h] = _hgrn_chunk(q_ref[pl.ds(r0, chunk), hsl[h]], kk,
                                    v_ref[pl.ds(r0, chunk), hsl[h]], g_ref[pl.ds(r0, chunk), hsl[h]],
                                    st_sc[h], tril, chunk, cum_sc.at[h], krow)
        while stages:
            for h in list(stages):
                out = next(stages[h])
                if out is not None:
                    o_ref[pl.ds(r0, chunk), hsl[h]] = out[0]
                    st_sc[h] = out[1]
                    del stages[h]
        return 0

    lax.fori_loop(0, tile // chunk, body, 0, unroll=4 if tile // chunk > 1 else 1)

    @pl.when(t == pl.num_programs(1) - 1)
    def _():
        for h in range(B_HEADS):
            st_ref[0, h] = st_sc[h].T if transpose_out else st_sc[h]


def _hgrn(hq, hk, hv, hg, st0, batch, seq, chunk, tile, transpose_out):
    nt = seq // tile
    row = pl.BlockSpec((tile, B_FDIM), lambda b, t: (b * nt + t, 0))
    return pl.pallas_call(
        functools.partial(_hgrn_kernel, chunk=chunk, tile=tile, transpose_out=transpose_out),
        grid=(batch, nt),
        in_specs=[row, row, row, row, _full(st0.shape)],
        out_specs=[row, pl.BlockSpec((1, B_HEADS, B_DV, B_DK), lambda b, t: (b, 0, 0, 0))],
        out_shape=[jax.ShapeDtypeStruct((batch * seq, B_WIDTH), F32),
                   jax.ShapeDtypeStruct((batch, B_HEADS, B_DV, B_DK), F32)],
        scratch_shapes=[pltpu.VMEM((B_HEADS, B_DV, B_DK), F32), pltpu.VMEM((B_HEADS, chunk, B_DK), F32),
                        pltpu.VMEM((B_HEADS, chunk, B_DK), F32)],
        compiler_params=_params(2), name="hgrn_chunks")(hq, hk, hv, hg, st0)


def _split2(x):
    hi = x.astype(BF16)
    return hi, (x - hi.astype(F32)).astype(BF16)


def _paged_kernel(pt_ref, lat_hbm, kr_hbm, qrow_ref, qr_ref, qh_ref, kh_ref, latn_ref, wukt_ref,
                  gkrow_ref, gkr_ref, wuv_ref, o_ref, lhs, latbuf, krbuf, cbuf, kx_sc, sems, *, layer, n_pages):
    b = pl.program_id(0)
    gp = PAGES_PER_GROUP
    n_groups = n_pages // gp
    gpos = gp * PAGE
    nk = A_HEADS * A_NOPE
    log2e = 1.4426950408889634
    sc2 = (A_QK ** -0.5) * log2e

    def page_copies(bb, g, slot):
        out = []
        for j in range(gp):
            page = pt_ref[bb, g * gp + j]
            dst = pl.ds(j * PAGE, PAGE)
            out.append(pltpu.make_async_copy(lat_hbm.at[layer, page], latbuf.at[slot, dst], sems.at[0, slot]))
            out.append(pltpu.make_async_copy(kr_hbm.at[layer, page], krbuf.at[slot, j], sems.at[1, slot]))
        return out

    def start(bb, g, slot):
        for cp in page_copies(bb, g, slot):
            cp.start()

    def wait(g, slot):
        for cp in page_copies(b, g, slot):
            cp.wait()

    ring = DMA_SLOTS
    ahead = ring - 1
    assert n_groups % ring == 0 and ahead <= n_groups

    def fetch_ahead(g):
        gg = g + ahead
        wrap = gg >= n_groups
        bb = jnp.where(wrap, b + 1, b)
        g2 = jnp.where(wrap, gg - n_groups, gg)

        @pl.when(bb < pl.num_programs(0))
        def _():
            start(bb, g2, lax.rem(gg, ring))

    @pl.when(b == 0)
    def _():
        for d in range(ahead):
            start(0, d, d)

    @pl.when(b == 0)
    def _():
        lhs[0:nk, :] = wukt_ref[...]

    hrow = lax.broadcasted_iota(jnp.int32, (A_HEADS, nk), 0)
    hcol = lax.broadcasted_iota(jnp.int32, (A_HEADS, nk), 1) // A_NOPE
    qsel = jnp.where(hrow == hcol, qrow_ref[0] * gkrow_ref[...], 0.0)
    q_hi, q_lo = _split2(qsel)
    qabs = _dot(q_hi, wukt_ref[...]) + _dot(q_lo, wukt_ref[...])
    lhs[nk:nk + 2 * A_HEADS, :] = jnp.concatenate(_split2(qabs), axis=0)
    qrg = jnp.concatenate(_split2(qr_ref[0] * gkr_ref[...]), axis=0)
    ppb = POS_BLOCK // PAGE

    def scores(slot, cslot, between=None):
        parts = []
        nblk = gpos // POS_BLOCK

        def project(j):
            rows = pl.ds(j * POS_BLOCK, POS_BLOCK)
            cb = latbuf[slot, rows, :].astype(BF16)
            cbuf[cslot, rows, :] = cb
            kx_sc[j] = _dot_nt(lhs[...], cb)

        for j in range(nblk):
            project(j)
        if between is not None:
            between()
        for j in range(nblk):
            kn = kx_sc[j, 0:nk, :]
            ss = jnp.sum((kn * kn).reshape(A_HEADS, A_NOPE, POS_BLOCK), axis=1)
            raw = kx_sc[j, nk:nk + A_HEADS, :] + kx_sc[j, nk + A_HEADS:nk + 2 * A_HEADS, :]
            krt = jnp.concatenate([krbuf[slot, j * ppb + t] for t in range(ppb)], axis=-1)
            rr = _dot(qrg, krt.astype(BF16))
            raw = raw + rr[:A_HEADS] + rr[A_HEADS:]
            ss = ss + jnp.sum(krt * krt, axis=0, keepdims=True)
            parts.append(raw * lax.rsqrt(ss * (1.0 / A_QK) + EPS) * sc2)
        return jnp.concatenate(parts, axis=-1)

    def absorb(s, slot, carry):
        m, l, acc = carry
        m_new = jnp.maximum(m, jnp.max(s, axis=-1, keepdims=True))
        a = jnp.exp2(m - m_new)
        p = jnp.exp2(s - m_new)
        l = l * a + jnp.sum(p, axis=-1, keepdims=True)
        acc = acc * a + _dot(p.astype(BF16), cbuf[slot])
        return m_new, l, acc

    wait(0, 0)
    fetch_ahead(0)
    s0 = scores(0, 0)
    init = (jnp.full((A_HEADS, 1), -jnp.inf, F32), jnp.zeros((A_HEADS, 1), F32),
            jnp.zeros((A_HEADS, KV_LORA), F32))

    def group(g, carry):
        s_prev, state = carry
        cslot = lax.rem(g, 2)
        wait(g, lax.rem(g, ring))
        fetch_ahead(g)
        box = []
        s_cur = scores(lax.rem(g, ring), cslot, lambda: box.append(absorb(s_prev, 1 - cslot, state)))
        return s_cur, box[0]

    s_last, state = lax.fori_loop(1, n_groups, group, (s0, init), unroll=True)
    m, l, acc = absorb(s_last, (n_groups - 1) % 2, state)

    s_new = jnp.sum(qh_ref[0].astype(F32) * kh_ref[0].astype(F32), axis=-1, keepdims=True) * sc2
    m_new = jnp.maximum(m, s_new)
    a = jnp.exp2(m - m_new)
    p = jnp.exp2(s_new - m_new)
    acc = acc * a + p * latn_ref[0]
    l = l * a + p
    o_lat = (acc / l).astype(BF16)
    full = _dot(o_lat, wuv_ref[...])
    vrow = lax.broadcasted_iota(jnp.int32, (A_HEADS, A_WIDTH), 0)
    vcol = lax.broadcasted_iota(jnp.int32, (A_HEADS, A_WIDTH), 1) // A_V
    o_ref[0] = jnp.sum(jnp.where(vrow == vcol, full, 0.0), axis=0, keepdims=True)


def _paged(page_table, cache_lat, cache_kr, qrow, qr, qh, kh, latn, wukt, gkrow, gkr, wuv, layer):
    nb, n_pages = page_table.shape
    gp = PAGES_PER_GROUP
    per_b = lambda shape: pl.BlockSpec((1,) + shape, lambda b, pt: (b,) + (0,) * len(shape))
    full = lambda a: pl.BlockSpec(a.shape, lambda b, pt: (0,) * a.ndim)
    grid_spec = pltpu.PrefetchScalarGridSpec(
        num_scalar_prefetch=1, grid=(nb,),
        in_specs=[pl.BlockSpec(memory_space=pl.ANY), pl.BlockSpec(memory_space=pl.ANY),
                  per_b((1, A_HEADS * A_NOPE)), per_b((A_HEADS, A_ROPE)), per_b((A_HEADS, HEAD_PAD)),
                  per_b((A_HEADS, HEAD_PAD)), per_b((1, KV_LORA)), full(wukt), full(gkrow), full(gkr), full(wuv)],
        out_specs=per_b((1, A_WIDTH)),
        scratch_shapes=[pltpu.VMEM((A_HEADS * A_NOPE + 2 * A_HEADS, KV_LORA), BF16),
                        pltpu.VMEM((DMA_SLOTS, gp * PAGE, KV_LORA), F32), pltpu.VMEM((DMA_SLOTS, gp, A_ROPE, PAGE), F32),
                        pltpu.VMEM((2, gp * PAGE, KV_LORA), BF16),
                        pltpu.VMEM((gp * PAGE // POS_BLOCK, A_HEADS * A_NOPE + 2 * A_HEADS, POS_BLOCK), F32),
                        pltpu.SemaphoreType.DMA((2, DMA_SLOTS))])
    return pl.pallas_call(
        functools.partial(_paged_kernel, layer=layer, n_pages=n_pages), grid_spec=grid_spec,
        out_shape=jax.ShapeDtypeStruct((nb, 1, A_WIDTH), F32),
        compiler_params=_params(1), name="paged_attn")(
            page_table, cache_lat, cache_kr, qrow, qr, qh, kh, latn, wukt, gkrow, gkr, wuv)


def _hstep_kernel(s_ref, q_ref, k_ref, g_ref, v_ref, o_ref, sn_ref, *, layer, rows):
    r = lax.broadcasted_iota(jnp.int32, (B_DK, B_DK), 0)
    c = lax.broadcasted_iota(jnp.int32, (B_DK, B_DK), 1)
    eye = r == c

    def column(row):
        return jnp.sum(jnp.where(eye, row, 0.0), axis=-1, keepdims=True)

    for i in range(rows):
        for h in range(B_HEADS):
            hs = slice(h * B_DK, (h + 1) * B_DK)
            sn = column(jnp.exp(g_ref[i, :, hs])) * s_ref[layer, i, h] + column(k_ref[i, :, hs]) * v_ref[i, :, hs]
            sn_ref[i, h] = sn
            o_ref[i, :, hs] = jnp.sum(column(q_ref[i, :, hs]) * sn, axis=0, keepdims=True)


def _hstep(state, q, k, g, v, layer):
    nb = state.shape[1]
    rows = HSTEP_ROWS
    st_in = pl.BlockSpec((state.shape[0], rows, B_HEADS, B_DK, B_DV), lambda b: (0, b, 0, 0, 0))
    st_out = pl.BlockSpec((rows, B_HEADS, B_DK, B_DV), lambda b: (b, 0, 0, 0))
    row = pl.BlockSpec((rows, 1, B_FDIM), lambda b: (b, 0, 0))
    return pl.pallas_call(
        functools.partial(_hstep_kernel, layer=layer, rows=rows), grid=(nb // rows,),
        in_specs=[st_in, row, row, row, row], out_specs=[row, st_out],
        out_shape=[jax.ShapeDtypeStruct((nb, 1, B_WIDTH), F32), jax.ShapeDtypeStruct(state.shape[1:], F32)],
        compiler_params=_params(1), name="hgrn_step")(state, q, k, g, v)


def _merge_kernel(x_ref, ng_ref, wg_ref, oa_ref, ob_ref, gbn_ref, woa_ref, wob_ref, wo_ref, y_ref):
    x = x_ref[...]
    xn = _rms(x, ng_ref[...]).astype(BF16)
    gates = _dot(xn, wg_ref[...])
    ga, gb = gates[:, :A_WIDTH], gates[:, A_WIDTH:A_WIDTH + B_WIDTH]
    ma = gates[:, A_WIDTH + B_WIDTH:A_WIDTH + B_WIDTH + D_MODEL]
    mb = gates[:, A_WIDTH + B_WIDTH + D_MODEL:]
    ya = _dot((oa_ref[...] * (ga * jax.nn.sigmoid(ga))).astype(BF16), woa_ref[...])
    gbn = gbn_ref[...]
    obn = jnp.concatenate([_rms(ob_ref[:, h * B_DV:(h + 1) * B_DV], gbn) for h in range(B_HEADS)], axis=-1)
    yb = _dot((obn * (gb * jax.nn.sigmoid(gb))).astype(BF16), wob_ref[...])
    mix = jax.nn.sigmoid(ma) * ya + jax.nn.sigmoid(mb) * yb
    y_ref[...] = x + _dot(mix.astype(BF16), wo_ref[...])


def _merge(x, oa, ob, w, tm):
    rows = x.shape[0]
    row = lambda n: pl.BlockSpec((tm, n), lambda i: (i, 0))
    ins = [x, w["norm_g"], w["wg"], oa, ob, w["g_bn"], w["w_oa"], w["w_ob"], w["w_o"]]
    in_specs = [row(D_MODEL), _full(ins[1].shape), _full(ins[2].shape), row(A_WIDTH), row(B_WIDTH)] + \
               [_full(a.shape) for a in ins[5:]]
    return pl.pallas_call(
        _merge_kernel, grid=(rows // tm,), in_specs=in_specs, out_specs=row(D_MODEL),
        out_shape=jax.ShapeDtypeStruct((rows, D_MODEL), F32),
        compiler_params=_params(1), name="merge_out")(*ins)


def _head_pad_cols(w3, width):
    pad = jnp.zeros(w3.shape[:2] + (HEAD_PAD - width,), w3.dtype)
    return jnp.concatenate([w3, pad], axis=-1).reshape(w3.shape[0], -1)


def _rope_tables(pos):
    half = A_ROPE // 2
    inv = ROPE_THETA ** (-jnp.arange(half, dtype=F32) / half)
    ang = pos.astype(F32)[:, None] * inv
    cos, sin = jnp.cos(ang), jnp.sin(ang)
    n = pos.shape[0]
    one, zero = jnp.ones((n, A_NOPE), F32), jnp.zeros((n, A_NOPE), F32)
    tail = jnp.zeros((n, HEAD_PAD - A_QK), F32)
    z16 = jnp.zeros((n, half), F32)
    c = jnp.concatenate([one, cos, cos, tail], axis=-1)
    s1 = jnp.concatenate([zero, -sin, z16, tail], axis=-1)
    s2 = jnp.concatenate([zero, z16, sin, tail], axis=-1)
    return c, s1, s2


def _lane_gain(g):
    return jnp.concatenate([g, jnp.zeros((HEAD_PAD - A_QK,), g.dtype)])[None, :]


def kernel(x_prompt, x_sample, cache_latent, cache_krope, state_hgrn, page_table, meta_tokens,
           norm_g, w_in, g_cq, w_uq, g_ckv, w_uk, w_uv, g_qn, g_kn, lb_logits, g_bn, w_oa, w_ob, w_o):
    batch, seq, _ = x_prompt.shape
    dec_batch, dec_seq, _ = x_sample.shape
    depth = w_in.shape[0]
    assert depth == 1 and dec_seq == 1
    past_len = page_table.shape[1] * PAGE
    l = 0

    o0 = Q_LORA
    o1 = o0 + KV_LORA
    o2 = o1 + A_ROPE
    o3 = o2 + 3 * B_FDIM
    wi = w_in[l]
    wkr = jnp.zeros((D_MODEL, HEAD_PAD), F32).at[:, A_NOPE:A_QK].set(wi[:, o1:o2])
    w = {
        "norm_g": norm_g[l][None, :],
        "wa": jnp.concatenate([wi[:, :o1], wkr, wi[:, o2:o3]], axis=1).astype(BF16), "wg": wi[:, o3:].astype(BF16),
        "g_cq": g_cq[l][None, :], "g_ckv": g_ckv[l][None, :],
        "w_uq": _head_pad_cols(w_uq[l].reshape(Q_LORA, A_HEADS, A_QK), A_QK).astype(BF16),
        "w_uk": _head_pad_cols(w_uk[l], A_NOPE).astype(BF16),
        "w_uv": w_uv[l].reshape(KV_LORA, A_WIDTH).astype(BF16),
        "w_uvt": w_uv[l].reshape(KV_LORA, A_WIDTH).T.astype(BF16),
        "g_qn": _lane_gain(g_qn[l]), "g_kn": _lane_gain(g_kn[l]),
        "lb_logits": lb_logits, "g_bn": g_bn[l][None, :],
        "w_oa": w_oa[l].astype(BF16), "w_ob": w_ob[l].astype(BF16), "w_o": w_o[l].astype(BF16),
    }

    xs = x_sample.reshape(dec_batch, D_MODEL)
    tabs_m = _rope_tables(jnp.arange(N_META))
    tabs_s = _rope_tables(jnp.full((dec_batch,), past_len, jnp.int32))
    tabs_ms = tuple(jnp.concatenate([a, b], axis=0) for a, b in zip(tabs_m, tabs_s))
    small = _pre(jnp.concatenate([meta_tokens, xs], axis=0), tabs_ms, w, N_META + dec_batch, 1, l)
    lat_m, krp_m, _, k_m, _, mq, mk, mv, mg = [a[:N_META] for a in small]
    lat_s, krp_s, q_s, k_s, _, sq, sk, sv, sg = [a[N_META:] for a in small]
    vt_m = small[4][:, :N_META]

    xp = x_prompt.reshape(batch * seq, D_MODEL)
    tabs_p = _rope_tables(N_META + jnp.arange(seq))
    qscale = (A_QK ** -0.5) * 1.4426950408889634
    lat_p, krp_p, q_p, k_p, vt_p, hq, hk, hv, hg = _pre(xp, tabs_p, w, PRE_TILE, seq // PRE_TILE, l, qscale, PRE_SUB)
    lat_all = jnp.concatenate([jnp.broadcast_to(lat_m[None], (batch, N_META, KV_LORA)),
                               lat_p.reshape(batch, seq, KV_LORA)], axis=1)
    kr_all = jnp.concatenate([jnp.broadcast_to(krp_m[None, :, A_NOPE:A_QK], (batch, N_META, A_ROPE)),
                              krp_p[:, A_NOPE:A_QK].reshape(batch, seq, A_ROPE)], axis=1)

    oa = _attn(q_p, k_p, vt_p, k_m, vt_m, batch, seq)
    zero_state = jnp.zeros((B_HEADS, B_DV, B_DK), F32)
    _, st_meta = _hgrn(mq, mk, mv, mg, zero_state, 1, N_META, N_META, N_META, False)
    ob, st_fin = _hgrn(hq, hk, hv, hg, st_meta[0], batch, seq, HGRN_CHUNK, HGRN_TILE, True)
    y_prompt = _merge(xp, oa, ob, w, ROW_TILE).reshape(batch, seq, D_MODEL)

    q_s3 = q_s.reshape(dec_batch, A_HEADS, HEAD_PAD)
    k_s3 = k_s.reshape(dec_batch, A_HEADS, HEAD_PAD)
    qrow = q_s3[:, :, :A_NOPE].astype(F32).reshape(dec_batch, 1, A_HEADS * A_NOPE)
    qrope = q_s3[:, :, A_NOPE:A_QK].astype(F32)
    gkrow = jnp.tile(g_kn[l][:A_NOPE], A_HEADS)[None, :]
    gkr = g_kn[l][None, A_NOPE:]
    wukt = w_uk[l].reshape(KV_LORA, A_HEADS * A_NOPE).T.astype(BF16)
    oa_s = _paged(page_table, cache_latent, jnp.swapaxes(cache_krope, 2, 3), qrow, qrope, q_s3, k_s3,
                  lat_s[:, None, :], wukt, gkrow, gkr, w["w_uv"], l)
    row3 = lambda a: a.reshape(dec_batch, 1, B_FDIM)
    ob_s, st_s = _hstep(state_hgrn, row3(sq), row3(sk), row3(sg), row3(sv), l)
    y_sample = _merge(xs, oa_s.reshape(dec_batch, A_WIDTH), ob_s.reshape(dec_batch, B_WIDTH), w, dec_batch)

    return (y_prompt, y_sample.reshape(dec_batch, dec_seq, D_MODEL),
            lat_all[None], kr_all[None], st_fin[None],
            lat_s.reshape(1, dec_batch, dec_seq, KV_LORA),
            krp_s[:, A_NOPE:A_QK].reshape(1, dec_batch, dec_seq, A_ROPE),
            st_s[None])
```
